```python
import numpy as np
import jax
import jax.numpy as jnp
from jax import lax

D_MODEL = 1024
BATCH = 8
SEQ = 4096
DEPTH = 1

HEAD_DIM = 64
ROPE_THETA = 500000.0
LN_EPS = 1e-5

NSA_HEADS = 8
NSA_KV_HEADS = 2
CMP_LEN = 32
CMP_STRIDE = 16
CMP_HIDDEN = 128
SEL_BLOCK = 64
SEL_COUNT = 16
WINDOW = 512
NSA_Q_CHUNK = 64
FORCE_BONUS = 1e4

DSA_HEADS = 4
IDX_HEADS = 8
IDX_DIM = 32
DSA_TOPK_MAX = 256
DSA_Q_CHUNK = 128

MEM_HEADS = 4
MEM_LEN = 256

PEER_HEADS = 8
PEER_N_KEYS = 128
PEER_KEY_DIM = 256
PEER_TOPK = 16
PEER_CHUNK = 128
PEER_N_EXPERTS = PEER_N_KEYS * PEER_N_KEYS

DEEPNORM_ALPHA = (2.0 * DEPTH) ** 0.25
DEEPNORM_BETA = (8.0 * DEPTH) ** -0.25

IN_SPLITS = (
    NSA_HEADS * HEAD_DIM,
    6 * NSA_KV_HEADS * HEAD_DIM,
    3 * NSA_HEADS,
    DSA_HEADS * HEAD_DIM,
    2 * HEAD_DIM,
    IDX_HEADS * IDX_DIM,
    IDX_DIM,
    IDX_HEADS,
    MEM_HEADS * HEAD_DIM,
)
IN_WIDTH = sum(IN_SPLITS)

kernel_name = 'hybrid_nsa_dsa_mem_peer_deepnorm'


def layer_norm(x, g, b):
    xf = x.astype(jnp.float32)
    mu = jnp.mean(xf, axis=-1, keepdims=True)
    var = jnp.mean(jnp.square(xf - mu), axis=-1, keepdims=True)
    return ((xf - mu) * lax.rsqrt(var + LN_EPS) * g + b).astype(x.dtype)


def rope_partial(x, pos):
    r = x.shape[-1] // 4
    half = r // 2
    freqs = jnp.power(ROPE_THETA, -jnp.arange(half, dtype=jnp.float32) * 2.0 / r)
    ang = pos.astype(jnp.float32)[..., None] * freqs
    cos = jnp.cos(ang)[:, :, None, :]
    sin = jnp.sin(ang)[:, :, None, :]
    xf = x.astype(jnp.float32)
    x1 = xf[..., :half]
    x2 = xf[..., half:r]
    out = jnp.concatenate([x1 * cos - x2 * sin, x2 * cos + x1 * sin, xf[..., r:]], axis=-1)
    return out.astype(x.dtype)


def masked_softmax(s, mask):
    s = jnp.where(mask, s.astype(jnp.float32), -jnp.inf)
    m = jnp.max(s, axis=-1, keepdims=True)
    m = jnp.where(jnp.isfinite(m), m, 0.0)
    e = jnp.exp(s - m)
    den = jnp.sum(e, axis=-1, keepdims=True)
    return e / jnp.where(den > 0, den, 1.0)


def compress_blocks(kv, pe, w1, w2):
    B, S, G, d = kv.shape
    nc = (S - CMP_LEN) // CMP_STRIDE + 1
    idx = jnp.arange(nc)[:, None] * CMP_STRIDE + jnp.arange(CMP_LEN)[None, :]
    blocks = kv[:, idx] + pe[:, None, :]
    flat = blocks.transpose(0, 1, 3, 2, 4).reshape(B, nc, G, CMP_LEN * d)
    return jax.nn.gelu(flat @ w1, approximate=False) @ w2


def nsa_attention(q, kc, vc, ks, vs, kw, vw, gates):
    B, S, H, d = q.shape
    G = ks.shape[2]
    hpg = H // G
    C = NSA_Q_CHUNK
    scale = d ** -0.5
    nc = kc.shape[1]
    n_blocks = S // SEL_BLOCK
    n_pick = min(SEL_COUNT, n_blocks)
    qg = q.reshape(B, S, G, hpg, d).transpose(0, 2, 3, 1, 4)
    gg = gates.reshape(B, S, G, hpg, 3).transpose(0, 2, 3, 1, 4)
    kc = kc.transpose(0, 2, 1, 3)
    vc = vc.transpose(0, 2, 1, 3)
    ks_b = ks.transpose(0, 2, 1, 3).reshape(B, G, n_blocks, SEL_BLOCK, d)
    vs_b = vs.transpose(0, 2, 1, 3).reshape(B, G, n_blocks, SEL_BLOCK, d)
    pad = ((0, 0), (0, 0), (WINDOW, 0), (0, 0))
    kw_p = jnp.pad(kw.transpose(0, 2, 1, 3), pad)
    vw_p = jnp.pad(vw.transpose(0, 2, 1, 3), pad)
    c_start = jnp.arange(nc) * CMP_STRIDE
    cmp_end = c_start + CMP_LEN - 1
    blk_ids = jnp.arange(n_blocks)
    s_start = blk_ids * SEL_BLOCK
    overlap = jnp.clip(jnp.minimum(c_start[:, None] + CMP_LEN, s_start[None, :] + SEL_BLOCK)
                       - jnp.maximum(c_start[:, None], s_start[None, :]), 0).astype(jnp.float32) / CMP_LEN
    in_blk = jnp.arange(SEL_BLOCK)
    b_ix = jnp.arange(B)[:, None, None, None]
    g_ix = jnp.arange(G)[None, :, None, None]

    def chunk(c):
        t0 = c * C
        t = t0 + jnp.arange(C)
        qc = lax.dynamic_slice_in_dim(qg, t0, C, axis=3)
        gc = lax.dynamic_slice_in_dim(gg, t0, C, axis=3)
        s_cmp = jnp.einsum('bghcd,bgnd->bghcn', qc, kc) * scale
        p_cmp = masked_softmax(s_cmp, cmp_end[None, :] <= t[:, None])
        o_cmp = jnp.einsum('bghcn,bgnd->bghcd', p_cmp, vc)
        imp = jnp.einsum('bghcn,nm->bgcm', p_cmp, overlap)
        cur = t // SEL_BLOCK
        forced = (blk_ids[None, :] == 0) | (blk_ids[None, :] == cur[:, None]) | (blk_ids[None, :] == cur[:, None] - 1)
        visible = s_start[None, :] <= t[:, None]
        sel_score = jnp.where(visible, imp + FORCE_BONUS * forced, -jnp.inf)
        _, sel = lax.top_k(sel_score, n_pick)
        k_sel = ks_b[b_ix, g_ix, sel].reshape(B, G, C, n_pick * SEL_BLOCK, d)
        v_sel = vs_b[b_ix, g_ix, sel].reshape(B, G, C, n_pick * SEL_BLOCK, d)
        pos_sel = (sel[..., None] * SEL_BLOCK + in_blk).reshape(B, G, C, n_pick * SEL_BLOCK)
        s_sel = jnp.einsum('bghcd,bgckd->bghck', qc, k_sel) * scale
        p_sel = masked_softmax(s_sel, (pos_sel <= t[:, None])[:, :, None])
        o_sel = jnp.einsum('bghck,bgckd->bghcd', p_sel, v_sel)
        kwc = lax.dynamic_slice_in_dim(kw_p, t0, WINDOW + C, axis=2)
        vwc = lax.dynamic_slice_in_dim(vw_p, t0, WINDOW + C, axis=2)
        pos_w = t0 - WINDOW + jnp.arange(WINDOW + C)
        win_mask = (pos_w[None, :] <= t[:, None]) & (pos_w[None, :] > t[:, None] - WINDOW) & (pos_w[None, :] >= 0)
        s_win = jnp.einsum('bghcd,bgkd->bghck', qc, kwc) * scale
        p_win = masked_softmax(s_win, win_mask)
        o_win = jnp.einsum('bghck,bgkd->bghcd', p_win, vwc)
        return gc[..., 0:1] * o_cmp + gc[..., 1:2] * o_sel + gc[..., 2:3] * o_win

    out = lax.map(chunk, jnp.arange(S // C))
    return out.transpose(1, 0, 4, 2, 3, 5).reshape(B, S, H * d)


def dsa_attention(q, k, v, q_idx, k_idx, w_idx):
    B, S, H, d = q.shape
    C = DSA_Q_CHUNK
    topk = min(DSA_TOPK_MAX, S // 4)
    scale = d ** -0.5
    w = w_idx * (IDX_HEADS ** -0.5 * IDX_DIM ** -0.5)
    key_pos = jnp.arange(S)
    b_ix = jnp.arange(B)[:, None, None]

    def chunk(c):
        t0 = c * C
        t = t0 + jnp.arange(C)
        qi = lax.dynamic_slice_in_dim(q_idx, t0, C, axis=1)
        wi = lax.dynamic_slice_in_dim(w, t0, C, axis=1)
        logits = jax.nn.relu(jnp.einsum('bchd,bsd->bchs', qi, k_idx))
        score = jnp.einsum('bchs,bch->bcs', logits, wi).astype(jnp.float32)
        score = jnp.where(key_pos[None, :] <= t[:, None], score, -jnp.inf)
        _, sel = lax.top_k(score, topk)
        k_sel = k[b_ix, sel]
        v_sel = v[b_ix, sel]
        qc = lax.dynamic_slice_in_dim(q, t0, C, axis=1)
        s = jnp.einsum('bchd,bckd->bchk', qc, k_sel) * scale
        p = masked_softmax(s, (sel <= t[:, None])[:, :, None, :])
        return jnp.einsum('bchk,bckd->bchd', p, v_sel)

    out = lax.map(chunk, jnp.arange(S // C))
    return out.transpose(1, 0, 2, 3, 4).reshape(B, S, H * d)


def memory_attention(q, mem_k, mem_v):
    B, S, H, d = q.shape
    s = jnp.einsum('bshd,bmhd->bhsm', q, mem_k) * (d ** -0.5)
    p = jax.nn.softmax(s.astype(jnp.float32), axis=-1)
    return jnp.einsum('bhsm,bmhd->bshd', p, mem_v).reshape(B, S, H * d)


def peer(x, w_query, sub_keys_1, sub_keys_2, expert_u, expert_v):
    B, S, D = x.shape
    T = B * S
    half = PEER_KEY_DIM // 2
    xt = x.reshape(T, D)
    qh = (xt @ w_query).reshape(T, PEER_HEADS, PEER_KEY_DIM)
    s1 = jnp.einsum('thd,nd->thn', qh[..., :half], sub_keys_1)
    s2 = jnp.einsum('thd,nd->thn', qh[..., half:], sub_keys_2)
    v1, i1 = lax.top_k(s1, PEER_TOPK)
    v2, i2 = lax.top_k(s2, PEER_TOPK)
    cand = (v1[..., :, None] + v2[..., None, :]).reshape(T, PEER_HEADS, PEER_TOPK * PEER_TOPK)
    cand_idx = (i1[..., :, None] * PEER_N_KEYS + i2[..., None, :]).reshape(T, PEER_HEADS, PEER_TOPK * PEER_TOPK)
    top_s, top_j = lax.top_k(cand, PEER_TOPK)
    experts = jnp.take_along_axis(cand_idx, top_j, axis=-1).reshape(T, PEER_HEADS * PEER_TOPK)
    gate = jax.nn.softmax(top_s.astype(jnp.float32), axis=-1).reshape(T, PEER_HEADS * PEER_TOPK)
    C = PEER_CHUNK

    def chunk(c):
        t0 = c * C
        xc = lax.dynamic_slice_in_dim(xt, t0, C, axis=0)
        ec = lax.dynamic_slice_in_dim(experts, t0, C, axis=0)
        gc = lax.dynamic_slice_in_dim(gate, t0, C, axis=0)
        u = expert_u[ec]
        a = jax.nn.gelu(jnp.einsum('cd,ced->ce', xc, u).astype(jnp.float32), approximate=False)
        return jnp.einsum('ce,ced->cd', (gc * a).astype(x.dtype), expert_v[ec])

    y = lax.map(chunk, jnp.arange(T // C))
    return y.reshape(B, S, D)


def setup_inputs(seed: int = 0) -> dict:
    key = jax.random.key(seed)
    ks = jax.random.split(key, 24)
    f32 = jnp.float32
    d = HEAD_DIM
    nrm = lambda k, shape, s: jax.random.normal(k, shape, f32) * s
    x = nrm(ks[0], (BATCH, SEQ, D_MODEL), 1.0)
    mem = nrm(ks[1], (BATCH, MEM_LEN, D_MODEL), 1.0)
    start = jax.random.randint(ks[2], (BATCH, 1), 0, 2048)
    positions = (start + jnp.arange(SEQ)[None, :]).astype(jnp.int32)
    half = PEER_KEY_DIM // 2
    return {
        'x': x,
        'mem': mem,
        'positions': positions,
        'w_in': nrm(ks[3], (DEPTH, D_MODEL, IN_WIDTH), D_MODEL ** -0.5),
        'nsa_pe_k': nrm(ks[4], (DEPTH, CMP_LEN, d), 0.1),
        'nsa_pe_v': nrm(ks[5], (DEPTH, CMP_LEN, d), 0.1),
        'nsa_cmp_w1_k': nrm(ks[6], (DEPTH, CMP_LEN * d, CMP_HIDDEN), (CMP_LEN * d) ** -0.5),
        'nsa_cmp_w2_k': nrm(ks[7], (DEPTH, CMP_HIDDEN, d), CMP_HIDDEN ** -0.5),
        'nsa_cmp_w1_v': nrm(ks[8], (DEPTH, CMP_LEN * d, CMP_HIDDEN), (CMP_LEN * d) ** -0.5),
        'nsa_cmp_w2_v': nrm(ks[9], (DEPTH, CMP_HIDDEN, d), CMP_HIDDEN ** -0.5),
        'w_mem_kv': nrm(ks[10], (DEPTH, D_MODEL, 2 * MEM_HEADS * d), D_MODEL ** -0.5),
        'w_out': nrm(ks[11], (DEPTH, D_MODEL, D_MODEL), D_MODEL ** -0.5 * DEEPNORM_BETA),
        'ln1_g': 1.0 + nrm(ks[12], (DEPTH, D_MODEL), 0.02),
        'ln1_b': nrm(ks[13], (DEPTH, D_MODEL), 0.02),
        'peer_w_query': nrm(ks[14], (DEPTH, D_MODEL, PEER_HEADS * PEER_KEY_DIM), D_MODEL ** -0.5),
        'peer_sub_keys_1': nrm(ks[15], (DEPTH, PEER_N_KEYS, half), half ** -0.5),
        'peer_sub_keys_2': nrm(ks[16], (DEPTH, PEER_N_KEYS, half), half ** -0.5),
        'peer_u': nrm(ks[17], (DEPTH, PEER_N_EXPERTS, D_MODEL), D_MODEL ** -0.5),
        'peer_v': nrm(ks[18], (DEPTH, PEER_N_EXPERTS, D_MODEL), DEEPNORM_BETA),
        'ln2_g': 1.0 + nrm(ks[19], (DEPTH, D_MODEL), 0.02),
        'ln2_b': nrm(ks[20], (DEPTH, D_MODEL), 0.02),
    }


def reference(x, mem, positions, w_in, nsa_pe_k, nsa_pe_v, nsa_cmp_w1_k, nsa_cmp_w2_k,
              nsa_cmp_w1_v, nsa_cmp_w2_v, w_mem_kv, w_out, ln1_g, ln1_b, peer_w_query,
              peer_sub_keys_1, peer_sub_keys_2, peer_u, peer_v, ln2_g, ln2_b):
    B, S, _ = x.shape
    M = mem.shape[1]
    d = HEAD_DIM
    G = NSA_KV_HEADS
    split_points = np.cumsum(IN_SPLITS)[:-1].tolist()
    for layer in range(DEPTH):
        h = x @ w_in[layer]
        nsa_q, nsa_kv, nsa_gate, dsa_q, dsa_kv, idx_q, idx_k, idx_w, mem_q = jnp.split(h, split_points, axis=-1)
        q_n = rope_partial(nsa_q.reshape(B, S, NSA_HEADS, d), positions)
        kv_n = nsa_kv.reshape(B, S, 6, G, d)
        kc = compress_blocks(rope_partial(kv_n[:, :, 0], positions), nsa_pe_k[layer],
                             nsa_cmp_w1_k[layer], nsa_cmp_w2_k[layer])
        vc = compress_blocks(kv_n[:, :, 1], nsa_pe_v[layer], nsa_cmp_w1_v[layer], nsa_cmp_w2_v[layer])
        k_slc = rope_partial(kv_n[:, :, 2], positions)
        k_win = rope_partial(kv_n[:, :, 4], positions)
        gates = jax.nn.sigmoid(nsa_gate.reshape(B, S, NSA_HEADS, 3).astype(jnp.float32))
        o_nsa = nsa_attention(q_n, kc, vc, k_slc, kv_n[:, :, 3], k_win, kv_n[:, :, 5], gates)
        q_d = rope_partial(dsa_q.reshape(B, S, DSA_HEADS, d), positions)
        k_d = rope_partial(dsa_kv[..., :d][:, :, None, :], positions)[:, :, 0]
        v_d = dsa_kv[..., d:]
        qi = rope_partial(idx_q.reshape(B, S, IDX_HEADS, IDX_DIM), positions)
        ki = rope_partial(idx_k[:, :, None, :], positions)[:, :, 0]
        o_dsa = dsa_attention(q_d, k_d, v_d, qi, ki, idx_w)
        mkv = (mem @ w_mem_kv[layer]).reshape(B, M, 2, MEM_HEADS, d)
        o_mem = memory_attention(mem_q.reshape(B, S, MEM_HEADS, d), mkv[:, :, 0], mkv[:, :, 1])
        mix = jnp.concatenate([o_nsa, o_dsa, o_mem], axis=-1).astype(x.dtype) @ w_out[layer]
        x = layer_norm(DEEPNORM_ALPHA * x + mix, ln1_g[layer], ln1_b[layer])
        y = peer(x, peer_w_query[layer], peer_sub_keys_1[layer], peer_sub_keys_2[layer],
                 peer_u[layer], peer_v[layer])
        x = layer_norm(DEEPNORM_ALPHA * x + y, ln2_g[layer], ln2_b[layer])
    return x
```

```python
import functools

import numpy as np
import jax
import jax.numpy as jnp
from jax import lax
from jax.experimental import pallas as pl
from jax.experimental.pallas import tpu as pltpu

F32 = jnp.float32
BF16 = jnp.bfloat16

LANES = 128
VMEM_LIMIT = 56 << 20

HEAD_DIM = 64
ROPE_THETA = 500000.0
LN_EPS = 1e-5
NSA_HEADS = 8
NSA_KV_HEADS = 2
CMP_LEN = 32
CMP_STRIDE = 16
CMP_HIDDEN = 128
SEL_BLOCK = 64
SEL_COUNT = 16
WINDOW = 512
FORCE_BONUS = 1e4
DSA_HEADS = 4
IDX_HEADS = 8
IDX_DIM = 32
DSA_TOPK_MAX = 256
MEM_HEADS = 4
PEER_HEADS = 8
PEER_N_KEYS = 128
PEER_KEY_DIM = 256
PEER_TOPK = 16
DEPTH = 1
DEEPNORM_ALPHA = (2.0 * DEPTH) ** 0.25

NEG = -1e30

BLK_QN = 0
BLK_NKV = 4
BLK_DQ = 10
BLK_IQ = 12
BLK_MQ = 14
BLK_DKV = 16
BLK_IK = 17
BLK_MISC = 18
N_BLK = 19
MISC_W = 0
MISC_G = IDX_HEADS


def _cparams(sem):
    return pltpu.CompilerParams(dimension_semantics=sem, vmem_limit_bytes=VMEM_LIMIT)


def _regroup_w_in(w_in):
    d = HEAD_DIM
    o_q = 0
    o_kv = o_q + NSA_HEADS * d
    o_g = o_kv + 6 * NSA_KV_HEADS * d
    o_dq = o_g + 3 * NSA_HEADS
    o_dkv = o_dq + DSA_HEADS * d
    o_iq = o_dkv + 2 * d
    o_ik = o_iq + IDX_HEADS * IDX_DIM
    o_iw = o_ik + IDX_DIM
    o_mq = o_iw + IDX_HEADS
    cols = [w_in[:, o_q:o_kv]]
    for br in range(3):
        for g in range(NSA_KV_HEADS):
            k0 = o_kv + ((2 * br) * NSA_KV_HEADS + g) * d
            v0 = o_kv + ((2 * br + 1) * NSA_KV_HEADS + g) * d
            cols += [w_in[:, k0:k0 + d], w_in[:, v0:v0 + d]]
    cols.append(w_in[:, o_dq:o_dkv])
    cols.append(w_in[:, o_iq:o_ik])
    cols.append(w_in[:, o_mq:o_mq + MEM_HEADS * d])
    cols.append(w_in[:, o_dkv:o_iq])
    cols += [w_in[:, o_ik:o_iw]] * (LANES // IDX_DIM)
    cols += [w_in[:, o_iw:o_mq], w_in[:, o_g:o_dq]]
    pad = LANES - IDX_HEADS - 3 * NSA_HEADS
    cols.append(jnp.zeros((w_in.shape[0], pad), w_in.dtype))
    w = jnp.concatenate(cols, axis=1)
    assert w.shape[1] == N_BLK * LANES
    return w.astype(BF16)


def _rope_tables(positions):
    pos = positions.reshape(-1).astype(F32)
    tabs = []
    for hd in (HEAD_DIM, IDX_DIM):
        r = hd // 4
        half = r // 2
        freqs = jnp.power(ROPE_THETA, -jnp.arange(half, dtype=F32) * 2.0 / r)
        ang = pos[:, None] * freqs
        cos, sin = jnp.cos(ang), jnp.sin(ang)
        t = pos.shape[0]
        c = jnp.concatenate([cos, cos, jnp.ones((t, hd - r), F32)], axis=1)
        sa = jnp.concatenate([-sin, jnp.zeros((t, hd - half), F32)], axis=1)
        sb = jnp.concatenate([jnp.zeros((t, half), F32), sin, jnp.zeros((t, hd - r), F32)], axis=1)
        rep = LANES // hd
        tabs += [jnp.tile(c, (1, rep)), jnp.tile(sa, (1, rep)), jnp.tile(sb, (1, rep))]
    return jnp.concatenate(tabs, axis=1)


def _proj_kernel(x_ref, w_ref, tab_ref, hb_ref, misc_ref):
    xb = x_ref[...].astype(BF16)
    tm = xb.shape[0]
    lane = lax.broadcasted_iota(jnp.int32, (tm, LANES), 1)
    lo = lane < HEAD_DIM
    c64, sa64, sb64 = (tab_ref[:, i * LANES:(i + 1) * LANES] for i in range(3))
    c32, sa32, sb32 = (tab_ref[:, i * LANES:(i + 1) * LANES] for i in range(3, 6))
    c64h, sa64h, sb64h = jnp.where(lo, c64, 1.0), jnp.where(lo, sa64, 0.0), jnp.where(lo, sb64, 0.0)

    def rope(h, c, sa, sb, half):
        return h * c + pltpu.roll(h, LANES - half, 1) * sa + pltpu.roll(h, half, 1) * sb

    def finish(blk, h):
        if blk < BLK_NKV or BLK_DQ <= blk < BLK_IQ:
            return rope(h, c64, sa64, sb64, HEAD_DIM // 8)
        if BLK_NKV <= blk < BLK_DQ or blk == BLK_DKV:
            return rope(h, c64h, sa64h, sb64h, HEAD_DIM // 8)
        if BLK_IQ <= blk < BLK_MQ or blk == BLK_IK:
            return rope(h, c32, sa32, sb32, IDX_DIM // 8)
        return h

    for j in range(BLK_MISC // 2):
        h = jnp.dot(xb, w_ref[:, j * 2 * LANES:(j + 1) * 2 * LANES], preferred_element_type=F32)
        for s in range(2):
            blk = 2 * j + s
            hb_ref[:, blk * LANES:(blk + 1) * LANES] = finish(blk, h[:, s * LANES:(s + 1) * LANES]).astype(BF16)
    misc_ref[...] = jnp.dot(xb, w_ref[:, BLK_MISC * LANES:], preferred_element_type=F32)


def _proj(x2, w2, tabs, tm):
    t, dm = x2.shape
    return pl.pallas_call(
        _proj_kernel,
        grid=(t // tm,),
        in_specs=[
            pl.BlockSpec((tm, dm), lambda i: (i, 0)),
            pl.BlockSpec((dm, N_BLK * LANES), lambda i: (0, 0)),
            pl.BlockSpec((tm, 6 * LANES), lambda i: (i, 0)),
        ],
        out_specs=[
            pl.BlockSpec((tm, BLK_MISC * LANES), lambda i: (i, 0)),
            pl.BlockSpec((tm, LANES), lambda i: (i, 0)),
        ],
        out_shape=[
            jax.ShapeDtypeStruct((t, BLK_MISC * LANES), BF16),
            jax.ShapeDtypeStruct((t, LANES), F32),
        ],
        compiler_params=_cparams(("parallel",)),
        name="proj",
    )(x2, w2, tabs)


def _gelu(x):
    return 0.5 * x * (1.0 + lax.erf(x * np.float32(1.0 / np.sqrt(2.0))))


def _compress_kernel(kch_ref, vch_ref, w1k_ref, w1v_ref, w2k_ref, w2v_ref, pek_ref, pev_ref, out_ref):
    half = w1k_ref.shape[0] // 2
    rows = kch_ref.shape[1]

    def hidden(ch_ref, w1_ref, pe_ref):
        ch = ch_ref[0]
        a = jnp.dot(ch, w1_ref[:half, :], preferred_element_type=F32)
        b = jnp.dot(ch, w1_ref[half:, :], preferred_element_type=F32)
        bias = jnp.dot(pe_ref[...], w1_ref[...], preferred_element_type=F32)[0:1, :]
        return _gelu(a + pltpu.roll(b, rows - 1, 0) + bias).astype(BF16)

    hk = hidden(kch_ref, w1k_ref, pek_ref)
    hv = hidden(vch_ref, w1v_ref, pev_ref)
    out = jnp.dot(hk, w2k_ref[...], preferred_element_type=F32) + jnp.dot(hv, w2v_ref[...], preferred_element_type=F32)
    out_ref[0] = out.astype(BF16)


def _compress(kch, vch, w1k, w1v, w2k, w2v, pek, pev):
    bg, rows, width = kch.shape
    full = lambda a: pl.BlockSpec(a.shape, lambda i: (0,) * a.ndim)
    return pl.pallas_call(
        _compress_kernel,
        grid=(bg,),
        in_specs=[
            pl.BlockSpec((1, rows, width), lambda i: (i, 0, 0)),
            pl.BlockSpec((1, rows, width), lambda i: (i, 0, 0)),
            full(w1k), full(w1v), full(w2k), full(w2v), full(pek), full(pev),
        ],
        out_specs=pl.BlockSpec((1, rows, LANES), lambda i: (i, 0, 0)),
        out_shape=jax.ShapeDtypeStruct((bg, rows, LANES), BF16),
        compiler_params=_cparams(("parallel",)),
        name="compress",
    )(kch, vch, w1k, w1v, w2k, w2v, pek, pev)


def _stack_heads(qpair_refs_or_vals, scale):
    outs = []
    for blk in qpair_refs_or_vals:
        b = blk.astype(F32) * scale
        lane = lax.broadcasted_iota(jnp.int32, b.shape, 1)
        lo = lane < HEAD_DIM
        outs.append(jnp.where(lo, b, 0.0))
        outs.append(jnp.where(lo, pltpu.roll(b, HEAD_DIM, 1), 0.0))
    return jnp.concatenate(outs, axis=0).astype(BF16)


def _qk(q, kv):
    return lax.dot_general(q, kv, (((1,), (1,)), ((), ())), preferred_element_type=F32)


def _flash_init(n_heads, c):
    return (jnp.full((n_heads, c, 1), NEG, F32), jnp.zeros((n_heads, c, 1), F32), jnp.zeros((n_heads, c, LANES), F32))


def _flash_step(s, valid, kv, m, l, acc):
    n_heads, c, tk = s.shape
    s = jnp.where(valid, s, NEG)
    m_new = jnp.maximum(m, jnp.max(s, axis=-1, keepdims=True))
    p = jnp.where(valid, jnp.exp(s - m_new), 0.0)
    corr = jnp.exp(m - m_new)
    l_new = corr * l + jnp.sum(p, axis=-1, keepdims=True)
    pv = jnp.dot(p.reshape(n_heads * c, tk).astype(BF16), kv, preferred_element_type=F32)
    return m_new, l_new, corr * acc + pv.reshape(n_heads, c, LANES)


def _normalize(l, acc):
    return acc / jnp.where(l > 0.0, l, 1.0)


def _pack_heads(o, n_heads, c):
    lane = lax.broadcasted_iota(jnp.int32, (c, LANES), 1)
    lo = lane < HEAD_DIM
    blocks = []
    for p in range(n_heads // 2):
        a = o[(2 * p) * c:(2 * p + 1) * c]
        b = o[(2 * p + 1) * c:(2 * p + 2) * c]
        blocks.append(jnp.where(lo, pltpu.roll(a, HEAD_DIM, 1), b))
    return blocks


def _nsa_kernel(q_ref, misc_ref, kc_ref, ks_ref, kw_ref, ovl_ref, exp_ref, o_ref, *, c, tk, seq):
    g = pl.program_id(1)
    ci = pl.program_id(2)
    t0 = ci * c
    hpg = NSA_HEADS // NSA_KV_HEADS
    rows = hpg * c
    scale = HEAD_DIM ** -0.5
    q4 = _stack_heads([q_ref[:, :LANES], q_ref[:, LANES:]], scale)
    trow = t0 + lax.broadcasted_iota(jnp.int32, (c, 1), 0)

    kc = kc_ref[0]
    ncp = kc.shape[0]
    s = _qk(q4, kc).reshape(hpg, c, ncp)
    ncol = lax.broadcasted_iota(jnp.int32, (c, ncp), 1)
    cvalid = (ncol * CMP_STRIDE + (CMP_LEN - 1)) <= trow
    s = jnp.where(cvalid, s, NEG)
    m = jnp.max(s, axis=-1, keepdims=True)
    e = jnp.where(cvalid, jnp.exp(s - m), 0.0)
    den = jnp.sum(e, axis=-1, keepdims=True)
    p = (e / jnp.where(den > 0.0, den, 1.0)).reshape(rows, ncp)
    pb = p.astype(BF16)
    o_cmp = jnp.dot(pb, kc, preferred_element_type=F32)

    imp_t = lax.dot_general(ovl_ref[...], pb, (((1,), (1,)), ((), ())), preferred_element_type=F32)
    imp = imp_t[:, 0:c]
    for h in range(1, hpg):
        imp = imp + imp_t[:, h * c:(h + 1) * c]
    nb = imp.shape[0]
    jrow = lax.broadcasted_iota(jnp.int32, (nb, c), 0)
    tcol = t0 + lax.broadcasted_iota(jnp.int32, (nb, c), 1)
    cur = tcol // SEL_BLOCK
    forced = (jrow == 0) | (jrow == cur) | (jrow == cur - 1)
    visible = (jrow * SEL_BLOCK) <= tcol
    score = jnp.where(visible, imp + jnp.where(forced, FORCE_BONUS, 0.0), -jnp.inf)
    n_pick = min(SEL_COUNT, seq // SEL_BLOCK)
    sel = jnp.zeros((nb, c), F32)
    for _ in range(n_pick):
        best = jnp.max(score, axis=0, keepdims=True)
        first = jnp.min(jnp.where(score == best, jrow, nb), axis=0, keepdims=True)
        hit = jrow == first
        sel = jnp.where(hit, 1.0, sel)
        score = jnp.where(hit, -jnp.inf, score)
    sel_b = jnp.transpose(sel).astype(BF16)

    n_tiles = (t0 + c + tk - 1) // tk
    kcol = lax.broadcasted_iota(jnp.int32, (c, tk), 1)

    def sel_body(kt, carry):
        k0 = pl.multiple_of(kt * tk, tk)
        kv = ks_ref[pl.ds(k0, tk), :]
        s_ = _qk(q4, kv).reshape(hpg, c, tk)
        tok = jnp.dot(sel_b, exp_ref[:, pl.ds(k0, tk)], preferred_element_type=F32)
        valid = (tok > 0.5) & ((k0 + kcol) <= trow)
        return _flash_step(s_, valid, kv, *carry)

    init = _flash_init(hpg, c)
    _, l_s, acc_s = lax.fori_loop(0, n_tiles, sel_body, init)
    o_sel = _normalize(l_s, acc_s).reshape(rows, LANES)

    wk = WINDOW + c
    w0 = pl.multiple_of(jnp.maximum(t0 - WINDOW, 0), c)
    kvw = kw_ref[pl.ds(w0, wk), :]
    s = _qk(q4, kvw).reshape(hpg, c, wk)
    wpos = w0 + lax.broadcasted_iota(jnp.int32, (c, wk), 1)
    wvalid = (wpos <= trow) & (wpos > trow - WINDOW)
    _, l_w, acc_w = _flash_step(s, wvalid, kvw, *init)
    o_win = _normalize(l_w, acc_w).reshape(rows, LANES)

    gates = jax.nn.sigmoid(misc_ref[...])
    outs = []
    for h in range(hpg):
        col = MISC_G + (g * hpg + h) * 3
        sl = slice(h * c, (h + 1) * c)
        outs.append(_lane_pick(gates, col) * o_cmp[sl] + _lane_pick(gates, col + 1) * o_sel[sl]
                    + _lane_pick(gates, col + 2) * o_win[sl])
    blocks = _pack_heads(jnp.concatenate(outs, axis=0), hpg, c)
    for i, blk in enumerate(blocks):
        o_ref[:, i * LANES:(i + 1) * LANES] = blk.astype(BF16)


def _nsa(hb, misc, kcvc, ovl_t, expand, bsz, seq, c, tk):
    t = hb.shape[0]
    g_n = NSA_KV_HEADS
    nch = seq // c
    ncp = kcvc.shape[1]
    nbp = ovl_t.shape[0]
    kern = functools.partial(_nsa_kernel, c=c, tk=tk, seq=seq)
    return pl.pallas_call(
        kern,
        grid=(bsz, g_n, nch),
        in_specs=[
            pl.BlockSpec((c, 2 * LANES), lambda b, g, i: (b * nch + i, g)),
            pl.BlockSpec((c, LANES), lambda b, g, i: (b * nch + i, 0)),
            pl.BlockSpec((1, ncp, LANES), lambda b, g, i: (b * g_n + g, 0, 0)),
            pl.BlockSpec((seq, LANES), lambda b, g, i: (b, BLK_NKV + 2 + g)),
            pl.BlockSpec((seq, LANES), lambda b, g, i: (b, BLK_NKV + 4 + g)),
            pl.BlockSpec((nbp, ncp), lambda b, g, i: (0, 0)),
            pl.BlockSpec((nbp, seq), lambda b, g, i: (0, 0)),
        ],
        out_specs=pl.BlockSpec((c, 2 * LANES), lambda b, g, i: (b * nch + i, g)),
        out_shape=jax.ShapeDtypeStruct((t, NSA_HEADS * HEAD_DIM), BF16),
        compiler_params=_cparams(("parallel", "parallel", "arbitrary")),
        name="nsa",
    )(hb, misc, kcvc, hb, hb, ovl_t, expand)


def _nsa_consts(seq):
    nb = seq // SEL_BLOCK
    nbp = max(LANES, nb)
    rows = seq // CMP_STRIDE
    nc = (seq - CMP_LEN) // CMP_STRIDE + 1
    j = np.arange(nbp)[:, None]
    n = np.arange(rows)[None, :]
    ovl = np.clip(np.minimum(n * CMP_STRIDE + CMP_LEN, j * SEL_BLOCK + SEL_BLOCK)
                  - np.maximum(n * CMP_STRIDE, j * SEL_BLOCK), 0, None).astype(np.float32) / CMP_LEN
    ovl = np.where((n < nc) & (j < nb), ovl, 0.0)
    s = np.arange(seq)[None, :]
    expand = (s // SEL_BLOCK == j).astype(np.float32)
    return jnp.asarray(ovl, BF16), jnp.asarray(expand, BF16)


INT_MIN = -2147483648


def _lane_pick(x, col):
    lane = lax.broadcasted_iota(jnp.int32, x.shape, 1)
    return jnp.sum(jnp.where(lane == col, x, 0.0), axis=1, keepdims=True)


def _dsa_kernel(q_ref, iq_ref, misc_ref, ik_ref, kv_ref, o_ref, key_scr, *, c, tk, seq, topk):
    ci = pl.program_id(1)
    t0 = ci * c
    n_tiles = (t0 + c + tk - 1) // tk
    trow = t0 + lax.broadcasted_iota(jnp.int32, (c, 1), 0)
    kcol = lax.broadcasted_iota(jnp.int32, (c, tk), 1)
    n_sl = tk // LANES

    lane = lax.broadcasted_iota(jnp.int32, (c, LANES), 1)
    per_blk = LANES // IDX_DIM
    qs = []
    for h in range(IDX_HEADS):
        blk = iq_ref[:, (h // per_blk) * LANES:(h // per_blk + 1) * LANES]
        qs.append(jnp.where(lane // IDX_DIM == h % per_blk, blk, jnp.zeros_like(blk)))
    qst = jnp.concatenate(qs, axis=0)
    w = misc_ref[:, MISC_W:MISC_W + IDX_HEADS] * (IDX_HEADS ** -0.5 * IDX_DIM ** -0.5)
    w3 = jnp.stack([_lane_pick(w, h) for h in range(IDX_HEADS)], axis=0)

    def score_body(kt, carry):
        k0 = pl.multiple_of(kt * tk, tk)
        lg = _qk(qst, ik_ref[pl.ds(k0, tk), :]).reshape(IDX_HEADS, c, tk)
        sc = jnp.sum(jnp.maximum(lg, 0.0) * w3, axis=0) + 0.0
        sc = jnp.where((k0 + kcol) <= trow, sc, -jnp.inf)
        bits = pltpu.bitcast(sc, jnp.int32)
        key_scr[:, pl.ds(k0, tk)] = jnp.where(bits < 0, bits ^ jnp.int32(0x7FFFFFFF), bits)
        return carry

    lax.fori_loop(0, n_tiles, score_body, 0)

    def count(pred):
        def body(kt, acc):
            k0 = pl.multiple_of(kt * tk, tk)
            hit = jnp.where(pred(key_scr[:, pl.ds(k0, tk)], k0 + kcol), 1.0, 0.0)
            for sl in range(n_sl):
                acc = acc + hit[:, sl * LANES:(sl + 1) * LANES]
            return acc
        acc = lax.fori_loop(0, n_tiles, body, jnp.zeros((c, LANES), F32))
        return jnp.sum(acc, axis=1, keepdims=True)

    def thr_body(i, ans):
        cand = ans | (jnp.int32(1) << (31 - i))
        cand_s = cand ^ jnp.int32(INT_MIN)
        cnt = count(lambda keys, idx: keys >= cand_s)
        return jnp.where(cnt >= topk, cand, ans)

    thr = lax.fori_loop(0, 32, thr_body, jnp.zeros((c, 1), jnp.int32)) ^ jnp.int32(INT_MIN)
    need = topk - count(lambda keys, idx: keys > thr)

    n_bits = max(1, int(np.ceil(np.log2(seq))))

    def tie_body(i, ans):
        cand = ans | (jnp.int32(1) << (n_bits - 1 - i))
        cnt = count(lambda keys, idx: (keys == thr) & (idx < cand))
        return jnp.where(cnt < need, cand, ans)

    last_tie = lax.fori_loop(0, n_bits, tie_body, jnp.zeros((c, 1), jnp.int32))

    q4 = _stack_heads([q_ref[:, :LANES], q_ref[:, LANES:]], HEAD_DIM ** -0.5)

    def att_body(kt, carry):
        k0 = pl.multiple_of(kt * tk, tk)
        kv = kv_ref[pl.ds(k0, tk), :]
        keys = key_scr[:, pl.ds(k0, tk)]
        idx = k0 + kcol
        valid = (idx <= trow) & ((keys > thr) | ((keys == thr) & (idx <= last_tie)))
        s = _qk(q4, kv).reshape(DSA_HEADS, c, tk)
        return _flash_step(s, valid, kv, *carry)

    _, l, acc = lax.fori_loop(0, n_tiles, att_body, _flash_init(DSA_HEADS, c))
    o = _normalize(l, acc).reshape(DSA_HEADS * c, LANES)
    for i, blk in enumerate(_pack_heads(o, DSA_HEADS, c)):
        o_ref[:, i * LANES:(i + 1) * LANES] = blk.astype(BF16)


def _dsa(hb, misc, bsz, seq, c, tk):
    t = hb.shape[0]
    nch = seq // c
    topk = min(DSA_TOPK_MAX, seq // 4)
    kern = functools.partial(_dsa_kernel, c=c, tk=tk, seq=seq, topk=topk)
    return pl.pallas_call(
        kern,
        grid=(bsz, nch),
        in_specs=[
            pl.BlockSpec((c, 2 * LANES), lambda b, i: (b * nch + i, BLK_DQ // 2)),
            pl.BlockSpec((c, 2 * LANES), lambda b, i: (b * nch + i, BLK_IQ // 2)),
            pl.BlockSpec((c, LANES), lambda b, i: (b * nch + i, 0)),
            pl.BlockSpec((seq, LANES), lambda b, i: (b, BLK_IK)),
            pl.BlockSpec((seq, LANES), lambda b, i: (b, BLK_DKV)),
        ],
        out_specs=pl.BlockSpec((c, 2 * LANES), lambda b, i: (b * nch + i, 0)),
        out_shape=jax.ShapeDtypeStruct((t, DSA_HEADS * HEAD_DIM), BF16),
        scratch_shapes=[pltpu.VMEM((c, seq), jnp.int32)],
        compiler_params=_cparams(("parallel", "arbitrary")),
        name="dsa",
    )(hb, hb, misc, hb, hb)


def _matmul_kernel(a_ref, w_ref, o_ref):
    o_ref[...] = jnp.dot(a_ref[...].astype(BF16), w_ref[...], preferred_element_type=F32).astype(o_ref.dtype)


def _matmul(a, w, tm, out_dtype):
    m, k = a.shape
    n = w.shape[1]
    return pl.pallas_call(
        _matmul_kernel,
        grid=(m // tm,),
        in_specs=[pl.BlockSpec((tm, k), lambda i: (i, 0)), pl.BlockSpec((k, n), lambda i: (0, 0))],
        out_specs=pl.BlockSpec((tm, n), lambda i: (i, 0)),
        out_shape=jax.ShapeDtypeStruct((m, n), out_dtype),
        compiler_params=_cparams(("parallel",)),
        name="matmul",
    )(a, w)


def _mem_kernel(q_ref, mkv_ref, o_ref, *, c):
    q4 = _stack_heads([q_ref[:, :LANES], q_ref[:, LANES:]], HEAD_DIM ** -0.5)
    outs = []
    for h in range(MEM_HEADS):
        kv = mkv_ref[:, h * LANES:(h + 1) * LANES]
        s = _qk(q4[h * c:(h + 1) * c], kv)
        e = jnp.exp(s - jnp.max(s, axis=-1, keepdims=True))
        p = e / jnp.sum(e, axis=-1, keepdims=True)
        outs.append(jnp.dot(p.astype(BF16), kv, preferred_element_type=F32))
    for i, blk in enumerate(_pack_heads(jnp.concatenate(outs, axis=0), MEM_HEADS, c)):
        o_ref[:, i * LANES:(i + 1) * LANES] = blk.astype(BF16)


def _mem(hb, mkv, bsz, seq, m_len, c):
    t = hb.shape[0]
    nch = seq // c
    return pl.pallas_call(
        functools.partial(_mem_kernel, c=c),
        grid=(bsz, nch),
        in_specs=[
            pl.BlockSpec((c, 2 * LANES), lambda b, i: (b * nch + i, BLK_MQ // 2)),
            pl.BlockSpec((m_len, MEM_HEADS * LANES), lambda b, i: (b, 0)),
        ],
        out_specs=pl.BlockSpec((c, 2 * LANES), lambda b, i: (b * nch + i, 0)),
        out_shape=jax.ShapeDtypeStruct((t, MEM_HEADS * HEAD_DIM), BF16),
        compiler_params=_cparams(("parallel", "parallel")),
        name="mem",
    )(hb, mkv)


def _layer_norm(v, g, b):
    mu = jnp.mean(v, axis=-1, keepdims=True)
    d = v - mu
    var = jnp.mean(d * d, axis=-1, keepdims=True)
    return d * lax.rsqrt(var + LN_EPS) * g + b


def _out_ln_kernel(on_ref, od_ref, om_ref, x_ref, w_ref, g_ref, b_ref, o_ref):
    n0 = on_ref.shape[1]
    n1 = n0 + od_ref.shape[1]
    mix = jnp.dot(on_ref[...], w_ref[:n0, :], preferred_element_type=F32)
    mix = mix + jnp.dot(od_ref[...], w_ref[n0:n1, :], preferred_element_type=F32)
    mix = mix + jnp.dot(om_ref[...], w_ref[n1:, :], preferred_element_type=F32)
    o_ref[...] = _layer_norm(DEEPNORM_ALPHA * x_ref[...] + mix, g_ref[...], b_ref[...])


def _out_ln(o_nsa, o_dsa, o_mem, x2, w_out, g, b, tm):
    t, dm = x2.shape
    row = lambda a: pl.BlockSpec((tm, a.shape[1]), lambda i: (i, 0))
    full = lambda a: pl.BlockSpec(a.shape, lambda i: (0, 0))
    return pl.pallas_call(
        _out_ln_kernel,
        grid=(t // tm,),
        in_specs=[row(o_nsa), row(o_dsa), row(o_mem), row(x2), full(w_out), full(g), full(b)],
        out_specs=pl.BlockSpec((tm, dm), lambda i: (i, 0)),
        out_shape=jax.ShapeDtypeStruct((t, dm), F32),
        compiler_params=_cparams(("parallel",)),
        name="out_ln",
    )(o_nsa, o_dsa, o_mem, x2, w_out, g, b)


def _top_rows(s, k):
    n_rows = s.shape[0]
    rows = lax.broadcasted_iota(jnp.int32, s.shape, 0)
    mask = jnp.zeros(s.shape, F32)
    vals = []
    for _ in range(k):
        best = jnp.max(s, axis=0, keepdims=True)
        first = jnp.min(jnp.where(s == best, rows, n_rows), axis=0, keepdims=True)
        hit = rows == first
        mask = jnp.where(hit, 1.0, mask)
        s = jnp.where(hit, -jnp.inf, s)
        vals.append(best)
    return mask, vals


def _peer_kernel(x1_ref, wqt_ref, k1_ref, k2_ref, u_ref, vt_ref, g_ref, b_ref, o_ref,
                 xt_scr, s1_scr, s2_scr, e1_scr, e2_scr, thr_scr, y_scr, *, tm, eb):
    j = pl.program_id(1)
    nk = PEER_N_KEYS
    half = PEER_KEY_DIM // 2
    n_lt = tm // LANES

    @pl.when(j == 0)
    def _select():
        xt = jnp.transpose(x1_ref[...]).astype(BF16)
        xt_scr[...] = xt
        for h in range(PEER_HEADS):
            qh = jnp.dot(wqt_ref[h * PEER_KEY_DIM:(h + 1) * PEER_KEY_DIM, :], xt, preferred_element_type=F32).astype(BF16)
            s1_scr[h] = jnp.dot(k1_ref[...], qh[:half], preferred_element_type=F32)
            s2_scr[h] = jnp.dot(k2_ref[...], qh[half:], preferred_element_type=F32)

        def chunk(i, carry):
            h = i // n_lt
            l0 = pl.multiple_of((i % n_lt) * LANES, LANES)
            s1 = s1_scr[h, :, pl.ds(l0, LANES)]
            s2 = s2_scr[h, :, pl.ds(l0, LANES)]
            m1, v1 = _top_rows(s1, PEER_TOPK)
            m2, v2 = _top_rows(s2, PEER_TOPK)
            v2a = jnp.concatenate(v2, axis=0)
            v2lo = v2a[:8]
            r8 = lax.broadcasted_iota(jnp.int32, (8, LANES), 0)
            pieces = [v1[0] + v2a, v1[1] + v2lo]
            for a in range(2, 8):
                pieces.append(jnp.where(r8 < PEER_TOPK // (a + 1), v1[a] + v2lo, -jnp.inf))
            pieces.append(jnp.concatenate(v1[8:], axis=0) + v2[0])
            cand = jnp.concatenate(pieces, axis=0)
            _, top = _top_rows(cand, PEER_TOPK)
            den = jnp.zeros_like(top[0])
            for tv in top:
                den = den + jnp.exp(tv - top[0])
            e1_scr[h, :, pl.ds(l0, LANES)] = jnp.where(m1 > 0.5, jnp.exp(s1 - v1[0]), 0.0)
            e2_scr[h, :, pl.ds(l0, LANES)] = jnp.where(m2 > 0.5, jnp.exp(s2 - v2[0]), 0.0) / den
            thr_scr[h, :, pl.ds(l0, LANES)] = jnp.broadcast_to(top[-1], (8, LANES))
            return carry

        lax.fori_loop(0, PEER_HEADS * n_lt, chunk, 0)
        y_scr[...] = jnp.zeros_like(y_scr)

    a = jnp.dot(u_ref[...], xt_scr[...], preferred_element_type=F32)
    i1_0 = pl.multiple_of(j * (eb // nk), eb // nk)
    zs = []
    for ib in range(eb // nk):
        zrow = []
        for lt in range(n_lt):
            ls = slice(lt * LANES, (lt + 1) * LANES)
            w = jnp.zeros((nk, LANES), F32)
            for h in range(PEER_HEADS):
                s1r = s1_scr[h, pl.ds(i1_0, eb // nk), ls][ib:ib + 1]
                e1r = e1_scr[h, pl.ds(i1_0, eb // nk), ls][ib:ib + 1]
                tot = s2_scr[h, :, ls] + s1r
                w = w + jnp.where(tot >= thr_scr[h, 0:1, ls], e2_scr[h, :, ls], 0.0) * e1r
            zrow.append((w * _gelu(a[ib * nk:(ib + 1) * nk, ls])).astype(BF16))
        zs.append(jnp.concatenate(zrow, axis=1))
    z = jnp.concatenate(zs, axis=0)
    y_scr[...] += jnp.dot(vt_ref[...], z, preferred_element_type=F32)

    @pl.when(j == pl.num_programs(1) - 1)
    def _finish():
        y = jnp.transpose(y_scr[...])
        o_ref[...] = _layer_norm(DEEPNORM_ALPHA * x1_ref[...] + y, g_ref[...], b_ref[...])


def _peer(x1, wqt, k1, k2, u, vt, g, b, tm, eb):
    t, dm = x1.shape
    n_e = u.shape[0]
    assert eb // PEER_N_KEYS == 8, "one aligned 8-row group of first-key rows per expert block"
    full = lambda a: pl.BlockSpec(a.shape, lambda i, j: (0, 0))
    tab = pltpu.VMEM((PEER_HEADS, PEER_N_KEYS, tm), F32)
    return pl.pallas_call(
        functools.partial(_peer_kernel, tm=tm, eb=eb),
        grid=(t // tm, n_e // eb),
        in_specs=[
            pl.BlockSpec((tm, dm), lambda i, j: (i, 0)),
            full(wqt), full(k1), full(k2),
            pl.BlockSpec((eb, dm), lambda i, j: (j, 0)),
            pl.BlockSpec((dm, eb), lambda i, j: (0, j)),
            full(g), full(b),
        ],
        out_specs=pl.BlockSpec((tm, dm), lambda i, j: (i, 0)),
        out_shape=jax.ShapeDtypeStruct((t, dm), F32),
        scratch_shapes=[
            pltpu.VMEM((dm, tm), BF16), tab, tab, tab, tab,
            pltpu.VMEM((PEER_HEADS, 8, tm), F32),
            pltpu.VMEM((dm, tm), F32),
        ],
        compiler_params=_cparams(("parallel", "arbitrary")),
        name="peer",
    )(x1, wqt, k1, k2, u, vt, g, b)


def _layer(x, mem, positions, w_in, pe_k, pe_v, w1k, w2k, w1v, w2v, w_mem_kv, w_out, ln1_g, ln1_b,
           w_query, sk1, sk2, pu, pv, ln2_g, ln2_b):
    bsz, seq, dm = x.shape
    m_len = mem.shape[1]
    t = bsz * seq
    d = HEAD_DIM
    g_n = NSA_KV_HEADS
    assert seq % 512 == 0 and seq >= WINDOW + 128
    x2 = x.reshape(t, dm)

    hb, misc = _proj(x2, _regroup_w_in(w_in), _rope_tables(positions), tm=512)

    rows = seq // CMP_STRIDE
    cmp = hb[:, BLK_NKV * LANES:(BLK_NKV + g_n) * LANES].reshape(bsz, seq, g_n, 2, d)
    cmp = cmp.transpose(0, 2, 3, 1, 4).reshape(bsz * g_n, 2, rows, CMP_STRIDE * d)
    zpad = jnp.zeros((CMP_HIDDEN, d), BF16)
    pe_rows = lambda pe: jnp.pad(pe.reshape(1, CMP_LEN * d), ((0, 7), (0, 0))).astype(BF16)
    kcvc = _compress(cmp[:, 0], cmp[:, 1], w1k.astype(BF16), w1v.astype(BF16),
                     jnp.concatenate([w2k.astype(BF16), zpad], axis=1), jnp.concatenate([zpad, w2v.astype(BF16)], axis=1),
                     pe_rows(pe_k), pe_rows(pe_v))

    ovl_t, expand = _nsa_consts(seq)
    o_nsa = _nsa(hb, misc, kcvc, ovl_t, expand, bsz, seq, c=128, tk=512)
    o_dsa = _dsa(hb, misc, bsz, seq, c=128, tk=512)

    wm = w_mem_kv.reshape(dm, 2, MEM_HEADS, d).transpose(0, 2, 1, 3).reshape(dm, MEM_HEADS * 2 * d).astype(BF16)
    mkv = _matmul(mem.reshape(bsz * m_len, dm), wm, tm=m_len, out_dtype=BF16)
    o_mem = _mem(hb, mkv, bsz, seq, m_len, c=256)

    x1 = _out_ln(o_nsa, o_dsa, o_mem, x2, w_out.astype(BF16), ln1_g.reshape(1, dm), ln1_b.reshape(1, dm), tm=512)

    x2o = _peer(x1, w_query.T.astype(BF16), sk1.astype(BF16), sk2.astype(BF16), pu.astype(BF16), pv.T.astype(BF16),
                ln2_g.reshape(1, dm), ln2_b.reshape(1, dm), tm=512, eb=1024)
    return x2o.reshape(bsz, seq, dm)


def kernel(x, mem, positions, w_in, nsa_pe_k, nsa_pe_v, nsa_cmp_w1_k, nsa_cmp_w2_k, nsa_cmp_w1_v, nsa_cmp_w2_v,
           w_mem_kv, w_out, ln1_g, ln1_b, peer_w_query, peer_sub_keys_1, peer_sub_keys_2, peer_u, peer_v, ln2_g, ln2_b):
    assert w_in.shape[0] == DEPTH
    return _layer(x, mem, positions, w_in[0], nsa_pe_k[0], nsa_pe_v[0], nsa_cmp_w1_k[0], nsa_cmp_w2_k[0],
                  nsa_cmp_w1_v[0], nsa_cmp_w2_v[0], w_mem_kv[0], w_out[0], ln1_g[0], ln1_b[0], peer_w_query[0],
                  peer_sub_keys_1[0], peer_sub_keys_2[0], peer_u[0], peer_v[0], ln2_g[0], ln2_b[0])
```

```python
import functools

import numpy as np
import jax
import jax.numpy as jnp
from jax import lax
from jax.experimental import pallas as pl
from jax.experimental.pallas import tpu as pltpu

F32 = jnp.float32
BF16 = jnp.bfloat16

LANES = 128
VMEM_LIMIT = 56 << 20

HEAD_DIM = 64
ROPE_THETA = 500000.0
LN_EPS = 1e-5
NSA_HEADS = 8
NSA_KV_HEADS = 2
CMP_LEN = 32
CMP_STRIDE = 16
CMP_HIDDEN = 128
SEL_BLOCK = 64
SEL_COUNT = 16
WINDOW = 512
FORCE_BONUS = 1e4
DSA_HEADS = 4
IDX_HEADS = 8
IDX_DIM = 32
DSA_TOPK_MAX = 256
MEM_HEADS = 4
PEER_HEADS = 8
PEER_N_KEYS = 128
PEER_KEY_DIM = 256
PEER_TOPK = 16
DEPTH = 1
DEEPNORM_ALPHA = (2.0 * DEPTH) ** 0.25

NEG = -1e30

BLK_QN = 0
BLK_NKV = 4
BLK_DQ = 10
BLK_IQ = 12
BLK_MQ = 14
BLK_DKV = 16
BLK_IK = 17
BLK_MISC = 18
N_BLK = 19
MISC_W = 0
MISC_G = IDX_HEADS


def _cparams(sem):
    return pltpu.CompilerParams(dimension_semantics=sem, vmem_limit_bytes=VMEM_LIMIT)


def _regroup_w_in(w_in):
    d = HEAD_DIM
    o_q = 0
    o_kv = o_q + NSA_HEADS * d
    o_g = o_kv + 6 * NSA_KV_HEADS * d
    o_dq = o_g + 3 * NSA_HEADS
    o_dkv = o_dq + DSA_HEADS * d
    o_iq = o_dkv + 2 * d
    o_ik = o_iq + IDX_HEADS * IDX_DIM
    o_iw = o_ik + IDX_DIM
    o_mq = o_iw + IDX_HEADS
    cols = [w_in[:, o_q:o_kv]]
    for br in range(3):
        for g in range(NSA_KV_HEADS):
            k0 = o_kv + ((2 * br) * NSA_KV_HEADS + g) * d
            v0 = o_kv + ((2 * br + 1) * NSA_KV_HEADS + g) * d
            cols += [w_in[:, k0:k0 + d], w_in[:, v0:v0 + d]]
    cols.append(w_in[:, o_dq:o_dkv])
    cols.append(w_in[:, o_iq:o_ik])
    cols.append(w_in[:, o_mq:o_mq + MEM_HEADS * d])
    cols.append(w_in[:, o_dkv:o_iq])
    cols += [w_in[:, o_ik:o_iw]] * (LANES // IDX_DIM)
    cols += [w_in[:, o_iw:o_mq], w_in[:, o_g:o_dq]]
    pad = LANES - IDX_HEADS - 3 * NSA_HEADS
    cols.append(jnp.zeros((w_in.shape[0], pad), w_in.dtype))
    w = jnp.concatenate(cols, axis=1)
    assert w.shape[1] == N_BLK * LANES
    return w.astype(BF16)


def _rope_tables(positions):
    pos = positions.reshape(-1).astype(F32)
    tabs = []
    for hd in (HEAD_DIM, IDX_DIM):
        r = hd // 4
        half = r // 2
        freqs = jnp.power(ROPE_THETA, -jnp.arange(half, dtype=F32) * 2.0 / r)
        ang = pos[:, None] * freqs
        cos, sin = jnp.cos(ang), jnp.sin(ang)
        t = pos.shape[0]
        c = jnp.concatenate([cos, cos, jnp.ones((t, hd - r), F32)], axis=1)
        sa = jnp.concatenate([-sin, jnp.zeros((t, hd - half), F32)], axis=1)
        sb = jnp.concatenate([jnp.zeros((t, half), F32), sin, jnp.zeros((t, hd - r), F32)], axis=1)
        rep = LANES // hd
        tabs += [jnp.tile(c, (1, rep)), jnp.tile(sa, (1, rep)), jnp.tile(sb, (1, rep))]
    return jnp.concatenate(tabs, axis=1)


def _proj_kernel(x_ref, w_ref, tab_ref, hb_ref, misc_ref):
    xb = x_ref[...].astype(BF16)
    tm = xb.shape[0]
    lane = lax.broadcasted_iota(jnp.int32, (tm, LANES), 1)
    lo = lane < HEAD_DIM
    c64, sa64, sb64 = (tab_ref[:, i * LANES:(i + 1) * LANES] for i in range(3))
    c32, sa32, sb32 = (tab_ref[:, i * LANES:(i + 1) * LANES] for i in range(3, 6))
    c64h, sa64h, sb64h = jnp.where(lo, c64, 1.0), jnp.where(lo, sa64, 0.0), jnp.where(lo, sb64, 0.0)

    def rope(h, c, sa, sb, half):
        return h * c + pltpu.roll(h, LANES - half, 1) * sa + pltpu.roll(h, half, 1) * sb

    def finish(blk, h):
        if blk < BLK_NKV or BLK_DQ <= blk < BLK_IQ:
            return rope(h, c64, sa64, sb64, HEAD_DIM // 8)
        if BLK_NKV <= blk < BLK_DQ or blk == BLK_DKV:
            return rope(h, c64h, sa64h, sb64h, HEAD_DIM // 8)
        if BLK_IQ <= blk < BLK_MQ or blk == BLK_IK:
            return rope(h, c32, sa32, sb32, IDX_DIM // 8)
        return h

    for j in range(BLK_MISC // 2):
        h = jnp.dot(xb, w_ref[:, j * 2 * LANES:(j + 1) * 2 * LANES], preferred_element_type=F32)
        for s in range(2):
            blk = 2 * j + s
            hb_ref[:, blk * LANES:(blk + 1) * LANES] = finish(blk, h[:, s * LANES:(s + 1) * LANES]).astype(BF16)
    misc_ref[...] = jnp.dot(xb, w_ref[:, BLK_MISC * LANES:], preferred_element_type=F32)


def _proj(x2, w2, tabs, tm):
    t, dm = x2.shape
    return pl.pallas_call(
        _proj_kernel,
        grid=(t // tm,),
        in_specs=[
            pl.BlockSpec((tm, dm), lambda i: (i, 0)),
            pl.BlockSpec((dm, N_BLK * LANES), lambda i: (0, 0)),
            pl.BlockSpec((tm, 6 * LANES), lambda i: (i, 0)),
        ],
        out_specs=[
            pl.BlockSpec((tm, BLK_MISC * LANES), lambda i: (i, 0)),
            pl.BlockSpec((tm, LANES), lambda i: (i, 0)),
        ],
        out_shape=[
            jax.ShapeDtypeStruct((t, BLK_MISC * LANES), BF16),
            jax.ShapeDtypeStruct((t, LANES), F32),
        ],
        compiler_params=_cparams(("parallel",)),
        name="proj",
    )(x2, w2, tabs)


def _gelu(x):
    return 0.5 * x * (1.0 + lax.erf(x * np.float32(1.0 / np.sqrt(2.0))))


def _compress_kernel(kch_ref, vch_ref, w1k_ref, w1v_ref, w2k_ref, w2v_ref, pek_ref, pev_ref, out_ref):
    half = w1k_ref.shape[0] // 2
    rows = kch_ref.shape[1]

    def hidden(ch_ref, w1_ref, pe_ref):
        ch = ch_ref[0]
        a = jnp.dot(ch, w1_ref[:half, :], preferred_element_type=F32)
        b = jnp.dot(ch, w1_ref[half:, :], preferred_element_type=F32)
        bias = jnp.dot(pe_ref[...], w1_ref[...], preferred_element_type=F32)[0:1, :]
        return _gelu(a + pltpu.roll(b, rows - 1, 0) + bias).astype(BF16)

    hk = hidden(kch_ref, w1k_ref, pek_ref)
    hv = hidden(vch_ref, w1v_ref, pev_ref)
    out = jnp.dot(hk, w2k_ref[...], preferred_element_type=F32) + jnp.dot(hv, w2v_ref[...], preferred_element_type=F32)
    out_ref[0] = out.astype(BF16)


def _compress(kch, vch, w1k, w1v, w2k, w2v, pek, pev):
    bg, rows, width = kch.shape
    full = lambda a: pl.BlockSpec(a.shape, lambda i: (0,) * a.ndim)
    return pl.pallas_call(
        _compress_kernel,
        grid=(bg,),
        in_specs=[
            pl.BlockSpec((1, rows, width), lambda i: (i, 0, 0)),
            pl.BlockSpec((1, rows, width), lambda i: (i, 0, 0)),
            full(w1k), full(w1v), full(w2k), full(w2v), full(pek), full(pev),
        ],
        out_specs=pl.BlockSpec((1, rows, LANES), lambda i: (i, 0, 0)),
        out_shape=jax.ShapeDtypeStruct((bg, rows, LANES), BF16),
        compiler_params=_cparams(("parallel",)),
        name="compress",
    )(kch, vch, w1k, w1v, w2k, w2v, pek, pev)


def _stack_heads(qpair_refs_or_vals, scale):
    outs = []
    for blk in qpair_refs_or_vals:
        b = blk.astype(F32) * scale
        lane = lax.broadcasted_iota(jnp.int32, b.shape, 1)
        lo = lane < HEAD_DIM
        outs.append(jnp.where(lo, b, 0.0))
        outs.append(jnp.where(lo, pltpu.roll(b, HEAD_DIM, 1), 0.0))
    return jnp.concatenate(outs, axis=0).astype(BF16)


def _qk(q, kv):
    return lax.dot_general(q, kv, (((1,), (1,)), ((), ())), preferred_element_type=F32)


def _flash_init(n_heads, c):
    return (jnp.full((n_heads, c, 1), NEG, F32), jnp.zeros((n_heads, c, 1), F32), jnp.zeros((n_heads, c, LANES), F32))


def _flash_step(s, valid, kv, m, l, acc):
    n_heads, c, tk = s.shape
    s = s + jnp.where(valid, 0.0, NEG)
    m_new = jnp.maximum(m, jnp.max(s, axis=-1, keepdims=True))
    p = jnp.exp(s - m_new)
    corr = jnp.exp(m - m_new)
    l_new = corr * l + jnp.sum(p, axis=-1, keepdims=True)
    pv = jnp.dot(p.reshape(n_heads * c, tk).astype(BF16), kv, preferred_element_type=F32)
    return m_new, l_new, corr * acc + pv.reshape(n_heads, c, LANES)


def _normalize(l, acc):
    return acc / jnp.where(l > 0.0, l, 1.0)


def _pack_heads(o, n_heads, c):
    lane = lax.broadcasted_iota(jnp.int32, (c, LANES), 1)
    lo = lane < HEAD_DIM
    blocks = []
    for p in range(n_heads // 2):
        a = o[(2 * p) * c:(2 * p + 1) * c]
        b = o[(2 * p + 1) * c:(2 * p + 2) * c]
        blocks.append(jnp.where(lo, pltpu.roll(a, HEAD_DIM, 1), b))
    return blocks


def _nsa_kernel(q_ref, misc_ref, kc_ref, ks_ref, kw_ref, ovl_ref, exp_ref, o_ref, *, c, tk, seq):
    g = pl.program_id(1)
    ci = pl.program_id(2)
    t0 = ci * c
    hpg = NSA_HEADS // NSA_KV_HEADS
    rows = hpg * c
    scale = HEAD_DIM ** -0.5
    q4 = _stack_heads([q_ref[:, :LANES], q_ref[:, LANES:]], scale)
    trow = t0 + lax.broadcasted_iota(jnp.int32, (c, 1), 0)

    kc = kc_ref[0]
    ncp = kc.shape[0]
    s = _qk(q4, kc).reshape(hpg, c, ncp)
    ncol = lax.broadcasted_iota(jnp.int32, (c, ncp), 1)
    cvalid = (ncol * CMP_STRIDE + (CMP_LEN - 1)) <= trow
    s = jnp.where(cvalid, s, NEG)
    m = jnp.max(s, axis=-1, keepdims=True)
    e = jnp.where(cvalid, jnp.exp(s - m), 0.0)
    den = jnp.sum(e, axis=-1, keepdims=True)
    p = (e / jnp.where(den > 0.0, den, 1.0)).reshape(rows, ncp)
    pb = p.astype(BF16)
    o_cmp = jnp.dot(pb, kc, preferred_element_type=F32)

    imp_t = lax.dot_general(ovl_ref[...], pb, (((1,), (1,)), ((), ())), preferred_element_type=F32)
    imp = imp_t[:, 0:c]
    for h in range(1, hpg):
        imp = imp + imp_t[:, h * c:(h + 1) * c]
    nb = imp.shape[0]
    jrow = lax.broadcasted_iota(jnp.int32, (nb, c), 0)
    tcol = t0 + lax.broadcasted_iota(jnp.int32, (nb, c), 1)
    cur = tcol // SEL_BLOCK
    forced = (jrow == 0) | (jrow == cur) | (jrow == cur - 1)
    visible = (jrow * SEL_BLOCK) <= tcol
    score = jnp.where(visible, imp + jnp.where(forced, FORCE_BONUS, 0.0), -jnp.inf)
    n_pick = min(SEL_COUNT, seq // SEL_BLOCK)
    sel = jnp.zeros((nb, c), F32)
    for _ in range(n_pick):
        best = jnp.max(score, axis=0, keepdims=True)
        first = jnp.min(jnp.where(score == best, jrow, nb), axis=0, keepdims=True)
        hit = jrow == first
        sel = jnp.where(hit, 1.0, sel)
        score = jnp.where(hit, -jnp.inf, score)
    sel_b = jnp.transpose(sel).astype(BF16)

    n_tiles = (t0 + c + tk - 1) // tk
    kcol = lax.broadcasted_iota(jnp.int32, (c, tk), 1)

    def sel_body(kt, carry):
        k0 = pl.multiple_of(kt * tk, tk)
        kv = ks_ref[pl.ds(k0, tk), :]
        s_ = _qk(q4, kv).reshape(hpg, c, tk)
        tok = jnp.dot(sel_b, exp_ref[:, pl.ds(k0, tk)], preferred_element_type=F32)
        valid = (tok > 0.5) & ((k0 + kcol) <= trow)
        return _flash_step(s_, valid, kv, *carry)

    init = _flash_init(hpg, c)
    _, l_s, acc_s = lax.fori_loop(0, n_tiles, sel_body, init)
    o_sel = _normalize(l_s, acc_s).reshape(rows, LANES)

    wk = WINDOW + c
    w0 = pl.multiple_of(jnp.maximum(t0 - WINDOW, 0), c)
    kvw = kw_ref[pl.ds(w0, wk), :]
    s = _qk(q4, kvw).reshape(hpg, c, wk)
    wpos = w0 + lax.broadcasted_iota(jnp.int32, (c, wk), 1)
    wvalid = (wpos <= trow) & (wpos > trow - WINDOW)
    _, l_w, acc_w = _flash_step(s, wvalid, kvw, *init)
    o_win = _normalize(l_w, acc_w).reshape(rows, LANES)

    gates = jax.nn.sigmoid(misc_ref[...])
    outs = []
    for h in range(hpg):
        col = MISC_G + (g * hpg + h) * 3
        sl = slice(h * c, (h + 1) * c)
        outs.append(_lane_pick(gates, col) * o_cmp[sl] + _lane_pick(gates, col + 1) * o_sel[sl]
                    + _lane_pick(gates, col + 2) * o_win[sl])
    blocks = _pack_heads(jnp.concatenate(outs, axis=0), hpg, c)
    for i, blk in enumerate(blocks):
        o_ref[:, i * LANES:(i + 1) * LANES] = blk.astype(BF16)


def _nsa(hb, misc, kcvc, ovl_t, expand, bsz, seq, c, tk):
    t = hb.shape[0]
    g_n = NSA_KV_HEADS
    nch = seq // c
    ncp = kcvc.shape[1]
    nbp = ovl_t.shape[0]
    kern = functools.partial(_nsa_kernel, c=c, tk=tk, seq=seq)
    return pl.pallas_call(
        kern,
        grid=(bsz, g_n, nch),
        in_specs=[
            pl.BlockSpec((c, 2 * LANES), lambda b, g, i: (b * nch + i, g)),
            pl.BlockSpec((c, LANES), lambda b, g, i: (b * nch + i, 0)),
            pl.BlockSpec((1, ncp, LANES), lambda b, g, i: (b * g_n + g, 0, 0)),
            pl.BlockSpec((seq, LANES), lambda b, g, i: (b, BLK_NKV + 2 + g)),
            pl.BlockSpec((seq, LANES), lambda b, g, i: (b, BLK_NKV + 4 + g)),
            pl.BlockSpec((nbp, ncp), lambda b, g, i: (0, 0)),
            pl.BlockSpec((nbp, seq), lambda b, g, i: (0, 0)),
        ],
        out_specs=pl.BlockSpec((c, 2 * LANES), lambda b, g, i: (b * nch + i, g)),
        out_shape=jax.ShapeDtypeStruct((t, NSA_HEADS * HEAD_DIM), BF16),
        compiler_params=_cparams(("parallel", "parallel", "arbitrary")),
        name="nsa",
    )(hb, misc, kcvc, hb, hb, ovl_t, expand)


def _nsa_consts(seq):
    nb = seq // SEL_BLOCK
    nbp = max(LANES, nb)
    rows = seq // CMP_STRIDE
    nc = (seq - CMP_LEN) // CMP_STRIDE + 1
    j = np.arange(nbp)[:, None]
    n = np.arange(rows)[None, :]
    ovl = np.clip(np.minimum(n * CMP_STRIDE + CMP_LEN, j * SEL_BLOCK + SEL_BLOCK)
                  - np.maximum(n * CMP_STRIDE, j * SEL_BLOCK), 0, None).astype(np.float32) / CMP_LEN
    ovl = np.where((n < nc) & (j < nb), ovl, 0.0)
    s = np.arange(seq)[None, :]
    expand = (s // SEL_BLOCK == j).astype(np.float32)
    return jnp.asarray(ovl, BF16), jnp.asarray(expand, BF16)


INT_MIN = -2147483648


def _lane_pick(x, col):
    lane = lax.broadcasted_iota(jnp.int32, x.shape, 1)
    return jnp.sum(jnp.where(lane == col, x, 0.0), axis=1, keepdims=True)


def _dsa_kernel(q_ref, iq_ref, misc_ref, ik_ref, kv_ref, o_ref, key_scr, *, c, tk, seq, topk):
    ci = pl.program_id(1)
    t0 = ci * c
    n_tiles = (t0 + c + tk - 1) // tk
    trow = t0 + lax.broadcasted_iota(jnp.int32, (c, 1), 0)
    kcol = lax.broadcasted_iota(jnp.int32, (c, tk), 1)
    n_sl = tk // LANES

    lane = lax.broadcasted_iota(jnp.int32, (c, LANES), 1)
    per_blk = LANES // IDX_DIM
    qs = []
    for h in range(IDX_HEADS):
        blk = iq_ref[:, (h // per_blk) * LANES:(h // per_blk + 1) * LANES]
        qs.append(jnp.where(lane // IDX_DIM == h % per_blk, blk, jnp.zeros_like(blk)))
    qst = jnp.concatenate(qs, axis=0)
    w = misc_ref[:, MISC_W:MISC_W + IDX_HEADS] * (IDX_HEADS ** -0.5 * IDX_DIM ** -0.5)
    w3 = jnp.stack([_lane_pick(w, h) for h in range(IDX_HEADS)], axis=0)

    def score_body(kt, carry):
        k0 = pl.multiple_of(kt * tk, tk)
        lg = _qk(qst, ik_ref[pl.ds(k0, tk), :]).reshape(IDX_HEADS, c, tk)
        sc = jnp.sum(jnp.maximum(lg, 0.0) * w3, axis=0) + 0.0
        sc = jnp.where((k0 + kcol) <= trow, sc, -jnp.inf)
        bits = pltpu.bitcast(sc, jnp.int32)
        key_scr[:, pl.ds(k0, tk)] = jnp.where(bits < 0, bits ^ jnp.int32(0x7FFFFFFF), bits)
        return carry

    lax.fori_loop(0, n_tiles, score_body, 0)

    def count(pred):
        def body(kt, acc):
            k0 = pl.multiple_of(kt * tk, tk)
            hit = jnp.where(pred(key_scr[:, pl.ds(k0, tk)], k0 + kcol), 1.0, 0.0)
            for sl in range(n_sl):
                acc = acc + hit[:, sl * LANES:(sl + 1) * LANES]
            return acc
        acc = lax.fori_loop(0, n_tiles, body, jnp.zeros((c, LANES), F32))
        return jnp.sum(acc, axis=1, keepdims=True)

    def thr_body(i, ans):
        cand = ans | (jnp.int32(1) << (31 - i))
        cand_s = cand ^ jnp.int32(INT_MIN)
        cnt = count(lambda keys, idx: keys >= cand_s)
        return jnp.where(cnt >= topk, cand, ans)

    thr = lax.fori_loop(0, 32, thr_body, jnp.zeros((c, 1), jnp.int32)) ^ jnp.int32(INT_MIN)
    need = topk - count(lambda keys, idx: keys > thr)
    n_ties = count(lambda keys, idx: keys == thr)

    n_bits = max(1, int(np.ceil(np.log2(seq))))

    def tie_search():
        def tie_body(i, ans):
            cand = ans | (jnp.int32(1) << (n_bits - 1 - i))
            cnt = count(lambda keys, idx: (keys == thr) & (idx < cand))
            return jnp.where(cnt < need, cand, ans)
        return lax.fori_loop(0, n_bits, tie_body, jnp.zeros((c, 1), jnp.int32))

    surplus = jnp.max(n_ties - need)
    last_tie = lax.cond(surplus > 0.0, tie_search, lambda: jnp.full((c, 1), seq, jnp.int32))

    q4 = _stack_heads([q_ref[:, :LANES], q_ref[:, LANES:]], HEAD_DIM ** -0.5)

    def att_body(kt, carry):
        k0 = pl.multiple_of(kt * tk, tk)
        kv = kv_ref[pl.ds(k0, tk), :]
        keys = key_scr[:, pl.ds(k0, tk)]
        idx = k0 + kcol
        valid = (idx <= trow) & ((keys > thr) | ((keys == thr) & (idx <= last_tie)))
        s = _qk(q4, kv).reshape(DSA_HEADS, c, tk)
        return _flash_step(s, valid, kv, *carry)

    _, l, acc = lax.fori_loop(0, n_tiles, att_body, _flash_init(DSA_HEADS, c))
    o = _normalize(l, acc).reshape(DSA_HEADS * c, LANES)
    for i, blk in enumerate(_pack_heads(o, DSA_HEADS, c)):
        o_ref[:, i * LANES:(i + 1) * LANES] = blk.astype(BF16)


def _dsa(hb, misc, bsz, seq, c, tk):
    t = hb.shape[0]
    nch = seq // c
    topk = min(DSA_TOPK_MAX, seq // 4)
    kern = functools.partial(_dsa_kernel, c=c, tk=tk, seq=seq, topk=topk)
    return pl.pallas_call(
        kern,
        grid=(bsz, nch),
        in_specs=[
            pl.BlockSpec((c, 2 * LANES), lambda b, i: (b * nch + i, BLK_DQ // 2)),
            pl.BlockSpec((c, 2 * LANES), lambda b, i: (b * nch + i, BLK_IQ // 2)),
            pl.BlockSpec((c, LANES), lambda b, i: (b * nch + i, 0)),
            pl.BlockSpec((seq, LANES), lambda b, i: (b, BLK_IK)),
            pl.BlockSpec((seq, LANES), lambda b, i: (b, BLK_DKV)),
        ],
        out_specs=pl.BlockSpec((c, 2 * LANES), lambda b, i: (b * nch + i, 0)),
        out_shape=jax.ShapeDtypeStruct((t, DSA_HEADS * HEAD_DIM), BF16),
        scratch_shapes=[pltpu.VMEM((c, seq), jnp.int32)],
        compiler_params=_cparams(("parallel", "arbitrary")),
        name="dsa",
    )(hb, hb, misc, hb, hb)


def _matmul_kernel(a_ref, w_ref, o_ref):
    o_ref[...] = jnp.dot(a_ref[...].astype(BF16), w_ref[...], preferred_element_type=F32).astype(o_ref.dtype)


def _matmul(a, w, tm, out_dtype):
    m, k = a.shape
    n = w.shape[1]
    return pl.pallas_call(
        _matmul_kernel,
        grid=(m // tm,),
        in_specs=[pl.BlockSpec((tm, k), lambda i: (i, 0)), pl.BlockSpec((k, n), lambda i: (0, 0))],
        out_specs=pl.BlockSpec((tm, n), lambda i: (i, 0)),
        out_shape=jax.ShapeDtypeStruct((m, n), out_dtype),
        compiler_params=_cparams(("parallel",)),
        name="matmul",
    )(a, w)


def _mem_kernel(q_ref, mkv_ref, o_ref, *, c):
    q4 = _stack_heads([q_ref[:, :LANES], q_ref[:, LANES:]], HEAD_DIM ** -0.5)
    outs = []
    for h in range(MEM_HEADS):
        kv = mkv_ref[:, h * LANES:(h + 1) * LANES]
        s = _qk(q4[h * c:(h + 1) * c], kv)
        e = jnp.exp(s - jnp.max(s, axis=-1, keepdims=True))
        p = e / jnp.sum(e, axis=-1, keepdims=True)
        outs.append(jnp.dot(p.astype(BF16), kv, preferred_element_type=F32))
    for i, blk in enumerate(_pack_heads(jnp.concatenate(outs, axis=0), MEM_HEADS, c)):
        o_ref[:, i * LANES:(i + 1) * LANES] = blk.astype(BF16)


def _mem(hb, mkv, bsz, seq, m_len, c):
    t = hb.shape[0]
    nch = seq // c
    return pl.pallas_call(
        functools.partial(_mem_kernel, c=c),
        grid=(bsz, nch),
        in_specs=[
            pl.BlockSpec((c, 2 * LANES), lambda b, i: (b * nch + i, BLK_MQ // 2)),
            pl.BlockSpec((m_len, MEM_HEADS * LANES), lambda b, i: (b, 0)),
        ],
        out_specs=pl.BlockSpec((c, 2 * LANES), lambda b, i: (b * nch + i, 0)),
        out_shape=jax.ShapeDtypeStruct((t, MEM_HEADS * HEAD_DIM), BF16),
        compiler_params=_cparams(("parallel", "parallel")),
        name="mem",
    )(hb, mkv)


def _layer_norm(v, g, b):
    mu = jnp.mean(v, axis=-1, keepdims=True)
    d = v - mu
    var = jnp.mean(d * d, axis=-1, keepdims=True)
    return d * lax.rsqrt(var + LN_EPS) * g + b


def _out_ln_kernel(on_ref, od_ref, om_ref, x_ref, w_ref, g_ref, b_ref, o_ref):
    n0 = on_ref.shape[1]
    n1 = n0 + od_ref.shape[1]
    mix = jnp.dot(on_ref[...], w_ref[:n0, :], preferred_element_type=F32)
    mix = mix + jnp.dot(od_ref[...], w_ref[n0:n1, :], preferred_element_type=F32)
    mix = mix + jnp.dot(om_ref[...], w_ref[n1:, :], preferred_element_type=F32)
    o_ref[...] = _layer_norm(DEEPNORM_ALPHA * x_ref[...] + mix, g_ref[...], b_ref[...])


def _out_ln(o_nsa, o_dsa, o_mem, x2, w_out, g, b, tm):
    t, dm = x2.shape
    row = lambda a: pl.BlockSpec((tm, a.shape[1]), lambda i: (i, 0))
    full = lambda a: pl.BlockSpec(a.shape, lambda i: (0, 0))
    return pl.pallas_call(
        _out_ln_kernel,
        grid=(t // tm,),
        in_specs=[row(o_nsa), row(o_dsa), row(o_mem), row(x2), full(w_out), full(g), full(b)],
        out_specs=pl.BlockSpec((tm, dm), lambda i: (i, 0)),
        out_shape=jax.ShapeDtypeStruct((t, dm), F32),
        compiler_params=_cparams(("parallel",)),
        name="out_ln",
    )(o_nsa, o_dsa, o_mem, x2, w_out, g, b)


NOT_PICKED = 64.0


def _top_rows(s, k, break_ties):
    n_rows = s.shape[0]
    rows = lax.broadcasted_iota(jnp.int32, s.shape, 0)
    rank = jnp.full(s.shape, NOT_PICKED, F32)
    vals = []
    for r in range(k):
        best = jnp.max(s, axis=0, keepdims=True)
        hit = s == best
        if break_ties:
            hit = rows == jnp.min(jnp.where(hit, rows, n_rows), axis=0, keepdims=True)
        rank = jnp.where(hit, float(r), rank)
        s = jnp.where(hit, -jnp.inf, s)
        vals.append(best)
    return rank, vals


def _ranked_exactly(rank, k):
    return jnp.sum(jnp.where(rank < k, 1.0, 0.0), axis=0, keepdims=True) == float(k)


def _peer_select(s1, s2, k, break_ties):
    rank1, v1 = _top_rows(s1, k, break_ties)
    rank2, v2 = _top_rows(s2, k, break_ties)
    v2a = jnp.concatenate(v2, axis=0)
    v2lo = v2a[:8]
    r8 = lax.broadcasted_iota(jnp.int32, v2lo.shape, 0)
    pieces = [v1[0] + v2a, v1[1] + v2lo]
    for a in range(2, 8):
        pieces.append(jnp.where(r8 < k // (a + 1), v1[a] + v2lo, -jnp.inf))
    pieces.append(jnp.concatenate(v1[8:], axis=0) + v2[0])
    cand = jnp.concatenate(pieces, axis=0)
    rank_c, top = _top_rows(cand, k, break_ties)
    return rank1, rank2, rank_c, v1[0], v2[0], jnp.concatenate(top, axis=0)


def _peer_kernel(x1_ref, wqt_ref, k1_ref, k2_ref, u_ref, vt_ref, g_ref, b_ref, o_ref,
                 xt_scr, s1_scr, s2_scr, e1_scr, r2_scr, e2_scr, y_scr, *, tm, eb):
    j = pl.program_id(1)
    nk = PEER_N_KEYS
    half = PEER_KEY_DIM // 2
    n_lt = tm // LANES
    k = PEER_TOPK
    pack = 16

    @pl.when(j == 0)
    def _select():
        xt = jnp.transpose(x1_ref[...]).astype(BF16)
        xt_scr[...] = xt
        for h in range(PEER_HEADS):
            qh = jnp.dot(wqt_ref[h * PEER_KEY_DIM:(h + 1) * PEER_KEY_DIM, :], xt, preferred_element_type=F32).astype(BF16)
            s1_scr[h] = jnp.dot(k1_ref[...], qh[:half], preferred_element_type=F32)
            s2_scr[h] = jnp.dot(k2_ref[...], qh[half:], preferred_element_type=F32)

        def chunk(i, carry):
            h = i // n_lt
            l0 = pl.multiple_of((i % n_lt) * LANES, LANES)
            s1 = s1_scr[h, :, pl.ds(l0, LANES)]
            s2 = s2_scr[h, :, pl.ds(l0, LANES)]
            fast = _peer_select(s1, s2, k, break_ties=False)
            clean = _ranked_exactly(fast[0], k) & _ranked_exactly(fast[1], k) & _ranked_exactly(fast[2], k)
            n_unclean = jnp.sum(jnp.where(clean, 0.0, 1.0))
            rank1, rank2, rank_c, v1_max, v2_max, top = lax.cond(
                n_unclean > 0.0, lambda: _peer_select(s1, s2, k, break_ties=True), lambda: fast)
            den = jnp.sum(jnp.exp(top - top[0:1]), axis=0, keepdims=True)
            picked = jnp.where(rank_c < k, 1.0, 0.0)
            n_of_rank = [jnp.sum(picked[0:16], axis=0, keepdims=True)]
            n_of_rank += [jnp.sum(picked[8 * a + 8:8 * a + 16], axis=0, keepdims=True) for a in range(1, 8)]
            n_of_rank += [picked[72 + a:73 + a] for a in range(8)]
            n1 = jnp.zeros_like(s1)
            for a in range(k):
                n1 = jnp.where(rank1 == float(a), n_of_rank[a], n1)
            s1_scr[h, :, pl.ds(l0, LANES)] = n1
            e1_scr[h, :, pl.ds(l0, LANES)] = jnp.where(rank1 < k, jnp.exp(s1 - v1_max), 0.0)
            r2_scr[h, :, pl.ds(l0, LANES)] = rank2.astype(BF16)
            e2_scr[h, :, pl.ds(l0, LANES)] = (jnp.where(rank2 < k, jnp.exp(s2 - v2_max), 0.0) / den).astype(BF16)
            return carry

        lax.fori_loop(0, PEER_HEADS * n_lt, chunk, 0)
        y_scr[...] = jnp.zeros_like(y_scr)

    a = jnp.dot(u_ref[...], xt_scr[...], preferred_element_type=F32)
    i1_0 = pl.multiple_of(j * (eb // nk), eb // nk)
    zs = []
    for ib in range(eb // nk):
        zrow = []
        for lt in range(n_lt):
            ls = slice(lt * LANES, (lt + 1) * LANES)
            w = jnp.zeros((nk // pack, pack, LANES), BF16)
            for h in range(PEER_HEADS):
                n1r = s1_scr[h, pl.ds(i1_0, eb // nk), ls][ib:ib + 1]
                e1r = e1_scr[h, pl.ds(i1_0, eb // nk), ls][ib:ib + 1]
                n1b = jnp.broadcast_to(n1r, (pack, LANES)).astype(BF16)[None]
                e1b = jnp.broadcast_to(e1r, (pack, LANES)).astype(BF16)[None]
                r2 = r2_scr[h, :, ls].reshape(nk // pack, pack, LANES)
                e2 = e2_scr[h, :, ls].reshape(nk // pack, pack, LANES)
                w = w + jnp.where(r2 < n1b, e2, jnp.zeros_like(e2)) * e1b
            gate = w.reshape(nk, LANES).astype(F32)
            zrow.append((gate * _gelu(a[ib * nk:(ib + 1) * nk, ls])).astype(BF16))
        zs.append(jnp.concatenate(zrow, axis=1))
    z = jnp.concatenate(zs, axis=0)
    y_scr[...] += jnp.dot(vt_ref[...], z, preferred_element_type=F32)

    @pl.when(j == pl.num_programs(1) - 1)
    def _finish():
        y = jnp.transpose(y_scr[...])
        o_ref[...] = _layer_norm(DEEPNORM_ALPHA * x1_ref[...] + y, g_ref[...], b_ref[...])


def _peer(x1, wqt, k1, k2, u, vt, g, b, tm, eb):
    t, dm = x1.shape
    n_e = u.shape[0]
    assert eb // PEER_N_KEYS == 8, "one aligned 8-row group of first-key rows per expert block"
    full = lambda a: pl.BlockSpec(a.shape, lambda i, j: (0, 0))
    tab = pltpu.VMEM((PEER_HEADS, PEER_N_KEYS, tm), F32)
    tab_b = pltpu.VMEM((PEER_HEADS, PEER_N_KEYS, tm), BF16)
    return pl.pallas_call(
        functools.partial(_peer_kernel, tm=tm, eb=eb),
        grid=(t // tm, n_e // eb),
        in_specs=[
            pl.BlockSpec((tm, dm), lambda i, j: (i, 0)),
            full(wqt), full(k1), full(k2),
            pl.BlockSpec((eb, dm), lambda i, j: (j, 0)),
            pl.BlockSpec((dm, eb), lambda i, j: (0, j)),
            full(g), full(b),
        ],
        out_specs=pl.BlockSpec((tm, dm), lambda i, j: (i, 0)),
        out_shape=jax.ShapeDtypeStruct((t, dm), F32),
        scratch_shapes=[
            pltpu.VMEM((dm, tm), BF16), tab, tab, tab, tab_b, tab_b,
            pltpu.VMEM((dm, tm), F32),
        ],
        compiler_params=_cparams(("parallel", "arbitrary")),
        name="peer",
    )(x1, wqt, k1, k2, u, vt, g, b)


def _layer(x, mem, positions, w_in, pe_k, pe_v, w1k, w2k, w1v, w2v, w_mem_kv, w_out, ln1_g, ln1_b,
           w_query, sk1, sk2, pu, pv, ln2_g, ln2_b):
    bsz, seq, dm = x.shape
    m_len = mem.shape[1]
    t = bsz * seq
    d = HEAD_DIM
    g_n = NSA_KV_HEADS
    assert seq % 512 == 0 and seq >= WINDOW + 128
    x2 = x.reshape(t, dm)

    hb, misc = _proj(x2, _regroup_w_in(w_in), _rope_tables(positions), tm=512)

    rows = seq // CMP_STRIDE
    cmp = hb[:, BLK_NKV * LANES:(BLK_NKV + g_n) * LANES].reshape(bsz, seq, g_n, 2, d)
    cmp = cmp.transpose(0, 2, 3, 1, 4).reshape(bsz * g_n, 2, rows, CMP_STRIDE * d)
    zpad = jnp.zeros((CMP_HIDDEN, d), BF16)
    pe_rows = lambda pe: jnp.pad(pe.reshape(1, CMP_LEN * d), ((0, 7), (0, 0))).astype(BF16)
    kcvc = _compress(cmp[:, 0], cmp[:, 1], w1k.astype(BF16), w1v.astype(BF16),
                     jnp.concatenate([w2k.astype(BF16), zpad], axis=1), jnp.concatenate([zpad, w2v.astype(BF16)], axis=1),
                     pe_rows(pe_k), pe_rows(pe_v))

    ovl_t, expand = _nsa_consts(seq)
    o_nsa = _nsa(hb, misc, kcvc, ovl_t, expand, bsz, seq, c=128, tk=512)
    o_dsa = _dsa(hb, misc, bsz, seq, c=128, tk=512)

    wm = w_mem_kv.reshape(dm, 2, MEM_HEADS, d).transpose(0, 2, 1, 3).reshape(dm, MEM_HEADS * 2 * d).astype(BF16)
    mkv = _matmul(mem.reshape(bsz * m_len, dm), wm, tm=m_len, out_dtype=BF16)
    o_mem = _mem(hb, mkv, bsz, seq, m_len, c=256)

    x1 = _out_ln(o_nsa, o_dsa, o_mem, x2, w_out.astype(BF16), ln1_g.reshape(1, dm), ln1_b.reshape(1, dm), tm=512)

    x2o = _peer(x1, w_query.T.astype(BF16), sk1.astype(BF16), sk2.astype(BF16), pu.astype(BF16), pv.T.astype(BF16),
                ln2_g.reshape(1, dm), ln2_b.reshape(1, dm), tm=512, eb=1024)
    return x2o.reshape(bsz, seq, dm)


def kernel(x, mem, positions, w_in, nsa_pe_k, nsa_pe_v, nsa_cmp_w1_k, nsa_cmp_w2_k, nsa_cmp_w1_v, nsa_cmp_w2_v,
           w_mem_kv, w_out, ln1_g, ln1_b, peer_w_query, peer_sub_keys_1, peer_sub_keys_2, peer_u, peer_v, ln2_g, ln2_b):
    assert w_in.shape[0] == DEPTH
    return _layer(x, mem, positions, w_in[0], nsa_pe_k[0], nsa_pe_v[0], nsa_cmp_w1_k[0], nsa_cmp_w2_k[0],
                  nsa_cmp_w1_v[0], nsa_cmp_w2_v[0], w_mem_kv[0], w_out[0], ln1_g[0], ln1_b[0], peer_w_query[0],
                  peer_sub_keys_1[0], peer_sub_keys_2[0], peer_u[0], peer_v[0], ln2_g[0], ln2_b[0])
```

```python
import functools

import numpy as np
import jax
import jax.numpy as jnp
from jax import lax
from jax.experimental import pallas as pl
from jax.experimental.pallas import tpu as pltpu

F32 = jnp.float32
BF16 = jnp.bfloat16

LANES = 128
VMEM_LIMIT = 56 << 20

HEAD_DIM = 64
ROPE_THETA = 500000.0
LN_EPS = 1e-5
NSA_HEADS = 8
NSA_KV_HEADS = 2
CMP_LEN = 32
CMP_STRIDE = 16
CMP_HIDDEN = 128
SEL_BLOCK = 64
SEL_COUNT = 16
WINDOW = 512
FORCE_BONUS = 1e4
DSA_HEADS = 4
IDX_HEADS = 8
IDX_DIM = 32
DSA_TOPK_MAX = 256
MEM_HEADS = 4
PEER_HEADS = 8
PEER_N_KEYS = 128
PEER_KEY_DIM = 256
PEER_TOPK = 16
DEPTH = 1
DEEPNORM_ALPHA = (2.0 * DEPTH) ** 0.25

NEG = -1e30

BLK_QN = 0
BLK_NKV = 4
BLK_DQ = 10
BLK_IQ = 12
BLK_MQ = 14
BLK_DKV = 16
BLK_IK = 17
BLK_MISC = 18
N_BLK = 19
MISC_W = 0
MISC_G = IDX_HEADS


def _cparams(sem):
    return pltpu.CompilerParams(dimension_semantics=sem, vmem_limit_bytes=VMEM_LIMIT)


def _regroup_w_in(w_in):
    d = HEAD_DIM
    o_q = 0
    o_kv = o_q + NSA_HEADS * d
    o_g = o_kv + 6 * NSA_KV_HEADS * d
    o_dq = o_g + 3 * NSA_HEADS
    o_dkv = o_dq + DSA_HEADS * d
    o_iq = o_dkv + 2 * d
    o_ik = o_iq + IDX_HEADS * IDX_DIM
    o_iw = o_ik + IDX_DIM
    o_mq = o_iw + IDX_HEADS
    cols = [w_in[:, o_q:o_kv]]
    for br in range(3):
        for g in range(NSA_KV_HEADS):
            k0 = o_kv + ((2 * br) * NSA_KV_HEADS + g) * d
            v0 = o_kv + ((2 * br + 1) * NSA_KV_HEADS + g) * d
            cols += [w_in[:, k0:k0 + d], w_in[:, v0:v0 + d]]
    cols.append(w_in[:, o_dq:o_dkv])
    cols.append(w_in[:, o_iq:o_ik])
    cols.append(w_in[:, o_mq:o_mq + MEM_HEADS * d])
    cols.append(w_in[:, o_dkv:o_iq])
    cols += [w_in[:, o_ik:o_iw]] * (LANES // IDX_DIM)
    cols += [w_in[:, o_iw:o_mq], w_in[:, o_g:o_dq]]
    pad = LANES - IDX_HEADS - 3 * NSA_HEADS
    cols.append(jnp.zeros((w_in.shape[0], pad), w_in.dtype))
    w = jnp.concatenate(cols, axis=1)
    assert w.shape[1] == N_BLK * LANES
    return w.astype(BF16)


def _rope_tables(positions):
    pos = positions.reshape(-1).astype(F32)
    tabs = []
    for hd in (HEAD_DIM, IDX_DIM):
        r = hd // 4
        half = r // 2
        freqs = jnp.power(ROPE_THETA, -jnp.arange(half, dtype=F32) * 2.0 / r)
        ang = pos[:, None] * freqs
        cos, sin = jnp.cos(ang), jnp.sin(ang)
        t = pos.shape[0]
        c = jnp.concatenate([cos, cos, jnp.ones((t, hd - r), F32)], axis=1)
        sa = jnp.concatenate([-sin, jnp.zeros((t, hd - half), F32)], axis=1)
        sb = jnp.concatenate([jnp.zeros((t, half), F32), sin, jnp.zeros((t, hd - r), F32)], axis=1)
        rep = LANES // hd
        tabs += [jnp.tile(c, (1, rep)), jnp.tile(sa, (1, rep)), jnp.tile(sb, (1, rep))]
    return jnp.concatenate(tabs, axis=1)


def _proj_kernel(x_ref, w_ref, tab_ref, hb_ref, misc_ref):
    xb = x_ref[...].astype(BF16)
    tm = xb.shape[0]
    lane = lax.broadcasted_iota(jnp.int32, (tm, LANES), 1)
    lo = lane < HEAD_DIM
    c64, sa64, sb64 = (tab_ref[:, i * LANES:(i + 1) * LANES] for i in range(3))
    c32, sa32, sb32 = (tab_ref[:, i * LANES:(i + 1) * LANES] for i in range(3, 6))
    c64h, sa64h, sb64h = jnp.where(lo, c64, 1.0), jnp.where(lo, sa64, 0.0), jnp.where(lo, sb64, 0.0)

    def rope(h, c, sa, sb, half):
        return h * c + pltpu.roll(h, LANES - half, 1) * sa + pltpu.roll(h, half, 1) * sb

    def finish(blk, h):
        if blk < BLK_NKV or BLK_DQ <= blk < BLK_IQ:
            return rope(h, c64, sa64, sb64, HEAD_DIM // 8)
        if BLK_NKV <= blk < BLK_DQ or blk == BLK_DKV:
            return rope(h, c64h, sa64h, sb64h, HEAD_DIM // 8)
        if BLK_IQ <= blk < BLK_MQ or blk == BLK_IK:
            return rope(h, c32, sa32, sb32, IDX_DIM // 8)
        return h

    for j in range(BLK_MISC // 2):
        h = jnp.dot(xb, w_ref[:, j * 2 * LANES:(j + 1) * 2 * LANES], preferred_element_type=F32)
        for s in range(2):
            blk = 2 * j + s
            hb_ref[:, blk * LANES:(blk + 1) * LANES] = finish(blk, h[:, s * LANES:(s + 1) * LANES]).astype(BF16)
    misc_ref[...] = jnp.dot(xb, w_ref[:, BLK_MISC * LANES:], preferred_element_type=F32)


def _proj(x2, w2, tabs, tm):
    t, dm = x2.shape
    return pl.pallas_call(
        _proj_kernel,
        grid=(t // tm,),
        in_specs=[
            pl.BlockSpec((tm, dm), lambda i: (i, 0)),
            pl.BlockSpec((dm, N_BLK * LANES), lambda i: (0, 0)),
            pl.BlockSpec((tm, 6 * LANES), lambda i: (i, 0)),
        ],
        out_specs=[
            pl.BlockSpec((tm, BLK_MISC * LANES), lambda i: (i, 0)),
            pl.BlockSpec((tm, LANES), lambda i: (i, 0)),
        ],
        out_shape=[
            jax.ShapeDtypeStruct((t, BLK_MISC * LANES), BF16),
            jax.ShapeDtypeStruct((t, LANES), F32),
        ],
        compiler_params=_cparams(("parallel",)),
        name="proj",
    )(x2, w2, tabs)


def _gelu(x):
    return 0.5 * x * (1.0 + lax.erf(x * np.float32(1.0 / np.sqrt(2.0))))


def _compress_kernel(kch_ref, vch_ref, w1k_ref, w1v_ref, w2k_ref, w2v_ref, pek_ref, pev_ref, out_ref):
    half = w1k_ref.shape[0] // 2
    rows = kch_ref.shape[1]

    def hidden(ch_ref, w1_ref, pe_ref):
        ch = ch_ref[0]
        a = jnp.dot(ch, w1_ref[:half, :], preferred_element_type=F32)
        b = jnp.dot(ch, w1_ref[half:, :], preferred_element_type=F32)
        bias = jnp.dot(pe_ref[...], w1_ref[...], preferred_element_type=F32)[0:1, :]
        return _gelu(a + pltpu.roll(b, rows - 1, 0) + bias).astype(BF16)

    hk = hidden(kch_ref, w1k_ref, pek_ref)
    hv = hidden(vch_ref, w1v_ref, pev_ref)
    out = jnp.dot(hk, w2k_ref[...], preferred_element_type=F32) + jnp.dot(hv, w2v_ref[...], preferred_element_type=F32)
    out_ref[0] = out.astype(BF16)


def _compress(kch, vch, w1k, w1v, w2k, w2v, pek, pev):
    bg, rows, width = kch.shape
    full = lambda a: pl.BlockSpec(a.shape, lambda i: (0,) * a.ndim)
    return pl.pallas_call(
        _compress_kernel,
        grid=(bg,),
        in_specs=[
            pl.BlockSpec((1, rows, width), lambda i: (i, 0, 0)),
            pl.BlockSpec((1, rows, width), lambda i: (i, 0, 0)),
            full(w1k), full(w1v), full(w2k), full(w2v), full(pek), full(pev),
        ],
        out_specs=pl.BlockSpec((1, rows, LANES), lambda i: (i, 0, 0)),
        out_shape=jax.ShapeDtypeStruct((bg, rows, LANES), BF16),
        compiler_params=_cparams(("parallel",)),
        name="compress",
    )(kch, vch, w1k, w1v, w2k, w2v, pek, pev)


def _stack_heads(qpair_refs_or_vals, scale):
    outs = []
    for blk in qpair_refs_or_vals:
        b = blk.astype(F32) * scale
        lane = lax.broadcasted_iota(jnp.int32, b.shape, 1)
        lo = lane < HEAD_DIM
        outs.append(jnp.where(lo, b, 0.0))
        outs.append(jnp.where(lo, pltpu.roll(b, HEAD_DIM, 1), 0.0))
    return jnp.concatenate(outs, axis=0).astype(BF16)


def _qk(q, kv):
    return lax.dot_general(q, kv, (((1,), (1,)), ((), ())), preferred_element_type=F32)


def _flash_init(n_heads, c):
    return jnp.full((n_heads, c, 1), NEG, F32), jnp.zeros((n_heads, c, LANES), F32)


def _flash_step(s, valid, kv, m, acc):
    n_heads, c, tk = s.shape
    sb = s.astype(BF16) + jnp.where(valid, 0.0, NEG).astype(BF16)
    m_new = jnp.maximum(m, jnp.max(sb, axis=-1, keepdims=True).astype(F32))
    p = jnp.exp(sb - m_new.astype(BF16))
    lane = lax.broadcasted_iota(jnp.int32, kv.shape, 1)
    ones_v = jnp.where(lane < HEAD_DIM, jnp.ones_like(kv), kv)
    pv = jnp.dot(p.reshape(n_heads * c, tk), ones_v, preferred_element_type=F32)
    return m_new, jnp.exp(m - m_new) * acc + pv.reshape(n_heads, c, LANES)


def _normalize(acc):
    den = acc[..., 0:1]
    return acc / jnp.where(den > 0.0, den, 1.0)


def _pack_heads(o, n_heads, c):
    lane = lax.broadcasted_iota(jnp.int32, (c, LANES), 1)
    lo = lane < HEAD_DIM
    blocks = []
    for p in range(n_heads // 2):
        a = o[(2 * p) * c:(2 * p + 1) * c]
        b = o[(2 * p + 1) * c:(2 * p + 2) * c]
        blocks.append(jnp.where(lo, pltpu.roll(a, HEAD_DIM, 1), b))
    return blocks


def _nsa_kernel(q_ref, misc_ref, kc_ref, ks_ref, kw_ref, ovl_ref, exp_ref, o_ref, *, c, tk, seq):
    g = pl.program_id(1)
    ci = pl.program_id(2)
    t0 = ci * c
    hpg = NSA_HEADS // NSA_KV_HEADS
    rows = hpg * c
    scale = HEAD_DIM ** -0.5
    q4 = _stack_heads([q_ref[:, :LANES], q_ref[:, LANES:]], scale)
    trow = t0 + lax.broadcasted_iota(jnp.int32, (c, 1), 0)

    kc = kc_ref[0]
    ncp = kc.shape[0]
    s = _qk(q4, kc).reshape(hpg, c, ncp)
    ncol = lax.broadcasted_iota(jnp.int32, (c, ncp), 1)
    cvalid = (ncol * CMP_STRIDE + (CMP_LEN - 1)) <= trow
    s = jnp.where(cvalid, s, NEG)
    m = jnp.max(s, axis=-1, keepdims=True)
    e = jnp.where(cvalid, jnp.exp(s - m), 0.0)
    den = jnp.sum(e, axis=-1, keepdims=True)
    p = (e / jnp.where(den > 0.0, den, 1.0)).reshape(rows, ncp)
    pb = p.astype(BF16)
    o_cmp = jnp.dot(pb, kc, preferred_element_type=F32)

    imp_t = lax.dot_general(ovl_ref[...], pb, (((1,), (1,)), ((), ())), preferred_element_type=F32)
    imp = imp_t[:, 0:c]
    for h in range(1, hpg):
        imp = imp + imp_t[:, h * c:(h + 1) * c]
    nb = imp.shape[0]
    jrow = lax.broadcasted_iota(jnp.int32, (nb, c), 0)
    tcol = t0 + lax.broadcasted_iota(jnp.int32, (nb, c), 1)
    cur = tcol // SEL_BLOCK
    forced = (jrow == 0) | (jrow == cur) | (jrow == cur - 1)
    visible = (jrow * SEL_BLOCK) <= tcol
    score = jnp.where(visible, imp + jnp.where(forced, FORCE_BONUS, 0.0), -jnp.inf)
    n_pick = min(SEL_COUNT, seq // SEL_BLOCK)
    sel = jnp.zeros((nb, c), F32)
    for _ in range(n_pick):
        best = jnp.max(score, axis=0, keepdims=True)
        first = jnp.min(jnp.where(score == best, jrow, nb), axis=0, keepdims=True)
        hit = jrow == first
        sel = jnp.where(hit, 1.0, sel)
        score = jnp.where(hit, -jnp.inf, score)
    sel_b = jnp.transpose(sel).astype(BF16)

    n_tiles = (t0 + c + tk - 1) // tk
    kcol = lax.broadcasted_iota(jnp.int32, (c, tk), 1)

    def sel_body(kt, carry):
        k0 = pl.multiple_of(kt * tk, tk)
        kv = ks_ref[pl.ds(k0, tk), :]
        s_ = _qk(q4, kv).reshape(hpg, c, tk)
        tok = jnp.dot(sel_b, exp_ref[:, pl.ds(k0, tk)], preferred_element_type=F32)
        valid = (tok > 0.5) & ((k0 + kcol) <= trow)
        return _flash_step(s_, valid, kv, *carry)

    init = _flash_init(hpg, c)
    _, acc_s = lax.fori_loop(0, n_tiles, sel_body, init)
    o_sel = _normalize(acc_s).reshape(rows, LANES)

    wk = WINDOW + c
    w0 = pl.multiple_of(jnp.maximum(t0 - WINDOW, 0), c)
    kvw = kw_ref[pl.ds(w0, wk), :]
    s = _qk(q4, kvw).reshape(hpg, c, wk)
    wpos = w0 + lax.broadcasted_iota(jnp.int32, (c, wk), 1)
    wvalid = (wpos <= trow) & (wpos > trow - WINDOW)
    _, acc_w = _flash_step(s, wvalid, kvw, *init)
    o_win = _normalize(acc_w).reshape(rows, LANES)

    gates = jax.nn.sigmoid(misc_ref[...])
    outs = []
    for h in range(hpg):
        col = MISC_G + (g * hpg + h) * 3
        sl = slice(h * c, (h + 1) * c)
        outs.append(_lane_pick(gates, col) * o_cmp[sl] + _lane_pick(gates, col + 1) * o_sel[sl]
                    + _lane_pick(gates, col + 2) * o_win[sl])
    blocks = _pack_heads(jnp.concatenate(outs, axis=0), hpg, c)
    for i, blk in enumerate(blocks):
        o_ref[:, i * LANES:(i + 1) * LANES] = blk.astype(BF16)


def _nsa(hb, misc, kcvc, ovl_t, expand, bsz, seq, c, tk):
    t = hb.shape[0]
    g_n = NSA_KV_HEADS
    nch = seq // c
    ncp = kcvc.shape[1]
    nbp = ovl_t.shape[0]
    kern = functools.partial(_nsa_kernel, c=c, tk=tk, seq=seq)
    return pl.pallas_call(
        kern,
        grid=(bsz, g_n, nch),
        in_specs=[
            pl.BlockSpec((c, 2 * LANES), lambda b, g, i: (b * nch + i, g)),
            pl.BlockSpec((c, LANES), lambda b, g, i: (b * nch + i, 0)),
            pl.BlockSpec((1, ncp, LANES), lambda b, g, i: (b * g_n + g, 0, 0)),
            pl.BlockSpec((seq, LANES), lambda b, g, i: (b, BLK_NKV + 2 + g)),
            pl.BlockSpec((seq, LANES), lambda b, g, i: (b, BLK_NKV + 4 + g)),
            pl.BlockSpec((nbp, ncp), lambda b, g, i: (0, 0)),
            pl.BlockSpec((nbp, seq), lambda b, g, i: (0, 0)),
        ],
        out_specs=pl.BlockSpec((c, 2 * LANES), lambda b, g, i: (b * nch + i, g)),
        out_shape=jax.ShapeDtypeStruct((t, NSA_HEADS * HEAD_DIM), BF16),
        compiler_params=_cparams(("parallel", "parallel", "arbitrary")),
        name="nsa",
    )(hb, misc, kcvc, hb, hb, ovl_t, expand)


def _nsa_consts(seq):
    nb = seq // SEL_BLOCK
    nbp = max(LANES, nb)
    rows = seq // CMP_STRIDE
    nc = (seq - CMP_LEN) // CMP_STRIDE + 1
    j = np.arange(nbp)[:, None]
    n = np.arange(rows)[None, :]
    ovl = np.clip(np.minimum(n * CMP_STRIDE + CMP_LEN, j * SEL_BLOCK + SEL_BLOCK)
                  - np.maximum(n * CMP_STRIDE, j * SEL_BLOCK), 0, None).astype(np.float32) / CMP_LEN
    ovl = np.where((n < nc) & (j < nb), ovl, 0.0)
    s = np.arange(seq)[None, :]
    expand = (s // SEL_BLOCK == j).astype(np.float32)
    return jnp.asarray(ovl, BF16), jnp.asarray(expand, BF16)


INT_MIN = -2147483648


def _lane_pick(x, col):
    lane = lax.broadcasted_iota(jnp.int32, x.shape, 1)
    return jnp.sum(jnp.where(lane == col, x, 0.0), axis=1, keepdims=True)


def _dsa_kernel(q_ref, iq_ref, misc_ref, ik_ref, kv_ref, o_ref, key_scr, *, c, tk, seq, topk):
    ci = pl.program_id(1)
    t0 = ci * c
    n_tiles = (t0 + c + tk - 1) // tk
    trow = t0 + lax.broadcasted_iota(jnp.int32, (c, 1), 0)
    kcol = lax.broadcasted_iota(jnp.int32, (c, tk), 1)
    n_sl = tk // LANES

    lane = lax.broadcasted_iota(jnp.int32, (c, LANES), 1)
    per_blk = LANES // IDX_DIM
    qs = []
    for h in range(IDX_HEADS):
        blk = iq_ref[:, (h // per_blk) * LANES:(h // per_blk + 1) * LANES]
        qs.append(jnp.where(lane // IDX_DIM == h % per_blk, blk, jnp.zeros_like(blk)))
    qst = jnp.concatenate(qs, axis=0)
    w = misc_ref[:, MISC_W:MISC_W + IDX_HEADS] * (IDX_HEADS ** -0.5 * IDX_DIM ** -0.5)
    w3 = jnp.stack([_lane_pick(w, h) for h in range(IDX_HEADS)], axis=0)

    def score_body(kt, carry):
        k0 = pl.multiple_of(kt * tk, tk)
        lg = _qk(qst, ik_ref[pl.ds(k0, tk), :]).reshape(IDX_HEADS, c, tk)
        sc = jnp.sum(jnp.maximum(lg, 0.0) * w3, axis=0) + 0.0
        sc = jnp.where((k0 + kcol) <= trow, sc, -jnp.inf)
        bits = pltpu.bitcast(sc, jnp.int32)
        key_scr[:, pl.ds(k0, tk)] = jnp.where(bits < 0, bits ^ jnp.int32(0x7FFFFFFF), bits)
        return carry

    lax.fori_loop(0, n_tiles, score_body, 0)

    def count(pred):
        def body(kt, acc):
            k0 = pl.multiple_of(kt * tk, tk)
            hit = jnp.where(pred(key_scr[:, pl.ds(k0, tk)], k0 + kcol), 1.0, 0.0)
            for sl in range(n_sl):
                acc = acc + hit[:, sl * LANES:(sl + 1) * LANES]
            return acc
        acc = lax.fori_loop(0, n_tiles, body, jnp.zeros((c, LANES), F32))
        return jnp.sum(acc, axis=1, keepdims=True)

    def thr_body(i, ans):
        cand = ans | (jnp.int32(1) << (31 - i))
        cand_s = cand ^ jnp.int32(INT_MIN)
        cnt = count(lambda keys, idx: keys >= cand_s)
        return jnp.where(cnt >= topk, cand, ans)

    thr = lax.fori_loop(0, 32, thr_body, jnp.zeros((c, 1), jnp.int32)) ^ jnp.int32(INT_MIN)
    need = topk - count(lambda keys, idx: keys > thr)
    n_ties = count(lambda keys, idx: keys == thr)

    n_bits = max(1, int(np.ceil(np.log2(seq))))

    def tie_search():
        def tie_body(i, ans):
            cand = ans | (jnp.int32(1) << (n_bits - 1 - i))
            cnt = count(lambda keys, idx: (keys == thr) & (idx < cand))
            return jnp.where(cnt < need, cand, ans)
        return lax.fori_loop(0, n_bits, tie_body, jnp.zeros((c, 1), jnp.int32))

    surplus = jnp.max(n_ties - need)
    last_tie = lax.cond(surplus > 0.0, tie_search, lambda: jnp.full((c, 1), seq, jnp.int32))

    q4 = _stack_heads([q_ref[:, :LANES], q_ref[:, LANES:]], HEAD_DIM ** -0.5)

    def att_body(kt, carry):
        k0 = pl.multiple_of(kt * tk, tk)
        kv = kv_ref[pl.ds(k0, tk), :]
        keys = key_scr[:, pl.ds(k0, tk)]
        idx = k0 + kcol
        valid = (idx <= trow) & ((keys > thr) | ((keys == thr) & (idx <= last_tie)))
        s = _qk(q4, kv).reshape(DSA_HEADS, c, tk)
        return _flash_step(s, valid, kv, *carry)

    _, acc = lax.fori_loop(0, n_tiles, att_body, _flash_init(DSA_HEADS, c))
    o = _normalize(acc).reshape(DSA_HEADS * c, LANES)
    for i, blk in enumerate(_pack_heads(o, DSA_HEADS, c)):
        o_ref[:, i * LANES:(i + 1) * LANES] = blk.astype(BF16)


def _dsa(hb, misc, bsz, seq, c, tk):
    t = hb.shape[0]
    nch = seq // c
    topk = min(DSA_TOPK_MAX, seq // 4)
    kern = functools.partial(_dsa_kernel, c=c, tk=tk, seq=seq, topk=topk)
    return pl.pallas_call(
        kern,
        grid=(bsz, nch),
        in_specs=[
            pl.BlockSpec((c, 2 * LANES), lambda b, i: (b * nch + i, BLK_DQ // 2)),
            pl.BlockSpec((c, 2 * LANES), lambda b, i: (b * nch + i, BLK_IQ // 2)),
            pl.BlockSpec((c, LANES), lambda b, i: (b * nch + i, 0)),
            pl.BlockSpec((seq, LANES), lambda b, i: (b, BLK_IK)),
            pl.BlockSpec((seq, LANES), lambda b, i: (b, BLK_DKV)),
        ],
        out_specs=pl.BlockSpec((c, 2 * LANES), lambda b, i: (b * nch + i, 0)),
        out_shape=jax.ShapeDtypeStruct((t, DSA_HEADS * HEAD_DIM), BF16),
        scratch_shapes=[pltpu.VMEM((c, seq), jnp.int32)],
        compiler_params=_cparams(("parallel", "arbitrary")),
        name="dsa",
    )(hb, hb, misc, hb, hb)


def _matmul_kernel(a_ref, w_ref, o_ref):
    o_ref[...] = jnp.dot(a_ref[...].astype(BF16), w_ref[...], preferred_element_type=F32).astype(o_ref.dtype)


def _matmul(a, w, tm, out_dtype):
    m, k = a.shape
    n = w.shape[1]
    return pl.pallas_call(
        _matmul_kernel,
        grid=(m // tm,),
        in_specs=[pl.BlockSpec((tm, k), lambda i: (i, 0)), pl.BlockSpec((k, n), lambda i: (0, 0))],
        out_specs=pl.BlockSpec((tm, n), lambda i: (i, 0)),
        out_shape=jax.ShapeDtypeStruct((m, n), out_dtype),
        compiler_params=_cparams(("parallel",)),
        name="matmul",
    )(a, w)


def _mem_kernel(q_ref, mkv_ref, o_ref, *, c):
    q4 = _stack_heads([q_ref[:, :LANES], q_ref[:, LANES:]], HEAD_DIM ** -0.5)
    outs = []
    for h in range(MEM_HEADS):
        kv = mkv_ref[:, h * LANES:(h + 1) * LANES]
        s = _qk(q4[h * c:(h + 1) * c], kv)
        e = jnp.exp(s - jnp.max(s, axis=-1, keepdims=True))
        p = e / jnp.sum(e, axis=-1, keepdims=True)
        outs.append(jnp.dot(p.astype(BF16), kv, preferred_element_type=F32))
    for i, blk in enumerate(_pack_heads(jnp.concatenate(outs, axis=0), MEM_HEADS, c)):
        o_ref[:, i * LANES:(i + 1) * LANES] = blk.astype(BF16)


def _mem(hb, mkv, bsz, seq, m_len, c):
    t = hb.shape[0]
    nch = seq // c
    return pl.pallas_call(
        functools.partial(_mem_kernel, c=c),
        grid=(bsz, nch),
        in_specs=[
            pl.BlockSpec((c, 2 * LANES), lambda b, i: (b * nch + i, BLK_MQ // 2)),
            pl.BlockSpec((m_len, MEM_HEADS * LANES), lambda b, i: (b, 0)),
        ],
        out_specs=pl.BlockSpec((c, 2 * LANES), lambda b, i: (b * nch + i, 0)),
        out_shape=jax.ShapeDtypeStruct((t, MEM_HEADS * HEAD_DIM), BF16),
        compiler_params=_cparams(("parallel", "parallel")),
        name="mem",
    )(hb, mkv)


def _layer_norm(v, g, b):
    mu = jnp.mean(v, axis=-1, keepdims=True)
    d = v - mu
    var = jnp.mean(d * d, axis=-1, keepdims=True)
    return d * lax.rsqrt(var + LN_EPS) * g + b


def _out_ln_kernel(on_ref, od_ref, om_ref, x_ref, w_ref, g_ref, b_ref, o_ref):
    n0 = on_ref.shape[1]
    n1 = n0 + od_ref.shape[1]
    mix = jnp.dot(on_ref[...], w_ref[:n0, :], preferred_element_type=F32)
    mix = mix + jnp.dot(od_ref[...], w_ref[n0:n1, :], preferred_element_type=F32)
    mix = mix + jnp.dot(om_ref[...], w_ref[n1:, :], preferred_element_type=F32)
    o_ref[...] = _layer_norm(DEEPNORM_ALPHA * x_ref[...] + mix, g_ref[...], b_ref[...])


def _out_ln(o_nsa, o_dsa, o_mem, x2, w_out, g, b, tm):
    t, dm = x2.shape
    row = lambda a: pl.BlockSpec((tm, a.shape[1]), lambda i: (i, 0))
    full = lambda a: pl.BlockSpec(a.shape, lambda i: (0, 0))
    return pl.pallas_call(
        _out_ln_kernel,
        grid=(t // tm,),
        in_specs=[row(o_nsa), row(o_dsa), row(o_mem), row(x2), full(w_out), full(g), full(b)],
        out_specs=pl.BlockSpec((tm, dm), lambda i: (i, 0)),
        out_shape=jax.ShapeDtypeStruct((t, dm), F32),
        compiler_params=_cparams(("parallel",)),
        name="out_ln",
    )(o_nsa, o_dsa, o_mem, x2, w_out, g, b)


NOT_PICKED = 64.0


def _top_rows(s, k, break_ties):
    n_rows = s.shape[0]
    rows = lax.broadcasted_iota(jnp.int32, s.shape, 0)
    rank = jnp.full(s.shape, NOT_PICKED, F32)
    vals = []
    for r in range(k):
        best = jnp.max(s, axis=0, keepdims=True)
        hit = s == best
        if break_ties:
            hit = rows == jnp.min(jnp.where(hit, rows, n_rows), axis=0, keepdims=True)
        rank = jnp.where(hit, float(r), rank)
        s = jnp.where(hit, -jnp.inf, s)
        vals.append(best)
    return rank, vals


def _ranked_exactly(rank, k):
    return jnp.sum(jnp.where(rank < k, 1.0, 0.0), axis=0, keepdims=True) == float(k)


def _peer_select(s1, s2, k, break_ties):
    rank1, v1 = _top_rows(s1, k, break_ties)
    rank2, v2 = _top_rows(s2, k, break_ties)
    v2a = jnp.concatenate(v2, axis=0)
    v2lo = v2a[:8]
    r8 = lax.broadcasted_iota(jnp.int32, v2lo.shape, 0)
    pieces = [v1[0] + v2a, v1[1] + v2lo]
    for a in range(2, 8):
        pieces.append(jnp.where(r8 < k // (a + 1), v1[a] + v2lo, -jnp.inf))
    pieces.append(jnp.concatenate(v1[8:], axis=0) + v2[0])
    cand = jnp.concatenate(pieces, axis=0)
    rank_c, top = _top_rows(cand, k, break_ties)
    return rank1, rank2, rank_c, v1[0], v2[0], jnp.concatenate(top, axis=0)


def _peer_kernel(x1_ref, wqt_ref, k1_ref, k2_ref, u_ref, vt_ref, g_ref, b_ref, o_ref,
                 xt_scr, s1_scr, s2_scr, e1_scr, r2_scr, e2_scr, y_scr, *, tm, eb, nsub):
    j = pl.program_id(1)
    nk = PEER_N_KEYS
    half = PEER_KEY_DIM // 2
    n_lt = tm // LANES
    k = PEER_TOPK
    pack = 16

    @pl.when(j == 0)
    def _select():
        xt = jnp.transpose(x1_ref[...]).astype(BF16)
        xt_scr[...] = xt
        for h in range(PEER_HEADS):
            qh = jnp.dot(wqt_ref[h * PEER_KEY_DIM:(h + 1) * PEER_KEY_DIM, :], xt, preferred_element_type=F32).astype(BF16)
            s1_scr[h] = jnp.dot(k1_ref[...], qh[:half], preferred_element_type=F32)
            s2_scr[h] = jnp.dot(k2_ref[...], qh[half:], preferred_element_type=F32)

        def chunk(i, carry):
            h = i // n_lt
            l0 = pl.multiple_of((i % n_lt) * LANES, LANES)
            s1 = s1_scr[h, :, pl.ds(l0, LANES)]
            s2 = s2_scr[h, :, pl.ds(l0, LANES)]
            fast = _peer_select(s1, s2, k, break_ties=False)
            clean = _ranked_exactly(fast[0], k) & _ranked_exactly(fast[1], k) & _ranked_exactly(fast[2], k)
            n_unclean = jnp.sum(jnp.where(clean, 0.0, 1.0))
            rank1, rank2, rank_c, v1_max, v2_max, top = lax.cond(
                n_unclean > 0.0, lambda: _peer_select(s1, s2, k, break_ties=True), lambda: fast)
            den = jnp.sum(jnp.exp(top - top[0:1]), axis=0, keepdims=True)
            picked = jnp.where(rank_c < k, 1.0, 0.0)
            n_of_rank = [jnp.sum(picked[0:16], axis=0, keepdims=True)]
            n_of_rank += [jnp.sum(picked[8 * a + 8:8 * a + 16], axis=0, keepdims=True) for a in range(1, 8)]
            n_of_rank += [picked[72 + a:73 + a] for a in range(8)]
            n1 = jnp.zeros_like(s1)
            for a in range(k):
                n1 = jnp.where(rank1 == float(a), n_of_rank[a], n1)
            s1_scr[h, :, pl.ds(l0, LANES)] = n1
            e1_scr[h, :, pl.ds(l0, LANES)] = jnp.where(rank1 < k, jnp.exp(s1 - v1_max), 0.0)
            r2_scr[h, :, pl.ds(l0, LANES)] = rank2.astype(BF16)
            e2_scr[h, :, pl.ds(l0, LANES)] = (jnp.where(rank2 < k, jnp.exp(s2 - v2_max), 0.0) / den).astype(BF16)
            return carry

        lax.fori_loop(0, PEER_HEADS * n_lt, chunk, 0)
        y_scr[...] = jnp.zeros_like(y_scr)

    xt = xt_scr[...]
    y_add = None
    for sub in range(nsub):
        i1_0 = pl.multiple_of((j * nsub + sub) * (eb // nk), eb // nk)
        a = jnp.dot(u_ref[sub * eb:(sub + 1) * eb, :], xt, preferred_element_type=F32)
        zs = []
        for ib in range(eb // nk):
            zrow = []
            for lt in range(n_lt):
                ls = slice(lt * LANES, (lt + 1) * LANES)
                w = jnp.zeros((nk // pack, pack, LANES), BF16)
                for h in range(PEER_HEADS):
                    n1r = s1_scr[h, pl.ds(i1_0, eb // nk), ls][ib:ib + 1]
                    e1r = e1_scr[h, pl.ds(i1_0, eb // nk), ls][ib:ib + 1]
                    n1b = jnp.broadcast_to(n1r, (pack, LANES)).astype(BF16)[None]
                    e1b = jnp.broadcast_to(e1r, (pack, LANES)).astype(BF16)[None]
                    r2 = r2_scr[h, :, ls].reshape(nk // pack, pack, LANES)
                    e2 = e2_scr[h, :, ls].reshape(nk // pack, pack, LANES)
                    w = w + jnp.where(r2 < n1b, e2, jnp.zeros_like(e2)) * e1b
                gate = w.reshape(nk, LANES).astype(F32)
                zrow.append((gate * _gelu(a[ib * nk:(ib + 1) * nk, ls])).astype(BF16))
            zs.append(jnp.concatenate(zrow, axis=1))
        z = jnp.concatenate(zs, axis=0)
        y_sub = jnp.dot(vt_ref[:, sub * eb:(sub + 1) * eb], z, preferred_element_type=F32)
        y_add = y_sub if y_add is None else y_add + y_sub
    y_scr[...] += y_add

    @pl.when(j == pl.num_programs(1) - 1)
    def _finish():
        y = jnp.transpose(y_scr[...])
        o_ref[...] = _layer_norm(DEEPNORM_ALPHA * x1_ref[...] + y, g_ref[...], b_ref[...])


def _peer(x1, wqt, k1, k2, u, vt, g, b, tm, eb, nsub):
    t, dm = x1.shape
    n_e = u.shape[0]
    assert eb // PEER_N_KEYS == 8, "one aligned 8-row group of first-key rows per expert block"
    full = lambda a: pl.BlockSpec(a.shape, lambda i, j: (0, 0))
    tab = pltpu.VMEM((PEER_HEADS, PEER_N_KEYS, tm), F32)
    tab_b = pltpu.VMEM((PEER_HEADS, PEER_N_KEYS, tm), BF16)
    return pl.pallas_call(
        functools.partial(_peer_kernel, tm=tm, eb=eb, nsub=nsub),
        grid=(t // tm, n_e // (nsub * eb)),
        in_specs=[
            pl.BlockSpec((tm, dm), lambda i, j: (i, 0)),
            full(wqt), full(k1), full(k2),
            pl.BlockSpec((nsub * eb, dm), lambda i, j: (j, 0)),
            pl.BlockSpec((dm, nsub * eb), lambda i, j: (0, j)),
            full(g), full(b),
        ],
        out_specs=pl.BlockSpec((tm, dm), lambda i, j: (i, 0)),
        out_shape=jax.ShapeDtypeStruct((t, dm), F32),
        scratch_shapes=[
            pltpu.VMEM((dm, tm), BF16), tab, tab, tab, tab_b, tab_b,
            pltpu.VMEM((dm, tm), F32),
        ],
        compiler_params=_cparams(("parallel", "arbitrary")),
        name="peer",
    )(x1, wqt, k1, k2, u, vt, g, b)


def _layer(x, mem, positions, w_in, pe_k, pe_v, w1k, w2k, w1v, w2v, w_mem_kv, w_out, ln1_g, ln1_b,
           w_query, sk1, sk2, pu, pv, ln2_g, ln2_b):
    bsz, seq, dm = x.shape
    m_len = mem.shape[1]
    t = bsz * seq
    d = HEAD_DIM
    g_n = NSA_KV_HEADS
    assert seq % 512 == 0 and seq >= WINDOW + 128
    x2 = x.reshape(t, dm)

    hb, misc = _proj(x2, _regroup_w_in(w_in), _rope_tables(positions), tm=512)

    rows = seq // CMP_STRIDE
    cmp = hb[:, BLK_NKV * LANES:(BLK_NKV + g_n) * LANES].reshape(bsz, seq, g_n, 2, d)
    cmp = cmp.transpose(0, 2, 3, 1, 4).reshape(bsz * g_n, 2, rows, CMP_STRIDE * d)
    zpad = jnp.zeros((CMP_HIDDEN, d), BF16)
    pe_rows = lambda pe: jnp.pad(pe.reshape(1, CMP_LEN * d), ((0, 7), (0, 0))).astype(BF16)
    kcvc = _compress(cmp[:, 0], cmp[:, 1], w1k.astype(BF16), w1v.astype(BF16),
                     jnp.concatenate([w2k.astype(BF16), zpad], axis=1), jnp.concatenate([zpad, w2v.astype(BF16)], axis=1),
                     pe_rows(pe_k), pe_rows(pe_v))

    ovl_t, expand = _nsa_consts(seq)
    o_nsa = _nsa(hb, misc, kcvc, ovl_t, expand, bsz, seq, c=128, tk=512)
    o_dsa = _dsa(hb, misc, bsz, seq, c=128, tk=512)

    wm = w_mem_kv.reshape(dm, 2, MEM_HEADS, d).transpose(0, 2, 1, 3).reshape(dm, MEM_HEADS * 2 * d).astype(BF16)
    mkv = _matmul(mem.reshape(bsz * m_len, dm), wm, tm=m_len, out_dtype=BF16)
    o_mem = _mem(hb, mkv, bsz, seq, m_len, c=256)

    x1 = _out_ln(o_nsa, o_dsa, o_mem, x2, w_out.astype(BF16), ln1_g.reshape(1, dm), ln1_b.reshape(1, dm), tm=512)

    x2o = _peer(x1, w_query.T.astype(BF16), sk1.astype(BF16), sk2.astype(BF16), pu.astype(BF16), pv.T.astype(BF16),
                ln2_g.reshape(1, dm), ln2_b.reshape(1, dm), tm=512, eb=1024, nsub=2)
    return x2o.reshape(bsz, seq, dm)


def kernel(x, mem, positions, w_in, nsa_pe_k, nsa_pe_v, nsa_cmp_w1_k, nsa_cmp_w2_k, nsa_cmp_w1_v, nsa_cmp_w2_v,
           w_mem_kv, w_out, ln1_g, ln1_b, peer_w_query, peer_sub_keys_1, peer_sub_keys_2, peer_u, peer_v, ln2_g, ln2_b):
    assert w_in.shape[0] == DEPTH
    return _layer(x, mem, positions, w_in[0], nsa_pe_k[0], nsa_pe_v[0], nsa_cmp_w1_k[0], nsa_cmp_w2_k[0],
                  nsa_cmp_w1_v[0], nsa_cmp_w2_v[0], w_mem_kv[0], w_out[0], ln1_g[0], ln1_b[0], peer_w_query[0],
                  peer_sub_keys_1[0], peer_sub_keys_2[0], peer_u[0], peer_v[0], ln2_g[0], ln2_b[0])
```

```python
import functools

import numpy as np
import jax
import jax.numpy as jnp
from jax import lax
from jax.experimental import pallas as pl
from jax.experimental.pallas import tpu as pltpu

F32 = jnp.float32
BF16 = jnp.bfloat16

LANES = 128
VMEM_LIMIT = 56 << 20

HEAD_DIM = 64
ROPE_THETA = 500000.0
LN_EPS = 1e-5
NSA_HEADS = 8
NSA_KV_HEADS = 2
CMP_LEN = 32
CMP_STRIDE = 16
CMP_HIDDEN = 128
SEL_BLOCK = 64
SEL_COUNT = 16
WINDOW = 512
FORCE_BONUS = 1e4
DSA_HEADS = 4
IDX_HEADS = 8
IDX_DIM = 32
DSA_TOPK_MAX = 256
MEM_HEADS = 4
PEER_HEADS = 8
PEER_N_KEYS = 128
PEER_KEY_DIM = 256
PEER_TOPK = 16
DEPTH = 1
DEEPNORM_ALPHA = (2.0 * DEPTH) ** 0.25

NEG = -1e30

BLK_QN = 0
BLK_NKV = 4
BLK_DQ = 10
BLK_IQ = 12
BLK_MQ = 14
BLK_DKV = 16
BLK_IK = 17
BLK_MISC = 18
N_BLK = 19
MISC_W = 0
MISC_G = IDX_HEADS


def _cparams(sem):
    return pltpu.CompilerParams(dimension_semantics=sem, vmem_limit_bytes=VMEM_LIMIT)


def _regroup_w_in(w_in):
    d = HEAD_DIM
    o_q = 0
    o_kv = o_q + NSA_HEADS * d
    o_g = o_kv + 6 * NSA_KV_HEADS * d
    o_dq = o_g + 3 * NSA_HEADS
    o_dkv = o_dq + DSA_HEADS * d
    o_iq = o_dkv + 2 * d
    o_ik = o_iq + IDX_HEADS * IDX_DIM
    o_iw = o_ik + IDX_DIM
    o_mq = o_iw + IDX_HEADS
    cols = [w_in[:, o_q:o_kv]]
    for br in range(3):
        for g in range(NSA_KV_HEADS):
            k0 = o_kv + ((2 * br) * NSA_KV_HEADS + g) * d
            v0 = o_kv + ((2 * br + 1) * NSA_KV_HEADS + g) * d
            cols += [w_in[:, k0:k0 + d], w_in[:, v0:v0 + d]]
    cols.append(w_in[:, o_dq:o_dkv])
    cols.append(w_in[:, o_iq:o_ik])
    cols.append(w_in[:, o_mq:o_mq + MEM_HEADS * d])
    cols.append(w_in[:, o_dkv:o_iq])
    cols += [w_in[:, o_ik:o_iw]] * (LANES // IDX_DIM)
    cols += [w_in[:, o_iw:o_mq], w_in[:, o_g:o_dq]]
    pad = LANES - IDX_HEADS - 3 * NSA_HEADS
    cols.append(jnp.zeros((w_in.shape[0], pad), w_in.dtype))
    w = jnp.concatenate(cols, axis=1)
    assert w.shape[1] == N_BLK * LANES
    return w.astype(BF16)


def _rope_tables(positions):
    pos = positions.reshape(-1).astype(F32)
    tabs = []
    for hd in (HEAD_DIM, IDX_DIM):
        r = hd // 4
        half = r // 2
        freqs = jnp.power(ROPE_THETA, -jnp.arange(half, dtype=F32) * 2.0 / r)
        ang = pos[:, None] * freqs
        cos, sin = jnp.cos(ang), jnp.sin(ang)
        t = pos.shape[0]
        c = jnp.concatenate([cos, cos, jnp.ones((t, hd - r), F32)], axis=1)
        sa = jnp.concatenate([-sin, jnp.zeros((t, hd - half), F32)], axis=1)
        sb = jnp.concatenate([jnp.zeros((t, half), F32), sin, jnp.zeros((t, hd - r), F32)], axis=1)
        rep = LANES // hd
        tabs += [jnp.tile(c, (1, rep)), jnp.tile(sa, (1, rep)), jnp.tile(sb, (1, rep))]
    return jnp.concatenate(tabs, axis=1)


def _proj_kernel(x_ref, w_ref, tab_ref, hb_ref, misc_ref):
    xb = x_ref[...].astype(BF16)
    tm = xb.shape[0]
    lane = lax.broadcasted_iota(jnp.int32, (tm, LANES), 1)
    lo = lane < HEAD_DIM
    c64, sa64, sb64 = (tab_ref[:, i * LANES:(i + 1) * LANES] for i in range(3))
    c32, sa32, sb32 = (tab_ref[:, i * LANES:(i + 1) * LANES] for i in range(3, 6))
    c64h, sa64h, sb64h = jnp.where(lo, c64, 1.0), jnp.where(lo, sa64, 0.0), jnp.where(lo, sb64, 0.0)

    def rope(h, c, sa, sb, half):
        return h * c + pltpu.roll(h, LANES - half, 1) * sa + pltpu.roll(h, half, 1) * sb

    def finish(blk, h):
        if blk < BLK_NKV or BLK_DQ <= blk < BLK_IQ:
            return rope(h, c64, sa64, sb64, HEAD_DIM // 8)
        if BLK_NKV <= blk < BLK_DQ or blk == BLK_DKV:
            return rope(h, c64h, sa64h, sb64h, HEAD_DIM // 8)
        if BLK_IQ <= blk < BLK_MQ or blk == BLK_IK:
            return rope(h, c32, sa32, sb32, IDX_DIM // 8)
        return h

    for j in range(BLK_MISC // 2):
        h = jnp.dot(xb, w_ref[:, j * 2 * LANES:(j + 1) * 2 * LANES], preferred_element_type=F32)
        for s in range(2):
            blk = 2 * j + s
            hb_ref[:, blk * LANES:(blk + 1) * LANES] = finish(blk, h[:, s * LANES:(s + 1) * LANES]).astype(BF16)
    misc_ref[...] = jnp.dot(xb, w_ref[:, BLK_MISC * LANES:], preferred_element_type=F32)


def _proj(x2, w2, tabs, tm):
    t, dm = x2.shape
    return pl.pallas_call(
        _proj_kernel,
        grid=(t // tm,),
        in_specs=[
            pl.BlockSpec((tm, dm), lambda i: (i, 0)),
            pl.BlockSpec((dm, N_BLK * LANES), lambda i: (0, 0)),
            pl.BlockSpec((tm, 6 * LANES), lambda i: (i, 0)),
        ],
        out_specs=[
            pl.BlockSpec((tm, BLK_MISC * LANES), lambda i: (i, 0)),
            pl.BlockSpec((tm, LANES), lambda i: (i, 0)),
        ],
        out_shape=[
            jax.ShapeDtypeStruct((t, BLK_MISC * LANES), BF16),
            jax.ShapeDtypeStruct((t, LANES), F32),
        ],
        compiler_params=_cparams(("parallel",)),
        name="proj",
    )(x2, w2, tabs)


def _gelu(x):
    return 0.5 * x * (1.0 + lax.erf(x * np.float32(1.0 / np.sqrt(2.0))))


def _compress_kernel(kch_ref, vch_ref, w1k_ref, w1v_ref, w2k_ref, w2v_ref, pek_ref, pev_ref, out_ref):
    half = w1k_ref.shape[0] // 2
    rows = kch_ref.shape[1]

    def hidden(ch_ref, w1_ref, pe_ref):
        ch = ch_ref[0]
        a = jnp.dot(ch, w1_ref[:half, :], preferred_element_type=F32)
        b = jnp.dot(ch, w1_ref[half:, :], preferred_element_type=F32)
        bias = jnp.dot(pe_ref[...], w1_ref[...], preferred_element_type=F32)[0:1, :]
        return _gelu(a + pltpu.roll(b, rows - 1, 0) + bias).astype(BF16)

    hk = hidden(kch_ref, w1k_ref, pek_ref)
    hv = hidden(vch_ref, w1v_ref, pev_ref)
    out = jnp.dot(hk, w2k_ref[...], preferred_element_type=F32) + jnp.dot(hv, w2v_ref[...], preferred_element_type=F32)
    out_ref[0] = out.astype(BF16)


def _compress(kch, vch, w1k, w1v, w2k, w2v, pek, pev):
    bg, rows, width = kch.shape
    full = lambda a: pl.BlockSpec(a.shape, lambda i: (0,) * a.ndim)
    return pl.pallas_call(
        _compress_kernel,
        grid=(bg,),
        in_specs=[
            pl.BlockSpec((1, rows, width), lambda i: (i, 0, 0)),
            pl.BlockSpec((1, rows, width), lambda i: (i, 0, 0)),
            full(w1k), full(w1v), full(w2k), full(w2v), full(pek), full(pev),
        ],
        out_specs=pl.BlockSpec((1, rows, LANES), lambda i: (i, 0, 0)),
        out_shape=jax.ShapeDtypeStruct((bg, rows, LANES), BF16),
        compiler_params=_cparams(("parallel",)),
        name="compress",
    )(kch, vch, w1k, w1v, w2k, w2v, pek, pev)


def _stack_heads(qpair_refs_or_vals, scale):
    outs = []
    for blk in qpair_refs_or_vals:
        b = blk.astype(F32) * scale
        lane = lax.broadcasted_iota(jnp.int32, b.shape, 1)
        lo = lane < HEAD_DIM
        outs.append(jnp.where(lo, b, 0.0))
        outs.append(jnp.where(lo, pltpu.roll(b, HEAD_DIM, 1), 0.0))
    return jnp.concatenate(outs, axis=0).astype(BF16)


def _qk(q, kv):
    return lax.dot_general(q, kv, (((1,), (1,)), ((), ())), preferred_element_type=F32)


def _flash_init(n_heads, c):
    return jnp.full((n_heads, c, 1), NEG, F32), jnp.zeros((n_heads, c, LANES), F32)


def _flash_step(s, valid, kv, m, acc):
    n_heads, c, tk = s.shape
    sb = s.astype(BF16) + jnp.where(valid, 0.0, NEG).astype(BF16)
    m_new = jnp.maximum(m, jnp.max(sb, axis=-1, keepdims=True).astype(F32))
    p = jnp.exp(sb - m_new.astype(BF16))
    lane = lax.broadcasted_iota(jnp.int32, kv.shape, 1)
    ones_v = jnp.where(lane < HEAD_DIM, jnp.ones_like(kv), kv)
    pv = jnp.dot(p.reshape(n_heads * c, tk), ones_v, preferred_element_type=F32)
    return m_new, jnp.exp(m - m_new) * acc + pv.reshape(n_heads, c, LANES)


def _normalize(acc):
    den = acc[..., 0:1]
    return acc / jnp.where(den > 0.0, den, 1.0)


def _pack_heads(o, n_heads, c):
    lane = lax.broadcasted_iota(jnp.int32, (c, LANES), 1)
    lo = lane < HEAD_DIM
    blocks = []
    for p in range(n_heads // 2):
        a = o[(2 * p) * c:(2 * p + 1) * c]
        b = o[(2 * p + 1) * c:(2 * p + 2) * c]
        blocks.append(jnp.where(lo, pltpu.roll(a, HEAD_DIM, 1), b))
    return blocks


def _nsa_kernel(q_ref, misc_ref, kc_ref, ks_ref, kw_ref, ovl_ref, exp_ref, o_ref, *, c, tk, seq):
    g = pl.program_id(1)
    ci = pl.program_id(2)
    t0 = ci * c
    hpg = NSA_HEADS // NSA_KV_HEADS
    rows = hpg * c
    scale = HEAD_DIM ** -0.5
    q4 = _stack_heads([q_ref[:, :LANES], q_ref[:, LANES:]], scale)
    trow = t0 + lax.broadcasted_iota(jnp.int32, (c, 1), 0)

    kc = kc_ref[0]
    ncp = kc.shape[0]
    s = _qk(q4, kc).reshape(hpg, c, ncp)
    ncol = lax.broadcasted_iota(jnp.int32, (c, ncp), 1)
    cvalid = (ncol * CMP_STRIDE + (CMP_LEN - 1)) <= trow
    s = jnp.where(cvalid, s, NEG)
    m = jnp.max(s, axis=-1, keepdims=True)
    e = jnp.where(cvalid, jnp.exp(s - m), 0.0)
    den = jnp.sum(e, axis=-1, keepdims=True)
    p = (e / jnp.where(den > 0.0, den, 1.0)).reshape(rows, ncp)
    pb = p.astype(BF16)
    o_cmp = jnp.dot(pb, kc, preferred_element_type=F32)

    imp_t = lax.dot_general(ovl_ref[...], pb, (((1,), (1,)), ((), ())), preferred_element_type=F32)
    imp = imp_t[:, 0:c]
    for h in range(1, hpg):
        imp = imp + imp_t[:, h * c:(h + 1) * c]
    nb = imp.shape[0]
    jrow = lax.broadcasted_iota(jnp.int32, (nb, c), 0)
    tcol = t0 + lax.broadcasted_iota(jnp.int32, (nb, c), 1)
    cur = tcol // SEL_BLOCK
    forced = (jrow == 0) | (jrow == cur) | (jrow == cur - 1)
    visible = (jrow * SEL_BLOCK) <= tcol
    score = jnp.where(visible, imp + jnp.where(forced, FORCE_BONUS, 0.0), -jnp.inf)
    n_pick = min(SEL_COUNT, seq // SEL_BLOCK)
    sel = jnp.zeros((nb, c), F32)
    for _ in range(n_pick):
        best = jnp.max(score, axis=0, keepdims=True)
        first = jnp.min(jnp.where(score == best, jrow, nb), axis=0, keepdims=True)
        hit = jrow == first
        sel = jnp.where(hit, 1.0, sel)
        score = jnp.where(hit, -jnp.inf, score)
    sel_b = jnp.transpose(sel).astype(BF16)

    n_tiles = (t0 + c + tk - 1) // tk
    kcol = lax.broadcasted_iota(jnp.int32, (c, tk), 1)

    def sel_body(kt, carry):
        k0 = pl.multiple_of(kt * tk, tk)
        kv = ks_ref[pl.ds(k0, tk), :]
        s_ = _qk(q4, kv).reshape(hpg, c, tk)
        tok = jnp.dot(sel_b, exp_ref[:, pl.ds(k0, tk)], preferred_element_type=F32)
        valid = (tok > 0.5) & ((k0 + kcol) <= trow)
        return _flash_step(s_, valid, kv, *carry)

    init = _flash_init(hpg, c)
    _, acc_s = lax.fori_loop(0, n_tiles, sel_body, init)
    o_sel = _normalize(acc_s).reshape(rows, LANES)

    wk = WINDOW + c
    w0 = pl.multiple_of(jnp.maximum(t0 - WINDOW, 0), c)
    kvw = kw_ref[pl.ds(w0, wk), :]
    s = _qk(q4, kvw).reshape(hpg, c, wk)
    wpos = w0 + lax.broadcasted_iota(jnp.int32, (c, wk), 1)
    wvalid = (wpos <= trow) & (wpos > trow - WINDOW)
    _, acc_w = _flash_step(s, wvalid, kvw, *init)
    o_win = _normalize(acc_w).reshape(rows, LANES)

    gates = jax.nn.sigmoid(misc_ref[...])
    outs = []
    for h in range(hpg):
        col = MISC_G + (g * hpg + h) * 3
        sl = slice(h * c, (h + 1) * c)
        outs.append(_lane_pick(gates, col) * o_cmp[sl] + _lane_pick(gates, col + 1) * o_sel[sl]
                    + _lane_pick(gates, col + 2) * o_win[sl])
    blocks = _pack_heads(jnp.concatenate(outs, axis=0), hpg, c)
    for i, blk in enumerate(blocks):
        o_ref[:, i * LANES:(i + 1) * LANES] = blk.astype(BF16)


def _nsa(hb, misc, kcvc, ovl_t, expand, bsz, seq, c, tk):
    t = hb.shape[0]
    g_n = NSA_KV_HEADS
    nch = seq // c
    ncp = kcvc.shape[1]
    nbp = ovl_t.shape[0]
    kern = functools.partial(_nsa_kernel, c=c, tk=tk, seq=seq)
    return pl.pallas_call(
        kern,
        grid=(bsz, g_n, nch),
        in_specs=[
            pl.BlockSpec((c, 2 * LANES), lambda b, g, i: (b * nch + i, g)),
            pl.BlockSpec((c, LANES), lambda b, g, i: (b * nch + i, 0)),
            pl.BlockSpec((1, ncp, LANES), lambda b, g, i: (b * g_n + g, 0, 0)),
            pl.BlockSpec((seq, LANES), lambda b, g, i: (b, BLK_NKV + 2 + g)),
            pl.BlockSpec((seq, LANES), lambda b, g, i: (b, BLK_NKV + 4 + g)),
            pl.BlockSpec((nbp, ncp), lambda b, g, i: (0, 0)),
            pl.BlockSpec((nbp, seq), lambda b, g, i: (0, 0)),
        ],
        out_specs=pl.BlockSpec((c, 2 * LANES), lambda b, g, i: (b * nch + i, g)),
        out_shape=jax.ShapeDtypeStruct((t, NSA_HEADS * HEAD_DIM), BF16),
        compiler_params=_cparams(("parallel", "parallel", "arbitrary")),
        name="nsa",
    )(hb, misc, kcvc, hb, hb, ovl_t, expand)


def _nsa_consts(seq):
    nb = seq // SEL_BLOCK
    nbp = max(LANES, nb)
    rows = seq // CMP_STRIDE
    nc = (seq - CMP_LEN) // CMP_STRIDE + 1
    j = np.arange(nbp)[:, None]
    n = np.arange(rows)[None, :]
    ovl = np.clip(np.minimum(n * CMP_STRIDE + CMP_LEN, j * SEL_BLOCK + SEL_BLOCK)
                  - np.maximum(n * CMP_STRIDE, j * SEL_BLOCK), 0, None).astype(np.float32) / CMP_LEN
    ovl = np.where((n < nc) & (j < nb), ovl, 0.0)
    s = np.arange(seq)[None, :]
    expand = (s // SEL_BLOCK == j).astype(np.float32)
    return jnp.asarray(ovl, BF16), jnp.asarray(expand, BF16)


INT_MIN = -2147483648


def _lane_pick(x, col):
    lane = lax.broadcasted_iota(jnp.int32, x.shape, 1)
    return jnp.sum(jnp.where(lane == col, x, 0.0), axis=1, keepdims=True)


def _dsa_kernel(q_ref, iq_ref, misc_ref, ik_ref, kv_ref, o_ref, key_scr, *, c, tk, seq, topk):
    ci = pl.program_id(1)
    t0 = ci * c
    n_tiles = (t0 + c + tk - 1) // tk
    tq = t0 + lax.broadcasted_iota(jnp.int32, (1, c), 1)
    krow = lax.broadcasted_iota(jnp.int32, (tk, c), 0)

    lane = lax.broadcasted_iota(jnp.int32, (c, LANES), 1)
    per_blk = LANES // IDX_DIM
    qs = []
    for h in range(IDX_HEADS):
        blk = iq_ref[:, (h // per_blk) * LANES:(h // per_blk + 1) * LANES]
        qs.append(jnp.where(lane // IDX_DIM == h % per_blk, blk, jnp.zeros_like(blk)))
    qst = jnp.concatenate(qs, axis=0)
    w_t = jnp.transpose(misc_ref[...] * (IDX_HEADS ** -0.5 * IDX_DIM ** -0.5))

    def score_body(kt, carry):
        k0 = pl.multiple_of(kt * tk, tk)
        lg = _qk(ik_ref[pl.ds(k0, tk), :], qst)
        sc = jnp.zeros((tk, c), F32)
        for h in range(IDX_HEADS):
            sc = sc + jnp.maximum(lg[:, h * c:(h + 1) * c], 0.0) * w_t[MISC_W + h:MISC_W + h + 1, :]
        sc = sc + 0.0
        sc = jnp.where((k0 + krow) <= tq, sc, -jnp.inf)
        bits = pltpu.bitcast(sc, jnp.int32)
        key_scr[pl.ds(k0, tk), :] = jnp.where(bits < 0, bits ^ jnp.int32(0x7FFFFFFF), bits)
        return carry

    lax.fori_loop(0, n_tiles, score_body, 0)

    n_acc = 8

    def count(pred):
        def body(kt, acc):
            k0 = pl.multiple_of(kt * tk, tk)
            hit = jnp.where(pred(key_scr[pl.ds(k0, tk), :], k0 + krow), 1.0, 0.0)
            return acc + jnp.sum(hit.reshape(tk // (8 * n_acc), n_acc * 8, c), axis=0)
        acc = lax.fori_loop(0, n_tiles, body, jnp.zeros((n_acc * 8, c), F32))
        return jnp.sum(acc, axis=0, keepdims=True)

    def thr_body(i, ans):
        cand = ans | (jnp.int32(1) << (31 - i))
        cand_s = cand ^ jnp.int32(INT_MIN)
        cnt = count(lambda keys, idx: keys >= cand_s)
        return jnp.where(cnt >= topk, cand, ans)

    thr = lax.fori_loop(0, 32, thr_body, jnp.zeros((1, c), jnp.int32)) ^ jnp.int32(INT_MIN)
    need = topk - count(lambda keys, idx: keys > thr)
    n_ties = count(lambda keys, idx: keys == thr)

    n_bits = max(1, int(np.ceil(np.log2(seq))))

    def tie_search():
        def tie_body(i, ans):
            cand = ans | (jnp.int32(1) << (n_bits - 1 - i))
            cnt = count(lambda keys, idx: (keys == thr) & (idx < cand))
            return jnp.where(cnt < need, cand, ans)
        return lax.fori_loop(0, n_bits, tie_body, jnp.zeros((1, c), jnp.int32))

    surplus = jnp.max(n_ties - need)
    last_tie = lax.cond(surplus > 0.0, tie_search, lambda: jnp.full((1, c), seq, jnp.int32))

    q4 = _stack_heads([q_ref[:, :LANES], q_ref[:, LANES:]], HEAD_DIM ** -0.5)
    eye = (lax.broadcasted_iota(jnp.int32, (c, c), 0) == lax.broadcasted_iota(jnp.int32, (c, c), 1)).astype(BF16)

    def att_body(kt, carry):
        k0 = pl.multiple_of(kt * tk, tk)
        kv = kv_ref[pl.ds(k0, tk), :]
        keys = key_scr[pl.ds(k0, tk), :]
        idx = k0 + krow
        picked = (idx <= tq) & ((keys > thr) | ((keys == thr) & (idx <= last_tie)))
        valid = _qk(eye, jnp.where(picked, 1.0, 0.0).astype(BF16)) > 0.5
        s = _qk(q4, kv).reshape(DSA_HEADS, c, tk)
        return _flash_step(s, valid, kv, *carry)

    _, acc = lax.fori_loop(0, n_tiles, att_body, _flash_init(DSA_HEADS, c))
    o = _normalize(acc).reshape(DSA_HEADS * c, LANES)
    for i, blk in enumerate(_pack_heads(o, DSA_HEADS, c)):
        o_ref[:, i * LANES:(i + 1) * LANES] = blk.astype(BF16)


def _dsa(hb, misc, bsz, seq, c, tk):
    t = hb.shape[0]
    nch = seq // c
    topk = min(DSA_TOPK_MAX, seq // 4)
    kern = functools.partial(_dsa_kernel, c=c, tk=tk, seq=seq, topk=topk)
    return pl.pallas_call(
        kern,
        grid=(bsz, nch),
        in_specs=[
            pl.BlockSpec((c, 2 * LANES), lambda b, i: (b * nch + i, BLK_DQ // 2)),
            pl.BlockSpec((c, 2 * LANES), lambda b, i: (b * nch + i, BLK_IQ // 2)),
            pl.BlockSpec((c, LANES), lambda b, i: (b * nch + i, 0)),
            pl.BlockSpec((seq, LANES), lambda b, i: (b, BLK_IK)),
            pl.BlockSpec((seq, LANES), lambda b, i: (b, BLK_DKV)),
        ],
        out_specs=pl.BlockSpec((c, 2 * LANES), lambda b, i: (b * nch + i, 0)),
        out_shape=jax.ShapeDtypeStruct((t, DSA_HEADS * HEAD_DIM), BF16),
        scratch_shapes=[pltpu.VMEM((seq, c), jnp.int32)],
        compiler_params=_cparams(("parallel", "arbitrary")),
        name="dsa",
    )(hb, hb, misc, hb, hb)


def _matmul_kernel(a_ref, w_ref, o_ref):
    o_ref[...] = jnp.dot(a_ref[...].astype(BF16), w_ref[...], preferred_element_type=F32).astype(o_ref.dtype)


def _matmul(a, w, tm, out_dtype):
    m, k = a.shape
    n = w.shape[1]
    return pl.pallas_call(
        _matmul_kernel,
        grid=(m // tm,),
        in_specs=[pl.BlockSpec((tm, k), lambda i: (i, 0)), pl.BlockSpec((k, n), lambda i: (0, 0))],
        out_specs=pl.BlockSpec((tm, n), lambda i: (i, 0)),
        out_shape=jax.ShapeDtypeStruct((m, n), out_dtype),
        compiler_params=_cparams(("parallel",)),
        name="matmul",
    )(a, w)


def _mem_kernel(q_ref, mkv_ref, o_ref, *, c):
    q4 = _stack_heads([q_ref[:, :LANES], q_ref[:, LANES:]], HEAD_DIM ** -0.5)
    outs = []
    for h in range(MEM_HEADS):
        kv = mkv_ref[:, h * LANES:(h + 1) * LANES]
        s = _qk(q4[h * c:(h + 1) * c], kv)
        e = jnp.exp(s - jnp.max(s, axis=-1, keepdims=True))
        p = e / jnp.sum(e, axis=-1, keepdims=True)
        outs.append(jnp.dot(p.astype(BF16), kv, preferred_element_type=F32))
    for i, blk in enumerate(_pack_heads(jnp.concatenate(outs, axis=0), MEM_HEADS, c)):
        o_ref[:, i * LANES:(i + 1) * LANES] = blk.astype(BF16)


def _mem(hb, mkv, bsz, seq, m_len, c):
    t = hb.shape[0]
    nch = seq // c
    return pl.pallas_call(
        functools.partial(_mem_kernel, c=c),
        grid=(bsz, nch),
        in_specs=[
            pl.BlockSpec((c, 2 * LANES), lambda b, i: (b * nch + i, BLK_MQ // 2)),
            pl.BlockSpec((m_len, MEM_HEADS * LANES), lambda b, i: (b, 0)),
        ],
        out_specs=pl.BlockSpec((c, 2 * LANES), lambda b, i: (b * nch + i, 0)),
        out_shape=jax.ShapeDtypeStruct((t, MEM_HEADS * HEAD_DIM), BF16),
        compiler_params=_cparams(("parallel", "parallel")),
        name="mem",
    )(hb, mkv)


def _layer_norm(v, g, b):
    mu = jnp.mean(v, axis=-1, keepdims=True)
    d = v - mu
    var = jnp.mean(d * d, axis=-1, keepdims=True)
    return d * lax.rsqrt(var + LN_EPS) * g + b


def _out_ln_kernel(on_ref, od_ref, om_ref, x_ref, w_ref, g_ref, b_ref, o_ref):
    n0 = on_ref.shape[1]
    n1 = n0 + od_ref.shape[1]
    mix = jnp.dot(on_ref[...], w_ref[:n0, :], preferred_element_type=F32)
    mix = mix + jnp.dot(od_ref[...], w_ref[n0:n1, :], preferred_element_type=F32)
    mix = mix + jnp.dot(om_ref[...], w_ref[n1:, :], preferred_element_type=F32)
    o_ref[...] = _layer_norm(DEEPNORM_ALPHA * x_ref[...] + mix, g_ref[...], b_ref[...])


def _out_ln(o_nsa, o_dsa, o_mem, x2, w_out, g, b, tm):
    t, dm = x2.shape
    row = lambda a: pl.BlockSpec((tm, a.shape[1]), lambda i: (i, 0))
    full = lambda a: pl.BlockSpec(a.shape, lambda i: (0, 0))
    return pl.pallas_call(
        _out_ln_kernel,
        grid=(t // tm,),
        in_specs=[row(o_nsa), row(o_dsa), row(o_mem), row(x2), full(w_out), full(g), full(b)],
        out_specs=pl.BlockSpec((tm, dm), lambda i: (i, 0)),
        out_shape=jax.ShapeDtypeStruct((t, dm), F32),
        compiler_params=_cparams(("parallel",)),
        name="out_ln",
    )(o_nsa, o_dsa, o_mem, x2, w_out, g, b)


NOT_PICKED = 64.0


def _col_max(s):
    parts = [s[i:i + 8] for i in range(0, s.shape[0], 8)]
    while len(parts) > 1:
        nxt = [jnp.maximum(parts[i], parts[i + 1]) for i in range(0, len(parts) - 1, 2)]
        if len(parts) % 2:
            nxt.append(parts[-1])
        parts = nxt
    return jnp.max(parts[0], axis=0, keepdims=True)


def _top_rows(s, k, break_ties):
    n_rows = s.shape[0]
    rows = lax.broadcasted_iota(jnp.int32, s.shape, 0)
    rank = jnp.full(s.shape, NOT_PICKED, F32)
    vals = []
    for r in range(k):
        best = _col_max(s)
        hit = s == best
        if break_ties:
            hit = rows == jnp.min(jnp.where(hit, rows, n_rows), axis=0, keepdims=True)
        rank = jnp.where(hit, float(r), rank)
        s = jnp.where(hit, -jnp.inf, s)
        vals.append(best)
    return rank, vals


def _ranked_exactly(rank, k):
    return jnp.sum(jnp.where(rank < k, 1.0, 0.0), axis=0, keepdims=True) == float(k)


def _peer_select(s1, s2, k, break_ties):
    rank1, v1 = _top_rows(s1, k, break_ties)
    rank2, v2 = _top_rows(s2, k, break_ties)
    v2a = jnp.concatenate(v2, axis=0)
    v2lo = v2a[:8]
    r8 = lax.broadcasted_iota(jnp.int32, v2lo.shape, 0)
    pieces = [v1[0] + v2a, v1[1] + v2lo]
    for a in range(2, 8):
        pieces.append(jnp.where(r8 < k // (a + 1), v1[a] + v2lo, -jnp.inf))
    pieces.append(jnp.concatenate(v1[8:], axis=0) + v2[0])
    cand = jnp.concatenate(pieces, axis=0)
    rank_c, top = _top_rows(cand, k, break_ties)
    return rank1, rank2, rank_c, v1[0], v2[0], jnp.concatenate(top, axis=0)


def _peer_kernel(x1_ref, wqt_ref, k1_ref, k2_ref, u_ref, vt_ref, g_ref, b_ref, o_ref,
                 xt_scr, s1_scr, s2_scr, e1_scr, r2_scr, e2_scr, y_scr, *, tm, eb, nsub):
    j = pl.program_id(1)
    nk = PEER_N_KEYS
    half = PEER_KEY_DIM // 2
    n_lt = tm // LANES
    k = PEER_TOPK
    pack = 16

    @pl.when(j == 0)
    def _select():
        xt = jnp.transpose(x1_ref[...]).astype(BF16)
        xt_scr[...] = xt
        for h in range(PEER_HEADS):
            qh = jnp.dot(wqt_ref[h * PEER_KEY_DIM:(h + 1) * PEER_KEY_DIM, :], xt, preferred_element_type=F32).astype(BF16)
            s1_scr[h] = jnp.dot(k1_ref[...], qh[:half], preferred_element_type=F32)
            s2_scr[h] = jnp.dot(k2_ref[...], qh[half:], preferred_element_type=F32)

        n_par = 4
        per_head = n_lt // n_par

        def chunk(i, carry):
            h = i // per_head
            lanes = [pl.multiple_of(((i % per_head) * n_par + p) * LANES, LANES) for p in range(n_par)]
            s1s = [s1_scr[h, :, pl.ds(l0, LANES)] for l0 in lanes]
            s2s = [s2_scr[h, :, pl.ds(l0, LANES)] for l0 in lanes]
            fast = [_peer_select(s1, s2, k, break_ties=False) for s1, s2 in zip(s1s, s2s)]
            n_unclean = 0.0
            for f in fast:
                clean = _ranked_exactly(f[0], k) & _ranked_exactly(f[1], k) & _ranked_exactly(f[2], k)
                n_unclean = n_unclean + jnp.sum(jnp.where(clean, 0.0, 1.0))
            picks = lax.cond(
                n_unclean > 0.0,
                lambda: [_peer_select(s1, s2, k, break_ties=True) for s1, s2 in zip(s1s, s2s)],
                lambda: fast)
            for l0, s1, s2, (rank1, rank2, rank_c, v1_max, v2_max, top) in zip(lanes, s1s, s2s, picks):
                den = jnp.sum(jnp.exp(top - top[0:1]), axis=0, keepdims=True)
                picked = jnp.where(rank_c < k, 1.0, 0.0)
                n_of_rank = [jnp.sum(picked[0:16], axis=0, keepdims=True)]
                n_of_rank += [jnp.sum(picked[8 * a + 8:8 * a + 16], axis=0, keepdims=True) for a in range(1, 8)]
                n_of_rank += [picked[72 + a:73 + a] for a in range(8)]
                n1 = jnp.zeros_like(s1)
                for a in range(k):
                    n1 = jnp.where(rank1 == float(a), n_of_rank[a], n1)
                s1_scr[h, :, pl.ds(l0, LANES)] = n1
                e1_scr[h, :, pl.ds(l0, LANES)] = jnp.where(rank1 < k, jnp.exp(s1 - v1_max), 0.0)
                r2_scr[h, :, pl.ds(l0, LANES)] = rank2.astype(BF16)
                e2_scr[h, :, pl.ds(l0, LANES)] = (jnp.where(rank2 < k, jnp.exp(s2 - v2_max), 0.0) / den).astype(BF16)
            return carry

        lax.fori_loop(0, PEER_HEADS * per_head, chunk, 0)
        y_scr[...] = jnp.zeros_like(y_scr)

    xt = xt_scr[...]
    y_add = None
    for sub in range(nsub):
        i1_0 = pl.multiple_of((j * nsub + sub) * (eb // nk), eb // nk)
        a = jnp.dot(u_ref[sub * eb:(sub + 1) * eb, :], xt, preferred_element_type=F32)
        zs = []
        for ib in range(eb // nk):
            zrow = []
            for lt in range(n_lt):
                ls = slice(lt * LANES, (lt + 1) * LANES)
                w = jnp.zeros((nk // pack, pack, LANES), BF16)
                for h in range(PEER_HEADS):
                    n1r = s1_scr[h, pl.ds(i1_0, eb // nk), ls][ib:ib + 1]
                    e1r = e1_scr[h, pl.ds(i1_0, eb // nk), ls][ib:ib + 1]
                    n1b = jnp.broadcast_to(n1r, (pack, LANES)).astype(BF16)[None]
                    e1b = jnp.broadcast_to(e1r, (pack, LANES)).astype(BF16)[None]
                    r2 = r2_scr[h, :, ls].reshape(nk // pack, pack, LANES)
                    e2 = e2_scr[h, :, ls].reshape(nk // pack, pack, LANES)
                    w = w + jnp.where(r2 < n1b, e2, jnp.zeros_like(e2)) * e1b
                gate = w.reshape(nk, LANES).astype(F32)
                zrow.append((gate * _gelu(a[ib * nk:(ib + 1) * nk, ls])).astype(BF16))
            zs.append(jnp.concatenate(zrow, axis=1))
        z = jnp.concatenate(zs, axis=0)
        y_sub = jnp.dot(vt_ref[:, sub * eb:(sub + 1) * eb], z, preferred_element_type=F32)
        y_add = y_sub if y_add is None else y_add + y_sub
    y_scr[...] += y_add

    @pl.when(j == pl.num_programs(1) - 1)
    def _finish():
        y = jnp.transpose(y_scr[...])
        o_ref[...] = _layer_norm(DEEPNORM_ALPHA * x1_ref[...] + y, g_ref[...], b_ref[...])


def _peer(x1, wqt, k1, k2, u, vt, g, b, tm, eb, nsub):
    t, dm = x1.shape
    n_e = u.shape[0]
    assert eb // PEER_N_KEYS == 8, "one aligned 8-row group of first-key rows per expert block"
    full = lambda a: pl.BlockSpec(a.shape, lambda i, j: (0, 0))
    tab = pltpu.VMEM((PEER_HEADS, PEER_N_KEYS, tm), F32)
    tab_b = pltpu.VMEM((PEER_HEADS, PEER_N_KEYS, tm), BF16)
    return pl.pallas_call(
        functools.partial(_peer_kernel, tm=tm, eb=eb, nsub=nsub),
        grid=(t // tm, n_e // (nsub * eb)),
        in_specs=[
            pl.BlockSpec((tm, dm), lambda i, j: (i, 0)),
            full(wqt), full(k1), full(k2),
            pl.BlockSpec((nsub * eb, dm), lambda i, j: (j, 0)),
            pl.BlockSpec((dm, nsub * eb), lambda i, j: (0, j)),
            full(g), full(b),
        ],
        out_specs=pl.BlockSpec((tm, dm), lambda i, j: (i, 0)),
        out_shape=jax.ShapeDtypeStruct((t, dm), F32),
        scratch_shapes=[
            pltpu.VMEM((dm, tm), BF16), tab, tab, tab, tab_b, tab_b,
            pltpu.VMEM((dm, tm), F32),
        ],
        compiler_params=_cparams(("parallel", "arbitrary")),
        name="peer",
    )(x1, wqt, k1, k2, u, vt, g, b)


def _layer(x, mem, positions, w_in, pe_k, pe_v, w1k, w2k, w1v, w2v, w_mem_kv, w_out, ln1_g, ln1_b,
           w_query, sk1, sk2, pu, pv, ln2_g, ln2_b):
    bsz, seq, dm = x.shape
    m_len = mem.shape[1]
    t = bsz * seq
    d = HEAD_DIM
    g_n = NSA_KV_HEADS
    assert seq % 512 == 0 and seq >= WINDOW + 128
    x2 = x.reshape(t, dm)

    hb, misc = _proj(x2, _regroup_w_in(w_in), _rope_tables(positions), tm=512)

    rows = seq // CMP_STRIDE
    cmp = hb[:, BLK_NKV * LANES:(BLK_NKV + g_n) * LANES].reshape(bsz, seq, g_n, 2, d)
    cmp = cmp.transpose(0, 2, 3, 1, 4).reshape(bsz * g_n, 2, rows, CMP_STRIDE * d)
    zpad = jnp.zeros((CMP_HIDDEN, d), BF16)
    pe_rows = lambda pe: jnp.pad(pe.reshape(1, CMP_LEN * d), ((0, 7), (0, 0))).astype(BF16)
    kcvc = _compress(cmp[:, 0], cmp[:, 1], w1k.astype(BF16), w1v.astype(BF16),
                     jnp.concatenate([w2k.astype(BF16), zpad], axis=1), jnp.concatenate([zpad, w2v.astype(BF16)], axis=1),
                     pe_rows(pe_k), pe_rows(pe_v))

    ovl_t, expand = _nsa_consts(seq)
    o_nsa = _nsa(hb, misc, kcvc, ovl_t, expand, bsz, seq, c=128, tk=512)
    o_dsa = _dsa(hb, misc, bsz, seq, c=128, tk=512)

    wm = w_mem_kv.reshape(dm, 2, MEM_HEADS, d).transpose(0, 2, 1, 3).reshape(dm, MEM_HEADS * 2 * d).astype(BF16)
    mkv = _matmul(mem.reshape(bsz * m_len, dm), wm, tm=m_len, out_dtype=BF16)
    o_mem = _mem(hb, mkv, bsz, seq, m_len, c=256)

    x1 = _out_ln(o_nsa, o_dsa, o_mem, x2, w_out.astype(BF16), ln1_g.reshape(1, dm), ln1_b.reshape(1, dm), tm=512)

    x2o = _peer(x1, w_query.T.astype(BF16), sk1.astype(BF16), sk2.astype(BF16), pu.astype(BF16), pv.T.astype(BF16),
                ln2_g.reshape(1, dm), ln2_b.reshape(1, dm), tm=512, eb=1024, nsub=2)
    return x2o.reshape(bsz, seq, dm)


def kernel(x, mem, positions, w_in, nsa_pe_k, nsa_pe_v, nsa_cmp_w1_k, nsa_cmp_w2_k, nsa_cmp_w1_v, nsa_cmp_w2_v,
           w_mem_kv, w_out, ln1_g, ln1_b, peer_w_query, peer_sub_keys_1, peer_sub_keys_2, peer_u, peer_v, ln2_g, ln2_b):
    assert w_in.shape[0] == DEPTH
    return _layer(x, mem, positions, w_in[0], nsa_pe_k[0], nsa_pe_v[0], nsa_cmp_w1_k[0], nsa_cmp_w2_k[0],
                  nsa_cmp_w1_v[0], nsa_cmp_w2_v[0], w_mem_kv[0], w_out[0], ln1_g[0], ln1_b[0], peer_w_query[0],
                  peer_sub_keys_1[0], peer_sub_keys_2[0], peer_u[0], peer_v[0], ln2_g[0], ln2_b[0])
```

```python
import functools

import numpy as np
import jax
import jax.numpy as jnp
from jax import lax
from jax.experimental import pallas as pl
from jax.experimental.pallas import tpu as pltpu

F32 = jnp.float32
BF16 = jnp.bfloat16

LANES = 128
VMEM_LIMIT = 56 << 20

HEAD_DIM = 64
ROPE_THETA = 500000.0
LN_EPS = 1e-5
NSA_HEADS = 8
NSA_KV_HEADS = 2
CMP_LEN = 32
CMP_STRIDE = 16
CMP_HIDDEN = 128
SEL_BLOCK = 64
SEL_COUNT = 16
WINDOW = 512
FORCE_BONUS = 1e4
DSA_HEADS = 4
IDX_HEADS = 8
IDX_DIM = 32
DSA_TOPK_MAX = 256
MEM_HEADS = 4
PEER_HEADS = 8
PEER_N_KEYS = 128
PEER_KEY_DIM = 256
PEER_TOPK = 16
DEPTH = 1
DEEPNORM_ALPHA = (2.0 * DEPTH) ** 0.25

NEG = -1e30

BLK_QN = 0
BLK_NKV = 4
BLK_DQ = 10
BLK_IQ = 12
BLK_MQ = 14
BLK_DKV = 16
BLK_IK = 17
BLK_MISC = 18
N_BLK = 19
MISC_W = 0
MISC_G = IDX_HEADS


def _cparams(sem):
    return pltpu.CompilerParams(dimension_semantics=sem, vmem_limit_bytes=VMEM_LIMIT)


def _regroup_w_in(w_in):
    d = HEAD_DIM
    o_q = 0
    o_kv = o_q + NSA_HEADS * d
    o_g = o_kv + 6 * NSA_KV_HEADS * d
    o_dq = o_g + 3 * NSA_HEADS
    o_dkv = o_dq + DSA_HEADS * d
    o_iq = o_dkv + 2 * d
    o_ik = o_iq + IDX_HEADS * IDX_DIM
    o_iw = o_ik + IDX_DIM
    o_mq = o_iw + IDX_HEADS
    cols = [w_in[:, o_q:o_kv]]
    for br in range(3):
        for g in range(NSA_KV_HEADS):
            k0 = o_kv + ((2 * br) * NSA_KV_HEADS + g) * d
            v0 = o_kv + ((2 * br + 1) * NSA_KV_HEADS + g) * d
            cols += [w_in[:, k0:k0 + d], w_in[:, v0:v0 + d]]
    cols.append(w_in[:, o_dq:o_dkv])
    cols.append(w_in[:, o_iq:o_ik])
    cols.append(w_in[:, o_mq:o_mq + MEM_HEADS * d])
    cols.append(w_in[:, o_dkv:o_iq])
    cols += [w_in[:, o_ik:o_iw]] * (LANES // IDX_DIM)
    cols += [w_in[:, o_iw:o_mq], w_in[:, o_g:o_dq]]
    pad = LANES - IDX_HEADS - 3 * NSA_HEADS
    cols.append(jnp.zeros((w_in.shape[0], pad), w_in.dtype))
    w = jnp.concatenate(cols, axis=1)
    assert w.shape[1] == N_BLK * LANES
    return w.astype(BF16)


def _rope_tables(positions):
    pos = positions.reshape(-1).astype(F32)
    tabs = []
    for hd in (HEAD_DIM, IDX_DIM):
        r = hd // 4
        half = r // 2
        freqs = jnp.power(ROPE_THETA, -jnp.arange(half, dtype=F32) * 2.0 / r)
        ang = pos[:, None] * freqs
        cos, sin = jnp.cos(ang), jnp.sin(ang)
        t = pos.shape[0]
        c = jnp.concatenate([cos, cos, jnp.ones((t, hd - r), F32)], axis=1)
        sa = jnp.concatenate([-sin, jnp.zeros((t, hd - half), F32)], axis=1)
        sb = jnp.concatenate([jnp.zeros((t, half), F32), sin, jnp.zeros((t, hd - r), F32)], axis=1)
        rep = LANES // hd
        tabs += [jnp.tile(c, (1, rep)), jnp.tile(sa, (1, rep)), jnp.tile(sb, (1, rep))]
    return jnp.concatenate(tabs, axis=1)


def _proj_kernel(x_ref, w_ref, tab_ref, hb_ref, misc_ref):
    xb = x_ref[...].astype(BF16)
    tm = xb.shape[0]
    lane = lax.broadcasted_iota(jnp.int32, (tm, LANES), 1)
    lo = lane < HEAD_DIM
    c64, sa64, sb64 = (tab_ref[:, i * LANES:(i + 1) * LANES] for i in range(3))
    c32, sa32, sb32 = (tab_ref[:, i * LANES:(i + 1) * LANES] for i in range(3, 6))
    c64h, sa64h, sb64h = jnp.where(lo, c64, 1.0), jnp.where(lo, sa64, 0.0), jnp.where(lo, sb64, 0.0)

    def rope(h, c, sa, sb, half):
        return h * c + pltpu.roll(h, LANES - half, 1) * sa + pltpu.roll(h, half, 1) * sb

    def finish(blk, h):
        if blk < BLK_NKV or BLK_DQ <= blk < BLK_IQ:
            return rope(h, c64, sa64, sb64, HEAD_DIM // 8)
        if BLK_NKV <= blk < BLK_DQ or blk == BLK_DKV:
            return rope(h, c64h, sa64h, sb64h, HEAD_DIM // 8)
        if BLK_IQ <= blk < BLK_MQ or blk == BLK_IK:
            return rope(h, c32, sa32, sb32, IDX_DIM // 8)
        return h

    for j in range(BLK_MISC // 2):
        h = jnp.dot(xb, w_ref[:, j * 2 * LANES:(j + 1) * 2 * LANES], preferred_element_type=F32)
        for s in range(2):
            blk = 2 * j + s
            hb_ref[:, blk * LANES:(blk + 1) * LANES] = finish(blk, h[:, s * LANES:(s + 1) * LANES]).astype(BF16)
    misc_ref[...] = jnp.dot(xb, w_ref[:, BLK_MISC * LANES:], preferred_element_type=F32)


def _proj(x2, w2, tabs, tm):
    t, dm = x2.shape
    return pl.pallas_call(
        _proj_kernel,
        grid=(t // tm,),
        in_specs=[
            pl.BlockSpec((tm, dm), lambda i: (i, 0)),
            pl.BlockSpec((dm, N_BLK * LANES), lambda i: (0, 0)),
            pl.BlockSpec((tm, 6 * LANES), lambda i: (i, 0)),
        ],
        out_specs=[
            pl.BlockSpec((tm, BLK_MISC * LANES), lambda i: (i, 0)),
            pl.BlockSpec((tm, LANES), lambda i: (i, 0)),
        ],
        out_shape=[
            jax.ShapeDtypeStruct((t, BLK_MISC * LANES), BF16),
            jax.ShapeDtypeStruct((t, LANES), F32),
        ],
        compiler_params=_cparams(("parallel",)),
        name="proj",
    )(x2, w2, tabs)


def _gelu(x):
    return 0.5 * x * (1.0 + lax.erf(x * np.float32(1.0 / np.sqrt(2.0))))


def _compress_kernel(kch_ref, vch_ref, w1k_ref, w1v_ref, w2k_ref, w2v_ref, pek_ref, pev_ref, out_ref):
    half = w1k_ref.shape[0] // 2
    rows = kch_ref.shape[1]

    def hidden(ch_ref, w1_ref, pe_ref):
        ch = ch_ref[0]
        a = jnp.dot(ch, w1_ref[:half, :], preferred_element_type=F32)
        b = jnp.dot(ch, w1_ref[half:, :], preferred_element_type=F32)
        bias = jnp.dot(pe_ref[...], w1_ref[...], preferred_element_type=F32)[0:1, :]
        return _gelu(a + pltpu.roll(b, rows - 1, 0) + bias).astype(BF16)

    hk = hidden(kch_ref, w1k_ref, pek_ref)
    hv = hidden(vch_ref, w1v_ref, pev_ref)
    out = jnp.dot(hk, w2k_ref[...], preferred_element_type=F32) + jnp.dot(hv, w2v_ref[...], preferred_element_type=F32)
    out_ref[0] = out.astype(BF16)


def _compress(kch, vch, w1k, w1v, w2k, w2v, pek, pev):
    bg, rows, width = kch.shape
    full = lambda a: pl.BlockSpec(a.shape, lambda i: (0,) * a.ndim)
    return pl.pallas_call(
        _compress_kernel,
        grid=(bg,),
        in_specs=[
            pl.BlockSpec((1, rows, width), lambda i: (i, 0, 0)),
            pl.BlockSpec((1, rows, width), lambda i: (i, 0, 0)),
            full(w1k), full(w1v), full(w2k), full(w2v), full(pek), full(pev),
        ],
        out_specs=pl.BlockSpec((1, rows, LANES), lambda i: (i, 0, 0)),
        out_shape=jax.ShapeDtypeStruct((bg, rows, LANES), BF16),
        compiler_params=_cparams(("parallel",)),
        name="compress",
    )(kch, vch, w1k, w1v, w2k, w2v, pek, pev)


def _stack_heads(qpair_refs_or_vals, scale):
    outs = []
    for blk in qpair_refs_or_vals:
        b = blk.astype(F32) * scale
        lane = lax.broadcasted_iota(jnp.int32, b.shape, 1)
        lo = lane < HEAD_DIM
        outs.append(jnp.where(lo, b, 0.0))
        outs.append(jnp.where(lo, pltpu.roll(b, HEAD_DIM, 1), 0.0))
    return jnp.concatenate(outs, axis=0).astype(BF16)


def _qk(q, kv):
    return lax.dot_general(q, kv, (((1,), (1,)), ((), ())), preferred_element_type=F32)


def _flash_init(n_heads, c):
    return jnp.full((n_heads, c, 1), NEG, F32), jnp.zeros((n_heads, c, LANES), F32)


def _flash_step(s, valid, kv, m, acc):
    n_heads, c, tk = s.shape
    sb = s.astype(BF16) + jnp.where(valid, 0.0, NEG).astype(BF16)
    m_new = jnp.maximum(m, jnp.max(sb, axis=-1, keepdims=True).astype(F32))
    p = jnp.exp(sb - m_new.astype(BF16))
    lane = lax.broadcasted_iota(jnp.int32, kv.shape, 1)
    ones_v = jnp.where(lane < HEAD_DIM, jnp.ones_like(kv), kv)
    pv = jnp.dot(p.reshape(n_heads * c, tk), ones_v, preferred_element_type=F32)
    return m_new, jnp.exp(m - m_new) * acc + pv.reshape(n_heads, c, LANES)


def _normalize(acc):
    den = acc[..., 0:1]
    return acc / jnp.where(den > 0.0, den, 1.0)


def _pack_heads(o, n_heads, c):
    lane = lax.broadcasted_iota(jnp.int32, (c, LANES), 1)
    lo = lane < HEAD_DIM
    blocks = []
    for p in range(n_heads // 2):
        a = o[(2 * p) * c:(2 * p + 1) * c]
        b = o[(2 * p + 1) * c:(2 * p + 2) * c]
        blocks.append(jnp.where(lo, pltpu.roll(a, HEAD_DIM, 1), b))
    return blocks


def _nsa_kernel(q_ref, misc_ref, kc_ref, ks_ref, kw_ref, ovl_ref, exp_ref, o_ref, *, c, tk, seq):
    g = pl.program_id(1)
    ci = pl.program_id(2)
    t0 = ci * c
    hpg = NSA_HEADS // NSA_KV_HEADS
    rows = hpg * c
    scale = HEAD_DIM ** -0.5
    q4 = _stack_heads([q_ref[:, :LANES], q_ref[:, LANES:]], scale)
    trow = t0 + lax.broadcasted_iota(jnp.int32, (c, 1), 0)

    kc = kc_ref[0]
    ncp = kc.shape[0]
    s = _qk(q4, kc).reshape(hpg, c, ncp)
    ncol = lax.broadcasted_iota(jnp.int32, (c, ncp), 1)
    cvalid = (ncol * CMP_STRIDE + (CMP_LEN - 1)) <= trow
    s = jnp.where(cvalid, s, NEG)
    m = jnp.max(s, axis=-1, keepdims=True)
    e = jnp.where(cvalid, jnp.exp(s - m), 0.0)
    den = jnp.sum(e, axis=-1, keepdims=True)
    p = (e / jnp.where(den > 0.0, den, 1.0)).reshape(rows, ncp)
    pb = p.astype(BF16)
    o_cmp = jnp.dot(pb, kc, preferred_element_type=F32)

    imp_t = lax.dot_general(ovl_ref[...], pb, (((1,), (1,)), ((), ())), preferred_element_type=F32)
    imp = imp_t[:, 0:c]
    for h in range(1, hpg):
        imp = imp + imp_t[:, h * c:(h + 1) * c]
    nb = imp.shape[0]
    jrow = lax.broadcasted_iota(jnp.int32, (nb, c), 0)
    tcol = t0 + lax.broadcasted_iota(jnp.int32, (nb, c), 1)
    cur = tcol // SEL_BLOCK
    forced = (jrow == 0) | (jrow == cur) | (jrow == cur - 1)
    visible = (jrow * SEL_BLOCK) <= tcol
    score = jnp.where(visible, imp + jnp.where(forced, FORCE_BONUS, 0.0), -jnp.inf)
    n_pick = min(SEL_COUNT, seq // SEL_BLOCK)
    rank, _ = _top_rows(score, n_pick, break_ties=True)
    sel_b = jnp.transpose(jnp.where(rank < n_pick, 1.0, 0.0)).astype(BF16)

    n_tiles = (t0 + c + tk - 1) // tk
    kcol = lax.broadcasted_iota(jnp.int32, (c, tk), 1)

    def sel_body(kt, carry):
        k0 = pl.multiple_of(kt * tk, tk)
        kv = ks_ref[pl.ds(k0, tk), :]
        s_ = _qk(q4, kv).reshape(hpg, c, tk)
        tok = jnp.dot(sel_b, exp_ref[:, pl.ds(k0, tk)], preferred_element_type=F32)
        valid = (tok > 0.5) & ((k0 + kcol) <= trow)
        return _flash_step(s_, valid, kv, *carry)

    init = _flash_init(hpg, c)
    _, acc_s = lax.fori_loop(0, n_tiles, sel_body, init)
    o_sel = _normalize(acc_s).reshape(rows, LANES)

    wk = WINDOW + c
    w0 = pl.multiple_of(jnp.maximum(t0 - WINDOW, 0), c)
    kvw = kw_ref[pl.ds(w0, wk), :]
    s = _qk(q4, kvw).reshape(hpg, c, wk)
    wpos = w0 + lax.broadcasted_iota(jnp.int32, (c, wk), 1)
    wvalid = (wpos <= trow) & (wpos > trow - WINDOW)
    _, acc_w = _flash_step(s, wvalid, kvw, *init)
    o_win = _normalize(acc_w).reshape(rows, LANES)

    gates = jax.nn.sigmoid(misc_ref[...])
    outs = []
    for h in range(hpg):
        col = MISC_G + (g * hpg + h) * 3
        sl = slice(h * c, (h + 1) * c)
        outs.append(_lane_pick(gates, col) * o_cmp[sl] + _lane_pick(gates, col + 1) * o_sel[sl]
                    + _lane_pick(gates, col + 2) * o_win[sl])
    blocks = _pack_heads(jnp.concatenate(outs, axis=0), hpg, c)
    for i, blk in enumerate(blocks):
        o_ref[:, i * LANES:(i + 1) * LANES] = blk.astype(BF16)


def _nsa(hb, misc, kcvc, ovl_t, expand, bsz, seq, c, tk):
    t = hb.shape[0]
    g_n = NSA_KV_HEADS
    nch = seq // c
    ncp = kcvc.shape[1]
    nbp = ovl_t.shape[0]
    kern = functools.partial(_nsa_kernel, c=c, tk=tk, seq=seq)
    return pl.pallas_call(
        kern,
        grid=(bsz, g_n, nch),
        in_specs=[
            pl.BlockSpec((c, 2 * LANES), lambda b, g, i: (b * nch + i, g)),
            pl.BlockSpec((c, LANES), lambda b, g, i: (b * nch + i, 0)),
            pl.BlockSpec((1, ncp, LANES), lambda b, g, i: (b * g_n + g, 0, 0)),
            pl.BlockSpec((seq, LANES), lambda b, g, i: (b, BLK_NKV + 2 + g)),
            pl.BlockSpec((seq, LANES), lambda b, g, i: (b, BLK_NKV + 4 + g)),
            pl.BlockSpec((nbp, ncp), lambda b, g, i: (0, 0)),
            pl.BlockSpec((nbp, seq), lambda b, g, i: (0, 0)),
        ],
        out_specs=pl.BlockSpec((c, 2 * LANES), lambda b, g, i: (b * nch + i, g)),
        out_shape=jax.ShapeDtypeStruct((t, NSA_HEADS * HEAD_DIM), BF16),
        compiler_params=_cparams(("parallel", "parallel", "arbitrary")),
        name="nsa",
    )(hb, misc, kcvc, hb, hb, ovl_t, expand)


def _nsa_consts(seq):
    nb = seq // SEL_BLOCK
    nbp = max(LANES, nb)
    rows = seq // CMP_STRIDE
    nc = (seq - CMP_LEN) // CMP_STRIDE + 1
    j = np.arange(nbp)[:, None]
    n = np.arange(rows)[None, :]
    ovl = np.clip(np.minimum(n * CMP_STRIDE + CMP_LEN, j * SEL_BLOCK + SEL_BLOCK)
                  - np.maximum(n * CMP_STRIDE, j * SEL_BLOCK), 0, None).astype(np.float32) / CMP_LEN
    ovl = np.where((n < nc) & (j < nb), ovl, 0.0)
    s = np.arange(seq)[None, :]
    expand = (s // SEL_BLOCK == j).astype(np.float32)
    return jnp.asarray(ovl, BF16), jnp.asarray(expand, BF16)


INT_MIN = -2147483648


def _lane_pick(x, col):
    lane = lax.broadcasted_iota(jnp.int32, x.shape, 1)
    return jnp.sum(jnp.where(lane == col, x, 0.0), axis=1, keepdims=True)


def _dsa_kernel(q_ref, iq_ref, misc_ref, ik_ref, kv_ref, o_ref, key_scr, *, c, tk, seq, topk):
    ci = pl.program_id(1)
    t0 = ci * c
    n_tiles = (t0 + c + tk - 1) // tk
    tq = t0 + lax.broadcasted_iota(jnp.int32, (1, c), 1)
    krow = lax.broadcasted_iota(jnp.int32, (tk, c), 0)

    lane = lax.broadcasted_iota(jnp.int32, (c, LANES), 1)
    per_blk = LANES // IDX_DIM
    qs = []
    for h in range(IDX_HEADS):
        blk = iq_ref[:, (h // per_blk) * LANES:(h // per_blk + 1) * LANES]
        qs.append(jnp.where(lane // IDX_DIM == h % per_blk, blk, jnp.zeros_like(blk)))
    qst = jnp.concatenate(qs, axis=0)
    w_t = jnp.transpose(misc_ref[...] * (IDX_HEADS ** -0.5 * IDX_DIM ** -0.5))

    def score_body(kt, carry):
        k0 = pl.multiple_of(kt * tk, tk)
        lg = _qk(ik_ref[pl.ds(k0, tk), :], qst)
        sc = jnp.zeros((tk, c), F32)
        for h in range(IDX_HEADS):
            sc = sc + jnp.maximum(lg[:, h * c:(h + 1) * c], 0.0) * w_t[MISC_W + h:MISC_W + h + 1, :]
        sc = sc + 0.0
        sc = jnp.where((k0 + krow) <= tq, sc, -jnp.inf)
        bits = pltpu.bitcast(sc, jnp.int32)
        key_scr[pl.ds(k0, tk), :] = jnp.where(bits < 0, bits ^ jnp.int32(0x7FFFFFFF), bits)
        return carry

    lax.fori_loop(0, n_tiles, score_body, 0)

    n_acc = 8

    def count(pred):
        def body(kt, acc):
            k0 = pl.multiple_of(kt * tk, tk)
            hit = jnp.where(pred(key_scr[pl.ds(k0, tk), :], k0 + krow), 1.0, 0.0)
            return acc + jnp.sum(hit.reshape(tk // (8 * n_acc), n_acc * 8, c), axis=0)
        acc = lax.fori_loop(0, n_tiles, body, jnp.zeros((n_acc * 8, c), F32))
        return jnp.sum(acc, axis=0, keepdims=True)

    def thr_body(i, ans):
        cand = ans | (jnp.int32(1) << (31 - i))
        cand_s = cand ^ jnp.int32(INT_MIN)
        cnt = count(lambda keys, idx: keys >= cand_s)
        return jnp.where(cnt >= topk, cand, ans)

    thr = lax.fori_loop(0, 32, thr_body, jnp.zeros((1, c), jnp.int32)) ^ jnp.int32(INT_MIN)
    need = topk - count(lambda keys, idx: keys > thr)
    n_ties = count(lambda keys, idx: keys == thr)

    n_bits = max(1, int(np.ceil(np.log2(seq))))

    def tie_search():
        def tie_body(i, ans):
            cand = ans | (jnp.int32(1) << (n_bits - 1 - i))
            cnt = count(lambda keys, idx: (keys == thr) & (idx < cand))
            return jnp.where(cnt < need, cand, ans)
        return lax.fori_loop(0, n_bits, tie_body, jnp.zeros((1, c), jnp.int32))

    surplus = jnp.max(n_ties - need)
    last_tie = lax.cond(surplus > 0.0, tie_search, lambda: jnp.full((1, c), seq, jnp.int32))

    q4 = _stack_heads([q_ref[:, :LANES], q_ref[:, LANES:]], HEAD_DIM ** -0.5)
    eye = (lax.broadcasted_iota(jnp.int32, (c, c), 0) == lax.broadcasted_iota(jnp.int32, (c, c), 1)).astype(BF16)

    def att_body(kt, carry):
        k0 = pl.multiple_of(kt * tk, tk)
        kv = kv_ref[pl.ds(k0, tk), :]
        keys = key_scr[pl.ds(k0, tk), :]
        idx = k0 + krow
        picked = (idx <= tq) & ((keys > thr) | ((keys == thr) & (idx <= last_tie)))
        valid = _qk(eye, jnp.where(picked, 1.0, 0.0).astype(BF16)) > 0.5
        s = _qk(q4, kv).reshape(DSA_HEADS, c, tk)
        return _flash_step(s, valid, kv, *carry)

    _, acc = lax.fori_loop(0, n_tiles, att_body, _flash_init(DSA_HEADS, c))
    o = _normalize(acc).reshape(DSA_HEADS * c, LANES)
    for i, blk in enumerate(_pack_heads(o, DSA_HEADS, c)):
        o_ref[:, i * LANES:(i + 1) * LANES] = blk.astype(BF16)


def _dsa(hb, misc, bsz, seq, c, tk):
    t = hb.shape[0]
    nch = seq // c
    topk = min(DSA_TOPK_MAX, seq // 4)
    kern = functools.partial(_dsa_kernel, c=c, tk=tk, seq=seq, topk=topk)
    return pl.pallas_call(
        kern,
        grid=(bsz, nch),
        in_specs=[
            pl.BlockSpec((c, 2 * LANES), lambda b, i: (b * nch + i, BLK_DQ // 2)),
            pl.BlockSpec((c, 2 * LANES), lambda b, i: (b * nch + i, BLK_IQ // 2)),
            pl.BlockSpec((c, LANES), lambda b, i: (b * nch + i, 0)),
            pl.BlockSpec((seq, LANES), lambda b, i: (b, BLK_IK)),
            pl.BlockSpec((seq, LANES), lambda b, i: (b, BLK_DKV)),
        ],
        out_specs=pl.BlockSpec((c, 2 * LANES), lambda b, i: (b * nch + i, 0)),
        out_shape=jax.ShapeDtypeStruct((t, DSA_HEADS * HEAD_DIM), BF16),
        scratch_shapes=[pltpu.VMEM((seq, c), jnp.int32)],
        compiler_params=_cparams(("parallel", "arbitrary")),
        name="dsa",
    )(hb, hb, misc, hb, hb)


def _matmul_kernel(a_ref, w_ref, o_ref):
    o_ref[...] = jnp.dot(a_ref[...].astype(BF16), w_ref[...], preferred_element_type=F32).astype(o_ref.dtype)


def _matmul(a, w, tm, out_dtype):
    m, k = a.shape
    n = w.shape[1]
    return pl.pallas_call(
        _matmul_kernel,
        grid=(m // tm,),
        in_specs=[pl.BlockSpec((tm, k), lambda i: (i, 0)), pl.BlockSpec((k, n), lambda i: (0, 0))],
        out_specs=pl.BlockSpec((tm, n), lambda i: (i, 0)),
        out_shape=jax.ShapeDtypeStruct((m, n), out_dtype),
        compiler_params=_cparams(("parallel",)),
        name="matmul",
    )(a, w)


def _mem_kernel(q_ref, mkv_ref, o_ref, *, c):
    q4 = _stack_heads([q_ref[:, :LANES], q_ref[:, LANES:]], HEAD_DIM ** -0.5)
    outs = []
    for h in range(MEM_HEADS):
        kv = mkv_ref[:, h * LANES:(h + 1) * LANES]
        s = _qk(q4[h * c:(h + 1) * c], kv)
        e = jnp.exp(s - jnp.max(s, axis=-1, keepdims=True))
        p = e / jnp.sum(e, axis=-1, keepdims=True)
        outs.append(jnp.dot(p.astype(BF16), kv, preferred_element_type=F32))
    for i, blk in enumerate(_pack_heads(jnp.concatenate(outs, axis=0), MEM_HEADS, c)):
        o_ref[:, i * LANES:(i + 1) * LANES] = blk.astype(BF16)


def _mem(hb, mkv, bsz, seq, m_len, c):
    t = hb.shape[0]
    nch = seq // c
    return pl.pallas_call(
        functools.partial(_mem_kernel, c=c),
        grid=(bsz, nch),
        in_specs=[
            pl.BlockSpec((c, 2 * LANES), lambda b, i: (b * nch + i, BLK_MQ // 2)),
            pl.BlockSpec((m_len, MEM_HEADS * LANES), lambda b, i: (b, 0)),
        ],
        out_specs=pl.BlockSpec((c, 2 * LANES), lambda b, i: (b * nch + i, 0)),
        out_shape=jax.ShapeDtypeStruct((t, MEM_HEADS * HEAD_DIM), BF16),
        compiler_params=_cparams(("parallel", "parallel")),
        name="mem",
    )(hb, mkv)


def _layer_norm(v, g, b):
    mu = jnp.mean(v, axis=-1, keepdims=True)
    d = v - mu
    var = jnp.mean(d * d, axis=-1, keepdims=True)
    return d * lax.rsqrt(var + LN_EPS) * g + b


def _out_ln_kernel(on_ref, od_ref, om_ref, x_ref, w_ref, g_ref, b_ref, o_ref):
    n0 = on_ref.shape[1]
    n1 = n0 + od_ref.shape[1]
    mix = jnp.dot(on_ref[...], w_ref[:n0, :], preferred_element_type=F32)
    mix = mix + jnp.dot(od_ref[...], w_ref[n0:n1, :], preferred_element_type=F32)
    mix = mix + jnp.dot(om_ref[...], w_ref[n1:, :], preferred_element_type=F32)
    o_ref[...] = _layer_norm(DEEPNORM_ALPHA * x_ref[...] + mix, g_ref[...], b_ref[...])


def _out_ln(o_nsa, o_dsa, o_mem, x2, w_out, g, b, tm):
    t, dm = x2.shape
    row = lambda a: pl.BlockSpec((tm, a.shape[1]), lambda i: (i, 0))
    full = lambda a: pl.BlockSpec(a.shape, lambda i: (0, 0))
    return pl.pallas_call(
        _out_ln_kernel,
        grid=(t // tm,),
        in_specs=[row(o_nsa), row(o_dsa), row(o_mem), row(x2), full(w_out), full(g), full(b)],
        out_specs=pl.BlockSpec((tm, dm), lambda i: (i, 0)),
        out_shape=jax.ShapeDtypeStruct((t, dm), F32),
        compiler_params=_cparams(("parallel",)),
        name="out_ln",
    )(o_nsa, o_dsa, o_mem, x2, w_out, g, b)


NOT_PICKED = 64.0


def _col_max(s):
    parts = [s[i:i + 8] for i in range(0, s.shape[0], 8)]
    while len(parts) > 1:
        nxt = [jnp.maximum(parts[i], parts[i + 1]) for i in range(0, len(parts) - 1, 2)]
        if len(parts) % 2:
            nxt.append(parts[-1])
        parts = nxt
    return jnp.max(parts[0], axis=0, keepdims=True)


def _top_rows(s, k, break_ties):
    n_rows = s.shape[0]
    rows = lax.broadcasted_iota(jnp.int32, s.shape, 0)
    rank = jnp.full(s.shape, NOT_PICKED, F32)
    vals = []
    for r in range(k):
        best = _col_max(s)
        hit = s == best
        if break_ties:
            hit = rows == jnp.min(jnp.where(hit, rows, n_rows), axis=0, keepdims=True)
        rank = jnp.where(hit, float(r), rank)
        s = jnp.where(hit, -jnp.inf, s)
        vals.append(best)
    return rank, vals


def _ranked_exactly(rank, k):
    return jnp.sum(jnp.where(rank < k, 1.0, 0.0), axis=0, keepdims=True) == float(k)


def _peer_select(s1, s2, k, break_ties):
    rank1, v1 = _top_rows(s1, k, break_ties)
    rank2, v2 = _top_rows(s2, k, break_ties)
    v2a = jnp.concatenate(v2, axis=0)
    v2lo = v2a[:8]
    r8 = lax.broadcasted_iota(jnp.int32, v2lo.shape, 0)
    pieces = [v1[0] + v2a, v1[1] + v2lo]
    for a in range(2, 8):
        pieces.append(jnp.where(r8 < k // (a + 1), v1[a] + v2lo, -jnp.inf))
    pieces.append(jnp.concatenate(v1[8:], axis=0) + v2[0])
    cand = jnp.concatenate(pieces, axis=0)
    rank_c, top = _top_rows(cand, k, break_ties)
    return rank1, rank2, rank_c, v1[0], v2[0], jnp.concatenate(top, axis=0)


def _peer_kernel(x1_ref, wqt_ref, k1_ref, k2_ref, u_ref, vt_ref, g_ref, b_ref, o_ref,
                 xt_scr, s1_scr, s2_scr, e1_scr, r2_scr, e2_scr, y_scr, *, tm, eb, nsub):
    j = pl.program_id(1)
    nk = PEER_N_KEYS
    half = PEER_KEY_DIM // 2
    n_lt = tm // LANES
    k = PEER_TOPK
    pack = 16

    @pl.when(j == 0)
    def _select():
        xt = jnp.transpose(x1_ref[...]).astype(BF16)
        xt_scr[...] = xt
        for h in range(PEER_HEADS):
            qh = jnp.dot(wqt_ref[h * PEER_KEY_DIM:(h + 1) * PEER_KEY_DIM, :], xt, preferred_element_type=F32).astype(BF16)
            s1_scr[h] = jnp.dot(k1_ref[...], qh[:half], preferred_element_type=F32)
            s2_scr[h] = jnp.dot(k2_ref[...], qh[half:], preferred_element_type=F32)

        n_par = 4
        per_head = n_lt // n_par

        def chunk(i, carry):
            h = i // per_head
            lanes = [pl.multiple_of(((i % per_head) * n_par + p) * LANES, LANES) for p in range(n_par)]
            s1s = [s1_scr[h, :, pl.ds(l0, LANES)] for l0 in lanes]
            s2s = [s2_scr[h, :, pl.ds(l0, LANES)] for l0 in lanes]
            fast = [_peer_select(s1, s2, k, break_ties=False) for s1, s2 in zip(s1s, s2s)]
            n_unclean = 0.0
            for f in fast:
                clean = _ranked_exactly(f[0], k) & _ranked_exactly(f[1], k) & _ranked_exactly(f[2], k)
                n_unclean = n_unclean + jnp.sum(jnp.where(clean, 0.0, 1.0))
            picks = lax.cond(
                n_unclean > 0.0,
                lambda: [_peer_select(s1, s2, k, break_ties=True) for s1, s2 in zip(s1s, s2s)],
                lambda: fast)
            for l0, s1, s2, (rank1, rank2, rank_c, v1_max, v2_max, top) in zip(lanes, s1s, s2s, picks):
                den = jnp.sum(jnp.exp(top - top[0:1]), axis=0, keepdims=True)
                picked = jnp.where(rank_c < k, 1.0, 0.0)
                n_of_rank = [jnp.sum(picked[0:16], axis=0, keepdims=True)]
                n_of_rank += [jnp.sum(picked[8 * a + 8:8 * a + 16], axis=0, keepdims=True) for a in range(1, 8)]
                n_of_rank += [picked[72 + a:73 + a] for a in range(8)]
                n1 = jnp.zeros_like(s1)
                for a in range(k):
                    n1 = jnp.where(rank1 == float(a), n_of_rank[a], n1)
                s1_scr[h, :, pl.ds(l0, LANES)] = n1
                e1_scr[h, :, pl.ds(l0, LANES)] = jnp.where(rank1 < k, 0.5 * jnp.exp(s1 - v1_max), 0.0)
                r2_scr[h, :, pl.ds(l0, LANES)] = rank2.astype(BF16)
                e2_scr[h, :, pl.ds(l0, LANES)] = (jnp.where(rank2 < k, jnp.exp(s2 - v2_max), 0.0) / den).astype(BF16)
            return carry

        lax.fori_loop(0, PEER_HEADS * per_head, chunk, 0)
        y_scr[...] = jnp.zeros_like(y_scr)

    xt = xt_scr[...]
    y_add = None
    rows_per_step = nsub * eb // nk
    for sub in range(nsub):
        a = jnp.dot(u_ref[sub * eb:(sub + 1) * eb, :], xt, preferred_element_type=F32)
        zs = []
        for ib in range(eb // nk):
            row = sub * (eb // nk) + ib
            grp = pl.multiple_of(j * rows_per_step + (row // 8) * 8, 8)
            zrow = []
            for lt in range(n_lt):
                ls = slice(lt * LANES, (lt + 1) * LANES)
                w = jnp.zeros((nk // pack, pack, LANES), BF16)
                for h in range(PEER_HEADS):
                    n1r = s1_scr[h, pl.ds(grp, 8), ls][row % 8:row % 8 + 1]
                    e1r = e1_scr[h, pl.ds(grp, 8), ls][row % 8:row % 8 + 1]
                    n1b = jnp.broadcast_to(n1r, (pack, LANES)).astype(BF16)[None]
                    e1b = jnp.broadcast_to(e1r, (pack, LANES)).astype(BF16)[None]
                    r2 = r2_scr[h, :, ls].reshape(nk // pack, pack, LANES)
                    e2 = e2_scr[h, :, ls].reshape(nk // pack, pack, LANES)
                    w = w + jnp.where(r2 < n1b, e2, jnp.zeros_like(e2)) * e1b
                x_blk = a[ib * nk:(ib + 1) * nk, ls]
                act = (x_blk * (1.0 + lax.erf(x_blk * np.float32(1.0 / np.sqrt(2.0))))).astype(BF16)
                zrow.append(w.reshape(nk, LANES) * act)
            zs.append(jnp.concatenate(zrow, axis=1))
        z = jnp.concatenate(zs, axis=0)
        y_sub = jnp.dot(vt_ref[:, sub * eb:(sub + 1) * eb], z, preferred_element_type=F32)
        y_add = y_sub if y_add is None else y_add + y_sub
    y_scr[...] += y_add

    @pl.when(j == pl.num_programs(1) - 1)
    def _finish():
        y = jnp.transpose(y_scr[...])
        o_ref[...] = _layer_norm(DEEPNORM_ALPHA * x1_ref[...] + y, g_ref[...], b_ref[...])


def _peer(x1, wqt, k1, k2, u, vt, g, b, tm, eb, nsub):
    t, dm = x1.shape
    n_e = u.shape[0]
    assert (nsub * eb // PEER_N_KEYS) % 8 == 0, "a grid step covers whole 8-row groups of first-key rows"
    full = lambda a: pl.BlockSpec(a.shape, lambda i, j: (0, 0))
    tab = pltpu.VMEM((PEER_HEADS, PEER_N_KEYS, tm), F32)
    tab_b = pltpu.VMEM((PEER_HEADS, PEER_N_KEYS, tm), BF16)
    return pl.pallas_call(
        functools.partial(_peer_kernel, tm=tm, eb=eb, nsub=nsub),
        grid=(t // tm, n_e // (nsub * eb)),
        in_specs=[
            pl.BlockSpec((tm, dm), lambda i, j: (i, 0)),
            full(wqt), full(k1), full(k2),
            pl.BlockSpec((nsub * eb, dm), lambda i, j: (j, 0)),
            pl.BlockSpec((dm, nsub * eb), lambda i, j: (0, j)),
            full(g), full(b),
        ],
        out_specs=pl.BlockSpec((tm, dm), lambda i, j: (i, 0)),
        out_shape=jax.ShapeDtypeStruct((t, dm), F32),
        scratch_shapes=[
            pltpu.VMEM((dm, tm), BF16), tab, tab, tab, tab_b, tab_b,
            pltpu.VMEM((dm, tm), F32),
        ],
        compiler_params=_cparams(("parallel", "arbitrary")),
        name="peer",
    )(x1, wqt, k1, k2, u, vt, g, b)


def _layer(x, mem, positions, w_in, pe_k, pe_v, w1k, w2k, w1v, w2v, w_mem_kv, w_out, ln1_g, ln1_b,
           w_query, sk1, sk2, pu, pv, ln2_g, ln2_b):
    bsz, seq, dm = x.shape
    m_len = mem.shape[1]
    t = bsz * seq
    d = HEAD_DIM
    g_n = NSA_KV_HEADS
    assert seq % 512 == 0 and seq >= WINDOW + 128
    x2 = x.reshape(t, dm)

    hb, misc = _proj(x2, _regroup_w_in(w_in), _rope_tables(positions), tm=512)

    rows = seq // CMP_STRIDE
    cmp = hb[:, BLK_NKV * LANES:(BLK_NKV + g_n) * LANES].reshape(bsz, seq, g_n, 2, d)
    cmp = cmp.transpose(0, 2, 3, 1, 4).reshape(bsz * g_n, 2, rows, CMP_STRIDE * d)
    zpad = jnp.zeros((CMP_HIDDEN, d), BF16)
    pe_rows = lambda pe: jnp.pad(pe.reshape(1, CMP_LEN * d), ((0, 7), (0, 0))).astype(BF16)
    kcvc = _compress(cmp[:, 0], cmp[:, 1], w1k.astype(BF16), w1v.astype(BF16),
                     jnp.concatenate([w2k.astype(BF16), zpad], axis=1), jnp.concatenate([zpad, w2v.astype(BF16)], axis=1),
                     pe_rows(pe_k), pe_rows(pe_v))

    ovl_t, expand = _nsa_consts(seq)
    o_nsa = _nsa(hb, misc, kcvc, ovl_t, expand, bsz, seq, c=256, tk=512)
    o_dsa = _dsa(hb, misc, bsz, seq, c=256, tk=512)

    wm = w_mem_kv.reshape(dm, 2, MEM_HEADS, d).transpose(0, 2, 1, 3).reshape(dm, MEM_HEADS * 2 * d).astype(BF16)
    mkv = _matmul(mem.reshape(bsz * m_len, dm), wm, tm=m_len, out_dtype=BF16)
    o_mem = _mem(hb, mkv, bsz, seq, m_len, c=256)

    x1 = _out_ln(o_nsa, o_dsa, o_mem, x2, w_out.astype(BF16), ln1_g.reshape(1, dm), ln1_b.reshape(1, dm), tm=512)

    x2o = _peer(x1, w_query.T.astype(BF16), sk1.astype(BF16), sk2.astype(BF16), pu.astype(BF16), pv.T.astype(BF16),
                ln2_g.reshape(1, dm), ln2_b.reshape(1, dm), tm=512, eb=1024, nsub=2)
    return x2o.reshape(bsz, seq, dm)


def kernel(x, mem, positions, w_in, nsa_pe_k, nsa_pe_v, nsa_cmp_w1_k, nsa_cmp_w2_k, nsa_cmp_w1_v, nsa_cmp_w2_v,
           w_mem_kv, w_out, ln1_g, ln1_b, peer_w_query, peer_sub_keys_1, peer_sub_keys_2, peer_u, peer_v, ln2_g, ln2_b):
    assert w_in.shape[0] == DEPTH
    return _layer(x, mem, positions, w_in[0], nsa_pe_k[0], nsa_pe_v[0], nsa_cmp_w1_k[0], nsa_cmp_w2_k[0],
                  nsa_cmp_w1_v[0], nsa_cmp_w2_v[0], w_mem_kv[0], w_out[0], ln1_g[0], ln1_b[0], peer_w_query[0],
                  peer_sub_keys_1[0], peer_sub_keys_2[0], peer_u[0], peer_v[0], ln2_g[0], ln2_b[0])
```

```python
import functools

import numpy as np
import jax
import jax.numpy as jnp
from jax import lax
from jax.experimental import pallas as pl
from jax.experimental.pallas import tpu as pltpu

F32 = jnp.float32
BF16 = jnp.bfloat16

LANES = 128
SUBLANES = 8
BF16_ROWS = 16
MIB = 1 << 20

TILE = dict(
    proj=dict(tm=512, vmem=40 * MIB),
    compress=dict(vmem=16 * MIB),
    nsa=dict(c=256, tk=512, vmem=32 * MIB),
    dsa=dict(c=256, tk=512, vmem=32 * MIB),
    mem=dict(c=256, vmem=16 * MIB),
    out_ln=dict(tm=512, vmem=24 * MIB),
    peer=dict(tm=512, eb=1024, nsub=2, vmem=56 * MIB),
)

HEAD_DIM = 64
ROPE_THETA = 500000.0
LN_EPS = 1e-5
NSA_HEADS = 8
NSA_KV_HEADS = 2
CMP_LEN = 32
CMP_STRIDE = 16
CMP_HIDDEN = 128
SEL_BLOCK = 64
SEL_COUNT = 16
WINDOW = 512
FORCE_BONUS = 1e4
DSA_HEADS = 4
IDX_HEADS = 8
IDX_DIM = 32
DSA_TOPK_MAX = 256
MEM_HEADS = 4
PEER_HEADS = 8
PEER_N_KEYS = 128
PEER_KEY_DIM = 256
PEER_TOPK = 16
DEPTH = 1
DEEPNORM_ALPHA = (2.0 * DEPTH) ** 0.25

NEG = -1e30

BLK_QN = 0
BLK_NKV = 4
BLK_DQ = 10
BLK_IQ = 12
BLK_MQ = 14
BLK_DKV = 16
BLK_IK = 17
BLK_MISC = 18
N_BLK = 19
MISC_W = 0
MISC_G = IDX_HEADS


def _cparams(sem, vmem):
    return pltpu.CompilerParams(dimension_semantics=sem, vmem_limit_bytes=vmem)


def _regroup_w_in(w_in):
    d = HEAD_DIM
    o_q = 0
    o_kv = o_q + NSA_HEADS * d
    o_g = o_kv + 6 * NSA_KV_HEADS * d
    o_dq = o_g + 3 * NSA_HEADS
    o_dkv = o_dq + DSA_HEADS * d
    o_iq = o_dkv + 2 * d
    o_ik = o_iq + IDX_HEADS * IDX_DIM
    o_iw = o_ik + IDX_DIM
    o_mq = o_iw + IDX_HEADS
    cols = [w_in[:, o_q:o_kv]]
    for br in range(3):
        for g in range(NSA_KV_HEADS):
            k0 = o_kv + ((2 * br) * NSA_KV_HEADS + g) * d
            v0 = o_kv + ((2 * br + 1) * NSA_KV_HEADS + g) * d
            cols += [w_in[:, k0:k0 + d], w_in[:, v0:v0 + d]]
    cols.append(w_in[:, o_dq:o_dkv])
    cols.append(w_in[:, o_iq:o_ik])
    cols.append(w_in[:, o_mq:o_mq + MEM_HEADS * d])
    cols.append(w_in[:, o_dkv:o_iq])
    cols += [w_in[:, o_ik:o_iw]] * (LANES // IDX_DIM)
    cols += [w_in[:, o_iw:o_mq], w_in[:, o_g:o_dq]]
    pad = LANES - IDX_HEADS - 3 * NSA_HEADS
    cols.append(jnp.zeros((w_in.shape[0], pad), w_in.dtype))
    w = jnp.concatenate(cols, axis=1)
    assert w.shape[1] == N_BLK * LANES
    return w.astype(BF16)


def _rot_half(head_dim):
    return head_dim // 8


def _rope_tables(positions):
    pos = positions.reshape(-1).astype(F32)
    vals, place, ones = [], [], []
    n_src = 2 * (_rot_half(HEAD_DIM) + _rot_half(IDX_DIM))
    src = 0
    for hd in (HEAD_DIM, IDX_DIM):
        half = _rot_half(hd)
        freqs = jnp.power(ROPE_THETA, -jnp.arange(half, dtype=F32) * 2.0 / (2 * half))
        ang = pos[:, None] * freqs
        vals += [jnp.cos(ang), jnp.sin(ang)]
        d = np.arange(LANES) % hd
        cos_src, sin_src = src + d % half, src + half + d % half
        sel = np.zeros((3, n_src, LANES), np.float32)
        sel[0, cos_src[d < 2 * half], np.nonzero(d < 2 * half)[0]] = 1.0
        sel[1, sin_src[d < half], np.nonzero(d < half)[0]] = -1.0
        sel[2, sin_src[(d >= half) & (d < 2 * half)], np.nonzero((d >= half) & (d < 2 * half))[0]] = 1.0
        place += [sel[0], sel[1], sel[2]]
        ones += [(d >= 2 * half).astype(np.float32), np.zeros(LANES, np.float32), np.zeros(LANES, np.float32)]
        src += 2 * half
    cs = jnp.concatenate(vals, axis=1)
    tabs = jnp.dot(cs, jnp.asarray(np.concatenate(place, axis=1)), precision=lax.Precision.HIGHEST)
    return tabs + jnp.asarray(np.concatenate(ones))[None, :]


def _proj_kernel(x_ref, w_ref, tab_ref, hb_ref, misc_ref):
    xb = x_ref[...].astype(BF16)
    tm = xb.shape[0]
    lane = lax.broadcasted_iota(jnp.int32, (tm, LANES), 1)
    lo = lane < HEAD_DIM
    c64, sa64, sb64 = (tab_ref[:, i * LANES:(i + 1) * LANES] for i in range(3))
    c32, sa32, sb32 = (tab_ref[:, i * LANES:(i + 1) * LANES] for i in range(3, 6))
    c64h, sa64h, sb64h = jnp.where(lo, c64, 1.0), jnp.where(lo, sa64, 0.0), jnp.where(lo, sb64, 0.0)

    def rope(h, c, sa, sb, half):
        return h * c + pltpu.roll(h, LANES - half, 1) * sa + pltpu.roll(h, half, 1) * sb

    def finish(blk, h):
        if blk < BLK_NKV or BLK_DQ <= blk < BLK_IQ:
            return rope(h, c64, sa64, sb64, _rot_half(HEAD_DIM))
        if BLK_NKV <= blk < BLK_DQ or blk == BLK_DKV:
            return rope(h, c64h, sa64h, sb64h, _rot_half(HEAD_DIM))
        if BLK_IQ <= blk < BLK_MQ or blk == BLK_IK:
            return rope(h, c32, sa32, sb32, _rot_half(IDX_DIM))
        return h

    for j in range(BLK_MISC // 2):
        h = jnp.dot(xb, w_ref[:, j * 2 * LANES:(j + 1) * 2 * LANES], preferred_element_type=F32)
        for s in range(2):
            blk = 2 * j + s
            hb_ref[:, blk * LANES:(blk + 1) * LANES] = finish(blk, h[:, s * LANES:(s + 1) * LANES]).astype(BF16)
    misc_ref[...] = jnp.dot(xb, w_ref[:, BLK_MISC * LANES:], preferred_element_type=F32)


def _proj(x2, w2, tabs, tm):
    t, dm = x2.shape
    return pl.pallas_call(
        _proj_kernel,
        grid=(t // tm,),
        in_specs=[
            pl.BlockSpec((tm, dm), lambda i: (i, 0)),
            pl.BlockSpec((dm, N_BLK * LANES), lambda i: (0, 0)),
            pl.BlockSpec((tm, 6 * LANES), lambda i: (i, 0)),
        ],
        out_specs=[
            pl.BlockSpec((tm, BLK_MISC * LANES), lambda i: (i, 0)),
            pl.BlockSpec((tm, LANES), lambda i: (i, 0)),
        ],
        out_shape=[
            jax.ShapeDtypeStruct((t, BLK_MISC * LANES), BF16),
            jax.ShapeDtypeStruct((t, LANES), F32),
        ],
        compiler_params=_cparams(("parallel",), TILE["proj"]["vmem"]),
        name="proj",
    )(x2, w2, tabs)


def _gelu(x):
    return 0.5 * x * (1.0 + lax.erf(x * np.float32(1.0 / np.sqrt(2.0))))


def _compress_kernel(kch_ref, vch_ref, w1k_ref, w1v_ref, w2k_ref, w2v_ref, pek_ref, pev_ref, out_ref):
    half = w1k_ref.shape[0] // 2
    rows = kch_ref.shape[1]

    def hidden(ch_ref, w1_ref, pe_ref):
        ch = ch_ref[0]
        a = jnp.dot(ch, w1_ref[:half, :], preferred_element_type=F32)
        b = jnp.dot(ch, w1_ref[half:, :], preferred_element_type=F32)
        bias = jnp.dot(pe_ref[...], w1_ref[...], preferred_element_type=F32)[0:1, :]
        return _gelu(a + pltpu.roll(b, rows - 1, 0) + bias).astype(BF16)

    hk = hidden(kch_ref, w1k_ref, pek_ref)
    hv = hidden(vch_ref, w1v_ref, pev_ref)
    out = jnp.dot(hk, w2k_ref[...], preferred_element_type=F32) + jnp.dot(hv, w2v_ref[...], preferred_element_type=F32)
    out_ref[0] = out.astype(BF16)


def _compress(kch, vch, w1k, w1v, w2k, w2v, pek, pev):
    bg, rows, width = kch.shape
    full = lambda a: pl.BlockSpec(a.shape, lambda i: (0,) * a.ndim)
    return pl.pallas_call(
        _compress_kernel,
        grid=(bg,),
        in_specs=[
            pl.BlockSpec((1, rows, width), lambda i: (i, 0, 0)),
            pl.BlockSpec((1, rows, width), lambda i: (i, 0, 0)),
            full(w1k), full(w1v), full(w2k), full(w2v), full(pek), full(pev),
        ],
        out_specs=pl.BlockSpec((1, rows, LANES), lambda i: (i, 0, 0)),
        out_shape=jax.ShapeDtypeStruct((bg, rows, LANES), BF16),
        compiler_params=_cparams(("parallel",), TILE["compress"]["vmem"]),
        name="compress",
    )(kch, vch, w1k, w1v, w2k, w2v, pek, pev)


def _stack_heads(qpair_refs_or_vals, scale):
    outs = []
    for blk in qpair_refs_or_vals:
        b = blk.astype(F32) * scale
        lane = lax.broadcasted_iota(jnp.int32, b.shape, 1)
        lo = lane < HEAD_DIM
        outs.append(jnp.where(lo, b, 0.0))
        outs.append(jnp.where(lo, pltpu.roll(b, HEAD_DIM, 1), 0.0))
    return jnp.concatenate(outs, axis=0).astype(BF16)


def _qk(q, kv):
    return lax.dot_general(q, kv, (((1,), (1,)), ((), ())), preferred_element_type=F32)


def _lane_pick(x, col):
    lane = lax.broadcasted_iota(jnp.int32, x.shape, 1)
    return jnp.sum(jnp.where(lane == col, x, 0.0), axis=1, keepdims=True)


def _flash_init(n_heads, c):
    return jnp.full((n_heads, c, 1), NEG, F32), jnp.zeros((n_heads, c, LANES), F32)


def _flash_step(s, valid, kv, m, acc):
    n_heads, c, tk = s.shape
    sb = s.astype(BF16) + jnp.where(valid, 0.0, NEG).astype(BF16)
    m_new = jnp.maximum(m, jnp.max(sb, axis=-1, keepdims=True).astype(F32))
    p = jnp.exp(sb - m_new.astype(BF16))
    lane = lax.broadcasted_iota(jnp.int32, kv.shape, 1)
    ones_v = jnp.where(lane < HEAD_DIM, jnp.ones_like(kv), kv)
    pv = jnp.dot(p.reshape(n_heads * c, tk), ones_v, preferred_element_type=F32)
    return m_new, jnp.exp(m - m_new) * acc + pv.reshape(n_heads, c, LANES)


def _normalize(acc):
    den = acc[..., 0:1]
    return acc / jnp.where(den > 0.0, den, 1.0)


def _pack_heads(o, n_heads, c):
    lane = lax.broadcasted_iota(jnp.int32, (c, LANES), 1)
    lo = lane < HEAD_DIM
    blocks = []
    for p in range(n_heads // 2):
        a = o[(2 * p) * c:(2 * p + 1) * c]
        b = o[(2 * p + 1) * c:(2 * p + 2) * c]
        blocks.append(jnp.where(lo, pltpu.roll(a, HEAD_DIM, 1), b))
    return blocks


def _nsa_kernel(q_ref, misc_ref, kc_ref, ks_ref, kw_ref, ovl_ref, exp_ref, o_ref, *, c, tk, seq):
    g = pl.program_id(1)
    ci = pl.program_id(2)
    t0 = ci * c
    hpg = NSA_HEADS // NSA_KV_HEADS
    rows = hpg * c
    scale = HEAD_DIM ** -0.5
    q4 = _stack_heads([q_ref[:, :LANES], q_ref[:, LANES:]], scale)
    trow = t0 + lax.broadcasted_iota(jnp.int32, (c, 1), 0)

    kc = kc_ref[0]
    ncp = kc.shape[0]
    s = _qk(q4, kc).reshape(hpg, c, ncp)
    ncol = lax.broadcasted_iota(jnp.int32, (c, ncp), 1)
    cvalid = (ncol * CMP_STRIDE + (CMP_LEN - 1)) <= trow
    s = jnp.where(cvalid, s, NEG)
    m = jnp.max(s, axis=-1, keepdims=True)
    e = jnp.where(cvalid, jnp.exp(s - m), 0.0)
    den = jnp.sum(e, axis=-1, keepdims=True)
    p = (e / jnp.where(den > 0.0, den, 1.0)).reshape(rows, ncp)
    pb = p.astype(BF16)
    o_cmp = jnp.dot(pb, kc, preferred_element_type=F32)

    imp_t = lax.dot_general(ovl_ref[...], pb, (((1,), (1,)), ((), ())), preferred_element_type=F32)
    imp = imp_t[:, 0:c]
    for h in range(1, hpg):
        imp = imp + imp_t[:, h * c:(h + 1) * c]
    nb = imp.shape[0]
    jrow = lax.broadcasted_iota(jnp.int32, (nb, c), 0)
    tcol = t0 + lax.broadcasted_iota(jnp.int32, (nb, c), 1)
    cur = tcol // SEL_BLOCK
    forced = (jrow == 0) | (jrow == cur) | (jrow == cur - 1)
    visible = (jrow * SEL_BLOCK) <= tcol
    score = jnp.where(visible, imp + jnp.where(forced, FORCE_BONUS, 0.0), -jnp.inf)
    n_pick = min(SEL_COUNT, seq // SEL_BLOCK)
    rank, _ = _top_rows(score, n_pick, break_ties=True)
    sel_b = jnp.transpose(jnp.where(rank < n_pick, 1.0, 0.0)).astype(BF16)

    n_tiles = (t0 + c + tk - 1) // tk
    kcol = lax.broadcasted_iota(jnp.int32, (c, tk), 1)

    def sel_body(kt, carry):
        k0 = pl.multiple_of(kt * tk, tk)
        kv = ks_ref[pl.ds(k0, tk), :]
        s_ = _qk(q4, kv).reshape(hpg, c, tk)
        tok = jnp.dot(sel_b, exp_ref[:, pl.ds(k0, tk)], preferred_element_type=F32)
        valid = (tok > 0.5) & ((k0 + kcol) <= trow)
        return _flash_step(s_, valid, kv, *carry)

    init = _flash_init(hpg, c)
    _, acc_s = lax.fori_loop(0, n_tiles, sel_body, init)
    o_sel = _normalize(acc_s).reshape(rows, LANES)

    wk = WINDOW + c
    w0 = pl.multiple_of(jnp.maximum(t0 - WINDOW, 0), c)
    kvw = kw_ref[pl.ds(w0, wk), :]
    s = _qk(q4, kvw).reshape(hpg, c, wk)
    wpos = w0 + lax.broadcasted_iota(jnp.int32, (c, wk), 1)
    wvalid = (wpos <= trow) & (wpos > trow - WINDOW)
    _, acc_w = _flash_step(s, wvalid, kvw, *init)
    o_win = _normalize(acc_w).reshape(rows, LANES)

    gates = jax.nn.sigmoid(misc_ref[...])
    outs = []
    for h in range(hpg):
        col = MISC_G + (g * hpg + h) * 3
        sl = slice(h * c, (h + 1) * c)
        outs.append(_lane_pick(gates, col) * o_cmp[sl] + _lane_pick(gates, col + 1) * o_sel[sl]
                    + _lane_pick(gates, col + 2) * o_win[sl])
    blocks = _pack_heads(jnp.concatenate(outs, axis=0), hpg, c)
    for i, blk in enumerate(blocks):
        o_ref[:, i * LANES:(i + 1) * LANES] = blk.astype(BF16)


def _nsa(hb, misc, kcvc, ovl_t, expand, bsz, seq, c, tk):
    t = hb.shape[0]
    g_n = NSA_KV_HEADS
    nch = seq // c
    ncp = kcvc.shape[1]
    nbp = ovl_t.shape[0]
    kern = functools.partial(_nsa_kernel, c=c, tk=tk, seq=seq)
    return pl.pallas_call(
        kern,
        grid=(bsz, g_n, nch),
        in_specs=[
            pl.BlockSpec((c, 2 * LANES), lambda b, g, i: (b * nch + i, g)),
            pl.BlockSpec((c, LANES), lambda b, g, i: (b * nch + i, 0)),
            pl.BlockSpec((1, ncp, LANES), lambda b, g, i: (b * g_n + g, 0, 0)),
            pl.BlockSpec((seq, LANES), lambda b, g, i: (b, BLK_NKV + 2 + g)),
            pl.BlockSpec((seq, LANES), lambda b, g, i: (b, BLK_NKV + 4 + g)),
            pl.BlockSpec((nbp, ncp), lambda b, g, i: (0, 0)),
            pl.BlockSpec((nbp, seq), lambda b, g, i: (0, 0)),
        ],
        out_specs=pl.BlockSpec((c, 2 * LANES), lambda b, g, i: (b * nch + i, g)),
        out_shape=jax.ShapeDtypeStruct((t, NSA_HEADS * HEAD_DIM), BF16),
        compiler_params=_cparams(("parallel", "parallel", "arbitrary"), TILE["nsa"]["vmem"]),
        name="nsa",
    )(hb, misc, kcvc, hb, hb, ovl_t, expand)


def _nsa_consts(seq):
    nb = seq // SEL_BLOCK
    nbp = max(LANES, nb)
    rows = seq // CMP_STRIDE
    nc = (seq - CMP_LEN) // CMP_STRIDE + 1
    j = np.arange(nbp)[:, None]
    n = np.arange(rows)[None, :]
    ovl = np.clip(np.minimum(n * CMP_STRIDE + CMP_LEN, j * SEL_BLOCK + SEL_BLOCK)
                  - np.maximum(n * CMP_STRIDE, j * SEL_BLOCK), 0, None).astype(np.float32) / CMP_LEN
    ovl = np.where((n < nc) & (j < nb), ovl, 0.0)
    s = np.arange(seq)[None, :]
    expand = (s // SEL_BLOCK == j).astype(np.float32)
    return jnp.asarray(ovl, BF16), jnp.asarray(expand, BF16)


INT_MIN = -2147483648


def _dsa_kernel(q_ref, iq_ref, misc_ref, ik_ref, kv_ref, o_ref, key_scr, *, c, tk, seq, topk):
    ci = pl.program_id(1)
    t0 = ci * c
    n_tiles = (t0 + c + tk - 1) // tk
    tq = t0 + lax.broadcasted_iota(jnp.int32, (1, c), 1)
    krow = lax.broadcasted_iota(jnp.int32, (tk, c), 0)

    lane = lax.broadcasted_iota(jnp.int32, (c, LANES), 1)
    per_blk = LANES // IDX_DIM
    qs = []
    for h in range(IDX_HEADS):
        blk = iq_ref[:, (h // per_blk) * LANES:(h // per_blk + 1) * LANES]
        qs.append(jnp.where(lane // IDX_DIM == h % per_blk, blk, jnp.zeros_like(blk)))
    qst = jnp.concatenate(qs, axis=0)
    w_t = jnp.transpose(misc_ref[...] * (IDX_HEADS ** -0.5 * IDX_DIM ** -0.5))

    def score_body(kt, carry):
        k0 = pl.multiple_of(kt * tk, tk)
        lg = _qk(ik_ref[pl.ds(k0, tk), :], qst)
        sc = jnp.zeros((tk, c), F32)
        for h in range(IDX_HEADS):
            sc = sc + jnp.maximum(lg[:, h * c:(h + 1) * c], 0.0) * w_t[MISC_W + h:MISC_W + h + 1, :]
        sc = sc + 0.0
        sc = jnp.where((k0 + krow) <= tq, sc, -jnp.inf)
        bits = pltpu.bitcast(sc, jnp.int32)
        key_scr[pl.ds(k0, tk), :] = jnp.where(bits < 0, bits ^ jnp.int32(0x7FFFFFFF), bits)
        return carry

    lax.fori_loop(0, n_tiles, score_body, 0)

    n_acc = 8

    def count(pred):
        def body(kt, acc):
            k0 = pl.multiple_of(kt * tk, tk)
            hit = jnp.where(pred(key_scr[pl.ds(k0, tk), :], k0 + krow), 1.0, 0.0)
            return acc + jnp.sum(hit.reshape(tk // (8 * n_acc), n_acc * 8, c), axis=0)
        acc = lax.fori_loop(0, n_tiles, body, jnp.zeros((n_acc * 8, c), F32))
        return jnp.sum(acc, axis=0, keepdims=True)

    def thr_body(i, ans):
        cand = ans | (jnp.int32(1) << (31 - i))
        cand_s = cand ^ jnp.int32(INT_MIN)
        cnt = count(lambda keys, idx: keys >= cand_s)
        return jnp.where(cnt >= topk, cand, ans)

    thr = lax.fori_loop(0, 32, thr_body, jnp.zeros((1, c), jnp.int32)) ^ jnp.int32(INT_MIN)
    need = topk - count(lambda keys, idx: keys > thr)
    n_ties = count(lambda keys, idx: keys == thr)

    n_bits = max(1, int(np.ceil(np.log2(seq))))

    def tie_search():
        def tie_body(i, ans):
            cand = ans | (jnp.int32(1) << (n_bits - 1 - i))
            cnt = count(lambda keys, idx: (keys == thr) & (idx < cand))
            return jnp.where(cnt < need, cand, ans)
        return lax.fori_loop(0, n_bits, tie_body, jnp.zeros((1, c), jnp.int32))

    surplus = jnp.max(n_ties - need)
    last_tie = lax.cond(surplus > 0.0, tie_search, lambda: jnp.full((1, c), seq, jnp.int32))

    q4 = _stack_heads([q_ref[:, :LANES], q_ref[:, LANES:]], HEAD_DIM ** -0.5)
    eye = (lax.broadcasted_iota(jnp.int32, (c, c), 0) == lax.broadcasted_iota(jnp.int32, (c, c), 1)).astype(BF16)

    def att_body(kt, carry):
        k0 = pl.multiple_of(kt * tk, tk)
        kv = kv_ref[pl.ds(k0, tk), :]
        keys = key_scr[pl.ds(k0, tk), :]
        idx = k0 + krow
        picked = (idx <= tq) & ((keys > thr) | ((keys == thr) & (idx <= last_tie)))
        valid = _qk(eye, jnp.where(picked, 1.0, 0.0).astype(BF16)) > 0.5
        s = _qk(q4, kv).reshape(DSA_HEADS, c, tk)
        return _flash_step(s, valid, kv, *carry)

    _, acc = lax.fori_loop(0, n_tiles, att_body, _flash_init(DSA_HEADS, c))
    o = _normalize(acc).reshape(DSA_HEADS * c, LANES)
    for i, blk in enumerate(_pack_heads(o, DSA_HEADS, c)):
        o_ref[:, i * LANES:(i + 1) * LANES] = blk.astype(BF16)


def _dsa(hb, misc, bsz, seq, c, tk):
    t = hb.shape[0]
    nch = seq // c
    topk = min(DSA_TOPK_MAX, seq // 4)
    kern = functools.partial(_dsa_kernel, c=c, tk=tk, seq=seq, topk=topk)
    return pl.pallas_call(
        kern,
        grid=(bsz, nch),
        in_specs=[
            pl.BlockSpec((c, 2 * LANES), lambda b, i: (b * nch + i, BLK_DQ // 2)),
            pl.BlockSpec((c, 2 * LANES), lambda b, i: (b * nch + i, BLK_IQ // 2)),
            pl.BlockSpec((c, LANES), lambda b, i: (b * nch + i, 0)),
            pl.BlockSpec((seq, LANES), lambda b, i: (b, BLK_IK)),
            pl.BlockSpec((seq, LANES), lambda b, i: (b, BLK_DKV)),
        ],
        out_specs=pl.BlockSpec((c, 2 * LANES), lambda b, i: (b * nch + i, 0)),
        out_shape=jax.ShapeDtypeStruct((t, DSA_HEADS * HEAD_DIM), BF16),
        scratch_shapes=[pltpu.VMEM((seq, c), jnp.int32)],
        compiler_params=_cparams(("parallel", "arbitrary"), TILE["dsa"]["vmem"]),
        name="dsa",
    )(hb, hb, misc, hb, hb)


def _matmul_kernel(a_ref, w_ref, o_ref):
    o_ref[...] = jnp.dot(a_ref[...].astype(BF16), w_ref[...], preferred_element_type=F32).astype(o_ref.dtype)


def _matmul(a, w, tm, out_dtype):
    m, k = a.shape
    n = w.shape[1]
    return pl.pallas_call(
        _matmul_kernel,
        grid=(m // tm,),
        in_specs=[pl.BlockSpec((tm, k), lambda i: (i, 0)), pl.BlockSpec((k, n), lambda i: (0, 0))],
        out_specs=pl.BlockSpec((tm, n), lambda i: (i, 0)),
        out_shape=jax.ShapeDtypeStruct((m, n), out_dtype),
        compiler_params=_cparams(("parallel",), TILE["mem"]["vmem"]),
        name="matmul",
    )(a, w)


def _mem_kernel(q_ref, mkv_ref, o_ref, *, c):
    q4 = _stack_heads([q_ref[:, :LANES], q_ref[:, LANES:]], HEAD_DIM ** -0.5)
    outs = []
    for h in range(MEM_HEADS):
        kv = mkv_ref[:, h * LANES:(h + 1) * LANES]
        s = _qk(q4[h * c:(h + 1) * c], kv)
        e = jnp.exp(s - jnp.max(s, axis=-1, keepdims=True))
        p = e / jnp.sum(e, axis=-1, keepdims=True)
        outs.append(jnp.dot(p.astype(BF16), kv, preferred_element_type=F32))
    for i, blk in enumerate(_pack_heads(jnp.concatenate(outs, axis=0), MEM_HEADS, c)):
        o_ref[:, i * LANES:(i + 1) * LANES] = blk.astype(BF16)


def _mem(hb, mkv, bsz, seq, m_len, c):
    t = hb.shape[0]
    nch = seq // c
    return pl.pallas_call(
        functools.partial(_mem_kernel, c=c),
        grid=(bsz, nch),
        in_specs=[
            pl.BlockSpec((c, 2 * LANES), lambda b, i: (b * nch + i, BLK_MQ // 2)),
            pl.BlockSpec((m_len, MEM_HEADS * LANES), lambda b, i: (b, 0)),
        ],
        out_specs=pl.BlockSpec((c, 2 * LANES), lambda b, i: (b * nch + i, 0)),
        out_shape=jax.ShapeDtypeStruct((t, MEM_HEADS * HEAD_DIM), BF16),
        compiler_params=_cparams(("parallel", "parallel"), TILE["mem"]["vmem"]),
        name="mem",
    )(hb, mkv)


def _layer_norm(v, g, b):
    mu = jnp.mean(v, axis=-1, keepdims=True)
    d = v - mu
    var = jnp.mean(d * d, axis=-1, keepdims=True)
    return d * lax.rsqrt(var + LN_EPS) * g + b


def _out_ln_kernel(on_ref, od_ref, om_ref, x_ref, w_ref, g_ref, b_ref, o_ref):
    n0 = on_ref.shape[1]
    n1 = n0 + od_ref.shape[1]
    mix = jnp.dot(on_ref[...], w_ref[:n0, :], preferred_element_type=F32)
    mix = mix + jnp.dot(od_ref[...], w_ref[n0:n1, :], preferred_element_type=F32)
    mix = mix + jnp.dot(om_ref[...], w_ref[n1:, :], preferred_element_type=F32)
    o_ref[...] = _layer_norm(DEEPNORM_ALPHA * x_ref[...] + mix, g_ref[...], b_ref[...])


def _out_ln(o_nsa, o_dsa, o_mem, x2, w_out, g, b, tm):
    t, dm = x2.shape
    row = lambda a: pl.BlockSpec((tm, a.shape[1]), lambda i: (i, 0))
    full = lambda a: pl.BlockSpec(a.shape, lambda i: (0, 0))
    return pl.pallas_call(
        _out_ln_kernel,
        grid=(t // tm,),
        in_specs=[row(o_nsa), row(o_dsa), row(o_mem), row(x2), full(w_out), full(g), full(b)],
        out_specs=pl.BlockSpec((tm, dm), lambda i: (i, 0)),
        out_shape=jax.ShapeDtypeStruct((t, dm), F32),
        compiler_params=_cparams(("parallel",), TILE["out_ln"]["vmem"]),
        name="out_ln",
    )(o_nsa, o_dsa, o_mem, x2, w_out, g, b)


NOT_PICKED = 64.0


def _top_rows(s, k, break_ties):
    n_rows = s.shape[0]
    rows = lax.broadcasted_iota(jnp.int32, s.shape, 0)
    rank = jnp.full(s.shape, NOT_PICKED, F32)
    vals = []
    for r in range(k):
        best = jnp.max(s, axis=0, keepdims=True)
        hit = s == best
        if break_ties:
            hit = rows == jnp.min(jnp.where(hit, rows, n_rows), axis=0, keepdims=True)
        rank = jnp.where(hit, float(r), rank)
        s = jnp.where(hit, -jnp.inf, s)
        vals.append(best)
    return rank, vals


def _ranked_exactly(rank, k):
    return jnp.sum(jnp.where(rank < k, 1.0, 0.0), axis=0, keepdims=True) == float(k)


def _peer_select(s1, s2, k, break_ties):
    rank1, v1 = _top_rows(s1, k, break_ties)
    rank2, v2 = _top_rows(s2, k, break_ties)
    v2a = jnp.concatenate(v2, axis=0)
    v2lo = v2a[:8]
    r8 = lax.broadcasted_iota(jnp.int32, v2lo.shape, 0)
    pieces = [v1[0] + v2a, v1[1] + v2lo]
    for a in range(2, 8):
        pieces.append(jnp.where(r8 < k // (a + 1), v1[a] + v2lo, -jnp.inf))
    pieces.append(jnp.concatenate(v1[8:], axis=0) + v2[0])
    cand = jnp.concatenate(pieces, axis=0)
    rank_c, top = _top_rows(cand, k, break_ties)
    return rank1, rank2, rank_c, v1[0], v2[0], jnp.concatenate(top, axis=0)


def _peer_kernel(x1_ref, wqt_ref, k1_ref, k2_ref, u_ref, vt_ref, g_ref, b_ref, o_ref,
                 xt_scr, s1_scr, s2_scr, e1_scr, r2_scr, e2_scr, y_scr, *, tm, eb, nsub):
    j = pl.program_id(1)
    nk = PEER_N_KEYS
    half = PEER_KEY_DIM // 2
    n_lt = tm // LANES
    k = PEER_TOPK
    pack = BF16_ROWS

    @pl.when(j == 0)
    def _select():
        xt = jnp.transpose(x1_ref[...]).astype(BF16)
        xt_scr[...] = xt
        for h in range(PEER_HEADS):
            qh = jnp.dot(wqt_ref[h * PEER_KEY_DIM:(h + 1) * PEER_KEY_DIM, :], xt, preferred_element_type=F32).astype(BF16)
            s1_scr[h] = jnp.dot(k1_ref[...], qh[:half], preferred_element_type=F32)
            s2_scr[h] = jnp.dot(k2_ref[...], qh[half:], preferred_element_type=F32)

        n_par = 4
        per_head = n_lt // n_par

        def chunk(i, carry):
            h = i // per_head
            lanes = [pl.multiple_of(((i % per_head) * n_par + p) * LANES, LANES) for p in range(n_par)]
            s1s = [s1_scr[h, :, pl.ds(l0, LANES)] for l0 in lanes]
            s2s = [s2_scr[h, :, pl.ds(l0, LANES)] for l0 in lanes]
            fast = [_peer_select(s1, s2, k, break_ties=False) for s1, s2 in zip(s1s, s2s)]
            n_unclean = 0.0
            for f in fast:
                clean = _ranked_exactly(f[0], k) & _ranked_exactly(f[1], k) & _ranked_exactly(f[2], k)
                n_unclean = n_unclean + jnp.sum(jnp.where(clean, 0.0, 1.0))
            picks = lax.cond(
                n_unclean > 0.0,
                lambda: [_peer_select(s1, s2, k, break_ties=True) for s1, s2 in zip(s1s, s2s)],
                lambda: fast)
            for l0, s1, s2, (rank1, rank2, rank_c, v1_max, v2_max, top) in zip(lanes, s1s, s2s, picks):
                den = jnp.sum(jnp.exp(top - top[0:1]), axis=0, keepdims=True)
                picked = jnp.where(rank_c < k, 1.0, 0.0)
                n_of_rank = [jnp.sum(picked[0:16], axis=0, keepdims=True)]
                n_of_rank += [jnp.sum(picked[8 * a + 8:8 * a + 16], axis=0, keepdims=True) for a in range(1, 8)]
                n_of_rank += [picked[72 + a:73 + a] for a in range(8)]
                n1 = jnp.zeros_like(s1)
                for a in range(k):
                    n1 = jnp.where(rank1 == float(a), n_of_rank[a], n1)
                s1_scr[h, :, pl.ds(l0, LANES)] = n1
                e1_scr[h, :, pl.ds(l0, LANES)] = jnp.where(rank1 < k, 0.5 * jnp.exp(s1 - v1_max), 0.0)
                r2_scr[h, :, pl.ds(l0, LANES)] = rank2.astype(BF16)
                e2_scr[h, :, pl.ds(l0, LANES)] = (jnp.where(rank2 < k, jnp.exp(s2 - v2_max), 0.0) / den).astype(BF16)
            return carry

        lax.fori_loop(0, PEER_HEADS * per_head, chunk, 0)
        y_scr[...] = jnp.zeros_like(y_scr)

    xt = xt_scr[...]
    y_add = None
    rows_per_step = nsub * eb // nk
    for sub in range(nsub):
        a = jnp.dot(u_ref[sub * eb:(sub + 1) * eb, :], xt, preferred_element_type=F32)
        zs = []
        for ib in range(eb // nk):
            row = sub * (eb // nk) + ib
            grp = pl.multiple_of(j * rows_per_step + (row // SUBLANES) * SUBLANES, SUBLANES)
            r_in = row % SUBLANES
            zrow = []
            for lt in range(n_lt):
                ls = slice(lt * LANES, (lt + 1) * LANES)
                w = jnp.zeros((nk // pack, pack, LANES), BF16)
                for h in range(PEER_HEADS):
                    n1r = s1_scr[h, pl.ds(grp, SUBLANES), ls][r_in:r_in + 1]
                    e1r = e1_scr[h, pl.ds(grp, SUBLANES), ls][r_in:r_in + 1]
                    n1b = jnp.broadcast_to(n1r, (pack, LANES)).astype(BF16)[None]
                    e1b = jnp.broadcast_to(e1r, (pack, LANES)).astype(BF16)[None]
                    r2 = r2_scr[h, :, ls].reshape(nk // pack, pack, LANES)
                    e2 = e2_scr[h, :, ls].reshape(nk // pack, pack, LANES)
                    w = w + jnp.where(r2 < n1b, e2, jnp.zeros_like(e2)) * e1b
                x_blk = a[ib * nk:(ib + 1) * nk, ls]
                act = (x_blk * (1.0 + lax.erf(x_blk * np.float32(1.0 / np.sqrt(2.0))))).astype(BF16)
                zrow.append(w.reshape(nk, LANES) * act)
            zs.append(jnp.concatenate(zrow, axis=1))
        z = jnp.concatenate(zs, axis=0)
        y_sub = jnp.dot(vt_ref[:, sub * eb:(sub + 1) * eb], z, preferred_element_type=F32)
        y_add = y_sub if y_add is None else y_add + y_sub
    y_scr[...] += y_add

    @pl.when(j == pl.num_programs(1) - 1)
    def _finish():
        y = jnp.transpose(y_scr[...])
        o_ref[...] = _layer_norm(DEEPNORM_ALPHA * x1_ref[...] + y, g_ref[...], b_ref[...])


def _peer(x1, wqt, k1, k2, u, vt, g, b, tm, eb, nsub):
    t, dm = x1.shape
    n_e = u.shape[0]
    assert (nsub * eb // PEER_N_KEYS) % SUBLANES == 0, "a grid step covers whole sublane groups of first-key rows"
    assert PEER_TOPK == 16 and PEER_N_KEYS == LANES, "the candidate-pair layout in _peer_select is written for k = 16"
    full = lambda a: pl.BlockSpec(a.shape, lambda i, j: (0, 0))
    tab = pltpu.VMEM((PEER_HEADS, PEER_N_KEYS, tm), F32)
    tab_b = pltpu.VMEM((PEER_HEADS, PEER_N_KEYS, tm), BF16)
    return pl.pallas_call(
        functools.partial(_peer_kernel, tm=tm, eb=eb, nsub=nsub),
        grid=(t // tm, n_e // (nsub * eb)),
        in_specs=[
            pl.BlockSpec((tm, dm), lambda i, j: (i, 0)),
            full(wqt), full(k1), full(k2),
            pl.BlockSpec((nsub * eb, dm), lambda i, j: (j, 0)),
            pl.BlockSpec((dm, nsub * eb), lambda i, j: (0, j)),
            full(g), full(b),
        ],
        out_specs=pl.BlockSpec((tm, dm), lambda i, j: (i, 0)),
        out_shape=jax.ShapeDtypeStruct((t, dm), F32),
        scratch_shapes=[
            pltpu.VMEM((dm, tm), BF16), tab, tab, tab, tab_b, tab_b,
            pltpu.VMEM((dm, tm), F32),
        ],
        compiler_params=_cparams(("parallel", "arbitrary"), TILE["peer"]["vmem"]),
        name="peer",
    )(x1, wqt, k1, k2, u, vt, g, b)


def _layer(x, mem, positions, w_in, pe_k, pe_v, w1k, w2k, w1v, w2v, w_mem_kv, w_out, ln1_g, ln1_b,
           w_query, sk1, sk2, pu, pv, ln2_g, ln2_b):
    bsz, seq, dm = x.shape
    m_len = mem.shape[1]
    t = bsz * seq
    d = HEAD_DIM
    g_n = NSA_KV_HEADS
    nsa_t, dsa_t, peer_t = TILE["nsa"], TILE["dsa"], TILE["peer"]
    assert seq % nsa_t["tk"] == 0 and seq % dsa_t["tk"] == 0 and seq >= WINDOW + nsa_t["c"]
    assert dsa_t["tk"] >= min(DSA_TOPK_MAX, seq // 4), "a key tile holds at least top-k candidates"
    assert t % peer_t["tm"] == 0 and t % TILE["proj"]["tm"] == 0 and t % TILE["out_ln"]["tm"] == 0
    x2 = x.reshape(t, dm)

    hb, misc = _proj(x2, _regroup_w_in(w_in), _rope_tables(positions), tm=TILE["proj"]["tm"])

    rows = seq // CMP_STRIDE
    cmp = hb[:, BLK_NKV * LANES:(BLK_NKV + g_n) * LANES].reshape(bsz, seq, g_n, 2, d)
    cmp = cmp.transpose(0, 2, 3, 1, 4).reshape(bsz * g_n, 2, rows, CMP_STRIDE * d)
    zpad = jnp.zeros((CMP_HIDDEN, d), BF16)
    pe_rows = lambda pe: jnp.pad(pe.reshape(1, CMP_LEN * d), ((0, 7), (0, 0))).astype(BF16)
    kcvc = _compress(cmp[:, 0], cmp[:, 1], w1k.astype(BF16), w1v.astype(BF16),
                     jnp.concatenate([w2k.astype(BF16), zpad], axis=1), jnp.concatenate([zpad, w2v.astype(BF16)], axis=1),
                     pe_rows(pe_k), pe_rows(pe_v))

    ovl_t, expand = _nsa_consts(seq)
    o_nsa = _nsa(hb, misc, kcvc, ovl_t, expand, bsz, seq, c=nsa_t["c"], tk=nsa_t["tk"])
    o_dsa = _dsa(hb, misc, bsz, seq, c=dsa_t["c"], tk=dsa_t["tk"])

    wm = w_mem_kv.reshape(dm, 2, MEM_HEADS, d).transpose(0, 2, 1, 3).reshape(dm, MEM_HEADS * 2 * d).astype(BF16)
    mkv = _matmul(mem.reshape(bsz * m_len, dm), wm, tm=m_len, out_dtype=BF16)
    o_mem = _mem(hb, mkv, bsz, seq, m_len, c=TILE["mem"]["c"])

    x1 = _out_ln(o_nsa, o_dsa, o_mem, x2, w_out.astype(BF16), ln1_g.reshape(1, dm), ln1_b.reshape(1, dm),
                 tm=TILE["out_ln"]["tm"])

    x2o = _peer(x1, w_query.T.astype(BF16), sk1.astype(BF16), sk2.astype(BF16), pu.astype(BF16), pv.T.astype(BF16),
                ln2_g.reshape(1, dm), ln2_b.reshape(1, dm), tm=peer_t["tm"], eb=peer_t["eb"], nsub=peer_t["nsub"])
    return x2o.reshape(bsz, seq, dm)


def kernel(x, mem, positions, w_in, nsa_pe_k, nsa_pe_v, nsa_cmp_w1_k, nsa_cmp_w2_k, nsa_cmp_w1_v, nsa_cmp_w2_v,
           w_mem_kv, w_out, ln1_g, ln1_b, peer_w_query, peer_sub_keys_1, peer_sub_keys_2, peer_u, peer_v, ln2_g, ln2_b):
    assert w_in.shape[0] == DEPTH
    return _layer(x, mem, positions, w_in[0], nsa_pe_k[0], nsa_pe_v[0], nsa_cmp_w1_k[0], nsa_cmp_w2_k[0],
                  nsa_cmp_w1_v[0], nsa_cmp_w2_v[0], w_mem_kv[0], w_out[0], ln1_g[0], ln1_b[0], peer_w_query[0],
                  peer_sub_keys_1[0], peer_sub_keys_2[0], peer_u[0], peer_v[0], ln2_g[0], ln2_b[0])
```

```python
import functools

import numpy as np
import jax
import jax.numpy as jnp
from jax import lax
from jax.experimental import pallas as pl
from jax.experimental.pallas import tpu as pltpu

F32 = jnp.float32
BF16 = jnp.bfloat16

LANES = 128
SUBLANES = 8
BF16_ROWS = 16
MIB = 1 << 20

TILE = dict(
    proj=dict(tm=512, vmem=40 * MIB),
    compress=dict(vmem=16 * MIB),
    nsa=dict(c=256, tk=512, vmem=32 * MIB),
    dsa=dict(c=256, tk=512, vmem=32 * MIB),
    mem=dict(c=256, vmem=16 * MIB),
    out_ln=dict(tm=512, vmem=24 * MIB),
    peer=dict(tm=512, eb=1024, nsub=2, vmem=56 * MIB),
)

HEAD_DIM = 64
ROPE_THETA = 500000.0
LN_EPS = 1e-5
NSA_HEADS = 8
NSA_KV_HEADS = 2
CMP_LEN = 32
CMP_STRIDE = 16
CMP_HIDDEN = 128
SEL_BLOCK = 64
SEL_COUNT = 16
WINDOW = 512
FORCE_BONUS = 1e4
DSA_HEADS = 4
IDX_HEADS = 8
IDX_DIM = 32
DSA_TOPK_MAX = 256
MEM_HEADS = 4
PEER_HEADS = 8
PEER_N_KEYS = 128
PEER_KEY_DIM = 256
PEER_TOPK = 16
DEPTH = 1
DEEPNORM_ALPHA = (2.0 * DEPTH) ** 0.25

NEG = -1e30

BLK_QN = 0
BLK_NKV = 4
BLK_DQ = 10
BLK_IQ = 12
BLK_MQ = 14
BLK_DKV = 16
BLK_IK = 17
BLK_MISC = 18
N_BLK = 19
MISC_W = 0
MISC_G = IDX_HEADS


def _cparams(sem, vmem):
    return pltpu.CompilerParams(dimension_semantics=sem, vmem_limit_bytes=vmem)


def _regroup_w_in(w_in):
    d = HEAD_DIM
    o_q = 0
    o_kv = o_q + NSA_HEADS * d
    o_g = o_kv + 6 * NSA_KV_HEADS * d
    o_dq = o_g + 3 * NSA_HEADS
    o_dkv = o_dq + DSA_HEADS * d
    o_iq = o_dkv + 2 * d
    o_ik = o_iq + IDX_HEADS * IDX_DIM
    o_iw = o_ik + IDX_DIM
    o_mq = o_iw + IDX_HEADS
    cols = [w_in[:, o_q:o_kv]]
    for br in range(3):
        for g in range(NSA_KV_HEADS):
            k0 = o_kv + ((2 * br) * NSA_KV_HEADS + g) * d
            v0 = o_kv + ((2 * br + 1) * NSA_KV_HEADS + g) * d
            cols += [w_in[:, k0:k0 + d], w_in[:, v0:v0 + d]]
    cols.append(w_in[:, o_dq:o_dkv])
    cols.append(w_in[:, o_iq:o_ik])
    cols.append(w_in[:, o_mq:o_mq + MEM_HEADS * d])
    cols.append(w_in[:, o_dkv:o_iq])
    cols += [w_in[:, o_ik:o_iw]] * (LANES // IDX_DIM)
    cols += [w_in[:, o_iw:o_mq], w_in[:, o_g:o_dq]]
    pad = LANES - IDX_HEADS - 3 * NSA_HEADS
    cols.append(jnp.zeros((w_in.shape[0], pad), w_in.dtype))
    w = jnp.concatenate(cols, axis=1)
    assert w.shape[1] == N_BLK * LANES
    return w.astype(BF16)


def _rot_half(head_dim):
    return head_dim // 8


def _rope_tables(positions):
    pos = positions.reshape(-1).astype(F32)
    t = pos.shape[0]
    tabs = []
    for hd in (HEAD_DIM, IDX_DIM):
        half = _rot_half(hd)
        freqs = jnp.power(ROPE_THETA, -jnp.arange(half, dtype=F32) * 2.0 / (2 * half))
        ang = pos[:, None] * freqs
        cos, sin = jnp.cos(ang), jnp.sin(ang)
        c = jnp.concatenate([cos, cos, jnp.ones((t, hd - 2 * half), F32)], axis=1)
        sa = jnp.concatenate([-sin, jnp.zeros((t, hd - half), F32)], axis=1)
        sb = jnp.concatenate([jnp.zeros((t, half), F32), sin, jnp.zeros((t, hd - 2 * half), F32)], axis=1)
        rep = LANES // hd
        tabs += [jnp.tile(c, (1, rep)), jnp.tile(sa, (1, rep)), jnp.tile(sb, (1, rep))]
    return jnp.concatenate(tabs, axis=1)


def _proj_kernel(x_ref, w_ref, tab_ref, hb_ref, misc_ref):
    xb = x_ref[...].astype(BF16)
    tm = xb.shape[0]
    lane = lax.broadcasted_iota(jnp.int32, (tm, LANES), 1)
    lo = lane < HEAD_DIM
    c64, sa64, sb64 = (tab_ref[:, i * LANES:(i + 1) * LANES] for i in range(3))
    c32, sa32, sb32 = (tab_ref[:, i * LANES:(i + 1) * LANES] for i in range(3, 6))
    c64h, sa64h, sb64h = jnp.where(lo, c64, 1.0), jnp.where(lo, sa64, 0.0), jnp.where(lo, sb64, 0.0)

    def rope(h, c, sa, sb, half):
        return h * c + pltpu.roll(h, LANES - half, 1) * sa + pltpu.roll(h, half, 1) * sb

    def finish(blk, h):
        if blk < BLK_NKV or BLK_DQ <= blk < BLK_IQ:
            return rope(h, c64, sa64, sb64, _rot_half(HEAD_DIM))
        if BLK_NKV <= blk < BLK_DQ or blk == BLK_DKV:
            return rope(h, c64h, sa64h, sb64h, _rot_half(HEAD_DIM))
        if BLK_IQ <= blk < BLK_MQ or blk == BLK_IK:
            return rope(h, c32, sa32, sb32, _rot_half(IDX_DIM))
        return h

    for j in range(BLK_MISC // 2):
        h = jnp.dot(xb, w_ref[:, j * 2 * LANES:(j + 1) * 2 * LANES], preferred_element_type=F32)
        for s in range(2):
            blk = 2 * j + s
            hb_ref[:, blk * LANES:(blk + 1) * LANES] = finish(blk, h[:, s * LANES:(s + 1) * LANES]).astype(BF16)
    misc_ref[...] = jnp.dot(xb, w_ref[:, BLK_MISC * LANES:], preferred_element_type=F32)


def _proj(x2, w2, tabs, tm):
    t, dm = x2.shape
    return pl.pallas_call(
        _proj_kernel,
        grid=(t // tm,),
        in_specs=[
            pl.BlockSpec((tm, dm), lambda i: (i, 0)),
            pl.BlockSpec((dm, N_BLK * LANES), lambda i: (0, 0)),
            pl.BlockSpec((tm, 6 * LANES), lambda i: (i, 0)),
        ],
        out_specs=[
            pl.BlockSpec((tm, BLK_MISC * LANES), lambda i: (i, 0)),
            pl.BlockSpec((tm, LANES), lambda i: (i, 0)),
        ],
        out_shape=[
            jax.ShapeDtypeStruct((t, BLK_MISC * LANES), BF16),
            jax.ShapeDtypeStruct((t, LANES), F32),
        ],
        compiler_params=_cparams(("parallel",), TILE["proj"]["vmem"]),
        name="proj",
    )(x2, w2, tabs)


def _gelu(x):
    return 0.5 * x * (1.0 + lax.erf(x * np.float32(1.0 / np.sqrt(2.0))))


def _compress_kernel(kch_ref, vch_ref, w1k_ref, w1v_ref, w2k_ref, w2v_ref, pek_ref, pev_ref, out_ref):
    half = w1k_ref.shape[0] // 2
    rows = kch_ref.shape[1]

    def hidden(ch_ref, w1_ref, pe_ref):
        ch = ch_ref[0]
        a = jnp.dot(ch, w1_ref[:half, :], preferred_element_type=F32)
        b = jnp.dot(ch, w1_ref[half:, :], preferred_element_type=F32)
        bias = jnp.dot(pe_ref[...], w1_ref[...], preferred_element_type=F32)[0:1, :]
        return _gelu(a + pltpu.roll(b, rows - 1, 0) + bias).astype(BF16)

    hk = hidden(kch_ref, w1k_ref, pek_ref)
    hv = hidden(vch_ref, w1v_ref, pev_ref)
    out = jnp.dot(hk, w2k_ref[...], preferred_element_type=F32) + jnp.dot(hv, w2v_ref[...], preferred_element_type=F32)
    out_ref[0] = out.astype(BF16)


def _compress(kch, vch, w1k, w1v, w2k, w2v, pek, pev):
    bg, rows, width = kch.shape
    full = lambda a: pl.BlockSpec(a.shape, lambda i: (0,) * a.ndim)
    return pl.pallas_call(
        _compress_kernel,
        grid=(bg,),
        in_specs=[
            pl.BlockSpec((1, rows, width), lambda i: (i, 0, 0)),
            pl.BlockSpec((1, rows, width), lambda i: (i, 0, 0)),
            full(w1k), full(w1v), full(w2k), full(w2v), full(pek), full(pev),
        ],
        out_specs=pl.BlockSpec((1, rows, LANES), lambda i: (i, 0, 0)),
        out_shape=jax.ShapeDtypeStruct((bg, rows, LANES), BF16),
        compiler_params=_cparams(("parallel",), TILE["compress"]["vmem"]),
        name="compress",
    )(kch, vch, w1k, w1v, w2k, w2v, pek, pev)


def _stack_heads(qpair_refs_or_vals, scale):
    outs = []
    for blk in qpair_refs_or_vals:
        b = blk.astype(F32) * scale
        lane = lax.broadcasted_iota(jnp.int32, b.shape, 1)
        lo = lane < HEAD_DIM
        outs.append(jnp.where(lo, b, 0.0))
        outs.append(jnp.where(lo, pltpu.roll(b, HEAD_DIM, 1), 0.0))
    return jnp.concatenate(outs, axis=0).astype(BF16)


def _qk(q, kv):
    return lax.dot_general(q, kv, (((1,), (1,)), ((), ())), preferred_element_type=F32)


def _lane_pick(x, col):
    lane = lax.broadcasted_iota(jnp.int32, x.shape, 1)
    return jnp.sum(jnp.where(lane == col, x, 0.0), axis=1, keepdims=True)


def _flash_init(n_heads, c):
    return jnp.full((n_heads, c, 1), NEG, F32), jnp.zeros((n_heads, c, LANES), F32)


def _flash_step(s, valid, kv, m, acc):
    n_heads, c, tk = s.shape
    sb = s.astype(BF16) + jnp.where(valid, 0.0, NEG).astype(BF16)
    m_new = jnp.maximum(m, jnp.max(sb, axis=-1, keepdims=True).astype(F32))
    p = jnp.exp(sb - m_new.astype(BF16))
    lane = lax.broadcasted_iota(jnp.int32, kv.shape, 1)
    ones_v = jnp.where(lane < HEAD_DIM, jnp.ones_like(kv), kv)
    pv = jnp.dot(p.reshape(n_heads * c, tk), ones_v, preferred_element_type=F32)
    return m_new, jnp.exp(m - m_new) * acc + pv.reshape(n_heads, c, LANES)


def _normalize(acc):
    den = acc[..., 0:1]
    return acc / jnp.where(den > 0.0, den, 1.0)


def _pack_heads(o, n_heads, c):
    lane = lax.broadcasted_iota(jnp.int32, (c, LANES), 1)
    lo = lane < HEAD_DIM
    blocks = []
    for p in range(n_heads // 2):
        a = o[(2 * p) * c:(2 * p + 1) * c]
        b = o[(2 * p + 1) * c:(2 * p + 2) * c]
        blocks.append(jnp.where(lo, pltpu.roll(a, HEAD_DIM, 1), b))
    return blocks


def _nsa_kernel(q_ref, misc_ref, kc_ref, ks_ref, kw_ref, ovl_ref, exp_ref, o_ref, *, c, tk, seq):
    g = pl.program_id(1)
    ci = pl.program_id(2)
    t0 = ci * c
    hpg = NSA_HEADS // NSA_KV_HEADS
    rows = hpg * c
    scale = HEAD_DIM ** -0.5
    q4 = _stack_heads([q_ref[:, :LANES], q_ref[:, LANES:]], scale)
    trow = t0 + lax.broadcasted_iota(jnp.int32, (c, 1), 0)

    kc = kc_ref[0]
    ncp = kc.shape[0]
    s = _qk(q4, kc).reshape(hpg, c, ncp)
    ncol = lax.broadcasted_iota(jnp.int32, (c, ncp), 1)
    cvalid = (ncol * CMP_STRIDE + (CMP_LEN - 1)) <= trow
    s = jnp.where(cvalid, s, NEG)
    m = jnp.max(s, axis=-1, keepdims=True)
    e = jnp.where(cvalid, jnp.exp(s - m), 0.0)
    den = jnp.sum(e, axis=-1, keepdims=True)
    p = (e / jnp.where(den > 0.0, den, 1.0)).reshape(rows, ncp)
    pb = p.astype(BF16)
    o_cmp = jnp.dot(pb, kc, preferred_element_type=F32)

    imp_t = lax.dot_general(ovl_ref[...], pb, (((1,), (1,)), ((), ())), preferred_element_type=F32)
    imp = imp_t[:, 0:c]
    for h in range(1, hpg):
        imp = imp + imp_t[:, h * c:(h + 1) * c]
    nb = imp.shape[0]
    jrow = lax.broadcasted_iota(jnp.int32, (nb, c), 0)
    tcol = t0 + lax.broadcasted_iota(jnp.int32, (nb, c), 1)
    cur = tcol // SEL_BLOCK
    forced = (jrow == 0) | (jrow == cur) | (jrow == cur - 1)
    visible = (jrow * SEL_BLOCK) <= tcol
    score = jnp.where(visible, imp + jnp.where(forced, FORCE_BONUS, 0.0), -jnp.inf)
    n_pick = min(SEL_COUNT, seq // SEL_BLOCK)
    rank, _ = _top_rows(score, n_pick, break_ties=True)
    sel_b = jnp.transpose(jnp.where(rank < n_pick, 1.0, 0.0)).astype(BF16)

    n_tiles = (t0 + c + tk - 1) // tk
    kcol = lax.broadcasted_iota(jnp.int32, (c, tk), 1)

    def sel_body(kt, carry):
        k0 = pl.multiple_of(kt * tk, tk)
        kv = ks_ref[pl.ds(k0, tk), :]
        s_ = _qk(q4, kv).reshape(hpg, c, tk)
        tok = jnp.dot(sel_b, exp_ref[:, pl.ds(k0, tk)], preferred_element_type=F32)
        valid = (tok > 0.5) & ((k0 + kcol) <= trow)
        return _flash_step(s_, valid, kv, *carry)

    init = _flash_init(hpg, c)
    _, acc_s = lax.fori_loop(0, n_tiles, sel_body, init)
    o_sel = _normalize(acc_s).reshape(rows, LANES)

    wk = WINDOW + c
    w0 = pl.multiple_of(jnp.maximum(t0 - WINDOW, 0), c)
    kvw = kw_ref[pl.ds(w0, wk), :]
    s = _qk(q4, kvw).reshape(hpg, c, wk)
    wpos = w0 + lax.broadcasted_iota(jnp.int32, (c, wk), 1)
    wvalid = (wpos <= trow) & (wpos > trow - WINDOW)
    _, acc_w = _flash_step(s, wvalid, kvw, *init)
    o_win = _normalize(acc_w).reshape(rows, LANES)

    gates = jax.nn.sigmoid(misc_ref[...])
    outs = []
    for h in range(hpg):
        col = MISC_G + (g * hpg + h) * 3
        sl = slice(h * c, (h + 1) * c)
        outs.append(_lane_pick(gates, col) * o_cmp[sl] + _lane_pick(gates, col + 1) * o_sel[sl]
                    + _lane_pick(gates, col + 2) * o_win[sl])
    blocks = _pack_heads(jnp.concatenate(outs, axis=0), hpg, c)
    for i, blk in enumerate(blocks):
        o_ref[:, i * LANES:(i + 1) * LANES] = blk.astype(BF16)


def _nsa(hb, misc, kcvc, ovl_t, expand, bsz, seq, c, tk):
    t = hb.shape[0]
    g_n = NSA_KV_HEADS
    nch = seq // c
    ncp = kcvc.shape[1]
    nbp = ovl_t.shape[0]
    kern = functools.partial(_nsa_kernel, c=c, tk=tk, seq=seq)
    return pl.pallas_call(
        kern,
        grid=(bsz, g_n, nch),
        in_specs=[
            pl.BlockSpec((c, 2 * LANES), lambda b, g, i: (b * nch + i, g)),
            pl.BlockSpec((c, LANES), lambda b, g, i: (b * nch + i, 0)),
            pl.BlockSpec((1, ncp, LANES), lambda b, g, i: (b * g_n + g, 0, 0)),
            pl.BlockSpec((seq, LANES), lambda b, g, i: (b, BLK_NKV + 2 + g)),
            pl.BlockSpec((seq, LANES), lambda b, g, i: (b, BLK_NKV + 4 + g)),
            pl.BlockSpec((nbp, ncp), lambda b, g, i: (0, 0)),
            pl.BlockSpec((nbp, seq), lambda b, g, i: (0, 0)),
        ],
        out_specs=pl.BlockSpec((c, 2 * LANES), lambda b, g, i: (b * nch + i, g)),
        out_shape=jax.ShapeDtypeStruct((t, NSA_HEADS * HEAD_DIM), BF16),
        compiler_params=_cparams(("parallel", "parallel", "arbitrary"), TILE["nsa"]["vmem"]),
        name="nsa",
    )(hb, misc, kcvc, hb, hb, ovl_t, expand)


def _nsa_consts(seq):
    nb = seq // SEL_BLOCK
    nbp = max(LANES, nb)
    rows = seq // CMP_STRIDE
    nc = (seq - CMP_LEN) // CMP_STRIDE + 1
    j = np.arange(nbp)[:, None]
    n = np.arange(rows)[None, :]
    ovl = np.clip(np.minimum(n * CMP_STRIDE + CMP_LEN, j * SEL_BLOCK + SEL_BLOCK)
                  - np.maximum(n * CMP_STRIDE, j * SEL_BLOCK), 0, None).astype(np.float32) / CMP_LEN
    ovl = np.where((n < nc) & (j < nb), ovl, 0.0)
    s = np.arange(seq)[None, :]
    expand = (s // SEL_BLOCK == j).astype(np.float32)
    return jnp.asarray(ovl, BF16), jnp.asarray(expand, BF16)


INT_MIN = -2147483648


def _dsa_kernel(q_ref, iq_ref, misc_ref, ik_ref, kv_ref, o_ref, key_scr, *, c, tk, seq, topk):
    ci = pl.program_id(1)
    t0 = ci * c
    n_tiles = (t0 + c + tk - 1) // tk
    tq = t0 + lax.broadcasted_iota(jnp.int32, (1, c), 1)
    krow = lax.broadcasted_iota(jnp.int32, (tk, c), 0)

    lane = lax.broadcasted_iota(jnp.int32, (c, LANES), 1)
    per_blk = LANES // IDX_DIM
    qs = []
    for h in range(IDX_HEADS):
        blk = iq_ref[:, (h // per_blk) * LANES:(h // per_blk + 1) * LANES]
        qs.append(jnp.where(lane // IDX_DIM == h % per_blk, blk, jnp.zeros_like(blk)))
    qst = jnp.concatenate(qs, axis=0)
    w_t = jnp.transpose(misc_ref[...] * (IDX_HEADS ** -0.5 * IDX_DIM ** -0.5))

    def score_body(kt, carry):
        k0 = pl.multiple_of(kt * tk, tk)
        lg = _qk(ik_ref[pl.ds(k0, tk), :], qst)
        sc = jnp.zeros((tk, c), F32)
        for h in range(IDX_HEADS):
            sc = sc + jnp.maximum(lg[:, h * c:(h + 1) * c], 0.0) * w_t[MISC_W + h:MISC_W + h + 1, :]
        sc = sc + 0.0
        sc = jnp.where((k0 + krow) <= tq, sc, -jnp.inf)
        bits = pltpu.bitcast(sc, jnp.int32)
        key_scr[pl.ds(k0, tk), :] = jnp.where(bits < 0, bits ^ jnp.int32(0x7FFFFFFF), bits)
        return carry

    lax.fori_loop(0, n_tiles, score_body, 0)

    n_acc = 8

    def count(pred):
        def body(kt, acc):
            k0 = pl.multiple_of(kt * tk, tk)
            hit = jnp.where(pred(key_scr[pl.ds(k0, tk), :], k0 + krow), 1.0, 0.0)
            return acc + jnp.sum(hit.reshape(tk // (8 * n_acc), n_acc * 8, c), axis=0)
        acc = lax.fori_loop(0, n_tiles, body, jnp.zeros((n_acc * 8, c), F32))
        return jnp.sum(acc, axis=0, keepdims=True)

    def thr_body(i, ans):
        cand = ans | (jnp.int32(1) << (31 - i))
        cand_s = cand ^ jnp.int32(INT_MIN)
        cnt = count(lambda keys, idx: keys >= cand_s)
        return jnp.where(cnt >= topk, cand, ans)

    thr = lax.fori_loop(0, 32, thr_body, jnp.zeros((1, c), jnp.int32)) ^ jnp.int32(INT_MIN)
    need = topk - count(lambda keys, idx: keys > thr)
    n_ties = count(lambda keys, idx: keys == thr)

    n_bits = max(1, int(np.ceil(np.log2(seq))))

    def tie_search():
        def tie_body(i, ans):
            cand = ans | (jnp.int32(1) << (n_bits - 1 - i))
            cnt = count(lambda keys, idx: (keys == thr) & (idx < cand))
            return jnp.where(cnt < need, cand, ans)
        return lax.fori_loop(0, n_bits, tie_body, jnp.zeros((1, c), jnp.int32))

    surplus = jnp.max(n_ties - need)
    last_tie = lax.cond(surplus > 0.0, tie_search, lambda: jnp.full((1, c), seq, jnp.int32))

    q4 = _stack_heads([q_ref[:, :LANES], q_ref[:, LANES:]], HEAD_DIM ** -0.5)
    eye = (lax.broadcasted_iota(jnp.int32, (c, c), 0) == lax.broadcasted_iota(jnp.int32, (c, c), 1)).astype(BF16)

    def att_body(kt, carry):
        k0 = pl.multiple_of(kt * tk, tk)
        kv = kv_ref[pl.ds(k0, tk), :]
        keys = key_scr[pl.ds(k0, tk), :]
        idx = k0 + krow
        picked = (idx <= tq) & ((keys > thr) | ((keys == thr) & (idx <= last_tie)))
        valid = _qk(eye, jnp.where(picked, 1.0, 0.0).astype(BF16)) > 0.5
        s = _qk(q4, kv).reshape(DSA_HEADS, c, tk)
        return _flash_step(s, valid, kv, *carry)

    _, acc = lax.fori_loop(0, n_tiles, att_body, _flash_init(DSA_HEADS, c))
    o = _normalize(acc).reshape(DSA_HEADS * c, LANES)
    for i, blk in enumerate(_pack_heads(o, DSA_HEADS, c)):
        o_ref[:, i * LANES:(i + 1) * LANES] = blk.astype(BF16)


def _dsa(hb, misc, bsz, seq, c, tk):
    t = hb.shape[0]
    nch = seq // c
    topk = min(DSA_TOPK_MAX, seq // 4)
    kern = functools.partial(_dsa_kernel, c=c, tk=tk, seq=seq, topk=topk)
    return pl.pallas_call(
        kern,
        grid=(bsz, nch),
        in_specs=[
            pl.BlockSpec((c, 2 * LANES), lambda b, i: (b * nch + i, BLK_DQ // 2)),
            pl.BlockSpec((c, 2 * LANES), lambda b, i: (b * nch + i, BLK_IQ // 2)),
            pl.BlockSpec((c, LANES), lambda b, i: (b * nch + i, 0)),
            pl.BlockSpec((seq, LANES), lambda b, i: (b, BLK_IK)),
            pl.BlockSpec((seq, LANES), lambda b, i: (b, BLK_DKV)),
        ],
        out_specs=pl.BlockSpec((c, 2 * LANES), lambda b, i: (b * nch + i, 0)),
        out_shape=jax.ShapeDtypeStruct((t, DSA_HEADS * HEAD_DIM), BF16),
        scratch_shapes=[pltpu.VMEM((seq, c), jnp.int32)],
        compiler_params=_cparams(("parallel", "arbitrary"), TILE["dsa"]["vmem"]),
        name="dsa",
    )(hb, hb, misc, hb, hb)


def _matmul_kernel(a_ref, w_ref, o_ref):
    o_ref[...] = jnp.dot(a_ref[...].astype(BF16), w_ref[...], preferred_element_type=F32).astype(o_ref.dtype)


def _matmul(a, w, tm, out_dtype):
    m, k = a.shape
    n = w.shape[1]
    return pl.pallas_call(
        _matmul_kernel,
        grid=(m // tm,),
        in_specs=[pl.BlockSpec((tm, k), lambda i: (i, 0)), pl.BlockSpec((k, n), lambda i: (0, 0))],
        out_specs=pl.BlockSpec((tm, n), lambda i: (i, 0)),
        out_shape=jax.ShapeDtypeStruct((m, n), out_dtype),
        compiler_params=_cparams(("parallel",), TILE["mem"]["vmem"]),
        name="matmul",
    )(a, w)


def _mem_kernel(q_ref, mkv_ref, o_ref, *, c):
    q4 = _stack_heads([q_ref[:, :LANES], q_ref[:, LANES:]], HEAD_DIM ** -0.5)
    outs = []
    for h in range(MEM_HEADS):
        kv = mkv_ref[:, h * LANES:(h + 1) * LANES]
        s = _qk(q4[h * c:(h + 1) * c], kv)
        e = jnp.exp(s - jnp.max(s, axis=-1, keepdims=True))
        p = e / jnp.sum(e, axis=-1, keepdims=True)
        outs.append(jnp.dot(p.astype(BF16), kv, preferred_element_type=F32))
    for i, blk in enumerate(_pack_heads(jnp.concatenate(outs, axis=0), MEM_HEADS, c)):
        o_ref[:, i * LANES:(i + 1) * LANES] = blk.astype(BF16)


def _mem(hb, mkv, bsz, seq, m_len, c):
    t = hb.shape[0]
    nch = seq // c
    return pl.pallas_call(
        functools.partial(_mem_kernel, c=c),
        grid=(bsz, nch),
        in_specs=[
            pl.BlockSpec((c, 2 * LANES), lambda b, i: (b * nch + i, BLK_MQ // 2)),
            pl.BlockSpec((m_len, MEM_HEADS * LANES), lambda b, i: (b, 0)),
        ],
        out_specs=pl.BlockSpec((c, 2 * LANES), lambda b, i: (b * nch + i, 0)),
        out_shape=jax.ShapeDtypeStruct((t, MEM_HEADS * HEAD_DIM), BF16),
        compiler_params=_cparams(("parallel", "parallel"), TILE["mem"]["vmem"]),
        name="mem",
    )(hb, mkv)


def _layer_norm(v, g, b):
    mu = jnp.mean(v, axis=-1, keepdims=True)
    d = v - mu
    var = jnp.mean(d * d, axis=-1, keepdims=True)
    return d * lax.rsqrt(var + LN_EPS) * g + b


def _out_ln_kernel(on_ref, od_ref, om_ref, x_ref, w_ref, g_ref, b_ref, o_ref):
    n0 = on_ref.shape[1]
    n1 = n0 + od_ref.shape[1]
    mix = jnp.dot(on_ref[...], w_ref[:n0, :], preferred_element_type=F32)
    mix = mix + jnp.dot(od_ref[...], w_ref[n0:n1, :], preferred_element_type=F32)
    mix = mix + jnp.dot(om_ref[...], w_ref[n1:, :], preferred_element_type=F32)
    o_ref[...] = _layer_norm(DEEPNORM_ALPHA * x_ref[...] + mix, g_ref[...], b_ref[...])


def _out_ln(o_nsa, o_dsa, o_mem, x2, w_out, g, b, tm):
    t, dm = x2.shape
    row = lambda a: pl.BlockSpec((tm, a.shape[1]), lambda i: (i, 0))
    full = lambda a: pl.BlockSpec(a.shape, lambda i: (0, 0))
    return pl.pallas_call(
        _out_ln_kernel,
        grid=(t // tm,),
        in_specs=[row(o_nsa), row(o_dsa), row(o_mem), row(x2), full(w_out), full(g), full(b)],
        out_specs=pl.BlockSpec((tm, dm), lambda i: (i, 0)),
        out_shape=jax.ShapeDtypeStruct((t, dm), F32),
        compiler_params=_cparams(("parallel",), TILE["out_ln"]["vmem"]),
        name="out_ln",
    )(o_nsa, o_dsa, o_mem, x2, w_out, g, b)


NOT_PICKED = 64.0


def _top_rows(s, k, break_ties):
    n_rows = s.shape[0]
    rows = lax.broadcasted_iota(jnp.int32, s.shape, 0)
    rank = jnp.full(s.shape, NOT_PICKED, F32)
    vals = []
    for r in range(k):
        best = jnp.max(s, axis=0, keepdims=True)
        hit = s == best
        if break_ties:
            hit = rows == jnp.min(jnp.where(hit, rows, n_rows), axis=0, keepdims=True)
        rank = jnp.where(hit, float(r), rank)
        s = jnp.where(hit, -jnp.inf, s)
        vals.append(best)
    return rank, vals


def _ranked_exactly(rank, k):
    return jnp.sum(jnp.where(rank < k, 1.0, 0.0), axis=0, keepdims=True) == float(k)


def _peer_select(s1, s2, k, break_ties):
    rank1, v1 = _top_rows(s1, k, break_ties)
    rank2, v2 = _top_rows(s2, k, break_ties)
    v2a = jnp.concatenate(v2, axis=0)
    v2lo = v2a[:8]
    r8 = lax.broadcasted_iota(jnp.int32, v2lo.shape, 0)
    pieces = [v1[0] + v2a, v1[1] + v2lo]
    for a in range(2, 8):
        pieces.append(jnp.where(r8 < k // (a + 1), v1[a] + v2lo, -jnp.inf))
    pieces.append(jnp.concatenate(v1[8:], axis=0) + v2[0])
    cand = jnp.concatenate(pieces, axis=0)
    rank_c, top = _top_rows(cand, k, break_ties)
    return rank1, rank2, rank_c, v1[0], v2[0], jnp.concatenate(top, axis=0)


def _peer_kernel(x1_ref, wqt_ref, k1_ref, k2_ref, u_ref, vt_ref, g_ref, b_ref, o_ref,
                 xt_scr, s1_scr, s2_scr, e1_scr, r2_scr, e2_scr, y_scr, *, tm, eb, nsub):
    j = pl.program_id(1)
    nk = PEER_N_KEYS
    half = PEER_KEY_DIM // 2
    n_lt = tm // LANES
    k = PEER_TOPK
    pack = BF16_ROWS

    @pl.when(j == 0)
    def _select():
        xt = jnp.transpose(x1_ref[...]).astype(BF16)
        xt_scr[...] = xt
        for h in range(PEER_HEADS):
            qh = jnp.dot(wqt_ref[h * PEER_KEY_DIM:(h + 1) * PEER_KEY_DIM, :], xt, preferred_element_type=F32).astype(BF16)
            s1_scr[h] = jnp.dot(k1_ref[...], qh[:half], preferred_element_type=F32)
            s2_scr[h] = jnp.dot(k2_ref[...], qh[half:], preferred_element_type=F32)

        n_par = 4
        per_head = n_lt // n_par

        def chunk(i, carry):
            h = i // per_head
            lanes = [pl.multiple_of(((i % per_head) * n_par + p) * LANES, LANES) for p in range(n_par)]
            s1s = [s1_scr[h, :, pl.ds(l0, LANES)] for l0 in lanes]
            s2s = [s2_scr[h, :, pl.ds(l0, LANES)] for l0 in lanes]
            fast = [_peer_select(s1, s2, k, break_ties=False) for s1, s2 in zip(s1s, s2s)]
            n_unclean = 0.0
            for f in fast:
                clean = _ranked_exactly(f[0], k) & _ranked_exactly(f[1], k) & _ranked_exactly(f[2], k)
                n_unclean = n_unclean + jnp.sum(jnp.where(clean, 0.0, 1.0))
            picks = lax.cond(
                n_unclean > 0.0,
                lambda: [_peer_select(s1, s2, k, break_ties=True) for s1, s2 in zip(s1s, s2s)],
                lambda: fast)
            for l0, s1, s2, (rank1, rank2, rank_c, v1_max, v2_max, top) in zip(lanes, s1s, s2s, picks):
                den = jnp.sum(jnp.exp(top - top[0:1]), axis=0, keepdims=True)
                picked = jnp.where(rank_c < k, 1.0, 0.0)
                n_of_rank = [jnp.sum(picked[0:16], axis=0, keepdims=True)]
                n_of_rank += [jnp.sum(picked[8 * a + 8:8 * a + 16], axis=0, keepdims=True) for a in range(1, 8)]
                n_of_rank += [picked[72 + a:73 + a] for a in range(8)]
                n1 = jnp.zeros_like(s1)
                for a in range(k):
                    n1 = jnp.where(rank1 == float(a), n_of_rank[a], n1)
                s1_scr[h, :, pl.ds(l0, LANES)] = n1
                e1_scr[h, :, pl.ds(l0, LANES)] = jnp.where(rank1 < k, 0.5 * jnp.exp(s1 - v1_max), 0.0)
                slab = h * n_lt + l0 // LANES
                r2_scr[slab] = rank2.astype(BF16)
                e2_scr[slab] = (jnp.where(rank2 < k, jnp.exp(s2 - v2_max), 0.0) / den).astype(BF16)
            return carry

        lax.fori_loop(0, PEER_HEADS * per_head, chunk, 0)
        y_scr[...] = jnp.zeros_like(y_scr)

    xt = xt_scr[...]
    y_add = None
    rows_per_step = nsub * eb // nk
    for sub in range(nsub):
        a = jnp.dot(u_ref[sub * eb:(sub + 1) * eb, :], xt, preferred_element_type=F32)
        zs = []
        for ib in range(eb // nk):
            row = sub * (eb // nk) + ib
            grp = pl.multiple_of(j * rows_per_step + (row // SUBLANES) * SUBLANES, SUBLANES)
            r_in = row % SUBLANES
            zrow = []
            for lt in range(n_lt):
                ls = slice(lt * LANES, (lt + 1) * LANES)
                w = jnp.zeros((nk // pack, pack, LANES), BF16)
                for h in range(PEER_HEADS):
                    n1r = s1_scr[h, pl.ds(grp, SUBLANES), ls][r_in:r_in + 1]
                    e1r = e1_scr[h, pl.ds(grp, SUBLANES), ls][r_in:r_in + 1]
                    n1b = jnp.broadcast_to(n1r, (pack, LANES)).astype(BF16)[None]
                    e1b = jnp.broadcast_to(e1r, (pack, LANES)).astype(BF16)[None]
                    r2 = r2_scr[h * n_lt + lt].reshape(nk // pack, pack, LANES)
                    e2 = e2_scr[h * n_lt + lt].reshape(nk // pack, pack, LANES)
                    w = w + jnp.where(r2 < n1b, e2, jnp.zeros_like(e2)) * e1b
                x_blk = a[ib * nk:(ib + 1) * nk, ls]
                act = (x_blk * (1.0 + lax.erf(x_blk * np.float32(1.0 / np.sqrt(2.0))))).astype(BF16)
                zrow.append(w.reshape(nk, LANES) * act)
            zs.append(jnp.concatenate(zrow, axis=1))
        z = jnp.concatenate(zs, axis=0)
        y_sub = jnp.dot(vt_ref[:, sub * eb:(sub + 1) * eb], z, preferred_element_type=F32)
        y_add = y_sub if y_add is None else y_add + y_sub
    y_scr[...] += y_add

    @pl.when(j == pl.num_programs(1) - 1)
    def _finish():
        y = jnp.transpose(y_scr[...])
        o_ref[...] = _layer_norm(DEEPNORM_ALPHA * x1_ref[...] + y, g_ref[...], b_ref[...])


def _peer(x1, wqt, k1, k2, u, vt, g, b, tm, eb, nsub):
    t, dm = x1.shape
    n_e = u.shape[0]
    assert (nsub * eb // PEER_N_KEYS) % SUBLANES == 0, "a grid step covers whole sublane groups of first-key rows"
    assert PEER_TOPK == 16 and PEER_N_KEYS == LANES, "the candidate-pair layout in _peer_select is written for k = 16"
    full = lambda a: pl.BlockSpec(a.shape, lambda i, j: (0, 0))
    tab = pltpu.VMEM((PEER_HEADS, PEER_N_KEYS, tm), F32)
    tab_b = pltpu.VMEM((PEER_HEADS * (tm // LANES), PEER_N_KEYS, LANES), BF16)
    return pl.pallas_call(
        functools.partial(_peer_kernel, tm=tm, eb=eb, nsub=nsub),
        grid=(t // tm, n_e // (nsub * eb)),
        in_specs=[
            pl.BlockSpec((tm, dm), lambda i, j: (i, 0)),
            full(wqt), full(k1), full(k2),
            pl.BlockSpec((nsub * eb, dm), lambda i, j: (j, 0)),
            pl.BlockSpec((dm, nsub * eb), lambda i, j: (0, j)),
            full(g), full(b),
        ],
        out_specs=pl.BlockSpec((tm, dm), lambda i, j: (i, 0)),
        out_shape=jax.ShapeDtypeStruct((t, dm), F32),
        scratch_shapes=[
            pltpu.VMEM((dm, tm), BF16), tab, tab, tab, tab_b, tab_b,
            pltpu.VMEM((dm, tm), F32),
        ],
        compiler_params=_cparams(("parallel", "arbitrary"), TILE["peer"]["vmem"]),
        name="peer",
    )(x1, wqt, k1, k2, u, vt, g, b)


def _layer(x, mem, positions, w_in, pe_k, pe_v, w1k, w2k, w1v, w2v, w_mem_kv, w_out, ln1_g, ln1_b,
           w_query, sk1, sk2, pu, pv, ln2_g, ln2_b):
    bsz, seq, dm = x.shape
    m_len = mem.shape[1]
    t = bsz * seq
    d = HEAD_DIM
    g_n = NSA_KV_HEADS
    nsa_t, dsa_t, peer_t = TILE["nsa"], TILE["dsa"], TILE["peer"]
    assert seq % nsa_t["tk"] == 0 and seq % dsa_t["tk"] == 0 and seq >= WINDOW + nsa_t["c"]
    assert dsa_t["tk"] >= min(DSA_TOPK_MAX, seq // 4), "a key tile holds at least top-k candidates"
    assert t % peer_t["tm"] == 0 and t % TILE["proj"]["tm"] == 0 and t % TILE["out_ln"]["tm"] == 0
    x2 = x.reshape(t, dm)

    hb, misc = _proj(x2, _regroup_w_in(w_in), _rope_tables(positions), tm=TILE["proj"]["tm"])

    rows = seq // CMP_STRIDE
    cmp = hb[:, BLK_NKV * LANES:(BLK_NKV + g_n) * LANES].reshape(bsz, seq, g_n, 2, d)
    cmp = cmp.transpose(0, 2, 3, 1, 4).reshape(bsz * g_n, 2, rows, CMP_STRIDE * d)
    zpad = jnp.zeros((CMP_HIDDEN, d), BF16)
    pe_rows = lambda pe: jnp.pad(pe.reshape(1, CMP_LEN * d), ((0, 7), (0, 0))).astype(BF16)
    kcvc = _compress(cmp[:, 0], cmp[:, 1], w1k.astype(BF16), w1v.astype(BF16),
                     jnp.concatenate([w2k.astype(BF16), zpad], axis=1), jnp.concatenate([zpad, w2v.astype(BF16)], axis=1),
                     pe_rows(pe_k), pe_rows(pe_v))

    ovl_t, expand = _nsa_consts(seq)
    o_nsa = _nsa(hb, misc, kcvc, ovl_t, expand, bsz, seq, c=nsa_t["c"], tk=nsa_t["tk"])
    o_dsa = _dsa(hb, misc, bsz, seq, c=dsa_t["c"], tk=dsa_t["tk"])

    wm = w_mem_kv.reshape(dm, 2, MEM_HEADS, d).transpose(0, 2, 1, 3).reshape(dm, MEM_HEADS * 2 * d).astype(BF16)
    mkv = _matmul(mem.reshape(bsz * m_len, dm), wm, tm=m_len, out_dtype=BF16)
    o_mem = _mem(hb, mkv, bsz, seq, m_len, c=TILE["mem"]["c"])

    x1 = _out_ln(o_nsa, o_dsa, o_mem, x2, w_out.astype(BF16), ln1_g.reshape(1, dm), ln1_b.reshape(1, dm),
                 tm=TILE["out_ln"]["tm"])

    x2o = _peer(x1, w_query.T.astype(BF16), sk1.astype(BF16), sk2.astype(BF16), pu.astype(BF16), pv.T.astype(BF16),
                ln2_g.reshape(1, dm), ln2_b.reshape(1, dm), tm=peer_t["tm"], eb=peer_t["eb"], nsub=peer_t["nsub"])
    return x2o.reshape(bsz, seq, dm)


def kernel(x, mem, positions, w_in, nsa_pe_k, nsa_pe_v, nsa_cmp_w1_k, nsa_cmp_w2_k, nsa_cmp_w1_v, nsa_cmp_w2_v,
           w_mem_kv, w_out, ln1_g, ln1_b, peer_w_query, peer_sub_keys_1, peer_sub_keys_2, peer_u, peer_v, ln2_g, ln2_b):
    assert w_in.shape[0] == DEPTH
    return _layer(x, mem, positions, w_in[0], nsa_pe_k[0], nsa_pe_v[0], nsa_cmp_w1_k[0], nsa_cmp_w2_k[0],
                  nsa_cmp_w1_v[0], nsa_cmp_w2_v[0], w_mem_kv[0], w_out[0], ln1_g[0], ln1_b[0], peer_w_query[0],
                  peer_sub_keys_1[0], peer_sub_keys_2[0], peer_u[0], peer_v[0], ln2_g[0], ln2_b[0])
```

```python
import functools

import numpy as np
import jax
import jax.numpy as jnp
from jax import lax
from jax.experimental import pallas as pl
from jax.experimental.pallas import tpu as pltpu

F32 = jnp.float32
BF16 = jnp.bfloat16

LANES = 128
SUBLANES = 8
BF16_ROWS = 16
MIB = 1 << 20

TILE = dict(
    proj=dict(tm=512, vmem=40 * MIB),
    compress=dict(vmem=16 * MIB),
    nsa=dict(c=256, tk=512, vmem=32 * MIB),
    dsa=dict(c=256, tk=512, vmem=32 * MIB),
    mem=dict(vmem=16 * MIB),
    out_ln=dict(tm=512, vmem=24 * MIB),
    peer=dict(tm=512, eb=1024, nsub=2, vmem=56 * MIB),
)

HEAD_DIM = 64
ROPE_THETA = 500000.0
LN_EPS = 1e-5
NSA_HEADS = 8
NSA_KV_HEADS = 2
CMP_LEN = 32
CMP_STRIDE = 16
CMP_HIDDEN = 128
SEL_BLOCK = 64
SEL_COUNT = 16
WINDOW = 512
FORCE_BONUS = 1e4
DSA_HEADS = 4
IDX_HEADS = 8
IDX_DIM = 32
DSA_TOPK_MAX = 256
MEM_HEADS = 4
PEER_HEADS = 8
PEER_N_KEYS = 128
PEER_KEY_DIM = 256
PEER_TOPK = 16
DEPTH = 1
DEEPNORM_ALPHA = (2.0 * DEPTH) ** 0.25

NEG = -1e30

BLK_QN = 0
BLK_NKV = 4
BLK_DQ = 10
BLK_IQ = 12
BLK_MQ = 14
BLK_DKV = 16
BLK_IK = 17
BLK_MISC = 18
N_BLK = 19
MISC_W = 0
MISC_G = IDX_HEADS


def _cparams(sem, vmem):
    return pltpu.CompilerParams(dimension_semantics=sem, vmem_limit_bytes=vmem)


def _regroup_w_in(w_in):
    d = HEAD_DIM
    o_q = 0
    o_kv = o_q + NSA_HEADS * d
    o_g = o_kv + 6 * NSA_KV_HEADS * d
    o_dq = o_g + 3 * NSA_HEADS
    o_dkv = o_dq + DSA_HEADS * d
    o_iq = o_dkv + 2 * d
    o_ik = o_iq + IDX_HEADS * IDX_DIM
    o_iw = o_ik + IDX_DIM
    o_mq = o_iw + IDX_HEADS
    cols = [w_in[:, o_q:o_kv]]
    for br in range(3):
        for g in range(NSA_KV_HEADS):
            k0 = o_kv + ((2 * br) * NSA_KV_HEADS + g) * d
            v0 = o_kv + ((2 * br + 1) * NSA_KV_HEADS + g) * d
            cols += [w_in[:, k0:k0 + d], w_in[:, v0:v0 + d]]
    cols.append(w_in[:, o_dq:o_dkv])
    cols.append(w_in[:, o_iq:o_ik])
    cols.append(w_in[:, o_mq:o_mq + MEM_HEADS * d])
    cols.append(w_in[:, o_dkv:o_iq])
    cols += [w_in[:, o_ik:o_iw]] * (LANES // IDX_DIM)
    cols += [w_in[:, o_iw:o_mq], w_in[:, o_g:o_dq]]
    pad = LANES - IDX_HEADS - 3 * NSA_HEADS
    cols.append(jnp.zeros((w_in.shape[0], pad), w_in.dtype))
    w = jnp.concatenate(cols, axis=1)
    assert w.shape[1] == N_BLK * LANES
    return w.astype(BF16)


def _rot_half(head_dim):
    return head_dim // 8


def _rope_tables(positions):
    pos = positions.reshape(-1).astype(F32)
    t = pos.shape[0]
    tabs = []
    for hd in (HEAD_DIM, IDX_DIM):
        half = _rot_half(hd)
        freqs = jnp.power(ROPE_THETA, -jnp.arange(half, dtype=F32) * 2.0 / (2 * half))
        ang = pos[:, None] * freqs
        cos, sin = jnp.cos(ang), jnp.sin(ang)
        c = jnp.concatenate([cos, cos, jnp.ones((t, hd - 2 * half), F32)], axis=1)
        sa = jnp.concatenate([-sin, jnp.zeros((t, hd - half), F32)], axis=1)
        sb = jnp.concatenate([jnp.zeros((t, half), F32), sin, jnp.zeros((t, hd - 2 * half), F32)], axis=1)
        rep = LANES // hd
        tabs += [jnp.tile(c, (1, rep)), jnp.tile(sa, (1, rep)), jnp.tile(sb, (1, rep))]
    return jnp.concatenate(tabs, axis=1)


def _proj_kernel(x_ref, w_ref, tab_ref, hb_ref, misc_ref):
    xb = x_ref[...].astype(BF16)
    tm = xb.shape[0]
    lane = lax.broadcasted_iota(jnp.int32, (tm, LANES), 1)
    lo = lane < HEAD_DIM
    c64, sa64, sb64 = (tab_ref[:, i * LANES:(i + 1) * LANES] for i in range(3))
    c32, sa32, sb32 = (tab_ref[:, i * LANES:(i + 1) * LANES] for i in range(3, 6))
    c64h, sa64h, sb64h = jnp.where(lo, c64, 1.0), jnp.where(lo, sa64, 0.0), jnp.where(lo, sb64, 0.0)

    def rope(h, c, sa, sb, half):
        return h * c + pltpu.roll(h, LANES - half, 1) * sa + pltpu.roll(h, half, 1) * sb

    def finish(blk, h):
        if blk < BLK_NKV or BLK_DQ <= blk < BLK_IQ:
            return rope(h, c64, sa64, sb64, _rot_half(HEAD_DIM))
        if BLK_NKV <= blk < BLK_DQ or blk == BLK_DKV:
            return rope(h, c64h, sa64h, sb64h, _rot_half(HEAD_DIM))
        if BLK_IQ <= blk < BLK_MQ or blk == BLK_IK:
            return rope(h, c32, sa32, sb32, _rot_half(IDX_DIM))
        return h

    for j in range(BLK_MISC // 2):
        h = jnp.dot(xb, w_ref[:, j * 2 * LANES:(j + 1) * 2 * LANES], preferred_element_type=F32)
        for s in range(2):
            blk = 2 * j + s
            hb_ref[:, blk * LANES:(blk + 1) * LANES] = finish(blk, h[:, s * LANES:(s + 1) * LANES]).astype(BF16)
    misc_ref[...] = jnp.dot(xb, w_ref[:, BLK_MISC * LANES:], preferred_element_type=F32)


def _proj(x2, w2, tabs, tm):
    t, dm = x2.shape
    return pl.pallas_call(
        _proj_kernel,
        grid=(t // tm,),
        in_specs=[
            pl.BlockSpec((tm, dm), lambda i: (i, 0)),
            pl.BlockSpec((dm, N_BLK * LANES), lambda i: (0, 0)),
            pl.BlockSpec((tm, 6 * LANES), lambda i: (i, 0)),
        ],
        out_specs=[
            pl.BlockSpec((tm, BLK_MISC * LANES), lambda i: (i, 0)),
            pl.BlockSpec((tm, LANES), lambda i: (i, 0)),
        ],
        out_shape=[
            jax.ShapeDtypeStruct((t, BLK_MISC * LANES), BF16),
            jax.ShapeDtypeStruct((t, LANES), F32),
        ],
        compiler_params=_cparams(("parallel",), TILE["proj"]["vmem"]),
        name="proj",
    )(x2, w2, tabs)


def _gelu(x):
    return 0.5 * x * (1.0 + lax.erf(x * np.float32(1.0 / np.sqrt(2.0))))


def _compress_kernel(kch_ref, vch_ref, w1k_ref, w1v_ref, w2k_ref, w2v_ref, pek_ref, pev_ref, out_ref):
    half = w1k_ref.shape[0] // 2
    rows = kch_ref.shape[1]

    def hidden(ch_ref, w1_ref, pe_ref):
        ch = ch_ref[0]
        a = jnp.dot(ch, w1_ref[:half, :], preferred_element_type=F32)
        b = jnp.dot(ch, w1_ref[half:, :], preferred_element_type=F32)
        bias = jnp.dot(pe_ref[...], w1_ref[...], preferred_element_type=F32)[0:1, :]
        return _gelu(a + pltpu.roll(b, rows - 1, 0) + bias).astype(BF16)

    hk = hidden(kch_ref, w1k_ref, pek_ref)
    hv = hidden(vch_ref, w1v_ref, pev_ref)
    out = jnp.dot(hk, w2k_ref[...], preferred_element_type=F32) + jnp.dot(hv, w2v_ref[...], preferred_element_type=F32)
    out_ref[0] = out.astype(BF16)


def _compress(kch, vch, w1k, w1v, w2k, w2v, pek, pev):
    bg, rows, width = kch.shape
    full = lambda a: pl.BlockSpec(a.shape, lambda i: (0,) * a.ndim)
    return pl.pallas_call(
        _compress_kernel,
        grid=(bg,),
        in_specs=[
            pl.BlockSpec((1, rows, width), lambda i: (i, 0, 0)),
            pl.BlockSpec((1, rows, width), lambda i: (i, 0, 0)),
            full(w1k), full(w1v), full(w2k), full(w2v), full(pek), full(pev),
        ],
        out_specs=pl.BlockSpec((1, rows, LANES), lambda i: (i, 0, 0)),
        out_shape=jax.ShapeDtypeStruct((bg, rows, LANES), BF16),
        compiler_params=_cparams(("parallel",), TILE["compress"]["vmem"]),
        name="compress",
    )(kch, vch, w1k, w1v, w2k, w2v, pek, pev)


def _stack_heads(qpair_refs_or_vals, scale):
    outs = []
    for blk in qpair_refs_or_vals:
        b = blk.astype(F32) * scale
        lane = lax.broadcasted_iota(jnp.int32, b.shape, 1)
        lo = lane < HEAD_DIM
        outs.append(jnp.where(lo, b, 0.0))
        outs.append(jnp.where(lo, pltpu.roll(b, HEAD_DIM, 1), 0.0))
    return jnp.concatenate(outs, axis=0).astype(BF16)


def _qk(q, kv):
    return lax.dot_general(q, kv, (((1,), (1,)), ((), ())), preferred_element_type=F32)


def _lane_pick(x, col):
    lane = lax.broadcasted_iota(jnp.int32, x.shape, 1)
    return jnp.sum(jnp.where(lane == col, x, 0.0), axis=1, keepdims=True)


def _flash_init(n_heads, c):
    return jnp.full((n_heads, c, 1), NEG, F32), jnp.zeros((n_heads, c, LANES), F32)


def _flash_step(s, valid, kv, m, acc):
    n_heads, c, tk = s.shape
    sb = s.astype(BF16) + jnp.where(valid, 0.0, NEG).astype(BF16)
    m_new = jnp.maximum(m, jnp.max(sb, axis=-1, keepdims=True).astype(F32))
    p = jnp.exp(sb - m_new.astype(BF16))
    lane = lax.broadcasted_iota(jnp.int32, kv.shape, 1)
    ones_v = jnp.where(lane < HEAD_DIM, jnp.ones_like(kv), kv)
    pv = jnp.dot(p.reshape(n_heads * c, tk), ones_v, preferred_element_type=F32)
    return m_new, jnp.exp(m - m_new) * acc + pv.reshape(n_heads, c, LANES)


def _normalize(acc):
    den = acc[..., 0:1]
    return acc / jnp.where(den > 0.0, den, 1.0)


def _pack_heads(o, n_heads, c):
    lane = lax.broadcasted_iota(jnp.int32, (c, LANES), 1)
    lo = lane < HEAD_DIM
    blocks = []
    for p in range(n_heads // 2):
        a = o[(2 * p) * c:(2 * p + 1) * c]
        b = o[(2 * p + 1) * c:(2 * p + 2) * c]
        blocks.append(jnp.where(lo, pltpu.roll(a, HEAD_DIM, 1), b))
    return blocks


def _nsa_kernel(q_ref, misc_ref, kc_ref, ks_ref, kw_ref, ovl_ref, exp_ref, o_ref, *, c, tk, seq):
    g = pl.program_id(1)
    ci = pl.program_id(2)
    t0 = ci * c
    hpg = NSA_HEADS // NSA_KV_HEADS
    rows = hpg * c
    scale = HEAD_DIM ** -0.5
    q4 = _stack_heads([q_ref[:, :LANES], q_ref[:, LANES:]], scale)
    trow = t0 + lax.broadcasted_iota(jnp.int32, (c, 1), 0)

    kc = kc_ref[0]
    ncp = kc.shape[0]
    s = _qk(q4, kc).reshape(hpg, c, ncp)
    ncol = lax.broadcasted_iota(jnp.int32, (c, ncp), 1)
    cvalid = (ncol * CMP_STRIDE + (CMP_LEN - 1)) <= trow
    s = jnp.where(cvalid, s, NEG)
    m = jnp.max(s, axis=-1, keepdims=True)
    e = jnp.where(cvalid, jnp.exp(s - m), 0.0)
    den = jnp.sum(e, axis=-1, keepdims=True)
    p = (e / jnp.where(den > 0.0, den, 1.0)).reshape(rows, ncp)
    pb = p.astype(BF16)
    o_cmp = jnp.dot(pb, kc, preferred_element_type=F32)

    imp_t = lax.dot_general(ovl_ref[...], pb, (((1,), (1,)), ((), ())), preferred_element_type=F32)
    imp = imp_t[:, 0:c]
    for h in range(1, hpg):
        imp = imp + imp_t[:, h * c:(h + 1) * c]
    nb = imp.shape[0]
    jrow = lax.broadcasted_iota(jnp.int32, (nb, c), 0)
    tcol = t0 + lax.broadcasted_iota(jnp.int32, (nb, c), 1)
    cur = tcol // SEL_BLOCK
    forced = (jrow == 0) | (jrow == cur) | (jrow == cur - 1)
    visible = (jrow * SEL_BLOCK) <= tcol
    score = jnp.where(visible, imp + jnp.where(forced, FORCE_BONUS, 0.0), -jnp.inf)
    n_pick = min(SEL_COUNT, seq // SEL_BLOCK)
    rank, _ = _top_rows(score, n_pick, break_ties=True)
    sel_b = jnp.transpose(jnp.where(rank < n_pick, 1.0, 0.0)).astype(BF16)

    n_tiles = (t0 + c + tk - 1) // tk
    kcol = lax.broadcasted_iota(jnp.int32, (c, tk), 1)

    def sel_body(kt, carry):
        k0 = pl.multiple_of(kt * tk, tk)
        kv = ks_ref[pl.ds(k0, tk), :]
        s_ = _qk(q4, kv).reshape(hpg, c, tk)
        tok = jnp.dot(sel_b, exp_ref[:, pl.ds(k0, tk)], preferred_element_type=F32)
        valid = (tok > 0.5) & ((k0 + kcol) <= trow)
        return _flash_step(s_, valid, kv, *carry)

    init = _flash_init(hpg, c)
    _, acc_s = lax.fori_loop(0, n_tiles, sel_body, init)
    o_sel = _normalize(acc_s).reshape(rows, LANES)

    wk = WINDOW + c
    w0 = pl.multiple_of(jnp.maximum(t0 - WINDOW, 0), c)
    kvw = kw_ref[pl.ds(w0, wk), :]
    s = _qk(q4, kvw).reshape(hpg, c, wk)
    wpos = w0 + lax.broadcasted_iota(jnp.int32, (c, wk), 1)
    wvalid = (wpos <= trow) & (wpos > trow - WINDOW)
    _, acc_w = _flash_step(s, wvalid, kvw, *init)
    o_win = _normalize(acc_w).reshape(rows, LANES)

    gates = jax.nn.sigmoid(misc_ref[...])
    outs = []
    for h in range(hpg):
        col = MISC_G + (g * hpg + h) * 3
        sl = slice(h * c, (h + 1) * c)
        outs.append(_lane_pick(gates, col) * o_cmp[sl] + _lane_pick(gates, col + 1) * o_sel[sl]
                    + _lane_pick(gates, col + 2) * o_win[sl])
    blocks = _pack_heads(jnp.concatenate(outs, axis=0), hpg, c)
    for i, blk in enumerate(blocks):
        o_ref[:, i * LANES:(i + 1) * LANES] = blk.astype(BF16)


def _nsa(hb, misc, kcvc, ovl_t, expand, bsz, seq, c, tk):
    t = hb.shape[0]
    g_n = NSA_KV_HEADS
    nch = seq // c
    ncp = kcvc.shape[1]
    nbp = ovl_t.shape[0]
    kern = functools.partial(_nsa_kernel, c=c, tk=tk, seq=seq)
    return pl.pallas_call(
        kern,
        grid=(bsz, g_n, nch),
        in_specs=[
            pl.BlockSpec((c, 2 * LANES), lambda b, g, i: (b * nch + i, g)),
            pl.BlockSpec((c, LANES), lambda b, g, i: (b * nch + i, 0)),
            pl.BlockSpec((1, ncp, LANES), lambda b, g, i: (b * g_n + g, 0, 0)),
            pl.BlockSpec((seq, LANES), lambda b, g, i: (b, BLK_NKV + 2 + g)),
            pl.BlockSpec((seq, LANES), lambda b, g, i: (b, BLK_NKV + 4 + g)),
            pl.BlockSpec((nbp, ncp), lambda b, g, i: (0, 0)),
            pl.BlockSpec((nbp, seq), lambda b, g, i: (0, 0)),
        ],
        out_specs=pl.BlockSpec((c, 2 * LANES), lambda b, g, i: (b * nch + i, g)),
        out_shape=jax.ShapeDtypeStruct((t, NSA_HEADS * HEAD_DIM), BF16),
        compiler_params=_cparams(("parallel", "parallel", "arbitrary"), TILE["nsa"]["vmem"]),
        name="nsa",
    )(hb, misc, kcvc, hb, hb, ovl_t, expand)


def _nsa_consts(seq):
    nb = seq // SEL_BLOCK
    nbp = max(LANES, nb)
    rows = seq // CMP_STRIDE
    nc = (seq - CMP_LEN) // CMP_STRIDE + 1
    j = np.arange(nbp)[:, None]
    n = np.arange(rows)[None, :]
    ovl = np.clip(np.minimum(n * CMP_STRIDE + CMP_LEN, j * SEL_BLOCK + SEL_BLOCK)
                  - np.maximum(n * CMP_STRIDE, j * SEL_BLOCK), 0, None).astype(np.float32) / CMP_LEN
    ovl = np.where((n < nc) & (j < nb), ovl, 0.0)
    s = np.arange(seq)[None, :]
    expand = (s // SEL_BLOCK == j).astype(np.float32)
    return jnp.asarray(ovl, BF16), jnp.asarray(expand, BF16)


INT_MIN = -2147483648


def _dsa_kernel(q_ref, iq_ref, misc_ref, ik_ref, kv_ref, o_ref, key_scr, *, c, tk, seq, topk):
    ci = pl.program_id(1)
    t0 = ci * c
    n_tiles = (t0 + c + tk - 1) // tk
    tq = t0 + lax.broadcasted_iota(jnp.int32, (1, c), 1)
    krow = lax.broadcasted_iota(jnp.int32, (tk, c), 0)

    lane = lax.broadcasted_iota(jnp.int32, (c, LANES), 1)
    per_blk = LANES // IDX_DIM
    qs = []
    for h in range(IDX_HEADS):
        blk = iq_ref[:, (h // per_blk) * LANES:(h // per_blk + 1) * LANES]
        qs.append(jnp.where(lane // IDX_DIM == h % per_blk, blk, jnp.zeros_like(blk)))
    qst = jnp.concatenate(qs, axis=0)
    w_t = jnp.transpose(misc_ref[...] * (IDX_HEADS ** -0.5 * IDX_DIM ** -0.5))

    def score_body(kt, carry):
        k0 = pl.multiple_of(kt * tk, tk)
        lg = _qk(ik_ref[pl.ds(k0, tk), :], qst)
        sc = jnp.zeros((tk, c), F32)
        for h in range(IDX_HEADS):
            sc = sc + jnp.maximum(lg[:, h * c:(h + 1) * c], 0.0) * w_t[MISC_W + h:MISC_W + h + 1, :]
        sc = sc + 0.0
        sc = jnp.where((k0 + krow) <= tq, sc, -jnp.inf)
        bits = pltpu.bitcast(sc, jnp.int32)
        key_scr[pl.ds(k0, tk), :] = jnp.where(bits < 0, bits ^ jnp.int32(0x7FFFFFFF), bits)
        return carry

    lax.fori_loop(0, n_tiles, score_body, 0)

    n_acc = 8

    def count(pred):
        def body(kt, acc):
            k0 = pl.multiple_of(kt * tk, tk)
            hit = jnp.where(pred(key_scr[pl.ds(k0, tk), :], k0 + krow), 1.0, 0.0)
            return acc + jnp.sum(hit.reshape(tk // (8 * n_acc), n_acc * 8, c), axis=0)
        acc = lax.fori_loop(0, n_tiles, body, jnp.zeros((n_acc * 8, c), F32))
        return jnp.sum(acc, axis=0, keepdims=True)

    def thr_body(i, ans):
        cand = ans | (jnp.int32(1) << (31 - i))
        cand_s = cand ^ jnp.int32(INT_MIN)
        cnt = count(lambda keys, idx: keys >= cand_s)
        return jnp.where(cnt >= topk, cand, ans)

    thr = lax.fori_loop(0, 32, thr_body, jnp.zeros((1, c), jnp.int32)) ^ jnp.int32(INT_MIN)
    need = topk - count(lambda keys, idx: keys > thr)
    n_ties = count(lambda keys, idx: keys == thr)

    n_bits = max(1, int(np.ceil(np.log2(seq))))

    def tie_search():
        def tie_body(i, ans):
            cand = ans | (jnp.int32(1) << (n_bits - 1 - i))
            cnt = count(lambda keys, idx: (keys == thr) & (idx < cand))
            return jnp.where(cnt < need, cand, ans)
        return lax.fori_loop(0, n_bits, tie_body, jnp.zeros((1, c), jnp.int32))

    surplus = jnp.max(n_ties - need)
    last_tie = lax.cond(surplus > 0.0, tie_search, lambda: jnp.full((1, c), seq, jnp.int32))

    q4 = _stack_heads([q_ref[:, :LANES], q_ref[:, LANES:]], HEAD_DIM ** -0.5)
    eye = (lax.broadcasted_iota(jnp.int32, (c, c), 0) == lax.broadcasted_iota(jnp.int32, (c, c), 1)).astype(BF16)

    def att_body(kt, carry):
        k0 = pl.multiple_of(kt * tk, tk)
        kv = kv_ref[pl.ds(k0, tk), :]
        keys = key_scr[pl.ds(k0, tk), :]
        idx = k0 + krow
        picked = (idx <= tq) & ((keys > thr) | ((keys == thr) & (idx <= last_tie)))
        valid = _qk(eye, jnp.where(picked, 1.0, 0.0).astype(BF16)) > 0.5
        s = _qk(q4, kv).reshape(DSA_HEADS, c, tk)
        return _flash_step(s, valid, kv, *carry)

    _, acc = lax.fori_loop(0, n_tiles, att_body, _flash_init(DSA_HEADS, c))
    o = _normalize(acc).reshape(DSA_HEADS * c, LANES)
    for i, blk in enumerate(_pack_heads(o, DSA_HEADS, c)):
        o_ref[:, i * LANES:(i + 1) * LANES] = blk.astype(BF16)


def _dsa(hb, misc, bsz, seq, c, tk):
    t = hb.shape[0]
    nch = seq // c
    topk = min(DSA_TOPK_MAX, seq // 4)
    kern = functools.partial(_dsa_kernel, c=c, tk=tk, seq=seq, topk=topk)
    return pl.pallas_call(
        kern,
        grid=(bsz, nch),
        in_specs=[
            pl.BlockSpec((c, 2 * LANES), lambda b, i: (b * nch + i, BLK_DQ // 2)),
            pl.BlockSpec((c, 2 * LANES), lambda b, i: (b * nch + i, BLK_IQ // 2)),
            pl.BlockSpec((c, LANES), lambda b, i: (b * nch + i, 0)),
            pl.BlockSpec((seq, LANES), lambda b, i: (b, BLK_IK)),
            pl.BlockSpec((seq, LANES), lambda b, i: (b, BLK_DKV)),
        ],
        out_specs=pl.BlockSpec((c, 2 * LANES), lambda b, i: (b * nch + i, 0)),
        out_shape=jax.ShapeDtypeStruct((t, DSA_HEADS * HEAD_DIM), BF16),
        scratch_shapes=[pltpu.VMEM((seq, c), jnp.int32)],
        compiler_params=_cparams(("parallel", "arbitrary"), TILE["dsa"]["vmem"]),
        name="dsa",
    )(hb, hb, misc, hb, hb)


def _matmul_kernel(a_ref, w_ref, o_ref):
    o_ref[...] = jnp.dot(a_ref[...].astype(BF16), w_ref[...], preferred_element_type=F32).astype(o_ref.dtype)


def _matmul(a, w, tm, out_dtype):
    m, k = a.shape
    n = w.shape[1]
    return pl.pallas_call(
        _matmul_kernel,
        grid=(m // tm,),
        in_specs=[pl.BlockSpec((tm, k), lambda i: (i, 0)), pl.BlockSpec((k, n), lambda i: (0, 0))],
        out_specs=pl.BlockSpec((tm, n), lambda i: (i, 0)),
        out_shape=jax.ShapeDtypeStruct((m, n), out_dtype),
        compiler_params=_cparams(("parallel",), TILE["mem"]["vmem"]),
        name="matmul",
    )(a, w)


def _mem_attention(q_ref, mkv_ref, c):
    q4 = _stack_heads([q_ref[:, :LANES], q_ref[:, LANES:]], HEAD_DIM ** -0.5)
    outs = []
    for h in range(MEM_HEADS):
        kv = mkv_ref[:, h * LANES:(h + 1) * LANES]
        s = _qk(q4[h * c:(h + 1) * c], kv)
        e = jnp.exp(s - jnp.max(s, axis=-1, keepdims=True))
        p = e / jnp.sum(e, axis=-1, keepdims=True)
        outs.append(jnp.dot(p.astype(BF16), kv, preferred_element_type=F32))
    return _pack_heads(jnp.concatenate(outs, axis=0), MEM_HEADS, c)


def _layer_norm(v, g, b):
    mu = jnp.mean(v, axis=-1, keepdims=True)
    d = v - mu
    var = jnp.mean(d * d, axis=-1, keepdims=True)
    return d * lax.rsqrt(var + LN_EPS) * g + b


def _out_ln_kernel(on_ref, od_ref, mq_ref, mkv_ref, x_ref, w_ref, g_ref, b_ref, o_ref):
    n0 = on_ref.shape[1]
    n1 = n0 + od_ref.shape[1]
    o_mem = jnp.concatenate(_mem_attention(mq_ref, mkv_ref, x_ref.shape[0]), axis=1).astype(BF16)
    mix = jnp.dot(on_ref[...], w_ref[:n0, :], preferred_element_type=F32)
    mix = mix + jnp.dot(od_ref[...], w_ref[n0:n1, :], preferred_element_type=F32)
    mix = mix + jnp.dot(o_mem, w_ref[n1:, :], preferred_element_type=F32)
    o_ref[...] = _layer_norm(DEEPNORM_ALPHA * x_ref[...] + mix, g_ref[...], b_ref[...])


def _out_ln(o_nsa, o_dsa, hb, mkv, x2, w_out, g, b, seq, m_len, tm):
    t, dm = x2.shape
    per_batch = seq // tm
    row = lambda a: pl.BlockSpec((tm, a.shape[1]), lambda i: (i, 0))
    full = lambda a: pl.BlockSpec(a.shape, lambda i: (0, 0))
    return pl.pallas_call(
        _out_ln_kernel,
        grid=(t // tm,),
        in_specs=[
            row(o_nsa), row(o_dsa),
            pl.BlockSpec((tm, 2 * LANES), lambda i: (i, BLK_MQ // 2)),
            pl.BlockSpec((m_len, MEM_HEADS * LANES), lambda i: (i // per_batch, 0)),
            row(x2), full(w_out), full(g), full(b),
        ],
        out_specs=pl.BlockSpec((tm, dm), lambda i: (i, 0)),
        out_shape=jax.ShapeDtypeStruct((t, dm), F32),
        compiler_params=_cparams(("parallel",), TILE["out_ln"]["vmem"]),
        name="out_ln",
    )(o_nsa, o_dsa, hb, mkv, x2, w_out, g, b)


NOT_PICKED = 64.0


def _top_rows(s, k, break_ties):
    n_rows = s.shape[0]
    rows = lax.broadcasted_iota(jnp.int32, s.shape, 0)
    rank = jnp.full(s.shape, NOT_PICKED, F32)
    vals = []
    for r in range(k):
        best = jnp.max(s, axis=0, keepdims=True)
        hit = s == best
        if break_ties:
            hit = rows == jnp.min(jnp.where(hit, rows, n_rows), axis=0, keepdims=True)
        rank = jnp.where(hit, float(r), rank)
        s = jnp.where(hit, -jnp.inf, s)
        vals.append(best)
    return rank, vals


def _ranked_exactly(rank, k):
    return jnp.sum(jnp.where(rank < k, 1.0, 0.0), axis=0, keepdims=True) == float(k)


def _peer_select(s1, s2, k, break_ties):
    rank1, v1 = _top_rows(s1, k, break_ties)
    rank2, v2 = _top_rows(s2, k, break_ties)
    v2a = jnp.concatenate(v2, axis=0)
    v2lo = v2a[:8]
    r8 = lax.broadcasted_iota(jnp.int32, v2lo.shape, 0)
    pieces = [v1[0] + v2a, v1[1] + v2lo]
    for a in range(2, 8):
        pieces.append(jnp.where(r8 < k // (a + 1), v1[a] + v2lo, -jnp.inf))
    pieces.append(jnp.concatenate(v1[8:], axis=0) + v2[0])
    cand = jnp.concatenate(pieces, axis=0)
    rank_c, top = _top_rows(cand, k, break_ties)
    return rank1, rank2, rank_c, v1[0], v2[0], jnp.concatenate(top, axis=0)


def _peer_kernel(x1_ref, wqt_ref, k1_ref, k2_ref, u_ref, vt_ref, g_ref, b_ref, o_ref,
                 xt_scr, s1_scr, s2_scr, e1_scr, r2_scr, e2_scr, y_scr, *, tm, eb, nsub):
    j = pl.program_id(1)
    nk = PEER_N_KEYS
    half = PEER_KEY_DIM // 2
    n_lt = tm // LANES
    k = PEER_TOPK
    pack = BF16_ROWS

    @pl.when(j == 0)
    def _select():
        xt = jnp.transpose(x1_ref[...]).astype(BF16)
        xt_scr[...] = xt
        for h in range(PEER_HEADS):
            qh = jnp.dot(wqt_ref[h * PEER_KEY_DIM:(h + 1) * PEER_KEY_DIM, :], xt, preferred_element_type=F32).astype(BF16)
            s1_scr[h] = jnp.dot(k1_ref[...], qh[:half], preferred_element_type=F32)
            s2_scr[h] = jnp.dot(k2_ref[...], qh[half:], preferred_element_type=F32)

        n_par = 4
        per_head = n_lt // n_par

        def chunk(i, carry):
            h = i // per_head
            lanes = [pl.multiple_of(((i % per_head) * n_par + p) * LANES, LANES) for p in range(n_par)]
            s1s = [s1_scr[h, :, pl.ds(l0, LANES)] for l0 in lanes]
            s2s = [s2_scr[h, :, pl.ds(l0, LANES)] for l0 in lanes]
            fast = [_peer_select(s1, s2, k, break_ties=False) for s1, s2 in zip(s1s, s2s)]
            n_unclean = 0.0
            for f in fast:
                clean = _ranked_exactly(f[0], k) & _ranked_exactly(f[1], k) & _ranked_exactly(f[2], k)
                n_unclean = n_unclean + jnp.sum(jnp.where(clean, 0.0, 1.0))
            picks = lax.cond(
                n_unclean > 0.0,
                lambda: [_peer_select(s1, s2, k, break_ties=True) for s1, s2 in zip(s1s, s2s)],
                lambda: fast)
            for l0, s1, s2, (rank1, rank2, rank_c, v1_max, v2_max, top) in zip(lanes, s1s, s2s, picks):
                den = jnp.sum(jnp.exp(top - top[0:1]), axis=0, keepdims=True)
                picked = jnp.where(rank_c < k, 1.0, 0.0)
                n_of_rank = [jnp.sum(picked[0:16], axis=0, keepdims=True)]
                n_of_rank += [jnp.sum(picked[8 * a + 8:8 * a + 16], axis=0, keepdims=True) for a in range(1, 8)]
                n_of_rank += [picked[72 + a:73 + a] for a in range(8)]
                n1 = jnp.zeros_like(s1)
                for a in range(k):
                    n1 = jnp.where(rank1 == float(a), n_of_rank[a], n1)
                s1_scr[h, :, pl.ds(l0, LANES)] = n1
                e1_scr[h, :, pl.ds(l0, LANES)] = jnp.where(rank1 < k, 0.5 * jnp.exp(s1 - v1_max), 0.0)
                slab = h * n_lt + l0 // LANES
                r2_scr[slab] = rank2.astype(BF16)
                e2_scr[slab] = (jnp.where(rank2 < k, jnp.exp(s2 - v2_max), 0.0) / den).astype(BF16)
            return carry

        lax.fori_loop(0, PEER_HEADS * per_head, chunk, 0)
        y_scr[...] = jnp.zeros_like(y_scr)

    xt = xt_scr[...]
    y_add = None
    rows_per_step = nsub * eb // nk
    for sub in range(nsub):
        a = jnp.dot(u_ref[sub * eb:(sub + 1) * eb, :], xt, preferred_element_type=F32)
        zs = []
        for ib in range(eb // nk):
            row = sub * (eb // nk) + ib
            grp = pl.multiple_of(j * rows_per_step + (row // SUBLANES) * SUBLANES, SUBLANES)
            r_in = row % SUBLANES
            zrow = []
            for lt in range(n_lt):
                ls = slice(lt * LANES, (lt + 1) * LANES)
                w = jnp.zeros((nk // pack, pack, LANES), BF16)
                for h in range(PEER_HEADS):
                    n1r = s1_scr[h, pl.ds(grp, SUBLANES), ls][r_in:r_in + 1]
                    e1r = e1_scr[h, pl.ds(grp, SUBLANES), ls][r_in:r_in + 1]
                    n1b = jnp.broadcast_to(n1r, (pack, LANES)).astype(BF16)[None]
                    e1b = jnp.broadcast_to(e1r, (pack, LANES)).astype(BF16)[None]
                    r2 = r2_scr[h * n_lt + lt].reshape(nk // pack, pack, LANES)
                    e2 = e2_scr[h * n_lt + lt].reshape(nk // pack, pack, LANES)
                    w = w + jnp.where(r2 < n1b, e2, jnp.zeros_like(e2)) * e1b
                x_blk = a[ib * nk:(ib + 1) * nk, ls]
                act = (x_blk * (1.0 + lax.erf(x_blk * np.float32(1.0 / np.sqrt(2.0))))).astype(BF16)
                zrow.append(w.reshape(nk, LANES) * act)
            zs.append(jnp.concatenate(zrow, axis=1))
        z = jnp.concatenate(zs, axis=0)
        y_sub = jnp.dot(vt_ref[:, sub * eb:(sub + 1) * eb], z, preferred_element_type=F32)
        y_add = y_sub if y_add is None else y_add + y_sub
    y_scr[...] += y_add

    @pl.when(j == pl.num_programs(1) - 1)
    def _finish():
        y = jnp.transpose(y_scr[...])
        o_ref[...] = _layer_norm(DEEPNORM_ALPHA * x1_ref[...] + y, g_ref[...], b_ref[...])


def _peer(x1, wqt, k1, k2, u, vt, g, b, tm, eb, nsub):
    t, dm = x1.shape
    n_e = u.shape[0]
    assert (nsub * eb // PEER_N_KEYS) % SUBLANES == 0, "a grid step covers whole sublane groups of first-key rows"
    assert PEER_TOPK == 16 and PEER_N_KEYS == LANES, "the candidate-pair layout in _peer_select is written for k = 16"
    full = lambda a: pl.BlockSpec(a.shape, lambda i, j: (0, 0))
    tab = pltpu.VMEM((PEER_HEADS, PEER_N_KEYS, tm), F32)
    tab_b = pltpu.VMEM((PEER_HEADS * (tm // LANES), PEER_N_KEYS, LANES), BF16)
    return pl.pallas_call(
        functools.partial(_peer_kernel, tm=tm, eb=eb, nsub=nsub),
        grid=(t // tm, n_e // (nsub * eb)),
        in_specs=[
            pl.BlockSpec((tm, dm), lambda i, j: (i, 0)),
            full(wqt), full(k1), full(k2),
            pl.BlockSpec((nsub * eb, dm), lambda i, j: (j, 0)),
            pl.BlockSpec((dm, nsub * eb), lambda i, j: (0, j)),
            full(g), full(b),
        ],
        out_specs=pl.BlockSpec((tm, dm), lambda i, j: (i, 0)),
        out_shape=jax.ShapeDtypeStruct((t, dm), F32),
        scratch_shapes=[
            pltpu.VMEM((dm, tm), BF16), tab, tab, tab, tab_b, tab_b,
            pltpu.VMEM((dm, tm), F32),
        ],
        compiler_params=_cparams(("parallel", "arbitrary"), TILE["peer"]["vmem"]),
        name="peer",
    )(x1, wqt, k1, k2, u, vt, g, b)


def _layer(x, mem, positions, w_in, pe_k, pe_v, w1k, w2k, w1v, w2v, w_mem_kv, w_out, ln1_g, ln1_b,
           w_query, sk1, sk2, pu, pv, ln2_g, ln2_b):
    bsz, seq, dm = x.shape
    m_len = mem.shape[1]
    t = bsz * seq
    d = HEAD_DIM
    g_n = NSA_KV_HEADS
    nsa_t, dsa_t, peer_t = TILE["nsa"], TILE["dsa"], TILE["peer"]
    assert seq % nsa_t["tk"] == 0 and seq % dsa_t["tk"] == 0 and seq >= WINDOW + nsa_t["c"]
    assert seq % TILE["out_ln"]["tm"] == 0, "an out_ln token tile stays inside one batch row (one memory block)"
    assert dsa_t["tk"] >= min(DSA_TOPK_MAX, seq // 4), "a key tile holds at least top-k candidates"
    assert t % peer_t["tm"] == 0 and t % TILE["proj"]["tm"] == 0 and t % TILE["out_ln"]["tm"] == 0
    x2 = x.reshape(t, dm)

    hb, misc = _proj(x2, _regroup_w_in(w_in), _rope_tables(positions), tm=TILE["proj"]["tm"])

    rows = seq // CMP_STRIDE
    cmp = hb[:, BLK_NKV * LANES:(BLK_NKV + g_n) * LANES].reshape(bsz, seq, g_n, 2, d)
    cmp = cmp.transpose(0, 2, 3, 1, 4).reshape(bsz * g_n, 2, rows, CMP_STRIDE * d)
    zpad = jnp.zeros((CMP_HIDDEN, d), BF16)
    pe_rows = lambda pe: jnp.pad(pe.reshape(1, CMP_LEN * d), ((0, 7), (0, 0))).astype(BF16)
    kcvc = _compress(cmp[:, 0], cmp[:, 1], w1k.astype(BF16), w1v.astype(BF16),
                     jnp.concatenate([w2k.astype(BF16), zpad], axis=1), jnp.concatenate([zpad, w2v.astype(BF16)], axis=1),
                     pe_rows(pe_k), pe_rows(pe_v))

    ovl_t, expand = _nsa_consts(seq)
    o_nsa = _nsa(hb, misc, kcvc, ovl_t, expand, bsz, seq, c=nsa_t["c"], tk=nsa_t["tk"])
    o_dsa = _dsa(hb, misc, bsz, seq, c=dsa_t["c"], tk=dsa_t["tk"])

    wm = w_mem_kv.reshape(dm, 2, MEM_HEADS, d).transpose(0, 2, 1, 3).reshape(dm, MEM_HEADS * 2 * d).astype(BF16)
    mkv = _matmul(mem.reshape(bsz * m_len, dm), wm, tm=m_len, out_dtype=BF16)
    x1 = _out_ln(o_nsa, o_dsa, hb, mkv, x2, w_out.astype(BF16), ln1_g.reshape(1, dm), ln1_b.reshape(1, dm),
                 seq, m_len, tm=TILE["out_ln"]["tm"])

    x2o = _peer(x1, w_query.T.astype(BF16), sk1.astype(BF16), sk2.astype(BF16), pu.astype(BF16), pv.T.astype(BF16),
                ln2_g.reshape(1, dm), ln2_b.reshape(1, dm), tm=peer_t["tm"], eb=peer_t["eb"], nsub=peer_t["nsub"])
    return x2o.reshape(bsz, seq, dm)


def kernel(x, mem, positions, w_in, nsa_pe_k, nsa_pe_v, nsa_cmp_w1_k, nsa_cmp_w2_k, nsa_cmp_w1_v, nsa_cmp_w2_v,
           w_mem_kv, w_out, ln1_g, ln1_b, peer_w_query, peer_sub_keys_1, peer_sub_keys_2, peer_u, peer_v, ln2_g, ln2_b):
    assert w_in.shape[0] == DEPTH
    return _layer(x, mem, positions, w_in[0], nsa_pe_k[0], nsa_pe_v[0], nsa_cmp_w1_k[0], nsa_cmp_w2_k[0],
                  nsa_cmp_w1_v[0], nsa_cmp_w2_v[0], w_mem_kv[0], w_out[0], ln1_g[0], ln1_b[0], peer_w_query[0],
                  peer_sub_keys_1[0], peer_sub_keys_2[0], peer_u[0], peer_v[0], ln2_g[0], ln2_b[0])
```

```python
import functools

import numpy as np
import jax
import jax.numpy as jnp
from jax import lax
from jax.experimental import pallas as pl
from jax.experimental.pallas import tpu as pltpu

F32 = jnp.float32
BF16 = jnp.bfloat16

LANES = 128
SUBLANES = 8
BF16_ROWS = 16
MIB = 1 << 20

TILE = dict(
    proj=dict(tm=512, vmem=40 * MIB),
    compress=dict(vmem=16 * MIB),
    nsa=dict(c=256, tk=512, vmem=32 * MIB),
    dsa=dict(c=256, tk=512, vmem=32 * MIB),
    mem=dict(vmem=16 * MIB),
    out_ln=dict(tm=512, vmem=24 * MIB),
    peer=dict(tm=512, eb=1024, nsub=2, vmem=56 * MIB),
)

HEAD_DIM = 64
ROPE_THETA = 500000.0
LN_EPS = 1e-5
NSA_HEADS = 8
NSA_KV_HEADS = 2
CMP_LEN = 32
CMP_STRIDE = 16
CMP_HIDDEN = 128
SEL_BLOCK = 64
SEL_COUNT = 16
WINDOW = 512
FORCE_BONUS = 1e4
DSA_HEADS = 4
IDX_HEADS = 8
IDX_DIM = 32
DSA_TOPK_MAX = 256
MEM_HEADS = 4
PEER_HEADS = 8
PEER_N_KEYS = 128
PEER_KEY_DIM = 256
PEER_TOPK = 16
DEPTH = 1
DEEPNORM_ALPHA = (2.0 * DEPTH) ** 0.25

NEG = -1e30

BLK_QN = 0
BLK_NKV = 4
BLK_DQ = 10
BLK_IQ = 12
BLK_MQ = 14
BLK_DKV = 16
BLK_IK = 17
BLK_MISC = 18
N_BLK = 19
MISC_W = 0
MISC_G = IDX_HEADS


def _cparams(sem, vmem):
    return pltpu.CompilerParams(dimension_semantics=sem, vmem_limit_bytes=vmem)


def _regroup_w_in(w_in):
    d = HEAD_DIM
    o_q = 0
    o_kv = o_q + NSA_HEADS * d
    o_g = o_kv + 6 * NSA_KV_HEADS * d
    o_dq = o_g + 3 * NSA_HEADS
    o_dkv = o_dq + DSA_HEADS * d
    o_iq = o_dkv + 2 * d
    o_ik = o_iq + IDX_HEADS * IDX_DIM
    o_iw = o_ik + IDX_DIM
    o_mq = o_iw + IDX_HEADS
    cols = [w_in[:, o_q:o_kv]]
    for br in range(3):
        for g in range(NSA_KV_HEADS):
            k0 = o_kv + ((2 * br) * NSA_KV_HEADS + g) * d
            v0 = o_kv + ((2 * br + 1) * NSA_KV_HEADS + g) * d
            cols += [w_in[:, k0:k0 + d], w_in[:, v0:v0 + d]]
    cols.append(w_in[:, o_dq:o_dkv])
    cols.append(w_in[:, o_iq:o_ik])
    cols.append(w_in[:, o_mq:o_mq + MEM_HEADS * d])
    cols.append(w_in[:, o_dkv:o_iq])
    cols += [w_in[:, o_ik:o_iw]] * (LANES // IDX_DIM)
    cols += [w_in[:, o_iw:o_mq], w_in[:, o_g:o_dq]]
    pad = LANES - IDX_HEADS - 3 * NSA_HEADS
    cols.append(jnp.zeros((w_in.shape[0], pad), w_in.dtype))
    w = jnp.concatenate(cols, axis=1)
    assert w.shape[1] == N_BLK * LANES
    return w.astype(BF16)


def _rot_half(head_dim):
    return head_dim // 8


def _rope_inputs(positions):
    pos = positions.reshape(-1).astype(F32)
    vals, place, ones = [], [], []
    src = 0
    for hd in (HEAD_DIM, IDX_DIM):
        half = _rot_half(hd)
        freqs = jnp.power(ROPE_THETA, -jnp.arange(half, dtype=F32) * 2.0 / (2 * half))
        ang = pos[:, None] * freqs
        vals += [jnp.cos(ang), jnp.sin(ang)]
        d = np.arange(LANES) % hd
        lanes = np.arange(LANES)
        sel = np.zeros((3, LANES, LANES), np.float32)
        rot, first, second = d < 2 * half, d < half, (d >= half) & (d < 2 * half)
        sel[0, src + d[rot] % half, lanes[rot]] = 1.0
        sel[1, src + half + d[first], lanes[first]] = -1.0
        sel[2, src + half + d[second] - half, lanes[second]] = 1.0
        place += [sel[0], sel[1], sel[2]]
        ones += [(~rot).astype(np.float32), np.zeros(LANES, np.float32), np.zeros(LANES, np.float32)]
        src += 2 * half
    cs = jnp.concatenate(vals, axis=1)
    cs = jnp.pad(cs, ((0, 0), (0, LANES - cs.shape[1])))
    return cs, jnp.asarray(np.concatenate(place, axis=1), BF16), jnp.asarray(np.concatenate(ones)[None, :], F32)


def _proj_kernel(x_ref, w_ref, cs_ref, place_ref, ones_ref, hb_ref, misc_ref):
    xb = x_ref[...].astype(BF16)
    tm = xb.shape[0]
    lane = lax.broadcasted_iota(jnp.int32, (tm, LANES), 1)
    lo = lane < HEAD_DIM
    cs = cs_ref[...]
    hi = cs.astype(BF16)
    rest = cs - hi.astype(F32)
    mid = rest.astype(BF16)
    low = (rest - mid.astype(F32)).astype(BF16)
    tabs = ones_ref[...]
    for piece in (hi, mid, low):
        tabs = tabs + jnp.dot(piece, place_ref[...], preferred_element_type=F32)
    c64, sa64, sb64 = (tabs[:, i * LANES:(i + 1) * LANES] for i in range(3))
    c32, sa32, sb32 = (tabs[:, i * LANES:(i + 1) * LANES] for i in range(3, 6))
    c64h, sa64h, sb64h = jnp.where(lo, c64, 1.0), jnp.where(lo, sa64, 0.0), jnp.where(lo, sb64, 0.0)

    def rope(h, c, sa, sb, half):
        return h * c + pltpu.roll(h, LANES - half, 1) * sa + pltpu.roll(h, half, 1) * sb

    def finish(blk, h):
        if blk < BLK_NKV or BLK_DQ <= blk < BLK_IQ:
            return rope(h, c64, sa64, sb64, _rot_half(HEAD_DIM))
        if BLK_NKV <= blk < BLK_DQ or blk == BLK_DKV:
            return rope(h, c64h, sa64h, sb64h, _rot_half(HEAD_DIM))
        if BLK_IQ <= blk < BLK_MQ or blk == BLK_IK:
            return rope(h, c32, sa32, sb32, _rot_half(IDX_DIM))
        return h

    for j in range(BLK_MISC // 2):
        h = jnp.dot(xb, w_ref[:, j * 2 * LANES:(j + 1) * 2 * LANES], preferred_element_type=F32)
        for s in range(2):
            blk = 2 * j + s
            hb_ref[:, blk * LANES:(blk + 1) * LANES] = finish(blk, h[:, s * LANES:(s + 1) * LANES]).astype(BF16)
    misc_ref[...] = jnp.dot(xb, w_ref[:, BLK_MISC * LANES:], preferred_element_type=F32)


def _proj(x2, w2, rope, tm):
    t, dm = x2.shape
    cs, place, ones = rope
    return pl.pallas_call(
        _proj_kernel,
        grid=(t // tm,),
        in_specs=[
            pl.BlockSpec((tm, dm), lambda i: (i, 0)),
            pl.BlockSpec((dm, N_BLK * LANES), lambda i: (0, 0)),
            pl.BlockSpec((tm, LANES), lambda i: (i, 0)),
            pl.BlockSpec(place.shape, lambda i: (0, 0)),
            pl.BlockSpec(ones.shape, lambda i: (0, 0)),
        ],
        out_specs=[
            pl.BlockSpec((tm, BLK_MISC * LANES), lambda i: (i, 0)),
            pl.BlockSpec((tm, LANES), lambda i: (i, 0)),
        ],
        out_shape=[
            jax.ShapeDtypeStruct((t, BLK_MISC * LANES), BF16),
            jax.ShapeDtypeStruct((t, LANES), F32),
        ],
        compiler_params=_cparams(("parallel",), TILE["proj"]["vmem"]),
        name="proj",
    )(x2, w2, cs, place, ones)


def _gelu(x):
    return 0.5 * x * (1.0 + lax.erf(x * np.float32(1.0 / np.sqrt(2.0))))


def _compress_kernel(kch_ref, vch_ref, w1k_ref, w1v_ref, w2k_ref, w2v_ref, pek_ref, pev_ref, out_ref):
    half = w1k_ref.shape[0] // 2
    rows = kch_ref.shape[1]

    def hidden(ch_ref, w1_ref, pe_ref):
        ch = ch_ref[0]
        a = jnp.dot(ch, w1_ref[:half, :], preferred_element_type=F32)
        b = jnp.dot(ch, w1_ref[half:, :], preferred_element_type=F32)
        bias = jnp.dot(pe_ref[...], w1_ref[...], preferred_element_type=F32)[0:1, :]
        return _gelu(a + pltpu.roll(b, rows - 1, 0) + bias).astype(BF16)

    hk = hidden(kch_ref, w1k_ref, pek_ref)
    hv = hidden(vch_ref, w1v_ref, pev_ref)
    out = jnp.dot(hk, w2k_ref[...], preferred_element_type=F32) + jnp.dot(hv, w2v_ref[...], preferred_element_type=F32)
    out_ref[0] = out.astype(BF16)


def _compress(kch, vch, w1k, w1v, w2k, w2v, pek, pev):
    bg, rows, width = kch.shape
    full = lambda a: pl.BlockSpec(a.shape, lambda i: (0,) * a.ndim)
    return pl.pallas_call(
        _compress_kernel,
        grid=(bg,),
        in_specs=[
            pl.BlockSpec((1, rows, width), lambda i: (i, 0, 0)),
            pl.BlockSpec((1, rows, width), lambda i: (i, 0, 0)),
            full(w1k), full(w1v), full(w2k), full(w2v), full(pek), full(pev),
        ],
        out_specs=pl.BlockSpec((1, rows, LANES), lambda i: (i, 0, 0)),
        out_shape=jax.ShapeDtypeStruct((bg, rows, LANES), BF16),
        compiler_params=_cparams(("parallel",), TILE["compress"]["vmem"]),
        name="compress",
    )(kch, vch, w1k, w1v, w2k, w2v, pek, pev)


def _stack_heads(qpair_refs_or_vals, scale):
    outs = []
    for blk in qpair_refs_or_vals:
        b = blk.astype(F32) * scale
        lane = lax.broadcasted_iota(jnp.int32, b.shape, 1)
        lo = lane < HEAD_DIM
        outs.append(jnp.where(lo, b, 0.0))
        outs.append(jnp.where(lo, pltpu.roll(b, HEAD_DIM, 1), 0.0))
    return jnp.concatenate(outs, axis=0).astype(BF16)


def _qk(q, kv):
    return lax.dot_general(q, kv, (((1,), (1,)), ((), ())), preferred_element_type=F32)


def _lane_pick(x, col):
    lane = lax.broadcasted_iota(jnp.int32, x.shape, 1)
    return jnp.sum(jnp.where(lane == col, x, 0.0), axis=1, keepdims=True)


def _flash_init(n_heads, c):
    return jnp.full((n_heads, c, 1), NEG, F32), jnp.zeros((n_heads, c, LANES), F32)


def _flash_step(s, valid, kv, m, acc):
    n_heads, c, tk = s.shape
    sb = s.astype(BF16) + jnp.where(valid, 0.0, NEG).astype(BF16)
    m_new = jnp.maximum(m, jnp.max(sb, axis=-1, keepdims=True).astype(F32))
    p = jnp.exp(sb - m_new.astype(BF16))
    lane = lax.broadcasted_iota(jnp.int32, kv.shape, 1)
    ones_v = jnp.where(lane < HEAD_DIM, jnp.ones_like(kv), kv)
    pv = jnp.dot(p.reshape(n_heads * c, tk), ones_v, preferred_element_type=F32)
    return m_new, jnp.exp(m - m_new) * acc + pv.reshape(n_heads, c, LANES)


def _normalize(acc):
    den = acc[..., 0:1]
    return acc / jnp.where(den > 0.0, den, 1.0)


def _pack_heads(o, n_heads, c):
    lane = lax.broadcasted_iota(jnp.int32, (c, LANES), 1)
    lo = lane < HEAD_DIM
    blocks = []
    for p in range(n_heads // 2):
        a = o[(2 * p) * c:(2 * p + 1) * c]
        b = o[(2 * p + 1) * c:(2 * p + 2) * c]
        blocks.append(jnp.where(lo, pltpu.roll(a, HEAD_DIM, 1), b))
    return blocks


def _nsa_kernel(q_ref, misc_ref, kc_ref, ks_ref, kw_ref, ovl_ref, exp_ref, o_ref, *, c, tk, seq):
    g = pl.program_id(1)
    ci = pl.program_id(2)
    t0 = ci * c
    hpg = NSA_HEADS // NSA_KV_HEADS
    rows = hpg * c
    scale = HEAD_DIM ** -0.5
    q4 = _stack_heads([q_ref[:, :LANES], q_ref[:, LANES:]], scale)
    trow = t0 + lax.broadcasted_iota(jnp.int32, (c, 1), 0)

    kc = kc_ref[0]
    ncp = kc.shape[0]
    s = _qk(q4, kc).reshape(hpg, c, ncp)
    ncol = lax.broadcasted_iota(jnp.int32, (c, ncp), 1)
    cvalid = (ncol * CMP_STRIDE + (CMP_LEN - 1)) <= trow
    s = jnp.where(cvalid, s, NEG)
    m = jnp.max(s, axis=-1, keepdims=True)
    e = jnp.where(cvalid, jnp.exp(s - m), 0.0)
    den = jnp.sum(e, axis=-1, keepdims=True)
    p = (e / jnp.where(den > 0.0, den, 1.0)).reshape(rows, ncp)
    pb = p.astype(BF16)
    o_cmp = jnp.dot(pb, kc, preferred_element_type=F32)

    imp_t = lax.dot_general(ovl_ref[...], pb, (((1,), (1,)), ((), ())), preferred_element_type=F32)
    nbp = imp_t.shape[0]
    nb = seq // SEL_BLOCK
    imp = imp_t[:nb, 0:c]
    for h in range(1, hpg):
        imp = imp + imp_t[:nb, h * c:(h + 1) * c]
    jrow = lax.broadcasted_iota(jnp.int32, (nb, c), 0)
    tcol = t0 + lax.broadcasted_iota(jnp.int32, (nb, c), 1)
    cur = tcol // SEL_BLOCK
    forced = (jrow == 0) | (jrow == cur) | (jrow == cur - 1)
    visible = (jrow * SEL_BLOCK) <= tcol
    score = jnp.where(visible, imp + jnp.where(forced, FORCE_BONUS, 0.0), -jnp.inf)
    n_pick = min(SEL_COUNT, nb)
    rank, _ = _top_rows(score, n_pick, break_ties=True)
    sel = jnp.where(rank < n_pick, 1.0, 0.0)
    if nbp > nb:
        sel = jnp.concatenate([sel, jnp.zeros((nbp - nb, c), F32)], axis=0)
    sel_b = jnp.transpose(sel).astype(BF16)

    n_tiles = (t0 + c + tk - 1) // tk
    kcol = lax.broadcasted_iota(jnp.int32, (c, tk), 1)

    def sel_body(kt, carry):
        k0 = pl.multiple_of(kt * tk, tk)
        kv = ks_ref[pl.ds(k0, tk), :]
        s_ = _qk(q4, kv).reshape(hpg, c, tk)
        tok = jnp.dot(sel_b, exp_ref[:, pl.ds(k0, tk)], preferred_element_type=F32)
        valid = (tok > 0.5) & ((k0 + kcol) <= trow)
        return _flash_step(s_, valid, kv, *carry)

    init = _flash_init(hpg, c)
    _, acc_s = lax.fori_loop(0, n_tiles, sel_body, init)
    o_sel = _normalize(acc_s).reshape(rows, LANES)

    wk = WINDOW + c
    w0 = pl.multiple_of(jnp.maximum(t0 - WINDOW, 0), c)
    kvw = kw_ref[pl.ds(w0, wk), :]
    s = _qk(q4, kvw).reshape(hpg, c, wk)
    wpos = w0 + lax.broadcasted_iota(jnp.int32, (c, wk), 1)
    wvalid = (wpos <= trow) & (wpos > trow - WINDOW)
    _, acc_w = _flash_step(s, wvalid, kvw, *init)
    o_win = _normalize(acc_w).reshape(rows, LANES)

    gates = jax.nn.sigmoid(misc_ref[...])
    outs = []
    for h in range(hpg):
        col = MISC_G + (g * hpg + h) * 3
        sl = slice(h * c, (h + 1) * c)
        outs.append(_lane_pick(gates, col) * o_cmp[sl] + _lane_pick(gates, col + 1) * o_sel[sl]
                    + _lane_pick(gates, col + 2) * o_win[sl])
    blocks = _pack_heads(jnp.concatenate(outs, axis=0), hpg, c)
    for i, blk in enumerate(blocks):
        o_ref[:, i * LANES:(i + 1) * LANES] = blk.astype(BF16)


def _nsa(hb, misc, kcvc, ovl_t, expand, bsz, seq, c, tk):
    t = hb.shape[0]
    g_n = NSA_KV_HEADS
    nch = seq // c
    ncp = kcvc.shape[1]
    nbp = ovl_t.shape[0]
    kern = functools.partial(_nsa_kernel, c=c, tk=tk, seq=seq)
    return pl.pallas_call(
        kern,
        grid=(bsz, g_n, nch),
        in_specs=[
            pl.BlockSpec((c, 2 * LANES), lambda b, g, i: (b * nch + i, g)),
            pl.BlockSpec((c, LANES), lambda b, g, i: (b * nch + i, 0)),
            pl.BlockSpec((1, ncp, LANES), lambda b, g, i: (b * g_n + g, 0, 0)),
            pl.BlockSpec((seq, LANES), lambda b, g, i: (b, BLK_NKV + 2 + g)),
            pl.BlockSpec((seq, LANES), lambda b, g, i: (b, BLK_NKV + 4 + g)),
            pl.BlockSpec((nbp, ncp), lambda b, g, i: (0, 0)),
            pl.BlockSpec((nbp, seq), lambda b, g, i: (0, 0)),
        ],
        out_specs=pl.BlockSpec((c, 2 * LANES), lambda b, g, i: (b * nch + i, g)),
        out_shape=jax.ShapeDtypeStruct((t, NSA_HEADS * HEAD_DIM), BF16),
        compiler_params=_cparams(("parallel", "parallel", "arbitrary"), TILE["nsa"]["vmem"]),
        name="nsa",
    )(hb, misc, kcvc, hb, hb, ovl_t, expand)


def _nsa_consts(seq):
    nb = seq // SEL_BLOCK
    nbp = max(LANES, nb)
    rows = seq // CMP_STRIDE
    nc = (seq - CMP_LEN) // CMP_STRIDE + 1
    j = np.arange(nbp)[:, None]
    n = np.arange(rows)[None, :]
    ovl = np.clip(np.minimum(n * CMP_STRIDE + CMP_LEN, j * SEL_BLOCK + SEL_BLOCK)
                  - np.maximum(n * CMP_STRIDE, j * SEL_BLOCK), 0, None).astype(np.float32) / CMP_LEN
    ovl = np.where((n < nc) & (j < nb), ovl, 0.0)
    s = np.arange(seq)[None, :]
    expand = (s // SEL_BLOCK == j).astype(np.float32)
    return jnp.asarray(ovl, BF16), jnp.asarray(expand, BF16)


INT_MIN = -2147483648


def _dsa_kernel(q_ref, iq_ref, misc_ref, ik_ref, kv_ref, o_ref, key_scr, *, c, tk, seq, topk):
    ci = pl.program_id(1)
    t0 = ci * c
    n_tiles = (t0 + c + tk - 1) // tk
    tq = t0 + lax.broadcasted_iota(jnp.int32, (1, c), 1)
    krow = lax.broadcasted_iota(jnp.int32, (tk, c), 0)

    lane = lax.broadcasted_iota(jnp.int32, (c, LANES), 1)
    per_blk = LANES // IDX_DIM
    qs = []
    for h in range(IDX_HEADS):
        blk = iq_ref[:, (h // per_blk) * LANES:(h // per_blk + 1) * LANES]
        qs.append(jnp.where(lane // IDX_DIM == h % per_blk, blk, jnp.zeros_like(blk)))
    qst = jnp.concatenate(qs, axis=0)
    w_t = jnp.transpose(misc_ref[...] * (IDX_HEADS ** -0.5 * IDX_DIM ** -0.5))

    def score_body(kt, carry):
        k0 = pl.multiple_of(kt * tk, tk)
        lg = _qk(ik_ref[pl.ds(k0, tk), :], qst)
        sc = jnp.zeros((tk, c), F32)
        for h in range(IDX_HEADS):
            sc = sc + jnp.maximum(lg[:, h * c:(h + 1) * c], 0.0) * w_t[MISC_W + h:MISC_W + h + 1, :]
        sc = sc + 0.0
        sc = jnp.where((k0 + krow) <= tq, sc, -jnp.inf)
        bits = pltpu.bitcast(sc, jnp.int32)
        key_scr[pl.ds(k0, tk), :] = jnp.where(bits < 0, bits ^ jnp.int32(0x7FFFFFFF), bits)
        return carry

    lax.fori_loop(0, n_tiles, score_body, 0)

    n_acc = 8

    def count(pred):
        def body(kt, acc):
            k0 = pl.multiple_of(kt * tk, tk)
            hit = jnp.where(pred(key_scr[pl.ds(k0, tk), :], k0 + krow), 1.0, 0.0)
            return acc + jnp.sum(hit.reshape(tk // (8 * n_acc), n_acc * 8, c), axis=0)
        acc = lax.fori_loop(0, n_tiles, body, jnp.zeros((n_acc * 8, c), F32))
        return jnp.sum(acc, axis=0, keepdims=True)

    def thr_body(i, ans):
        cand = ans | (jnp.int32(1) << (31 - i))
        cand_s = cand ^ jnp.int32(INT_MIN)
        cnt = count(lambda keys, idx: keys >= cand_s)
        return jnp.where(cnt >= topk, cand, ans)

    thr = lax.fori_loop(0, 32, thr_body, jnp.zeros((1, c), jnp.int32)) ^ jnp.int32(INT_MIN)
    need = topk - count(lambda keys, idx: keys > thr)
    n_ties = count(lambda keys, idx: keys == thr)

    n_bits = max(1, int(np.ceil(np.log2(seq))))

    def tie_search():
        def tie_body(i, ans):
            cand = ans | (jnp.int32(1) << (n_bits - 1 - i))
            cnt = count(lambda keys, idx: (keys == thr) & (idx < cand))
            return jnp.where(cnt < need, cand, ans)
        return lax.fori_loop(0, n_bits, tie_body, jnp.zeros((1, c), jnp.int32))

    surplus = jnp.max(n_ties - need)
    last_tie = lax.cond(surplus > 0.0, tie_search, lambda: jnp.full((1, c), seq, jnp.int32))

    q4 = _stack_heads([q_ref[:, :LANES], q_ref[:, LANES:]], HEAD_DIM ** -0.5)
    eye = (lax.broadcasted_iota(jnp.int32, (c, c), 0) == lax.broadcasted_iota(jnp.int32, (c, c), 1)).astype(BF16)

    def att_body(kt, carry):
        k0 = pl.multiple_of(kt * tk, tk)
        kv = kv_ref[pl.ds(k0, tk), :]
        keys = key_scr[pl.ds(k0, tk), :]
        idx = k0 + krow
        picked = (idx <= tq) & ((keys > thr) | ((keys == thr) & (idx <= last_tie)))
        valid = _qk(eye, jnp.where(picked, 1.0, 0.0).astype(BF16)) > 0.5
        s = _qk(q4, kv).reshape(DSA_HEADS, c, tk)
        return _flash_step(s, valid, kv, *carry)

    _, acc = lax.fori_loop(0, n_tiles, att_body, _flash_init(DSA_HEADS, c))
    o = _normalize(acc).reshape(DSA_HEADS * c, LANES)
    for i, blk in enumerate(_pack_heads(o, DSA_HEADS, c)):
        o_ref[:, i * LANES:(i + 1) * LANES] = blk.astype(BF16)


def _dsa(hb, misc, bsz, seq, c, tk):
    t = hb.shape[0]
    nch = seq // c
    topk = min(DSA_TOPK_MAX, seq // 4)
    kern = functools.partial(_dsa_kernel, c=c, tk=tk, seq=seq, topk=topk)
    return pl.pallas_call(
        kern,
        grid=(bsz, nch),
        in_specs=[
            pl.BlockSpec((c, 2 * LANES), lambda b, i: (b * nch + i, BLK_DQ // 2)),
            pl.BlockSpec((c, 2 * LANES), lambda b, i: (b * nch + i, BLK_IQ // 2)),
            pl.BlockSpec((c, LANES), lambda b, i: (b * nch + i, 0)),
            pl.BlockSpec((seq, LANES), lambda b, i: (b, BLK_IK)),
            pl.BlockSpec((seq, LANES), lambda b, i: (b, BLK_DKV)),
        ],
        out_specs=pl.BlockSpec((c, 2 * LANES), lambda b, i: (b * nch + i, 0)),
        out_shape=jax.ShapeDtypeStruct((t, DSA_HEADS * HEAD_DIM), BF16),
        scratch_shapes=[pltpu.VMEM((seq, c), jnp.int32)],
        compiler_params=_cparams(("parallel", "arbitrary"), TILE["dsa"]["vmem"]),
        name="dsa",
    )(hb, hb, misc, hb, hb)


def _matmul_kernel(a_ref, w_ref, o_ref):
    o_ref[...] = jnp.dot(a_ref[...].astype(BF16), w_ref[...], preferred_element_type=F32).astype(o_ref.dtype)


def _matmul(a, w, tm, out_dtype):
    m, k = a.shape
    n = w.shape[1]
    return pl.pallas_call(
        _matmul_kernel,
        grid=(m // tm,),
        in_specs=[pl.BlockSpec((tm, k), lambda i: (i, 0)), pl.BlockSpec((k, n), lambda i: (0, 0))],
        out_specs=pl.BlockSpec((tm, n), lambda i: (i, 0)),
        out_shape=jax.ShapeDtypeStruct((m, n), out_dtype),
        compiler_params=_cparams(("parallel",), TILE["mem"]["vmem"]),
        name="matmul",
    )(a, w)


def _mem_attention(q_ref, mkv_ref, c):
    q4 = _stack_heads([q_ref[:, :LANES], q_ref[:, LANES:]], HEAD_DIM ** -0.5)
    outs = []
    for h in range(MEM_HEADS):
        kv = mkv_ref[:, h * LANES:(h + 1) * LANES]
        s = _qk(q4[h * c:(h + 1) * c], kv)
        e = jnp.exp(s - jnp.max(s, axis=-1, keepdims=True))
        p = e / jnp.sum(e, axis=-1, keepdims=True)
        outs.append(jnp.dot(p.astype(BF16), kv, preferred_element_type=F32))
    return _pack_heads(jnp.concatenate(outs, axis=0), MEM_HEADS, c)


def _layer_norm(v, g, b):
    mu = jnp.mean(v, axis=-1, keepdims=True)
    d = v - mu
    var = jnp.mean(d * d, axis=-1, keepdims=True)
    return d * lax.rsqrt(var + LN_EPS) * g + b


def _out_ln_kernel(on_ref, od_ref, mq_ref, mkv_ref, x_ref, w_ref, g_ref, b_ref, o_ref):
    n0 = on_ref.shape[1]
    n1 = n0 + od_ref.shape[1]
    o_mem = jnp.concatenate(_mem_attention(mq_ref, mkv_ref, x_ref.shape[0]), axis=1).astype(BF16)
    mix = jnp.dot(on_ref[...], w_ref[:n0, :], preferred_element_type=F32)
    mix = mix + jnp.dot(od_ref[...], w_ref[n0:n1, :], preferred_element_type=F32)
    mix = mix + jnp.dot(o_mem, w_ref[n1:, :], preferred_element_type=F32)
    o_ref[...] = _layer_norm(DEEPNORM_ALPHA * x_ref[...] + mix, g_ref[...], b_ref[...])


def _out_ln(o_nsa, o_dsa, hb, mkv, x2, w_out, g, b, seq, m_len, tm):
    t, dm = x2.shape
    per_batch = seq // tm
    row = lambda a: pl.BlockSpec((tm, a.shape[1]), lambda i: (i, 0))
    full = lambda a: pl.BlockSpec(a.shape, lambda i: (0, 0))
    return pl.pallas_call(
        _out_ln_kernel,
        grid=(t // tm,),
        in_specs=[
            row(o_nsa), row(o_dsa),
            pl.BlockSpec((tm, 2 * LANES), lambda i: (i, BLK_MQ // 2)),
            pl.BlockSpec((m_len, MEM_HEADS * LANES), lambda i: (i // per_batch, 0)),
            row(x2), full(w_out), full(g), full(b),
        ],
        out_specs=pl.BlockSpec((tm, dm), lambda i: (i, 0)),
        out_shape=jax.ShapeDtypeStruct((t, dm), F32),
        compiler_params=_cparams(("parallel",), TILE["out_ln"]["vmem"]),
        name="out_ln",
    )(o_nsa, o_dsa, hb, mkv, x2, w_out, g, b)


NOT_PICKED = 64.0


def _top_rows(s, k, break_ties):
    n_rows = s.shape[0]
    rows = lax.broadcasted_iota(jnp.int32, s.shape, 0)
    rank = jnp.full(s.shape, NOT_PICKED, F32)
    vals = []
    for r in range(k):
        best = jnp.max(s, axis=0, keepdims=True)
        hit = s == best
        if break_ties:
            hit = rows == jnp.min(jnp.where(hit, rows, n_rows), axis=0, keepdims=True)
        rank = jnp.where(hit, float(r), rank)
        s = jnp.where(hit, -jnp.inf, s)
        vals.append(best)
    return rank, vals


def _ranked_exactly(rank, k):
    return jnp.sum(jnp.where(rank < k, 1.0, 0.0), axis=0, keepdims=True) == float(k)


def _peer_select(s1, s2, k, break_ties):
    rank1, v1 = _top_rows(s1, k, break_ties)
    rank2, v2 = _top_rows(s2, k, break_ties)
    v2a = jnp.concatenate(v2, axis=0)
    v2lo = v2a[:8]
    r8 = lax.broadcasted_iota(jnp.int32, v2lo.shape, 0)
    pieces = [v1[0] + v2a, v1[1] + v2lo]
    for a in range(2, 8):
        pieces.append(jnp.where(r8 < k // (a + 1), v1[a] + v2lo, -jnp.inf))
    pieces.append(jnp.concatenate(v1[8:], axis=0) + v2[0])
    cand = jnp.concatenate(pieces, axis=0)
    rank_c, top = _top_rows(cand, k, break_ties)
    return rank1, rank2, rank_c, v1[0], v2[0], jnp.concatenate(top, axis=0)


def _peer_kernel(x1_ref, wqt_ref, k1_ref, k2_ref, u_ref, vt_ref, g_ref, b_ref, o_ref,
                 xt_scr, s1_scr, s2_scr, e1_scr, r2_scr, e2_scr, y_scr, *, tm, eb, nsub):
    j = pl.program_id(1)
    nk = PEER_N_KEYS
    half = PEER_KEY_DIM // 2
    n_lt = tm // LANES
    k = PEER_TOPK
    pack = BF16_ROWS

    @pl.when(j == 0)
    def _select():
        xt = jnp.transpose(x1_ref[...]).astype(BF16)
        xt_scr[...] = xt
        for h in range(PEER_HEADS):
            qh = jnp.dot(wqt_ref[h * PEER_KEY_DIM:(h + 1) * PEER_KEY_DIM, :], xt, preferred_element_type=F32).astype(BF16)
            s1_scr[h] = jnp.dot(k1_ref[...], qh[:half], preferred_element_type=F32)
            s2_scr[h] = jnp.dot(k2_ref[...], qh[half:], preferred_element_type=F32)

        n_par = 4
        per_head = n_lt // n_par

        def chunk(i, carry):
            h = i // per_head
            lanes = [pl.multiple_of(((i % per_head) * n_par + p) * LANES, LANES) for p in range(n_par)]
            s1s = [s1_scr[h, :, pl.ds(l0, LANES)] for l0 in lanes]
            s2s = [s2_scr[h, :, pl.ds(l0, LANES)] for l0 in lanes]
            fast = [_peer_select(s1, s2, k, break_ties=False) for s1, s2 in zip(s1s, s2s)]
            n_unclean = 0.0
            for f in fast:
                clean = _ranked_exactly(f[0], k) & _ranked_exactly(f[1], k) & _ranked_exactly(f[2], k)
                n_unclean = n_unclean + jnp.sum(jnp.where(clean, 0.0, 1.0))
            picks = lax.cond(
                n_unclean > 0.0,
                lambda: [_peer_select(s1, s2, k, break_ties=True) for s1, s2 in zip(s1s, s2s)],
                lambda: fast)
            for l0, s1, s2, (rank1, rank2, rank_c, v1_max, v2_max, top) in zip(lanes, s1s, s2s, picks):
                den = jnp.sum(jnp.exp(top - top[0:1]), axis=0, keepdims=True)
                picked = jnp.where(rank_c < k, 1.0, 0.0)
                n_of_rank = [jnp.sum(picked[0:16], axis=0, keepdims=True)]
                n_of_rank += [jnp.sum(picked[8 * a + 8:8 * a + 16], axis=0, keepdims=True) for a in range(1, 8)]
                n_of_rank += [picked[72 + a:73 + a] for a in range(8)]
                n1 = jnp.zeros_like(s1)
                for a in range(k):
                    n1 = jnp.where(rank1 == float(a), n_of_rank[a], n1)
                s1_scr[h, :, pl.ds(l0, LANES)] = n1
                e1_scr[h, :, pl.ds(l0, LANES)] = jnp.where(rank1 < k, 0.5 * jnp.exp(s1 - v1_max), 0.0)
                slab = h * n_lt + l0 // LANES
                r2_scr[slab] = rank2.astype(BF16)
                e2_scr[slab] = (jnp.where(rank2 < k, jnp.exp(s2 - v2_max), 0.0) / den).astype(BF16)
            return carry

        lax.fori_loop(0, PEER_HEADS * per_head, chunk, 0)
        y_scr[...] = jnp.zeros_like(y_scr)

    xt = xt_scr[...]
    y_add = None
    rows_per_step = nsub * eb // nk
    for sub in range(nsub):
        a = jnp.dot(u_ref[sub * eb:(sub + 1) * eb, :], xt, preferred_element_type=F32)
        zs = []
        for ib in range(eb // nk):
            row = sub * (eb // nk) + ib
            grp = pl.multiple_of(j * rows_per_step + (row // SUBLANES) * SUBLANES, SUBLANES)
            r_in = row % SUBLANES
            zrow = []
            for lt in range(n_lt):
                ls = slice(lt * LANES, (lt + 1) * LANES)
                w = jnp.zeros((nk // pack, pack, LANES), BF16)
                for h in range(PEER_HEADS):
                    n1r = s1_scr[h, pl.ds(grp, SUBLANES), ls][r_in:r_in + 1]
                    e1r = e1_scr[h, pl.ds(grp, SUBLANES), ls][r_in:r_in + 1]
                    n1b = jnp.broadcast_to(n1r, (pack, LANES)).astype(BF16)[None]
                    e1b = jnp.broadcast_to(e1r, (pack, LANES)).astype(BF16)[None]
                    r2 = r2_scr[h * n_lt + lt].reshape(nk // pack, pack, LANES)
                    e2 = e2_scr[h * n_lt + lt].reshape(nk // pack, pack, LANES)
                    w = w + jnp.where(r2 < n1b, e2, jnp.zeros_like(e2)) * e1b
                x_blk = a[ib * nk:(ib + 1) * nk, ls]
                act = (x_blk * (1.0 + lax.erf(x_blk * np.float32(1.0 / np.sqrt(2.0))))).astype(BF16)
                zrow.append(w.reshape(nk, LANES) * act)
            zs.append(jnp.concatenate(zrow, axis=1))
        z = jnp.concatenate(zs, axis=0)
        y_sub = jnp.dot(vt_ref[:, sub * eb:(sub + 1) * eb], z, preferred_element_type=F32)
        y_add = y_sub if y_add is None else y_add + y_sub
    y_scr[...] += y_add

    @pl.when(j == pl.num_programs(1) - 1)
    def _finish():
        y = jnp.transpose(y_scr[...])
        o_ref[...] = _layer_norm(DEEPNORM_ALPHA * x1_ref[...] + y, g_ref[...], b_ref[...])


def _peer(x1, wqt, k1, k2, u, vt, g, b, tm, eb, nsub):
    t, dm = x1.shape
    n_e = u.shape[0]
    assert (nsub * eb // PEER_N_KEYS) % SUBLANES == 0, "a grid step covers whole sublane groups of first-key rows"
    assert PEER_TOPK == 16 and PEER_N_KEYS == LANES, "the candidate-pair layout in _peer_select is written for k = 16"
    full = lambda a: pl.BlockSpec(a.shape, lambda i, j: (0, 0))
    tab = pltpu.VMEM((PEER_HEADS, PEER_N_KEYS, tm), F32)
    tab_b = pltpu.VMEM((PEER_HEADS * (tm // LANES), PEER_N_KEYS, LANES), BF16)
    return pl.pallas_call(
        functools.partial(_peer_kernel, tm=tm, eb=eb, nsub=nsub),
        grid=(t // tm, n_e // (nsub * eb)),
        in_specs=[
            pl.BlockSpec((tm, dm), lambda i, j: (i, 0)),
            full(wqt), full(k1), full(k2),
            pl.BlockSpec((nsub * eb, dm), lambda i, j: (j, 0)),
            pl.BlockSpec((dm, nsub * eb), lambda i, j: (0, j)),
            full(g), full(b),
        ],
        out_specs=pl.BlockSpec((tm, dm), lambda i, j: (i, 0)),
        out_shape=jax.ShapeDtypeStruct((t, dm), F32),
        scratch_shapes=[
            pltpu.VMEM((dm, tm), BF16), tab, tab, tab, tab_b, tab_b,
            pltpu.VMEM((dm, tm), F32),
        ],
        compiler_params=_cparams(("parallel", "arbitrary"), TILE["peer"]["vmem"]),
        name="peer",
    )(x1, wqt, k1, k2, u, vt, g, b)


def _layer(x, mem, positions, w_in, pe_k, pe_v, w1k, w2k, w1v, w2v, w_mem_kv, w_out, ln1_g, ln1_b,
           w_query, sk1, sk2, pu, pv, ln2_g, ln2_b):
    bsz, seq, dm = x.shape
    m_len = mem.shape[1]
    t = bsz * seq
    d = HEAD_DIM
    g_n = NSA_KV_HEADS
    nsa_t, dsa_t, peer_t = TILE["nsa"], TILE["dsa"], TILE["peer"]
    assert seq % nsa_t["tk"] == 0 and seq % dsa_t["tk"] == 0 and seq >= WINDOW + nsa_t["c"]
    assert seq % TILE["out_ln"]["tm"] == 0, "an out_ln token tile stays inside one batch row (one memory block)"
    assert dsa_t["tk"] >= min(DSA_TOPK_MAX, seq // 4), "a key tile holds at least top-k candidates"
    assert t % peer_t["tm"] == 0 and t % TILE["proj"]["tm"] == 0 and t % TILE["out_ln"]["tm"] == 0
    x2 = x.reshape(t, dm)

    hb, misc = _proj(x2, _regroup_w_in(w_in), _rope_inputs(positions), tm=TILE["proj"]["tm"])

    rows = seq // CMP_STRIDE
    cmp = hb[:, BLK_NKV * LANES:(BLK_NKV + g_n) * LANES].reshape(bsz, seq, g_n, 2, d)
    cmp = cmp.transpose(0, 2, 3, 1, 4).reshape(bsz * g_n, 2, rows, CMP_STRIDE * d)
    zpad = jnp.zeros((CMP_HIDDEN, d), BF16)
    pe_rows = lambda pe: jnp.pad(pe.reshape(1, CMP_LEN * d), ((0, 7), (0, 0))).astype(BF16)
    kcvc = _compress(cmp[:, 0], cmp[:, 1], w1k.astype(BF16), w1v.astype(BF16),
                     jnp.concatenate([w2k.astype(BF16), zpad], axis=1), jnp.concatenate([zpad, w2v.astype(BF16)], axis=1),
                     pe_rows(pe_k), pe_rows(pe_v))

    ovl_t, expand = _nsa_consts(seq)
    o_nsa = _nsa(hb, misc, kcvc, ovl_t, expand, bsz, seq, c=nsa_t["c"], tk=nsa_t["tk"])
    o_dsa = _dsa(hb, misc, bsz, seq, c=dsa_t["c"], tk=dsa_t["tk"])

    wm = w_mem_kv.reshape(dm, 2, MEM_HEADS, d).transpose(0, 2, 1, 3).reshape(dm, MEM_HEADS * 2 * d).astype(BF16)
    mkv = _matmul(mem.reshape(bsz * m_len, dm), wm, tm=m_len, out_dtype=BF16)
    x1 = _out_ln(o_nsa, o_dsa, hb, mkv, x2, w_out.astype(BF16), ln1_g.reshape(1, dm), ln1_b.reshape(1, dm),
                 seq, m_len, tm=TILE["out_ln"]["tm"])

    x2o = _peer(x1, w_query.T.astype(BF16), sk1.astype(BF16), sk2.astype(BF16), pu.astype(BF16), pv.T.astype(BF16),
                ln2_g.reshape(1, dm), ln2_b.reshape(1, dm), tm=peer_t["tm"], eb=peer_t["eb"], nsub=peer_t["nsub"])
    return x2o.reshape(bsz, seq, dm)


def kernel(x, mem, positions, w_in, nsa_pe_k, nsa_pe_v, nsa_cmp_w1_k, nsa_cmp_w2_k, nsa_cmp_w1_v, nsa_cmp_w2_v,
           w_mem_kv, w_out, ln1_g, ln1_b, peer_w_query, peer_sub_keys_1, peer_sub_keys_2, peer_u, peer_v, ln2_g, ln2_b):
    assert w_in.shape[0] == DEPTH
    return _layer(x, mem, positions, w_in[0], nsa_pe_k[0], nsa_pe_v[0], nsa_cmp_w1_k[0], nsa_cmp_w2_k[0],
                  nsa_cmp_w1_v[0], nsa_cmp_w2_v[0], w_mem_kv[0], w_out[0], ln1_g[0], ln1_b[0], peer_w_query[0],
                  peer_sub_keys_1[0], peer_sub_keys_2[0], peer_u[0], peer_v[0], ln2_g[0], ln2_b[0])
```

```python
import functools

import numpy as np
import jax
import jax.numpy as jnp
from jax import lax
from jax.experimental import pallas as pl
from jax.experimental.pallas import tpu as pltpu

F32 = jnp.float32
BF16 = jnp.bfloat16

LANES = 128
SUBLANES = 8
BF16_ROWS = 16
MIB = 1 << 20

TILE = dict(
    proj=dict(tm=512, vmem=40 * MIB),
    compress=dict(vmem=16 * MIB),
    nsa=dict(c=256, tk=512, vmem=32 * MIB),
    dsa=dict(c=256, tk=512, vmem=32 * MIB),
    mem=dict(vmem=16 * MIB),
    out_ln=dict(tm=512, vmem=24 * MIB),
    peer=dict(tm=512, eb=1024, nsub=2, vmem=56 * MIB),
)

HEAD_DIM = 64
ROPE_THETA = 500000.0
LN_EPS = 1e-5
NSA_HEADS = 8
NSA_KV_HEADS = 2
CMP_LEN = 32
CMP_STRIDE = 16
CMP_HIDDEN = 128
SEL_BLOCK = 64
SEL_COUNT = 16
WINDOW = 512
FORCE_BONUS = 1e4
DSA_HEADS = 4
IDX_HEADS = 8
IDX_DIM = 32
DSA_TOPK_MAX = 256
MEM_HEADS = 4
PEER_HEADS = 8
PEER_N_KEYS = 128
PEER_KEY_DIM = 256
PEER_TOPK = 16
DEPTH = 1
DEEPNORM_ALPHA = (2.0 * DEPTH) ** 0.25

NEG = -1e30

BLK_QN = 0
BLK_NKV = 4
BLK_DQ = 10
BLK_IQ = 12
BLK_MQ = 14
BLK_DKV = 16
BLK_IK = 17
BLK_MISC = 18
N_BLK = 19
MISC_W = 0
MISC_G = IDX_HEADS


def _cparams(sem, vmem):
    return pltpu.CompilerParams(dimension_semantics=sem, vmem_limit_bytes=vmem)


def _regroup_w_in(w_in):
    d = HEAD_DIM
    o_q = 0
    o_kv = o_q + NSA_HEADS * d
    o_g = o_kv + 6 * NSA_KV_HEADS * d
    o_dq = o_g + 3 * NSA_HEADS
    o_dkv = o_dq + DSA_HEADS * d
    o_iq = o_dkv + 2 * d
    o_ik = o_iq + IDX_HEADS * IDX_DIM
    o_iw = o_ik + IDX_DIM
    o_mq = o_iw + IDX_HEADS
    cols = [w_in[:, o_q:o_kv]]
    for br in range(3):
        for g in range(NSA_KV_HEADS):
            k0 = o_kv + ((2 * br) * NSA_KV_HEADS + g) * d
            v0 = o_kv + ((2 * br + 1) * NSA_KV_HEADS + g) * d
            cols += [w_in[:, k0:k0 + d], w_in[:, v0:v0 + d]]
    cols.append(w_in[:, o_dq:o_dkv])
    cols.append(w_in[:, o_iq:o_ik])
    cols.append(w_in[:, o_mq:o_mq + MEM_HEADS * d])
    cols.append(w_in[:, o_dkv:o_iq])
    cols += [w_in[:, o_ik:o_iw]] * (LANES // IDX_DIM)
    cols += [w_in[:, o_iw:o_mq], w_in[:, o_g:o_dq]]
    pad = LANES - IDX_HEADS - 3 * NSA_HEADS
    cols.append(jnp.zeros((w_in.shape[0], pad), w_in.dtype))
    w = jnp.concatenate(cols, axis=1)
    assert w.shape[1] == N_BLK * LANES
    return w.astype(BF16)


def _rot_half(head_dim):
    return head_dim // 8


def _rope_inputs(positions):
    pos = positions.reshape(-1).astype(F32)
    vals, place, ones = [], [], []
    src = 0
    for hd in (HEAD_DIM, IDX_DIM):
        half = _rot_half(hd)
        freqs = jnp.power(ROPE_THETA, -jnp.arange(half, dtype=F32) * 2.0 / (2 * half))
        ang = pos[:, None] * freqs
        vals += [jnp.cos(ang), jnp.sin(ang)]
        d = np.arange(LANES) % hd
        lanes = np.arange(LANES)
        sel = np.zeros((3, LANES, LANES), np.float32)
        rot, first, second = d < 2 * half, d < half, (d >= half) & (d < 2 * half)
        sel[0, src + d[rot] % half, lanes[rot]] = 1.0
        sel[1, src + half + d[first], lanes[first]] = -1.0
        sel[2, src + half + d[second] - half, lanes[second]] = 1.0
        place += [sel[0], sel[1], sel[2]]
        ones += [(~rot).astype(np.float32), np.zeros(LANES, np.float32), np.zeros(LANES, np.float32)]
        src += 2 * half
    cs = jnp.concatenate(vals, axis=1)
    cs = jnp.pad(cs, ((0, 0), (0, LANES - cs.shape[1])))
    return cs, jnp.asarray(np.concatenate(place, axis=1), BF16), jnp.asarray(np.concatenate(ones)[None, :], F32)


def _proj_kernel(x_ref, w_ref, cs_ref, place_ref, ones_ref, hb_ref, misc_ref):
    xb = x_ref[...].astype(BF16)
    tm = xb.shape[0]
    lane = lax.broadcasted_iota(jnp.int32, (tm, LANES), 1)
    lo = lane < HEAD_DIM
    cs = cs_ref[...]
    hi = cs.astype(BF16)
    rest = cs - hi.astype(F32)
    mid = rest.astype(BF16)
    low = (rest - mid.astype(F32)).astype(BF16)
    tabs = ones_ref[...]
    for piece in (hi, mid, low):
        tabs = tabs + jnp.dot(piece, place_ref[...], preferred_element_type=F32)
    c64, sa64, sb64 = (tabs[:, i * LANES:(i + 1) * LANES] for i in range(3))
    c32, sa32, sb32 = (tabs[:, i * LANES:(i + 1) * LANES] for i in range(3, 6))
    c64h, sa64h, sb64h = jnp.where(lo, c64, 1.0), jnp.where(lo, sa64, 0.0), jnp.where(lo, sb64, 0.0)

    def rope(h, c, sa, sb, half):
        return h * c + pltpu.roll(h, LANES - half, 1) * sa + pltpu.roll(h, half, 1) * sb

    def finish(blk, h):
        if blk < BLK_NKV or BLK_DQ <= blk < BLK_IQ:
            return rope(h, c64, sa64, sb64, _rot_half(HEAD_DIM))
        if BLK_NKV <= blk < BLK_DQ or blk == BLK_DKV:
            return rope(h, c64h, sa64h, sb64h, _rot_half(HEAD_DIM))
        if BLK_IQ <= blk < BLK_MQ or blk == BLK_IK:
            return rope(h, c32, sa32, sb32, _rot_half(IDX_DIM))
        return h

    for j in range(BLK_MISC // 2):
        h = jnp.dot(xb, w_ref[:, j * 2 * LANES:(j + 1) * 2 * LANES], preferred_element_type=F32)
        for s in range(2):
            blk = 2 * j + s
            hb_ref[:, blk * LANES:(blk + 1) * LANES] = finish(blk, h[:, s * LANES:(s + 1) * LANES]).astype(BF16)
    misc_ref[...] = jnp.dot(xb, w_ref[:, BLK_MISC * LANES:], preferred_element_type=F32)


def _proj(x2, w2, rope, tm):
    t, dm = x2.shape
    cs, place, ones = rope
    return pl.pallas_call(
        _proj_kernel,
        grid=(t // tm,),
        in_specs=[
            pl.BlockSpec((tm, dm), lambda i: (i, 0)),
            pl.BlockSpec((dm, N_BLK * LANES), lambda i: (0, 0)),
            pl.BlockSpec((tm, LANES), lambda i: (i, 0)),
            pl.BlockSpec(place.shape, lambda i: (0, 0)),
            pl.BlockSpec(ones.shape, lambda i: (0, 0)),
        ],
        out_specs=[
            pl.BlockSpec((tm, BLK_MISC * LANES), lambda i: (i, 0)),
            pl.BlockSpec((tm, LANES), lambda i: (i, 0)),
        ],
        out_shape=[
            jax.ShapeDtypeStruct((t, BLK_MISC * LANES), BF16),
            jax.ShapeDtypeStruct((t, LANES), F32),
        ],
        compiler_params=_cparams(("parallel",), TILE["proj"]["vmem"]),
        name="proj",
    )(x2, w2, cs, place, ones)


def _gelu(x):
    return 0.5 * x * (1.0 + lax.erf(x * np.float32(1.0 / np.sqrt(2.0))))


def _compress_kernel(kch_ref, vch_ref, w1k_ref, w1v_ref, w2k_ref, w2v_ref, pek_ref, pev_ref, out_ref):
    half = w1k_ref.shape[0] // 2
    rows = kch_ref.shape[1]

    def hidden(ch_ref, w1_ref, pe_ref):
        ch = ch_ref[0]
        a = jnp.dot(ch, w1_ref[:half, :], preferred_element_type=F32)
        b = jnp.dot(ch, w1_ref[half:, :], preferred_element_type=F32)
        bias = jnp.dot(pe_ref[...], w1_ref[...], preferred_element_type=F32)[0:1, :]
        return _gelu(a + pltpu.roll(b, rows - 1, 0) + bias).astype(BF16)

    hk = hidden(kch_ref, w1k_ref, pek_ref)
    hv = hidden(vch_ref, w1v_ref, pev_ref)
    out = jnp.dot(hk, w2k_ref[...], preferred_element_type=F32) + jnp.dot(hv, w2v_ref[...], preferred_element_type=F32)
    out_ref[0] = out.astype(BF16)


def _compress(kch, vch, w1k, w1v, w2k, w2v, pek, pev):
    bg, rows, width = kch.shape
    full = lambda a: pl.BlockSpec(a.shape, lambda i: (0,) * a.ndim)
    return pl.pallas_call(
        _compress_kernel,
        grid=(bg,),
        in_specs=[
            pl.BlockSpec((1, rows, width), lambda i: (i, 0, 0)),
            pl.BlockSpec((1, rows, width), lambda i: (i, 0, 0)),
            full(w1k), full(w1v), full(w2k), full(w2v), full(pek), full(pev),
        ],
        out_specs=pl.BlockSpec((1, rows, LANES), lambda i: (i, 0, 0)),
        out_shape=jax.ShapeDtypeStruct((bg, rows, LANES), BF16),
        compiler_params=_cparams(("parallel",), TILE["compress"]["vmem"]),
        name="compress",
    )(kch, vch, w1k, w1v, w2k, w2v, pek, pev)


def _stack_heads(qpair_refs_or_vals, scale):
    outs = []
    for blk in qpair_refs_or_vals:
        b = blk.astype(F32) * scale
        lane = lax.broadcasted_iota(jnp.int32, b.shape, 1)
        lo = lane < HEAD_DIM
        outs.append(jnp.where(lo, b, 0.0))
        outs.append(jnp.where(lo, pltpu.roll(b, HEAD_DIM, 1), 0.0))
    return jnp.concatenate(outs, axis=0).astype(BF16)


def _qk(q, kv):
    return lax.dot_general(q, kv, (((1,), (1,)), ((), ())), preferred_element_type=F32)


def _lane_pick(x, col):
    lane = lax.broadcasted_iota(jnp.int32, x.shape, 1)
    return jnp.sum(jnp.where(lane == col, x, 0.0), axis=1, keepdims=True)


def _flash_init(n_heads, c):
    return jnp.full((n_heads, c, 1), NEG, F32), jnp.zeros((n_heads, c, LANES), F32)


def _flash_step(s, valid, kv, m, acc):
    n_heads, c, tk = s.shape
    sb = s.astype(BF16) + jnp.where(valid, 0.0, NEG).astype(BF16)
    m_new = jnp.maximum(m, jnp.max(sb, axis=-1, keepdims=True).astype(F32))
    p = jnp.exp(sb - m_new.astype(BF16))
    lane = lax.broadcasted_iota(jnp.int32, kv.shape, 1)
    ones_v = jnp.where(lane < HEAD_DIM, jnp.ones_like(kv), kv)
    pv = jnp.dot(p.reshape(n_heads * c, tk), ones_v, preferred_element_type=F32)
    return m_new, jnp.exp(m - m_new) * acc + pv.reshape(n_heads, c, LANES)


def _normalize(acc):
    den = acc[..., 0:1]
    return acc / jnp.where(den > 0.0, den, 1.0)


def _pack_heads(o, n_heads, c):
    lane = lax.broadcasted_iota(jnp.int32, (c, LANES), 1)
    lo = lane < HEAD_DIM
    blocks = []
    for p in range(n_heads // 2):
        a = o[(2 * p) * c:(2 * p + 1) * c]
        b = o[(2 * p + 1) * c:(2 * p + 2) * c]
        blocks.append(jnp.where(lo, pltpu.roll(a, HEAD_DIM, 1), b))
    return blocks


def _nsa_kernel(q_ref, misc_ref, kc_ref, ks_ref, kw_ref, ovl_ref, exp_ref, o_ref, *, c, tk, seq):
    g = pl.program_id(1)
    ci = pl.program_id(2)
    t0 = ci * c
    hpg = NSA_HEADS // NSA_KV_HEADS
    rows = hpg * c
    scale = HEAD_DIM ** -0.5
    q4 = _stack_heads([q_ref[:, :LANES], q_ref[:, LANES:]], scale)
    trow = t0 + lax.broadcasted_iota(jnp.int32, (c, 1), 0)

    kc = kc_ref[0]
    ncp = kc.shape[0]
    s = _qk(q4, kc).reshape(hpg, c, ncp)
    ncol = lax.broadcasted_iota(jnp.int32, (c, ncp), 1)
    cvalid = (ncol * CMP_STRIDE + (CMP_LEN - 1)) <= trow
    s = jnp.where(cvalid, s, NEG)
    m = jnp.max(s, axis=-1, keepdims=True)
    e = jnp.where(cvalid, jnp.exp(s - m), 0.0)
    den = jnp.sum(e, axis=-1, keepdims=True)
    p = (e / jnp.where(den > 0.0, den, 1.0)).reshape(rows, ncp)
    pb = p.astype(BF16)
    o_cmp = jnp.dot(pb, kc, preferred_element_type=F32)

    imp_t = lax.dot_general(ovl_ref[...], pb, (((1,), (1,)), ((), ())), preferred_element_type=F32)
    nbp = imp_t.shape[0]
    nb = seq // SEL_BLOCK
    imp = imp_t[:nb, 0:c]
    for h in range(1, hpg):
        imp = imp + imp_t[:nb, h * c:(h + 1) * c]
    jrow = lax.broadcasted_iota(jnp.int32, (nb, c), 0)
    tcol = t0 + lax.broadcasted_iota(jnp.int32, (nb, c), 1)
    cur = tcol // SEL_BLOCK
    forced = (jrow == 0) | (jrow == cur) | (jrow == cur - 1)
    visible = (jrow * SEL_BLOCK) <= tcol
    score = jnp.where(visible, imp + jnp.where(forced, FORCE_BONUS, 0.0), -jnp.inf)
    n_pick = min(SEL_COUNT, nb)
    rank, _ = _top_rows(score, n_pick, break_ties=True)
    sel = jnp.where(rank < n_pick, 1.0, 0.0)
    if nbp > nb:
        sel = jnp.concatenate([sel, jnp.zeros((nbp - nb, c), F32)], axis=0)
    sel_b = jnp.transpose(sel).astype(BF16)

    n_tiles = (t0 + c + tk - 1) // tk
    kcol = lax.broadcasted_iota(jnp.int32, (c, tk), 1)

    def sel_body(kt, carry):
        k0 = pl.multiple_of(kt * tk, tk)
        kv = ks_ref[pl.ds(k0, tk), :]
        s_ = _qk(q4, kv).reshape(hpg, c, tk)
        tok = jnp.dot(sel_b, exp_ref[:, pl.ds(k0, tk)], preferred_element_type=F32)
        valid = (tok > 0.5) & ((k0 + kcol) <= trow)
        return _flash_step(s_, valid, kv, *carry)

    init = _flash_init(hpg, c)
    _, acc_s = lax.fori_loop(0, n_tiles, sel_body, init)
    o_sel = _normalize(acc_s).reshape(rows, LANES)

    wk = WINDOW + c
    w0 = pl.multiple_of(jnp.maximum(t0 - WINDOW, 0), c)
    kvw = kw_ref[pl.ds(w0, wk), :]
    s = _qk(q4, kvw).reshape(hpg, c, wk)
    wpos = w0 + lax.broadcasted_iota(jnp.int32, (c, wk), 1)
    wvalid = (wpos <= trow) & (wpos > trow - WINDOW)
    _, acc_w = _flash_step(s, wvalid, kvw, *init)
    o_win = _normalize(acc_w).reshape(rows, LANES)

    gates = jax.nn.sigmoid(misc_ref[...])
    outs = []
    for h in range(hpg):
        col = MISC_G + (g * hpg + h) * 3
        sl = slice(h * c, (h + 1) * c)
        outs.append(_lane_pick(gates, col) * o_cmp[sl] + _lane_pick(gates, col + 1) * o_sel[sl]
                    + _lane_pick(gates, col + 2) * o_win[sl])
    blocks = _pack_heads(jnp.concatenate(outs, axis=0), hpg, c)
    for i, blk in enumerate(blocks):
        o_ref[:, i * LANES:(i + 1) * LANES] = blk.astype(BF16)


def _nsa(hb, misc, kcvc, ovl_t, expand, bsz, seq, c, tk):
    t = hb.shape[0]
    g_n = NSA_KV_HEADS
    nch = seq // c
    ncp = kcvc.shape[1]
    nbp = ovl_t.shape[0]
    kern = functools.partial(_nsa_kernel, c=c, tk=tk, seq=seq)
    return pl.pallas_call(
        kern,
        grid=(bsz, g_n, nch),
        in_specs=[
            pl.BlockSpec((c, 2 * LANES), lambda b, g, i: (b * nch + i, g)),
            pl.BlockSpec((c, LANES), lambda b, g, i: (b * nch + i, 0)),
            pl.BlockSpec((1, ncp, LANES), lambda b, g, i: (b * g_n + g, 0, 0)),
            pl.BlockSpec((seq, LANES), lambda b, g, i: (b, BLK_NKV + 2 + g)),
            pl.BlockSpec((seq, LANES), lambda b, g, i: (b, BLK_NKV + 4 + g)),
            pl.BlockSpec((nbp, ncp), lambda b, g, i: (0, 0)),
            pl.BlockSpec((nbp, seq), lambda b, g, i: (0, 0)),
        ],
        out_specs=pl.BlockSpec((c, 2 * LANES), lambda b, g, i: (b * nch + i, g)),
        out_shape=jax.ShapeDtypeStruct((t, NSA_HEADS * HEAD_DIM), BF16),
        compiler_params=_cparams(("parallel", "parallel", "arbitrary"), TILE["nsa"]["vmem"]),
        name="nsa",
    )(hb, misc, kcvc, hb, hb, ovl_t, expand)


def _nsa_consts(seq):
    nb = seq // SEL_BLOCK
    nbp = max(LANES, nb)
    rows = seq // CMP_STRIDE
    nc = (seq - CMP_LEN) // CMP_STRIDE + 1
    j = np.arange(nbp)[:, None]
    n = np.arange(rows)[None, :]
    ovl = np.clip(np.minimum(n * CMP_STRIDE + CMP_LEN, j * SEL_BLOCK + SEL_BLOCK)
                  - np.maximum(n * CMP_STRIDE, j * SEL_BLOCK), 0, None).astype(np.float32) / CMP_LEN
    ovl = np.where((n < nc) & (j < nb), ovl, 0.0)
    s = np.arange(seq)[None, :]
    expand = (s // SEL_BLOCK == j).astype(np.float32)
    return jnp.asarray(ovl, BF16), jnp.asarray(expand, BF16)


INT_MIN = -2147483648


def _dsa_kernel(q_ref, iq_ref, misc_ref, ik_ref, kv_ref, o_ref, key_scr, hi_scr, lo_scr, *, c, tk, seq, topk):
    ci = pl.program_id(1)
    t0 = ci * c
    n_tiles = (t0 + c + tk - 1) // tk
    tq = t0 + lax.broadcasted_iota(jnp.int32, (1, c), 1)
    krow = lax.broadcasted_iota(jnp.int32, (tk, c), 0)

    lane = lax.broadcasted_iota(jnp.int32, (c, LANES), 1)
    per_blk = LANES // IDX_DIM
    qs = []
    for h in range(IDX_HEADS):
        blk = iq_ref[:, (h // per_blk) * LANES:(h // per_blk + 1) * LANES]
        qs.append(jnp.where(lane // IDX_DIM == h % per_blk, blk, jnp.zeros_like(blk)))
    qst = jnp.concatenate(qs, axis=0)
    w_t = jnp.transpose(misc_ref[...] * (IDX_HEADS ** -0.5 * IDX_DIM ** -0.5))

    def score_body(kt, carry):
        k0 = pl.multiple_of(kt * tk, tk)
        lg = _qk(ik_ref[pl.ds(k0, tk), :], qst)
        sc = jnp.zeros((tk, c), F32)
        for h in range(IDX_HEADS):
            sc = sc + jnp.maximum(lg[:, h * c:(h + 1) * c], 0.0) * w_t[MISC_W + h:MISC_W + h + 1, :]
        sc = sc + 0.0
        sc = jnp.where((k0 + krow) <= tq, sc, -jnp.inf)
        bits = pltpu.bitcast(sc, jnp.int32)
        key = jnp.where(bits < 0, bits ^ jnp.int32(0x7FFFFFFF), bits)
        key_scr[pl.ds(k0, tk), :] = key
        hi_scr[pl.ds(k0, tk), :] = (key >> 16).astype(jnp.int16)
        lo_scr[pl.ds(k0, tk), :] = ((key & jnp.int32(0xFFFF)) - 32768).astype(jnp.int16)
        return carry

    lax.fori_loop(0, n_tiles, score_body, 0)

    n_acc = 8

    def count(pred):
        def body(kt, acc):
            k0 = pl.multiple_of(kt * tk, tk)
            hit = jnp.where(pred(key_scr[pl.ds(k0, tk), :], k0 + krow), 1.0, 0.0)
            return acc + jnp.sum(hit.reshape(tk // (8 * n_acc), n_acc * 8, c), axis=0)
        acc = lax.fori_loop(0, n_tiles, body, jnp.zeros((n_acc * 8, c), F32))
        return jnp.sum(acc, axis=0, keepdims=True)

    n_acc_h = 4
    assert seq // (BF16_ROWS * n_acc_h) < 2 ** 15

    def half_rows(x):
        return jnp.broadcast_to(x, (BF16_ROWS, c)).astype(jnp.int16)[None]

    def tiles16(ref, kt):
        return ref[pl.ds(pl.multiple_of(kt * tk, tk), tk), :].reshape(tk // BF16_ROWS, BF16_ROWS, c)

    def count16(ref, pred):
        def body(kt, acc):
            x = tiles16(ref, kt)
            hit = jnp.where(pred(x), jnp.ones_like(x), jnp.zeros_like(x))
            parts = hit.reshape(tk // (BF16_ROWS * n_acc_h), n_acc_h, BF16_ROWS, c)
            for g in range(parts.shape[0]):
                acc = acc + parts[g]
            return acc
        acc = lax.fori_loop(0, n_tiles, body, jnp.zeros((n_acc_h, BF16_ROWS, c), jnp.int16))
        return jnp.sum(acc.astype(F32).reshape(n_acc_h * BF16_ROWS, c), axis=0, keepdims=True)

    def hi_body(i, ans):
        cand = ans | (jnp.int32(1) << (31 - i))
        cand_b = half_rows((cand ^ jnp.int32(INT_MIN)) >> 16)
        cnt = count16(hi_scr, lambda hi: hi >= cand_b)
        return jnp.where(cnt >= topk, cand, ans)

    thr_hi = lax.fori_loop(0, 16, hi_body, jnp.zeros((1, c), jnp.int32))
    hi_b = half_rows((thr_hi ^ jnp.int32(INT_MIN)) >> 16)
    above = count16(hi_scr, lambda hi: hi > hi_b)

    def keep_equal(kt, carry):
        k0 = pl.multiple_of(kt * tk, tk)
        lo = jnp.where(tiles16(hi_scr, kt) == hi_b, tiles16(lo_scr, kt), jnp.int16(-32768))
        lo_scr[pl.ds(k0, tk), :] = lo.reshape(tk, c)
        return carry

    lax.fori_loop(0, n_tiles, keep_equal, 0)

    def lo_body(i, ans):
        cand = ans | (jnp.int32(1) << (15 - i))
        cand_b = half_rows(cand - 32768)
        cnt = above + count16(lo_scr, lambda lo: lo >= cand_b)
        return jnp.where(cnt >= topk, cand, ans)

    thr_lo = lax.fori_loop(0, 16, lo_body, jnp.zeros((1, c), jnp.int32))
    thr = (thr_hi | thr_lo) ^ jnp.int32(INT_MIN)
    lo_b = half_rows(thr_lo - 32768)
    need = topk - above - count16(lo_scr, lambda lo: lo > lo_b)
    n_ties = count(lambda keys, idx: keys == thr)

    n_bits = max(1, int(np.ceil(np.log2(seq))))

    def tie_search():
        def tie_body(i, ans):
            cand = ans | (jnp.int32(1) << (n_bits - 1 - i))
            cnt = count(lambda keys, idx: (keys == thr) & (idx < cand))
            return jnp.where(cnt < need, cand, ans)
        return lax.fori_loop(0, n_bits, tie_body, jnp.zeros((1, c), jnp.int32))

    surplus = jnp.max(n_ties - need)
    last_tie = lax.cond(surplus > 0.0, tie_search, lambda: jnp.full((1, c), seq, jnp.int32))

    q4 = _stack_heads([q_ref[:, :LANES], q_ref[:, LANES:]], HEAD_DIM ** -0.5)
    eye = (lax.broadcasted_iota(jnp.int32, (c, c), 0) == lax.broadcasted_iota(jnp.int32, (c, c), 1)).astype(BF16)

    def att_body(kt, carry):
        k0 = pl.multiple_of(kt * tk, tk)
        kv = kv_ref[pl.ds(k0, tk), :]
        keys = key_scr[pl.ds(k0, tk), :]
        idx = k0 + krow
        picked = (idx <= tq) & ((keys > thr) | ((keys == thr) & (idx <= last_tie)))
        valid = _qk(eye, jnp.where(picked, 1.0, 0.0).astype(BF16)) > 0.5
        s = _qk(q4, kv).reshape(DSA_HEADS, c, tk)
        return _flash_step(s, valid, kv, *carry)

    _, acc = lax.fori_loop(0, n_tiles, att_body, _flash_init(DSA_HEADS, c))
    o = _normalize(acc).reshape(DSA_HEADS * c, LANES)
    for i, blk in enumerate(_pack_heads(o, DSA_HEADS, c)):
        o_ref[:, i * LANES:(i + 1) * LANES] = blk.astype(BF16)


def _dsa(hb, misc, bsz, seq, c, tk):
    t = hb.shape[0]
    nch = seq // c
    topk = min(DSA_TOPK_MAX, seq // 4)
    kern = functools.partial(_dsa_kernel, c=c, tk=tk, seq=seq, topk=topk)
    return pl.pallas_call(
        kern,
        grid=(bsz, nch),
        in_specs=[
            pl.BlockSpec((c, 2 * LANES), lambda b, i: (b * nch + i, BLK_DQ // 2)),
            pl.BlockSpec((c, 2 * LANES), lambda b, i: (b * nch + i, BLK_IQ // 2)),
            pl.BlockSpec((c, LANES), lambda b, i: (b * nch + i, 0)),
            pl.BlockSpec((seq, LANES), lambda b, i: (b, BLK_IK)),
            pl.BlockSpec((seq, LANES), lambda b, i: (b, BLK_DKV)),
        ],
        out_specs=pl.BlockSpec((c, 2 * LANES), lambda b, i: (b * nch + i, 0)),
        out_shape=jax.ShapeDtypeStruct((t, DSA_HEADS * HEAD_DIM), BF16),
        scratch_shapes=[pltpu.VMEM((seq, c), jnp.int32), pltpu.VMEM((seq, c), jnp.int16),
                        pltpu.VMEM((seq, c), jnp.int16)],
        compiler_params=_cparams(("parallel", "arbitrary"), TILE["dsa"]["vmem"]),
        name="dsa",
    )(hb, hb, misc, hb, hb)


def _matmul_kernel(a_ref, w_ref, o_ref):
    o_ref[...] = jnp.dot(a_ref[...].astype(BF16), w_ref[...], preferred_element_type=F32).astype(o_ref.dtype)


def _matmul(a, w, tm, out_dtype):
    m, k = a.shape
    n = w.shape[1]
    return pl.pallas_call(
        _matmul_kernel,
        grid=(m // tm,),
        in_specs=[pl.BlockSpec((tm, k), lambda i: (i, 0)), pl.BlockSpec((k, n), lambda i: (0, 0))],
        out_specs=pl.BlockSpec((tm, n), lambda i: (i, 0)),
        out_shape=jax.ShapeDtypeStruct((m, n), out_dtype),
        compiler_params=_cparams(("parallel",), TILE["mem"]["vmem"]),
        name="matmul",
    )(a, w)


def _mem_attention(q_ref, mkv_ref, c):
    q4 = _stack_heads([q_ref[:, :LANES], q_ref[:, LANES:]], HEAD_DIM ** -0.5)
    outs = []
    for h in range(MEM_HEADS):
        kv = mkv_ref[:, h * LANES:(h + 1) * LANES]
        s = _qk(q4[h * c:(h + 1) * c], kv)
        e = jnp.exp(s - jnp.max(s, axis=-1, keepdims=True))
        p = e / jnp.sum(e, axis=-1, keepdims=True)
        outs.append(jnp.dot(p.astype(BF16), kv, preferred_element_type=F32))
    return _pack_heads(jnp.concatenate(outs, axis=0), MEM_HEADS, c)


def _layer_norm(v, g, b):
    mu = jnp.mean(v, axis=-1, keepdims=True)
    d = v - mu
    var = jnp.mean(d * d, axis=-1, keepdims=True)
    return d * lax.rsqrt(var + LN_EPS) * g + b


def _out_ln_kernel(on_ref, od_ref, mq_ref, mkv_ref, x_ref, w_ref, g_ref, b_ref, o_ref):
    n0 = on_ref.shape[1]
    n1 = n0 + od_ref.shape[1]
    o_mem = jnp.concatenate(_mem_attention(mq_ref, mkv_ref, x_ref.shape[0]), axis=1).astype(BF16)
    mix = jnp.dot(on_ref[...], w_ref[:n0, :], preferred_element_type=F32)
    mix = mix + jnp.dot(od_ref[...], w_ref[n0:n1, :], preferred_element_type=F32)
    mix = mix + jnp.dot(o_mem, w_ref[n1:, :], preferred_element_type=F32)
    o_ref[...] = _layer_norm(DEEPNORM_ALPHA * x_ref[...] + mix, g_ref[...], b_ref[...])


def _out_ln(o_nsa, o_dsa, hb, mkv, x2, w_out, g, b, seq, m_len, tm):
    t, dm = x2.shape
    per_batch = seq // tm
    row = lambda a: pl.BlockSpec((tm, a.shape[1]), lambda i: (i, 0))
    full = lambda a: pl.BlockSpec(a.shape, lambda i: (0, 0))
    return pl.pallas_call(
        _out_ln_kernel,
        grid=(t // tm,),
        in_specs=[
            row(o_nsa), row(o_dsa),
            pl.BlockSpec((tm, 2 * LANES), lambda i: (i, BLK_MQ // 2)),
            pl.BlockSpec((m_len, MEM_HEADS * LANES), lambda i: (i // per_batch, 0)),
            row(x2), full(w_out), full(g), full(b),
        ],
        out_specs=pl.BlockSpec((tm, dm), lambda i: (i, 0)),
        out_shape=jax.ShapeDtypeStruct((t, dm), F32),
        compiler_params=_cparams(("parallel",), TILE["out_ln"]["vmem"]),
        name="out_ln",
    )(o_nsa, o_dsa, hb, mkv, x2, w_out, g, b)


NOT_PICKED = 64.0


def _top_rows(s, k, break_ties):
    n_rows = s.shape[0]
    rows = lax.broadcasted_iota(jnp.int32, s.shape, 0)
    rank = jnp.full(s.shape, NOT_PICKED, F32)
    vals = []
    for r in range(k):
        best = jnp.max(s, axis=0, keepdims=True)
        hit = s == best
        if break_ties:
            hit = rows == jnp.min(jnp.where(hit, rows, n_rows), axis=0, keepdims=True)
        rank = jnp.where(hit, float(r), rank)
        s = jnp.where(hit, -jnp.inf, s)
        vals.append(best)
    return rank, vals


def _ranked_exactly(rank, k):
    return jnp.sum(jnp.where(rank < k, 1.0, 0.0), axis=0, keepdims=True) == float(k)


def _peer_select(s1, s2, k, break_ties):
    rank1, v1 = _top_rows(s1, k, break_ties)
    rank2, v2 = _top_rows(s2, k, break_ties)
    v2a = jnp.concatenate(v2, axis=0)
    v2lo = v2a[:8]
    r8 = lax.broadcasted_iota(jnp.int32, v2lo.shape, 0)
    pieces = [v1[0] + v2a, v1[1] + v2lo]
    for a in range(2, 8):
        pieces.append(jnp.where(r8 < k // (a + 1), v1[a] + v2lo, -jnp.inf))
    pieces.append(jnp.concatenate(v1[8:], axis=0) + v2[0])
    cand = jnp.concatenate(pieces, axis=0)
    rank_c, top = _top_rows(cand, k, break_ties)
    return rank1, rank2, rank_c, v1[0], v2[0], jnp.concatenate(top, axis=0)


def _peer_kernel(x1_ref, wqt_ref, k1_ref, k2_ref, u_ref, vt_ref, g_ref, b_ref, o_ref,
                 xt_scr, s1_scr, s2_scr, e1_scr, r2_scr, e2_scr, y_scr, *, tm, eb, nsub):
    j = pl.program_id(1)
    nk = PEER_N_KEYS
    half = PEER_KEY_DIM // 2
    n_lt = tm // LANES
    k = PEER_TOPK
    pack = BF16_ROWS

    @pl.when(j == 0)
    def _select():
        xt = jnp.transpose(x1_ref[...]).astype(BF16)
        xt_scr[...] = xt
        for h in range(PEER_HEADS):
            qh = jnp.dot(wqt_ref[h * PEER_KEY_DIM:(h + 1) * PEER_KEY_DIM, :], xt, preferred_element_type=F32).astype(BF16)
            s1_scr[h] = jnp.dot(k1_ref[...], qh[:half], preferred_element_type=F32)
            s2_scr[h] = jnp.dot(k2_ref[...], qh[half:], preferred_element_type=F32)

        n_par = 4
        per_head = n_lt // n_par

        def chunk(i, carry):
            h = i // per_head
            lanes = [pl.multiple_of(((i % per_head) * n_par + p) * LANES, LANES) for p in range(n_par)]
            s1s = [s1_scr[h, :, pl.ds(l0, LANES)] for l0 in lanes]
            s2s = [s2_scr[h, :, pl.ds(l0, LANES)] for l0 in lanes]
            fast = [_peer_select(s1, s2, k, break_ties=False) for s1, s2 in zip(s1s, s2s)]
            n_unclean = 0.0
            for f in fast:
                clean = _ranked_exactly(f[0], k) & _ranked_exactly(f[1], k) & _ranked_exactly(f[2], k)
                n_unclean = n_unclean + jnp.sum(jnp.where(clean, 0.0, 1.0))
            picks = lax.cond(
                n_unclean > 0.0,
                lambda: [_peer_select(s1, s2, k, break_ties=True) for s1, s2 in zip(s1s, s2s)],
                lambda: fast)
            for l0, s1, s2, (rank1, rank2, rank_c, v1_max, v2_max, top) in zip(lanes, s1s, s2s, picks):
                den = jnp.sum(jnp.exp(top - top[0:1]), axis=0, keepdims=True)
                picked = jnp.where(rank_c < k, 1.0, 0.0)
                n_of_rank = [jnp.sum(picked[0:16], axis=0, keepdims=True)]
                n_of_rank += [jnp.sum(picked[8 * a + 8:8 * a + 16], axis=0, keepdims=True) for a in range(1, 8)]
                n_of_rank += [picked[72 + a:73 + a] for a in range(8)]
                n1 = jnp.zeros_like(s1)
                for a in range(k):
                    n1 = jnp.where(rank1 == float(a), n_of_rank[a], n1)
                s1_scr[h, :, pl.ds(l0, LANES)] = n1
                e1_scr[h, :, pl.ds(l0, LANES)] = jnp.where(rank1 < k, 0.5 * jnp.exp(s1 - v1_max), 0.0)
                slab = h * n_lt + l0 // LANES
                r2_scr[slab] = rank2.astype(BF16)
                e2_scr[slab] = (jnp.where(rank2 < k, jnp.exp(s2 - v2_max), 0.0) / den).astype(BF16)
            return carry

        lax.fori_loop(0, PEER_HEADS * per_head, chunk, 0)
        y_scr[...] = jnp.zeros_like(y_scr)

    xt = xt_scr[...]
    y_add = None
    rows_per_step = nsub * eb // nk
    for sub in range(nsub):
        a = jnp.dot(u_ref[sub * eb:(sub + 1) * eb, :], xt, preferred_element_type=F32)
        zs = []
        for ib in range(eb // nk):
            row = sub * (eb // nk) + ib
            grp = pl.multiple_of(j * rows_per_step + (row // SUBLANES) * SUBLANES, SUBLANES)
            r_in = row % SUBLANES
            zrow = []
            for lt in range(n_lt):
                ls = slice(lt * LANES, (lt + 1) * LANES)
                w = jnp.zeros((nk // pack, pack, LANES), BF16)
                for h in range(PEER_HEADS):
                    n1r = s1_scr[h, pl.ds(grp, SUBLANES), ls][r_in:r_in + 1]
                    e1r = e1_scr[h, pl.ds(grp, SUBLANES), ls][r_in:r_in + 1]
                    n1b = jnp.broadcast_to(n1r, (pack, LANES)).astype(BF16)[None]
                    e1b = jnp.broadcast_to(e1r, (pack, LANES)).astype(BF16)[None]
                    r2 = r2_scr[h * n_lt + lt].reshape(nk // pack, pack, LANES)
                    e2 = e2_scr[h * n_lt + lt].reshape(nk // pack, pack, LANES)
                    w = w + jnp.where(r2 < n1b, e2, jnp.zeros_like(e2)) * e1b
                x_blk = a[ib * nk:(ib + 1) * nk, ls]
                act = (x_blk * (1.0 + lax.erf(x_blk * np.float32(1.0 / np.sqrt(2.0))))).astype(BF16)
                zrow.append(w.reshape(nk, LANES) * act)
            zs.append(jnp.concatenate(zrow, axis=1))
        z = jnp.concatenate(zs, axis=0)
        y_sub = jnp.dot(vt_ref[:, sub * eb:(sub + 1) * eb], z, preferred_element_type=F32)
        y_add = y_sub if y_add is None else y_add + y_sub
    y_scr[...] += y_add

    @pl.when(j == pl.num_programs(1) - 1)
    def _finish():
        y = jnp.transpose(y_scr[...])
        o_ref[...] = _layer_norm(DEEPNORM_ALPHA * x1_ref[...] + y, g_ref[...], b_ref[...])


def _peer(x1, wqt, k1, k2, u, vt, g, b, tm, eb, nsub):
    t, dm = x1.shape
    n_e = u.shape[0]
    assert (nsub * eb // PEER_N_KEYS) % SUBLANES == 0, "a grid step covers whole sublane groups of first-key rows"
    assert PEER_TOPK == 16 and PEER_N_KEYS == LANES, "the candidate-pair layout in _peer_select is written for k = 16"
    full = lambda a: pl.BlockSpec(a.shape, lambda i, j: (0, 0))
    tab = pltpu.VMEM((PEER_HEADS, PEER_N_KEYS, tm), F32)
    tab_b = pltpu.VMEM((PEER_HEADS * (tm // LANES), PEER_N_KEYS, LANES), BF16)
    return pl.pallas_call(
        functools.partial(_peer_kernel, tm=tm, eb=eb, nsub=nsub),
        grid=(t // tm, n_e // (nsub * eb)),
        in_specs=[
            pl.BlockSpec((tm, dm), lambda i, j: (i, 0)),
            full(wqt), full(k1), full(k2),
            pl.BlockSpec((nsub * eb, dm), lambda i, j: (j, 0)),
            pl.BlockSpec((dm, nsub * eb), lambda i, j: (0, j)),
            full(g), full(b),
        ],
        out_specs=pl.BlockSpec((tm, dm), lambda i, j: (i, 0)),
        out_shape=jax.ShapeDtypeStruct((t, dm), F32),
        scratch_shapes=[
            pltpu.VMEM((dm, tm), BF16), tab, tab, tab, tab_b, tab_b,
            pltpu.VMEM((dm, tm), F32),
        ],
        compiler_params=_cparams(("parallel", "arbitrary"), TILE["peer"]["vmem"]),
        name="peer",
    )(x1, wqt, k1, k2, u, vt, g, b)


def _layer(x, mem, positions, w_in, pe_k, pe_v, w1k, w2k, w1v, w2v, w_mem_kv, w_out, ln1_g, ln1_b,
           w_query, sk1, sk2, pu, pv, ln2_g, ln2_b):
    bsz, seq, dm = x.shape
    m_len = mem.shape[1]
    t = bsz * seq
    d = HEAD_DIM
    g_n = NSA_KV_HEADS
    nsa_t, dsa_t, peer_t = TILE["nsa"], TILE["dsa"], TILE["peer"]
    assert seq % nsa_t["tk"] == 0 and seq % dsa_t["tk"] == 0 and seq >= WINDOW + nsa_t["c"]
    assert seq % TILE["out_ln"]["tm"] == 0, "an out_ln token tile stays inside one batch row (one memory block)"
    assert dsa_t["tk"] >= min(DSA_TOPK_MAX, seq // 4), "a key tile holds at least top-k candidates"
    assert t % peer_t["tm"] == 0 and t % TILE["proj"]["tm"] == 0 and t % TILE["out_ln"]["tm"] == 0
    x2 = x.reshape(t, dm)

    hb, misc = _proj(x2, _regroup_w_in(w_in), _rope_inputs(positions), tm=TILE["proj"]["tm"])

    rows = seq // CMP_STRIDE
    cmp = hb[:, BLK_NKV * LANES:(BLK_NKV + g_n) * LANES].reshape(bsz, seq, g_n, 2, d)
    cmp = cmp.transpose(0, 2, 3, 1, 4).reshape(bsz * g_n, 2, rows, CMP_STRIDE * d)
    zpad = jnp.zeros((CMP_HIDDEN, d), BF16)
    pe_rows = lambda pe: jnp.pad(pe.reshape(1, CMP_LEN * d), ((0, 7), (0, 0))).astype(BF16)
    kcvc = _compress(cmp[:, 0], cmp[:, 1], w1k.astype(BF16), w1v.astype(BF16),
                     jnp.concatenate([w2k.astype(BF16), zpad], axis=1), jnp.concatenate([zpad, w2v.astype(BF16)], axis=1),
                     pe_rows(pe_k), pe_rows(pe_v))

    ovl_t, expand = _nsa_consts(seq)
    o_nsa = _nsa(hb, misc, kcvc, ovl_t, expand, bsz, seq, c=nsa_t["c"], tk=nsa_t["tk"])
    o_dsa = _dsa(hb, misc, bsz, seq, c=dsa_t["c"], tk=dsa_t["tk"])

    wm = w_mem_kv.reshape(dm, 2, MEM_HEADS, d).transpose(0, 2, 1, 3).reshape(dm, MEM_HEADS * 2 * d).astype(BF16)
    mkv = _matmul(mem.reshape(bsz * m_len, dm), wm, tm=m_len, out_dtype=BF16)
    x1 = _out_ln(o_nsa, o_dsa, hb, mkv, x2, w_out.astype(BF16), ln1_g.reshape(1, dm), ln1_b.reshape(1, dm),
                 seq, m_len, tm=TILE["out_ln"]["tm"])

    x2o = _peer(x1, w_query.T.astype(BF16), sk1.astype(BF16), sk2.astype(BF16), pu.astype(BF16), pv.T.astype(BF16),
                ln2_g.reshape(1, dm), ln2_b.reshape(1, dm), tm=peer_t["tm"], eb=peer_t["eb"], nsub=peer_t["nsub"])
    return x2o.reshape(bsz, seq, dm)


def kernel(x, mem, positions, w_in, nsa_pe_k, nsa_pe_v, nsa_cmp_w1_k, nsa_cmp_w2_k, nsa_cmp_w1_v, nsa_cmp_w2_v,
           w_mem_kv, w_out, ln1_g, ln1_b, peer_w_query, peer_sub_keys_1, peer_sub_keys_2, peer_u, peer_v, ln2_g, ln2_b):
    assert w_in.shape[0] == DEPTH
    return _layer(x, mem, positions, w_in[0], nsa_pe_k[0], nsa_pe_v[0], nsa_cmp_w1_k[0], nsa_cmp_w2_k[0],
                  nsa_cmp_w1_v[0], nsa_cmp_w2_v[0], w_mem_kv[0], w_out[0], ln1_g[0], ln1_b[0], peer_w_query[0],
                  peer_sub_keys_1[0], peer_sub_keys_2[0], peer_u[0], peer_v[0], ln2_g[0], ln2_b[0])
```

```python
import functools

import numpy as np
import jax
import jax.numpy as jnp
from jax import lax
from jax.experimental import pallas as pl
from jax.experimental.pallas import tpu as pltpu

F32 = jnp.float32
BF16 = jnp.bfloat16

LANES = 128
SUBLANES = 8
BF16_ROWS = 16
MIB = 1 << 20

TILE = dict(
    proj=dict(tm=512, vmem=40 * MIB),
    compress=dict(vmem=16 * MIB),
    nsa=dict(c=256, tk=512, vmem=32 * MIB),
    dsa=dict(c=256, tk=512, vmem=32 * MIB),
    mem=dict(vmem=16 * MIB),
    out_ln=dict(tm=512, vmem=24 * MIB),
    peer=dict(tm=512, eb=1024, nsub=2, vmem=56 * MIB),
)

HEAD_DIM = 64
ROPE_THETA = 500000.0
LN_EPS = 1e-5
NSA_HEADS = 8
NSA_KV_HEADS = 2
CMP_LEN = 32
CMP_STRIDE = 16
CMP_HIDDEN = 128
SEL_BLOCK = 64
SEL_COUNT = 16
WINDOW = 512
FORCE_BONUS = 1e4
DSA_HEADS = 4
IDX_HEADS = 8
IDX_DIM = 32
DSA_TOPK_MAX = 256
MEM_HEADS = 4
PEER_HEADS = 8
PEER_N_KEYS = 128
PEER_KEY_DIM = 256
PEER_TOPK = 16
DEPTH = 1
DEEPNORM_ALPHA = (2.0 * DEPTH) ** 0.25

NEG = -1e30

BLK_QN = 0
BLK_NKV = 4
BLK_DQ = 10
BLK_IQ = 12
BLK_MQ = 14
BLK_DKV = 16
BLK_IK = 17
BLK_MISC = 18
N_BLK = 19
MISC_W = 0
MISC_G = IDX_HEADS


def _cparams(sem, vmem):
    return pltpu.CompilerParams(dimension_semantics=sem, vmem_limit_bytes=vmem)


def _regroup_w_in(w_in):
    d = HEAD_DIM
    o_q = 0
    o_kv = o_q + NSA_HEADS * d
    o_g = o_kv + 6 * NSA_KV_HEADS * d
    o_dq = o_g + 3 * NSA_HEADS
    o_dkv = o_dq + DSA_HEADS * d
    o_iq = o_dkv + 2 * d
    o_ik = o_iq + IDX_HEADS * IDX_DIM
    o_iw = o_ik + IDX_DIM
    o_mq = o_iw + IDX_HEADS
    cols = [w_in[:, o_q:o_kv]]
    for br in range(3):
        for g in range(NSA_KV_HEADS):
            k0 = o_kv + ((2 * br) * NSA_KV_HEADS + g) * d
            v0 = o_kv + ((2 * br + 1) * NSA_KV_HEADS + g) * d
            cols += [w_in[:, k0:k0 + d], w_in[:, v0:v0 + d]]
    cols.append(w_in[:, o_dq:o_dkv])
    cols.append(w_in[:, o_iq:o_ik])
    cols.append(w_in[:, o_mq:o_mq + MEM_HEADS * d])
    cols.append(w_in[:, o_dkv:o_iq])
    cols += [w_in[:, o_ik:o_iw]] * (LANES // IDX_DIM)
    cols += [w_in[:, o_iw:o_mq], w_in[:, o_g:o_dq]]
    pad = LANES - IDX_HEADS - 3 * NSA_HEADS
    cols.append(jnp.zeros((w_in.shape[0], pad), w_in.dtype))
    w = jnp.concatenate(cols, axis=1)
    assert w.shape[1] == N_BLK * LANES
    return w.astype(BF16)


def _rot_half(head_dim):
    return head_dim // 8


def _rope_inputs(positions):
    pos = positions.reshape(-1).astype(F32)
    vals, place, ones = [], [], []
    src = 0
    for hd in (HEAD_DIM, IDX_DIM):
        half = _rot_half(hd)
        freqs = jnp.power(ROPE_THETA, -jnp.arange(half, dtype=F32) * 2.0 / (2 * half))
        ang = pos[:, None] * freqs
        vals += [jnp.cos(ang), jnp.sin(ang)]
        d = np.arange(LANES) % hd
        lanes = np.arange(LANES)
        sel = np.zeros((3, LANES, LANES), np.float32)
        rot, first, second = d < 2 * half, d < half, (d >= half) & (d < 2 * half)
        sel[0, src + d[rot] % half, lanes[rot]] = 1.0
        sel[1, src + half + d[first], lanes[first]] = -1.0
        sel[2, src + half + d[second] - half, lanes[second]] = 1.0
        place += [sel[0], sel[1], sel[2]]
        ones += [(~rot).astype(np.float32), np.zeros(LANES, np.float32), np.zeros(LANES, np.float32)]
        src += 2 * half
    cs = jnp.concatenate(vals, axis=1)
    cs = jnp.pad(cs, ((0, 0), (0, LANES - cs.shape[1])))
    return cs, jnp.asarray(np.concatenate(place, axis=1), BF16), jnp.asarray(np.concatenate(ones)[None, :], F32)


def _proj_kernel(x_ref, w_ref, cs_ref, place_ref, ones_ref, hb_ref, misc_ref):
    xb = x_ref[...].astype(BF16)
    tm = xb.shape[0]
    lane = lax.broadcasted_iota(jnp.int32, (tm, LANES), 1)
    lo = lane < HEAD_DIM
    cs = cs_ref[...]
    hi = cs.astype(BF16)
    rest = cs - hi.astype(F32)
    mid = rest.astype(BF16)
    low = (rest - mid.astype(F32)).astype(BF16)
    tabs = ones_ref[...]
    for piece in (hi, mid, low):
        tabs = tabs + jnp.dot(piece, place_ref[...], preferred_element_type=F32)
    c64, sa64, sb64 = (tabs[:, i * LANES:(i + 1) * LANES] for i in range(3))
    c32, sa32, sb32 = (tabs[:, i * LANES:(i + 1) * LANES] for i in range(3, 6))
    c64h, sa64h, sb64h = jnp.where(lo, c64, 1.0), jnp.where(lo, sa64, 0.0), jnp.where(lo, sb64, 0.0)

    def rope(h, c, sa, sb, half):
        return h * c + pltpu.roll(h, LANES - half, 1) * sa + pltpu.roll(h, half, 1) * sb

    def finish(blk, h):
        if blk < BLK_NKV or BLK_DQ <= blk < BLK_IQ:
            return rope(h, c64, sa64, sb64, _rot_half(HEAD_DIM))
        if BLK_NKV <= blk < BLK_DQ or blk == BLK_DKV:
            return rope(h, c64h, sa64h, sb64h, _rot_half(HEAD_DIM))
        if BLK_IQ <= blk < BLK_MQ or blk == BLK_IK:
            return rope(h, c32, sa32, sb32, _rot_half(IDX_DIM))
        return h

    for j in range(BLK_MISC // 2):
        h = jnp.dot(xb, w_ref[:, j * 2 * LANES:(j + 1) * 2 * LANES], preferred_element_type=F32)
        for s in range(2):
            blk = 2 * j + s
            hb_ref[:, blk * LANES:(blk + 1) * LANES] = finish(blk, h[:, s * LANES:(s + 1) * LANES]).astype(BF16)
    misc_ref[...] = jnp.dot(xb, w_ref[:, BLK_MISC * LANES:], preferred_element_type=F32)


def _proj(x2, w2, rope, tm):
    t, dm = x2.shape
    cs, place, ones = rope
    return pl.pallas_call(
        _proj_kernel,
        grid=(t // tm,),
        in_specs=[
            pl.BlockSpec((tm, dm), lambda i: (i, 0)),
            pl.BlockSpec((dm, N_BLK * LANES), lambda i: (0, 0)),
            pl.BlockSpec((tm, LANES), lambda i: (i, 0)),
            pl.BlockSpec(place.shape, lambda i: (0, 0)),
            pl.BlockSpec(ones.shape, lambda i: (0, 0)),
        ],
        out_specs=[
            pl.BlockSpec((tm, BLK_MISC * LANES), lambda i: (i, 0)),
            pl.BlockSpec((tm, LANES), lambda i: (i, 0)),
        ],
        out_shape=[
            jax.ShapeDtypeStruct((t, BLK_MISC * LANES), BF16),
            jax.ShapeDtypeStruct((t, LANES), F32),
        ],
        compiler_params=_cparams(("parallel",), TILE["proj"]["vmem"]),
        name="proj",
    )(x2, w2, cs, place, ones)


def _gelu(x):
    return 0.5 * x * (1.0 + lax.erf(x * np.float32(1.0 / np.sqrt(2.0))))


def _compress_kernel(kch_ref, vch_ref, w1k_ref, w1v_ref, w2k_ref, w2v_ref, pek_ref, pev_ref, out_ref):
    half = w1k_ref.shape[0] // 2
    rows = kch_ref.shape[1]

    def hidden(ch_ref, w1_ref, pe_ref):
        ch = ch_ref[0]
        a = jnp.dot(ch, w1_ref[:half, :], preferred_element_type=F32)
        b = jnp.dot(ch, w1_ref[half:, :], preferred_element_type=F32)
        bias = jnp.dot(pe_ref[...], w1_ref[...], preferred_element_type=F32)[0:1, :]
        return _gelu(a + pltpu.roll(b, rows - 1, 0) + bias).astype(BF16)

    hk = hidden(kch_ref, w1k_ref, pek_ref)
    hv = hidden(vch_ref, w1v_ref, pev_ref)
    out = jnp.dot(hk, w2k_ref[...], preferred_element_type=F32) + jnp.dot(hv, w2v_ref[...], preferred_element_type=F32)
    out_ref[0] = out.astype(BF16)


def _compress(kch, vch, w1k, w1v, w2k, w2v, pek, pev):
    bg, rows, width = kch.shape
    full = lambda a: pl.BlockSpec(a.shape, lambda i: (0,) * a.ndim)
    return pl.pallas_call(
        _compress_kernel,
        grid=(bg,),
        in_specs=[
            pl.BlockSpec((1, rows, width), lambda i: (i, 0, 0)),
            pl.BlockSpec((1, rows, width), lambda i: (i, 0, 0)),
            full(w1k), full(w1v), full(w2k), full(w2v), full(pek), full(pev),
        ],
        out_specs=pl.BlockSpec((1, rows, LANES), lambda i: (i, 0, 0)),
        out_shape=jax.ShapeDtypeStruct((bg, rows, LANES), BF16),
        compiler_params=_cparams(("parallel",), TILE["compress"]["vmem"]),
        name="compress",
    )(kch, vch, w1k, w1v, w2k, w2v, pek, pev)


def _stack_heads(qpair_refs_or_vals, scale):
    outs = []
    for blk in qpair_refs_or_vals:
        b = blk.astype(F32) * scale
        lane = lax.broadcasted_iota(jnp.int32, b.shape, 1)
        lo = lane < HEAD_DIM
        outs.append(jnp.where(lo, b, 0.0))
        outs.append(jnp.where(lo, pltpu.roll(b, HEAD_DIM, 1), 0.0))
    return jnp.concatenate(outs, axis=0).astype(BF16)


def _qk(q, kv):
    return lax.dot_general(q, kv, (((1,), (1,)), ((), ())), preferred_element_type=F32)


def _lane_pick(x, col):
    lane = lax.broadcasted_iota(jnp.int32, x.shape, 1)
    return jnp.sum(jnp.where(lane == col, x, 0.0), axis=1, keepdims=True)


def _flash_init(n_heads, c):
    return jnp.full((n_heads, c, 1), NEG, F32), jnp.zeros((n_heads, c, LANES), F32)


def _flash_step(s, valid, kv, m, acc):
    n_heads, c, tk = s.shape
    sb = s.astype(BF16) + jnp.where(valid, 0.0, NEG).astype(BF16)
    m_new = jnp.maximum(m, jnp.max(sb, axis=-1, keepdims=True).astype(F32))
    p = jnp.exp(sb - m_new.astype(BF16))
    lane = lax.broadcasted_iota(jnp.int32, kv.shape, 1)
    ones_v = jnp.where(lane < HEAD_DIM, jnp.ones_like(kv), kv)
    pv = jnp.dot(p.reshape(n_heads * c, tk), ones_v, preferred_element_type=F32)
    return m_new, jnp.exp(m - m_new) * acc + pv.reshape(n_heads, c, LANES)


def _normalize(acc):
    den = acc[..., 0:1]
    return acc / jnp.where(den > 0.0, den, 1.0)


def _pack_heads(o, n_heads, c):
    lane = lax.broadcasted_iota(jnp.int32, (c, LANES), 1)
    lo = lane < HEAD_DIM
    blocks = []
    for p in range(n_heads // 2):
        a = o[(2 * p) * c:(2 * p + 1) * c]
        b = o[(2 * p + 1) * c:(2 * p + 2) * c]
        blocks.append(jnp.where(lo, pltpu.roll(a, HEAD_DIM, 1), b))
    return blocks


def _nsa_kernel(q_ref, misc_ref, kc_ref, ks_ref, kw_ref, ovl_ref, exp_ref, o_ref, *, c, tk, seq):
    g = pl.program_id(1)
    ci = pl.program_id(2)
    t0 = ci * c
    hpg = NSA_HEADS // NSA_KV_HEADS
    rows = hpg * c
    scale = HEAD_DIM ** -0.5
    q4 = _stack_heads([q_ref[:, :LANES], q_ref[:, LANES:]], scale)
    trow = t0 + lax.broadcasted_iota(jnp.int32, (c, 1), 0)

    kc = kc_ref[0]
    ncp = kc.shape[0]
    s = _qk(q4, kc).reshape(hpg, c, ncp)
    ncol = lax.broadcasted_iota(jnp.int32, (c, ncp), 1)
    cvalid = (ncol * CMP_STRIDE + (CMP_LEN - 1)) <= trow
    s = jnp.where(cvalid, s, NEG)
    m = jnp.max(s, axis=-1, keepdims=True)
    e = jnp.where(cvalid, jnp.exp(s - m), 0.0)
    den = jnp.sum(e, axis=-1, keepdims=True)
    p = (e / jnp.where(den > 0.0, den, 1.0)).reshape(rows, ncp)
    pb = p.astype(BF16)
    o_cmp = jnp.dot(pb, kc, preferred_element_type=F32)

    imp_t = lax.dot_general(ovl_ref[...], pb, (((1,), (1,)), ((), ())), preferred_element_type=F32)
    nbp = imp_t.shape[0]
    nb = seq // SEL_BLOCK
    imp = imp_t[:nb, 0:c]
    for h in range(1, hpg):
        imp = imp + imp_t[:nb, h * c:(h + 1) * c]
    jrow = lax.broadcasted_iota(jnp.int32, (nb, c), 0)
    tcol = t0 + lax.broadcasted_iota(jnp.int32, (nb, c), 1)
    cur = tcol // SEL_BLOCK
    forced = (jrow == 0) | (jrow == cur) | (jrow == cur - 1)
    visible = (jrow * SEL_BLOCK) <= tcol
    score = jnp.where(visible, imp + jnp.where(forced, FORCE_BONUS, 0.0), -jnp.inf)
    n_pick = min(SEL_COUNT, nb)
    rank, _ = _top_rows(score, n_pick, break_ties=True)
    sel = jnp.where(rank < n_pick, 1.0, 0.0)
    if nbp > nb:
        sel = jnp.concatenate([sel, jnp.zeros((nbp - nb, c), F32)], axis=0)
    sel_b = jnp.transpose(sel).astype(BF16)

    n_tiles = (t0 + c + tk - 1) // tk
    kcol = lax.broadcasted_iota(jnp.int32, (c, tk), 1)

    def sel_body(kt, carry):
        k0 = pl.multiple_of(kt * tk, tk)
        kv = ks_ref[pl.ds(k0, tk), :]
        s_ = _qk(q4, kv).reshape(hpg, c, tk)
        tok = jnp.dot(sel_b, exp_ref[:, pl.ds(k0, tk)], preferred_element_type=F32)
        valid = (tok > 0.5) & ((k0 + kcol) <= trow)
        return _flash_step(s_, valid, kv, *carry)

    init = _flash_init(hpg, c)
    _, acc_s = lax.fori_loop(0, n_tiles, sel_body, init)
    o_sel = _normalize(acc_s).reshape(rows, LANES)

    wk = WINDOW + c
    w0 = pl.multiple_of(jnp.maximum(t0 - WINDOW, 0), c)
    kvw = kw_ref[pl.ds(w0, wk), :]
    s = _qk(q4, kvw).reshape(hpg, c, wk)
    wpos = w0 + lax.broadcasted_iota(jnp.int32, (c, wk), 1)
    wvalid = (wpos <= trow) & (wpos > trow - WINDOW)
    _, acc_w = _flash_step(s, wvalid, kvw, *init)
    o_win = _normalize(acc_w).reshape(rows, LANES)

    gates = jax.nn.sigmoid(misc_ref[...])
    outs = []
    for h in range(hpg):
        col = MISC_G + (g * hpg + h) * 3
        sl = slice(h * c, (h + 1) * c)
        outs.append(_lane_pick(gates, col) * o_cmp[sl] + _lane_pick(gates, col + 1) * o_sel[sl]
                    + _lane_pick(gates, col + 2) * o_win[sl])
    blocks = _pack_heads(jnp.concatenate(outs, axis=0), hpg, c)
    for i, blk in enumerate(blocks):
        o_ref[:, i * LANES:(i + 1) * LANES] = blk.astype(BF16)


def _nsa(hb, misc, kcvc, ovl_t, expand, bsz, seq, c, tk):
    t = hb.shape[0]
    g_n = NSA_KV_HEADS
    nch = seq // c
    ncp = kcvc.shape[1]
    nbp = ovl_t.shape[0]
    kern = functools.partial(_nsa_kernel, c=c, tk=tk, seq=seq)
    return pl.pallas_call(
        kern,
        grid=(bsz, g_n, nch),
        in_specs=[
            pl.BlockSpec((c, 2 * LANES), lambda b, g, i: (b * nch + i, g)),
            pl.BlockSpec((c, LANES), lambda b, g, i: (b * nch + i, 0)),
            pl.BlockSpec((1, ncp, LANES), lambda b, g, i: (b * g_n + g, 0, 0)),
            pl.BlockSpec((seq, LANES), lambda b, g, i: (b, BLK_NKV + 2 + g)),
            pl.BlockSpec((seq, LANES), lambda b, g, i: (b, BLK_NKV + 4 + g)),
            pl.BlockSpec((nbp, ncp), lambda b, g, i: (0, 0)),
            pl.BlockSpec((nbp, seq), lambda b, g, i: (0, 0)),
        ],
        out_specs=pl.BlockSpec((c, 2 * LANES), lambda b, g, i: (b * nch + i, g)),
        out_shape=jax.ShapeDtypeStruct((t, NSA_HEADS * HEAD_DIM), BF16),
        compiler_params=_cparams(("parallel", "parallel", "arbitrary"), TILE["nsa"]["vmem"]),
        name="nsa",
    )(hb, misc, kcvc, hb, hb, ovl_t, expand)


def _nsa_consts(seq):
    nb = seq // SEL_BLOCK
    nbp = max(LANES, nb)
    rows = seq // CMP_STRIDE
    nc = (seq - CMP_LEN) // CMP_STRIDE + 1
    j = np.arange(nbp)[:, None]
    n = np.arange(rows)[None, :]
    ovl = np.clip(np.minimum(n * CMP_STRIDE + CMP_LEN, j * SEL_BLOCK + SEL_BLOCK)
                  - np.maximum(n * CMP_STRIDE, j * SEL_BLOCK), 0, None).astype(np.float32) / CMP_LEN
    ovl = np.where((n < nc) & (j < nb), ovl, 0.0)
    s = np.arange(seq)[None, :]
    expand = (s // SEL_BLOCK == j).astype(np.float32)
    return jnp.asarray(ovl, BF16), jnp.asarray(expand, BF16)


INT_MIN = -2147483648


def _dsa_kernel(q_ref, iq_ref, misc_ref, ik_ref, kv_ref, o_ref, key_scr, hi_scr, lo_scr, *, c, tk, seq, topk):
    ci = pl.program_id(1)
    t0 = ci * c
    n_tiles = (t0 + c + tk - 1) // tk
    tq = t0 + lax.broadcasted_iota(jnp.int32, (1, c), 1)
    krow = lax.broadcasted_iota(jnp.int32, (tk, c), 0)

    lane = lax.broadcasted_iota(jnp.int32, (c, LANES), 1)
    per_blk = LANES // IDX_DIM
    qs = []
    for h in range(IDX_HEADS):
        blk = iq_ref[:, (h // per_blk) * LANES:(h // per_blk + 1) * LANES]
        qs.append(jnp.where(lane // IDX_DIM == h % per_blk, blk, jnp.zeros_like(blk)))
    qst = jnp.concatenate(qs, axis=0)
    w_t = jnp.transpose(misc_ref[...] * (IDX_HEADS ** -0.5 * IDX_DIM ** -0.5))

    def score_body(kt, carry):
        k0 = pl.multiple_of(kt * tk, tk)
        lg = _qk(ik_ref[pl.ds(k0, tk), :], qst)
        sc = jnp.zeros((tk, c), F32)
        for h in range(IDX_HEADS):
            sc = sc + jnp.maximum(lg[:, h * c:(h + 1) * c], 0.0) * w_t[MISC_W + h:MISC_W + h + 1, :]
        sc = sc + 0.0
        sc = jnp.where((k0 + krow) <= tq, sc, -jnp.inf)
        bits = pltpu.bitcast(sc, jnp.int32)
        key = jnp.where(bits < 0, bits ^ jnp.int32(0x7FFFFFFF), bits)
        key_scr[pl.ds(k0, tk), :] = key
        hi_scr[pl.ds(k0, tk), :] = (key >> 16).astype(jnp.int16)
        lo_scr[pl.ds(k0, tk), :] = ((key & jnp.int32(0xFFFF)) - 32768).astype(jnp.int16)
        return carry

    lax.fori_loop(0, n_tiles, score_body, 0)

    n_acc = 8

    def count(pred):
        def body(kt, acc):
            k0 = pl.multiple_of(kt * tk, tk)
            hit = jnp.where(pred(key_scr[pl.ds(k0, tk), :], k0 + krow), 1.0, 0.0)
            return acc + jnp.sum(hit.reshape(tk // (8 * n_acc), n_acc * 8, c), axis=0)
        acc = lax.fori_loop(0, n_tiles, body, jnp.zeros((n_acc * 8, c), F32))
        return jnp.sum(acc, axis=0, keepdims=True)

    n_acc_h = 4
    assert seq // (BF16_ROWS * n_acc_h) < 2 ** 15

    def half_rows(x):
        return jnp.broadcast_to(x, (BF16_ROWS, c)).astype(jnp.int16)[None]

    def tiles16(ref, kt):
        return ref[pl.ds(pl.multiple_of(kt * tk, tk), tk), :].reshape(tk // BF16_ROWS, BF16_ROWS, c)

    def count16(ref, pred):
        def body(kt, acc):
            x = tiles16(ref, kt)
            hit = jnp.where(pred(x), jnp.ones_like(x), jnp.zeros_like(x))
            parts = hit.reshape(tk // (BF16_ROWS * n_acc_h), n_acc_h, BF16_ROWS, c)
            for g in range(parts.shape[0]):
                acc = acc + parts[g]
            return acc
        acc = lax.fori_loop(0, n_tiles, body, jnp.zeros((n_acc_h, BF16_ROWS, c), jnp.int16))
        return jnp.sum(acc.astype(F32).reshape(n_acc_h * BF16_ROWS, c), axis=0, keepdims=True)

    def hi_body(i, ans):
        cand = ans | (jnp.int32(1) << (31 - i))
        cand_b = half_rows((cand ^ jnp.int32(INT_MIN)) >> 16)
        cnt = count16(hi_scr, lambda hi: hi >= cand_b)
        return jnp.where(cnt >= topk, cand, ans)

    thr_hi = lax.fori_loop(0, 16, hi_body, jnp.zeros((1, c), jnp.int32))
    hi_b = half_rows((thr_hi ^ jnp.int32(INT_MIN)) >> 16)
    above = count16(hi_scr, lambda hi: hi > hi_b)

    def keep_equal(kt, carry):
        k0 = pl.multiple_of(kt * tk, tk)
        lo = jnp.where(tiles16(hi_scr, kt) == hi_b, tiles16(lo_scr, kt), jnp.int16(-32768))
        lo_scr[pl.ds(k0, tk), :] = lo.reshape(tk, c)
        return carry

    lax.fori_loop(0, n_tiles, keep_equal, 0)

    def lo_body(i, ans):
        cand = ans | (jnp.int32(1) << (15 - i))
        cand_b = half_rows(cand - 32768)
        cnt = above + count16(lo_scr, lambda lo: lo >= cand_b)
        return jnp.where(cnt >= topk, cand, ans)

    thr_lo = lax.fori_loop(0, 16, lo_body, jnp.zeros((1, c), jnp.int32))
    thr = (thr_hi | thr_lo) ^ jnp.int32(INT_MIN)
    lo_b = half_rows(thr_lo - 32768)
    need = topk - above - count16(lo_scr, lambda lo: lo > lo_b)
    n_ties = count(lambda keys, idx: keys == thr)

    n_bits = max(1, int(np.ceil(np.log2(seq))))

    def tie_search():
        def tie_body(i, ans):
            cand = ans | (jnp.int32(1) << (n_bits - 1 - i))
            cnt = count(lambda keys, idx: (keys == thr) & (idx < cand))
            return jnp.where(cnt < need, cand, ans)
        return lax.fori_loop(0, n_bits, tie_body, jnp.zeros((1, c), jnp.int32))

    surplus = jnp.max(n_ties - need)
    last_tie = lax.cond(surplus > 0.0, tie_search, lambda: jnp.full((1, c), seq, jnp.int32))

    q4 = _stack_heads([q_ref[:, :LANES], q_ref[:, LANES:]], HEAD_DIM ** -0.5)
    eye = (lax.broadcasted_iota(jnp.int32, (c, c), 0) == lax.broadcasted_iota(jnp.int32, (c, c), 1)).astype(BF16)

    def att_body(kt, carry):
        k0 = pl.multiple_of(kt * tk, tk)
        kv = kv_ref[pl.ds(k0, tk), :]
        keys = key_scr[pl.ds(k0, tk), :]
        idx = k0 + krow
        picked = (idx <= tq) & ((keys > thr) | ((keys == thr) & (idx <= last_tie)))
        valid = _qk(eye, jnp.where(picked, 1.0, 0.0).astype(BF16)) > 0.5
        s = _qk(q4, kv).reshape(DSA_HEADS, c, tk)
        return _flash_step(s, valid, kv, *carry)

    _, acc = lax.fori_loop(0, n_tiles, att_body, _flash_init(DSA_HEADS, c))
    o = _normalize(acc).reshape(DSA_HEADS * c, LANES)
    for i, blk in enumerate(_pack_heads(o, DSA_HEADS, c)):
        o_ref[:, i * LANES:(i + 1) * LANES] = blk.astype(BF16)


def _dsa(hb, misc, bsz, seq, c, tk):
    t = hb.shape[0]
    nch = seq // c
    topk = min(DSA_TOPK_MAX, seq // 4)
    kern = functools.partial(_dsa_kernel, c=c, tk=tk, seq=seq, topk=topk)
    return pl.pallas_call(
        kern,
        grid=(bsz, nch),
        in_specs=[
            pl.BlockSpec((c, 2 * LANES), lambda b, i: (b * nch + i, BLK_DQ // 2)),
            pl.BlockSpec((c, 2 * LANES), lambda b, i: (b * nch + i, BLK_IQ // 2)),
            pl.BlockSpec((c, LANES), lambda b, i: (b * nch + i, 0)),
            pl.BlockSpec((seq, LANES), lambda b, i: (b, BLK_IK)),
            pl.BlockSpec((seq, LANES), lambda b, i: (b, BLK_DKV)),
        ],
        out_specs=pl.BlockSpec((c, 2 * LANES), lambda b, i: (b * nch + i, 0)),
        out_shape=jax.ShapeDtypeStruct((t, DSA_HEADS * HEAD_DIM), BF16),
        scratch_shapes=[pltpu.VMEM((seq, c), jnp.int32), pltpu.VMEM((seq, c), jnp.int16),
                        pltpu.VMEM((seq, c), jnp.int16)],
        compiler_params=_cparams(("parallel", "arbitrary"), TILE["dsa"]["vmem"]),
        name="dsa",
    )(hb, hb, misc, hb, hb)


def _matmul_kernel(a_ref, w_ref, o_ref):
    o_ref[...] = jnp.dot(a_ref[...].astype(BF16), w_ref[...], preferred_element_type=F32).astype(o_ref.dtype)


def _matmul(a, w, tm, out_dtype):
    m, k = a.shape
    n = w.shape[1]
    return pl.pallas_call(
        _matmul_kernel,
        grid=(m // tm,),
        in_specs=[pl.BlockSpec((tm, k), lambda i: (i, 0)), pl.BlockSpec((k, n), lambda i: (0, 0))],
        out_specs=pl.BlockSpec((tm, n), lambda i: (i, 0)),
        out_shape=jax.ShapeDtypeStruct((m, n), out_dtype),
        compiler_params=_cparams(("parallel",), TILE["mem"]["vmem"]),
        name="matmul",
    )(a, w)


def _mem_attention(q_ref, mkv_ref, c):
    q4 = _stack_heads([q_ref[:, :LANES], q_ref[:, LANES:]], HEAD_DIM ** -0.5)
    outs = []
    for h in range(MEM_HEADS):
        kv = mkv_ref[:, h * LANES:(h + 1) * LANES]
        s = _qk(q4[h * c:(h + 1) * c], kv)
        e = jnp.exp(s - jnp.max(s, axis=-1, keepdims=True))
        p = e / jnp.sum(e, axis=-1, keepdims=True)
        outs.append(jnp.dot(p.astype(BF16), kv, preferred_element_type=F32))
    return _pack_heads(jnp.concatenate(outs, axis=0), MEM_HEADS, c)


def _layer_norm(v, g, b):
    mu = jnp.mean(v, axis=-1, keepdims=True)
    d = v - mu
    var = jnp.mean(d * d, axis=-1, keepdims=True)
    return d * lax.rsqrt(var + LN_EPS) * g + b


def _out_ln_kernel(on_ref, od_ref, mq_ref, mkv_ref, x_ref, w_ref, g_ref, b_ref, o_ref):
    n0 = on_ref.shape[1]
    n1 = n0 + od_ref.shape[1]
    o_mem = jnp.concatenate(_mem_attention(mq_ref, mkv_ref, x_ref.shape[0]), axis=1).astype(BF16)
    mix = jnp.dot(on_ref[...], w_ref[:n0, :], preferred_element_type=F32)
    mix = mix + jnp.dot(od_ref[...], w_ref[n0:n1, :], preferred_element_type=F32)
    mix = mix + jnp.dot(o_mem, w_ref[n1:, :], preferred_element_type=F32)
    o_ref[...] = _layer_norm(DEEPNORM_ALPHA * x_ref[...] + mix, g_ref[...], b_ref[...])


def _out_ln(o_nsa, o_dsa, hb, mkv, x2, w_out, g, b, seq, m_len, tm):
    t, dm = x2.shape
    per_batch = seq // tm
    row = lambda a: pl.BlockSpec((tm, a.shape[1]), lambda i: (i, 0))
    full = lambda a: pl.BlockSpec(a.shape, lambda i: (0, 0))
    return pl.pallas_call(
        _out_ln_kernel,
        grid=(t // tm,),
        in_specs=[
            row(o_nsa), row(o_dsa),
            pl.BlockSpec((tm, 2 * LANES), lambda i: (i, BLK_MQ // 2)),
            pl.BlockSpec((m_len, MEM_HEADS * LANES), lambda i: (i // per_batch, 0)),
            row(x2), full(w_out), full(g), full(b),
        ],
        out_specs=pl.BlockSpec((tm, dm), lambda i: (i, 0)),
        out_shape=jax.ShapeDtypeStruct((t, dm), F32),
        compiler_params=_cparams(("parallel",), TILE["out_ln"]["vmem"]),
        name="out_ln",
    )(o_nsa, o_dsa, hb, mkv, x2, w_out, g, b)


NOT_PICKED = 64.0
PICK_BASE = -(2.0 ** 126)
PICK_STEPS = 32


def _top_rows(s, k, break_ties):
    n_rows = s.shape[0]
    rows = lax.broadcasted_iota(jnp.int32, s.shape, 0)
    assert k <= PICK_STEPS
    vals = []
    for r in range(k):
        best = jnp.max(s, axis=0, keepdims=True)
        hit = s == best
        if break_ties:
            hit = rows == jnp.min(jnp.where(hit, rows, n_rows), axis=0, keepdims=True)
        s = jnp.where(hit, PICK_BASE * (1.0 + r / PICK_STEPS), s)
        vals.append(best)
    rank = jnp.where(s <= PICK_BASE, (s * (1.0 / PICK_BASE) - 1.0) * PICK_STEPS, NOT_PICKED)
    return rank, vals


def _ranked_exactly(rank, k):
    return jnp.sum(jnp.where(rank < k, 1.0, 0.0), axis=0, keepdims=True) == float(k)


def _peer_select(s1, s2, k, break_ties):
    rank1, v1 = _top_rows(s1, k, break_ties)
    rank2, v2 = _top_rows(s2, k, break_ties)
    v2a = jnp.concatenate(v2, axis=0)
    v2lo = v2a[:8]
    r8 = lax.broadcasted_iota(jnp.int32, v2lo.shape, 0)
    pieces = [v1[0] + v2a, v1[1] + v2lo]
    for a in range(2, 8):
        pieces.append(jnp.where(r8 < k // (a + 1), v1[a] + v2lo, -jnp.inf))
    pieces.append(jnp.concatenate(v1[8:], axis=0) + v2[0])
    cand = jnp.concatenate(pieces, axis=0)
    rank_c, top = _top_rows(cand, k, break_ties)
    return rank1, rank2, rank_c, v1[0], v2[0], jnp.concatenate(top, axis=0)


def _peer_kernel(x1_ref, wqt_ref, k1_ref, k2_ref, u_ref, vt_ref, g_ref, b_ref, o_ref,
                 xt_scr, s1_scr, s2_scr, e1_scr, r2_scr, e2_scr, y_scr, *, tm, eb, nsub):
    j = pl.program_id(1)
    nk = PEER_N_KEYS
    half = PEER_KEY_DIM // 2
    n_lt = tm // LANES
    k = PEER_TOPK
    pack = BF16_ROWS

    @pl.when(j == 0)
    def _select():
        xt = jnp.transpose(x1_ref[...]).astype(BF16)
        xt_scr[...] = xt
        for h in range(PEER_HEADS):
            qh = jnp.dot(wqt_ref[h * PEER_KEY_DIM:(h + 1) * PEER_KEY_DIM, :], xt, preferred_element_type=F32).astype(BF16)
            s1_scr[h] = jnp.dot(k1_ref[...], qh[:half], preferred_element_type=F32)
            s2_scr[h] = jnp.dot(k2_ref[...], qh[half:], preferred_element_type=F32)

        n_par = 4
        per_head = n_lt // n_par

        def chunk(i, carry):
            h = i // per_head
            lanes = [pl.multiple_of(((i % per_head) * n_par + p) * LANES, LANES) for p in range(n_par)]
            s1s = [s1_scr[h, :, pl.ds(l0, LANES)] for l0 in lanes]
            s2s = [s2_scr[h, :, pl.ds(l0, LANES)] for l0 in lanes]
            fast = [_peer_select(s1, s2, k, break_ties=False) for s1, s2 in zip(s1s, s2s)]
            n_unclean = 0.0
            for f in fast:
                clean = _ranked_exactly(f[0], k) & _ranked_exactly(f[1], k) & _ranked_exactly(f[2], k)
                n_unclean = n_unclean + jnp.sum(jnp.where(clean, 0.0, 1.0))
            picks = lax.cond(
                n_unclean > 0.0,
                lambda: [_peer_select(s1, s2, k, break_ties=True) for s1, s2 in zip(s1s, s2s)],
                lambda: fast)
            for l0, s1, s2, (rank1, rank2, rank_c, v1_max, v2_max, top) in zip(lanes, s1s, s2s, picks):
                den = jnp.sum(jnp.exp(top - top[0:1]), axis=0, keepdims=True)
                picked = jnp.where(rank_c < k, 1.0, 0.0)
                n_of_rank = [jnp.sum(picked[0:16], axis=0, keepdims=True)]
                n_of_rank += [jnp.sum(picked[8 * a + 8:8 * a + 16], axis=0, keepdims=True) for a in range(1, 8)]
                n_of_rank += [picked[72 + a:73 + a] for a in range(8)]
                n1 = jnp.zeros_like(s1)
                for a in range(k):
                    n1 = jnp.where(rank1 == float(a), n_of_rank[a], n1)
                s1_scr[h, :, pl.ds(l0, LANES)] = n1
                e1_scr[h, :, pl.ds(l0, LANES)] = jnp.where(rank1 < k, 0.5 * jnp.exp(s1 - v1_max), 0.0)
                slab = h * n_lt + l0 // LANES
                r2_scr[slab] = rank2.astype(BF16)
                e2_scr[slab] = (jnp.where(rank2 < k, jnp.exp(s2 - v2_max), 0.0) / den).astype(BF16)
            return carry

        lax.fori_loop(0, PEER_HEADS * per_head, chunk, 0)
        y_scr[...] = jnp.zeros_like(y_scr)

    xt = xt_scr[...]
    y_add = None
    rows_per_step = nsub * eb // nk
    for sub in range(nsub):
        a = jnp.dot(u_ref[sub * eb:(sub + 1) * eb, :], xt, preferred_element_type=F32)
        zs = []
        for ib in range(eb // nk):
            row = sub * (eb // nk) + ib
            grp = pl.multiple_of(j * rows_per_step + (row // SUBLANES) * SUBLANES, SUBLANES)
            r_in = row % SUBLANES
            zrow = []
            for lt in range(n_lt):
                ls = slice(lt * LANES, (lt + 1) * LANES)
                w = jnp.zeros((nk // pack, pack, LANES), BF16)
                for h in range(PEER_HEADS):
                    n1r = s1_scr[h, pl.ds(grp, SUBLANES), ls][r_in:r_in + 1]
                    e1r = e1_scr[h, pl.ds(grp, SUBLANES), ls][r_in:r_in + 1]
                    n1b = jnp.broadcast_to(n1r, (pack, LANES)).astype(BF16)[None]
                    e1b = jnp.broadcast_to(e1r, (pack, LANES)).astype(BF16)[None]
                    r2 = r2_scr[h * n_lt + lt].reshape(nk // pack, pack, LANES)
                    e2 = e2_scr[h * n_lt + lt].reshape(nk // pack, pack, LANES)
                    w = w + jnp.where(r2 < n1b, e2, jnp.zeros_like(e2)) * e1b
                x_blk = a[ib * nk:(ib + 1) * nk, ls]
                act = (x_blk * (1.0 + lax.erf(x_blk * np.float32(1.0 / np.sqrt(2.0))))).astype(BF16)
                zrow.append(w.reshape(nk, LANES) * act)
            zs.append(jnp.concatenate(zrow, axis=1))
        z = jnp.concatenate(zs, axis=0)
        y_sub = jnp.dot(vt_ref[:, sub * eb:(sub + 1) * eb], z, preferred_element_type=F32)
        y_add = y_sub if y_add is None else y_add + y_sub
    y_scr[...] += y_add

    @pl.when(j == pl.num_programs(1) - 1)
    def _finish():
        y = jnp.transpose(y_scr[...])
        o_ref[...] = _layer_norm(DEEPNORM_ALPHA * x1_ref[...] + y, g_ref[...], b_ref[...])


def _peer(x1, wqt, k1, k2, u, vt, g, b, tm, eb, nsub):
    t, dm = x1.shape
    n_e = u.shape[0]
    assert (nsub * eb // PEER_N_KEYS) % SUBLANES == 0, "a grid step covers whole sublane groups of first-key rows"
    assert PEER_TOPK == 16 and PEER_N_KEYS == LANES, "the candidate-pair layout in _peer_select is written for k = 16"
    full = lambda a: pl.BlockSpec(a.shape, lambda i, j: (0, 0))
    tab = pltpu.VMEM((PEER_HEADS, PEER_N_KEYS, tm), F32)
    tab_b = pltpu.VMEM((PEER_HEADS * (tm // LANES), PEER_N_KEYS, LANES), BF16)
    return pl.pallas_call(
        functools.partial(_peer_kernel, tm=tm, eb=eb, nsub=nsub),
        grid=(t // tm, n_e // (nsub * eb)),
        in_specs=[
            pl.BlockSpec((tm, dm), lambda i, j: (i, 0)),
            full(wqt), full(k1), full(k2),
            pl.BlockSpec((nsub * eb, dm), lambda i, j: (j, 0)),
            pl.BlockSpec((dm, nsub * eb), lambda i, j: (0, j)),
            full(g), full(b),
        ],
        out_specs=pl.BlockSpec((tm, dm), lambda i, j: (i, 0)),
        out_shape=jax.ShapeDtypeStruct((t, dm), F32),
        scratch_shapes=[
            pltpu.VMEM((dm, tm), BF16), tab, tab, tab, tab_b, tab_b,
            pltpu.VMEM((dm, tm), F32),
        ],
        compiler_params=_cparams(("parallel", "arbitrary"), TILE["peer"]["vmem"]),
        name="peer",
    )(x1, wqt, k1, k2, u, vt, g, b)


def _layer(x, mem, positions, w_in, pe_k, pe_v, w1k, w2k, w1v, w2v, w_mem_kv, w_out, ln1_g, ln1_b,
           w_query, sk1, sk2, pu, pv, ln2_g, ln2_b):
    bsz, seq, dm = x.shape
    m_len = mem.shape[1]
    t = bsz * seq
    d = HEAD_DIM
    g_n = NSA_KV_HEADS
    nsa_t, dsa_t, peer_t = TILE["nsa"], TILE["dsa"], TILE["peer"]
    assert seq % nsa_t["tk"] == 0 and seq % dsa_t["tk"] == 0 and seq >= WINDOW + nsa_t["c"]
    assert seq % TILE["out_ln"]["tm"] == 0, "an out_ln token tile stays inside one batch row (one memory block)"
    assert dsa_t["tk"] >= min(DSA_TOPK_MAX, seq // 4), "a key tile holds at least top-k candidates"
    assert t % peer_t["tm"] == 0 and t % TILE["proj"]["tm"] == 0 and t % TILE["out_ln"]["tm"] == 0
    x2 = x.reshape(t, dm)

    hb, misc = _proj(x2, _regroup_w_in(w_in), _rope_inputs(positions), tm=TILE["proj"]["tm"])

    rows = seq // CMP_STRIDE
    cmp = hb[:, BLK_NKV * LANES:(BLK_NKV + g_n) * LANES].reshape(bsz, seq, g_n, 2, d)
    cmp = cmp.transpose(0, 2, 3, 1, 4).reshape(bsz * g_n, 2, rows, CMP_STRIDE * d)
    zpad = jnp.zeros((CMP_HIDDEN, d), BF16)
    pe_rows = lambda pe: jnp.pad(pe.reshape(1, CMP_LEN * d), ((0, 7), (0, 0))).astype(BF16)
    kcvc = _compress(cmp[:, 0], cmp[:, 1], w1k.astype(BF16), w1v.astype(BF16),
                     jnp.concatenate([w2k.astype(BF16), zpad], axis=1), jnp.concatenate([zpad, w2v.astype(BF16)], axis=1),
                     pe_rows(pe_k), pe_rows(pe_v))

    ovl_t, expand = _nsa_consts(seq)
    o_nsa = _nsa(hb, misc, kcvc, ovl_t, expand, bsz, seq, c=nsa_t["c"], tk=nsa_t["tk"])
    o_dsa = _dsa(hb, misc, bsz, seq, c=dsa_t["c"], tk=dsa_t["tk"])

    wm = w_mem_kv.reshape(dm, 2, MEM_HEADS, d).transpose(0, 2, 1, 3).reshape(dm, MEM_HEADS * 2 * d).astype(BF16)
    mkv = _matmul(mem.reshape(bsz * m_len, dm), wm, tm=m_len, out_dtype=BF16)
    x1 = _out_ln(o_nsa, o_dsa, hb, mkv, x2, w_out.astype(BF16), ln1_g.reshape(1, dm), ln1_b.reshape(1, dm),
                 seq, m_len, tm=TILE["out_ln"]["tm"])

    x2o = _peer(x1, w_query.T.astype(BF16), sk1.astype(BF16), sk2.astype(BF16), pu.astype(BF16), pv.T.astype(BF16),
                ln2_g.reshape(1, dm), ln2_b.reshape(1, dm), tm=peer_t["tm"], eb=peer_t["eb"], nsub=peer_t["nsub"])
    return x2o.reshape(bsz, seq, dm)


def kernel(x, mem, positions, w_in, nsa_pe_k, nsa_pe_v, nsa_cmp_w1_k, nsa_cmp_w2_k, nsa_cmp_w1_v, nsa_cmp_w2_v,
           w_mem_kv, w_out, ln1_g, ln1_b, peer_w_query, peer_sub_keys_1, peer_sub_keys_2, peer_u, peer_v, ln2_g, ln2_b):
    assert w_in.shape[0] == DEPTH
    return _layer(x, mem, positions, w_in[0], nsa_pe_k[0], nsa_pe_v[0], nsa_cmp_w1_k[0], nsa_cmp_w2_k[0],
                  nsa_cmp_w1_v[0], nsa_cmp_w2_v[0], w_mem_kv[0], w_out[0], ln1_g[0], ln1_b[0], peer_w_query[0],
                  peer_sub_keys_1[0], peer_sub_keys_2[0], peer_u[0], peer_v[0], ln2_g[0], ln2_b[0])
```

```python
import functools

import numpy as np
import jax
import jax.numpy as jnp
from jax import lax
from jax.experimental import pallas as pl
from jax.experimental.pallas import tpu as pltpu

F32 = jnp.float32
BF16 = jnp.bfloat16

LANES = 128
SUBLANES = 8
BF16_ROWS = 16
MIB = 1 << 20

TILE = dict(
    proj=dict(tm=512, vmem=40 * MIB),
    compress=dict(vmem=16 * MIB),
    nsa=dict(c=512, tk=1024, vmem=56 * MIB),
    dsa=dict(c=256, tk=512, vmem=32 * MIB),
    mem=dict(vmem=16 * MIB),
    out_ln=dict(tm=512, vmem=24 * MIB),
    peer=dict(tm=512, eb=1024, nsub=2, vmem=56 * MIB),
)

HEAD_DIM = 64
ROPE_THETA = 500000.0
LN_EPS = 1e-5
NSA_HEADS = 8
NSA_KV_HEADS = 2
CMP_LEN = 32
CMP_STRIDE = 16
CMP_HIDDEN = 128
SEL_BLOCK = 64
SEL_COUNT = 16
WINDOW = 512
FORCE_BONUS = 1e4
DSA_HEADS = 4
IDX_HEADS = 8
IDX_DIM = 32
DSA_TOPK_MAX = 256
MEM_HEADS = 4
PEER_HEADS = 8
PEER_N_KEYS = 128
PEER_KEY_DIM = 256
PEER_TOPK = 16
DEPTH = 1
DEEPNORM_ALPHA = (2.0 * DEPTH) ** 0.25

NEG = -1e30

BLK_QN = 0
BLK_NKV = 4
BLK_DQ = 10
BLK_IQ = 12
BLK_MQ = 14
BLK_DKV = 16
BLK_IK = 17
BLK_MISC = 18
N_BLK = 19
MISC_W = 0
MISC_G = IDX_HEADS


def _cparams(sem, vmem):
    return pltpu.CompilerParams(dimension_semantics=sem, vmem_limit_bytes=vmem)


def _regroup_w_in(w_in):
    d = HEAD_DIM
    o_q = 0
    o_kv = o_q + NSA_HEADS * d
    o_g = o_kv + 6 * NSA_KV_HEADS * d
    o_dq = o_g + 3 * NSA_HEADS
    o_dkv = o_dq + DSA_HEADS * d
    o_iq = o_dkv + 2 * d
    o_ik = o_iq + IDX_HEADS * IDX_DIM
    o_iw = o_ik + IDX_DIM
    o_mq = o_iw + IDX_HEADS
    cols = [w_in[:, o_q:o_kv]]
    for br in range(3):
        for g in range(NSA_KV_HEADS):
            k0 = o_kv + ((2 * br) * NSA_KV_HEADS + g) * d
            v0 = o_kv + ((2 * br + 1) * NSA_KV_HEADS + g) * d
            cols += [w_in[:, k0:k0 + d], w_in[:, v0:v0 + d]]
    cols.append(w_in[:, o_dq:o_dkv])
    cols.append(w_in[:, o_iq:o_ik])
    cols.append(w_in[:, o_mq:o_mq + MEM_HEADS * d])
    cols.append(w_in[:, o_dkv:o_iq])
    cols += [w_in[:, o_ik:o_iw]] * (LANES // IDX_DIM)
    cols += [w_in[:, o_iw:o_mq], w_in[:, o_g:o_dq]]
    pad = LANES - IDX_HEADS - 3 * NSA_HEADS
    cols.append(jnp.zeros((w_in.shape[0], pad), w_in.dtype))
    w = jnp.concatenate(cols, axis=1)
    assert w.shape[1] == N_BLK * LANES
    return w.astype(BF16)


def _rot_half(head_dim):
    return head_dim // 8


def _rope_inputs(positions):
    pos = positions.reshape(-1).astype(F32)
    vals, place, ones = [], [], []
    src = 0
    for hd in (HEAD_DIM, IDX_DIM):
        half = _rot_half(hd)
        freqs = jnp.power(ROPE_THETA, -jnp.arange(half, dtype=F32) * 2.0 / (2 * half))
        ang = pos[:, None] * freqs
        vals += [jnp.cos(ang), jnp.sin(ang)]
        d = np.arange(LANES) % hd
        lanes = np.arange(LANES)
        sel = np.zeros((3, LANES, LANES), np.float32)
        rot, first, second = d < 2 * half, d < half, (d >= half) & (d < 2 * half)
        sel[0, src + d[rot] % half, lanes[rot]] = 1.0
        sel[1, src + half + d[first], lanes[first]] = -1.0
        sel[2, src + half + d[second] - half, lanes[second]] = 1.0
        place += [sel[0], sel[1], sel[2]]
        ones += [(~rot).astype(np.float32), np.zeros(LANES, np.float32), np.zeros(LANES, np.float32)]
        src += 2 * half
    cs = jnp.concatenate(vals, axis=1)
    cs = jnp.pad(cs, ((0, 0), (0, LANES - cs.shape[1])))
    return cs, jnp.asarray(np.concatenate(place, axis=1), BF16), jnp.asarray(np.concatenate(ones)[None, :], F32)


def _proj_kernel(x_ref, w_ref, cs_ref, place_ref, ones_ref, hb_ref, misc_ref, cmp_ref):
    xb = x_ref[...].astype(BF16)
    tm = xb.shape[0]
    lane = lax.broadcasted_iota(jnp.int32, (tm, LANES), 1)
    lo = lane < HEAD_DIM
    cs = cs_ref[...]
    hi = cs.astype(BF16)
    rest = cs - hi.astype(F32)
    mid = rest.astype(BF16)
    low = (rest - mid.astype(F32)).astype(BF16)
    tabs = ones_ref[...]
    for piece in (hi, mid, low):
        tabs = tabs + jnp.dot(piece, place_ref[...], preferred_element_type=F32)
    c64, sa64, sb64 = (tabs[:, i * LANES:(i + 1) * LANES] for i in range(3))
    c32, sa32, sb32 = (tabs[:, i * LANES:(i + 1) * LANES] for i in range(3, 6))
    c64h, sa64h, sb64h = jnp.where(lo, c64, 1.0), jnp.where(lo, sa64, 0.0), jnp.where(lo, sb64, 0.0)

    def rope(h, c, sa, sb, half):
        return h * c + pltpu.roll(h, LANES - half, 1) * sa + pltpu.roll(h, half, 1) * sb

    def finish(blk, h):
        if blk < BLK_NKV or BLK_DQ <= blk < BLK_IQ:
            return rope(h, c64, sa64, sb64, _rot_half(HEAD_DIM))
        if BLK_NKV <= blk < BLK_DQ or blk == BLK_DKV:
            return rope(h, c64h, sa64h, sb64h, _rot_half(HEAD_DIM))
        if BLK_IQ <= blk < BLK_MQ or blk == BLK_IK:
            return rope(h, c32, sa32, sb32, _rot_half(IDX_DIM))
        return h

    for j in range(BLK_MISC // 2):
        h = jnp.dot(xb, w_ref[:, j * 2 * LANES:(j + 1) * 2 * LANES], preferred_element_type=F32)
        for s in range(2):
            blk = 2 * j + s
            roped = finish(blk, h[:, s * LANES:(s + 1) * LANES])
            hb_ref[:, blk * LANES:(blk + 1) * LANES] = roped.astype(BF16)
            if BLK_NKV <= blk < BLK_NKV + NSA_KV_HEADS:
                cmp_ref[:, (blk - BLK_NKV) * LANES:(blk - BLK_NKV + 1) * LANES] = roped
    misc_ref[...] = jnp.dot(xb, w_ref[:, BLK_MISC * LANES:], preferred_element_type=F32)


def _proj(x2, w2, rope, tm):
    t, dm = x2.shape
    cs, place, ones = rope
    return pl.pallas_call(
        _proj_kernel,
        grid=(t // tm,),
        in_specs=[
            pl.BlockSpec((tm, dm), lambda i: (i, 0)),
            pl.BlockSpec((dm, N_BLK * LANES), lambda i: (0, 0)),
            pl.BlockSpec((tm, LANES), lambda i: (i, 0)),
            pl.BlockSpec(place.shape, lambda i: (0, 0)),
            pl.BlockSpec(ones.shape, lambda i: (0, 0)),
        ],
        out_specs=[
            pl.BlockSpec((tm, BLK_MISC * LANES), lambda i: (i, 0)),
            pl.BlockSpec((tm, LANES), lambda i: (i, 0)),
            pl.BlockSpec((tm, NSA_KV_HEADS * LANES), lambda i: (i, 0)),
        ],
        out_shape=[
            jax.ShapeDtypeStruct((t, BLK_MISC * LANES), BF16),
            jax.ShapeDtypeStruct((t, LANES), F32),
            jax.ShapeDtypeStruct((t, NSA_KV_HEADS * LANES), F32),
        ],
        compiler_params=_cparams(("parallel",), TILE["proj"]["vmem"]),
        name="proj",
    )(x2, w2, cs, place, ones)


def _gelu(x):
    return 0.5 * x * (1.0 + lax.erf(x * np.float32(1.0 / np.sqrt(2.0))))


def _compress_kernel(kv_ref, w1cat_ref, w1k_ref, w1v_ref, w2k_ref, w2v_ref, pek_ref, pev_ref, out_ref):
    rows = out_ref.shape[1]
    hid = w1k_ref.shape[1]
    acc = jnp.zeros((rows, 4 * hid), F32)
    for l in range(CMP_STRIDE):
        tok = kv_ref[pl.ds(l, rows, stride=CMP_STRIDE), :].astype(BF16)
        acc = acc + jnp.dot(tok, w1cat_ref[l], preferred_element_type=F32)

    def hidden(a, b, w1_ref, pe_ref):
        bias = jnp.dot(pe_ref[...], w1_ref[...], preferred_element_type=F32)[0:1, :]
        return _gelu(a + pltpu.roll(b, rows - 1, 0) + bias).astype(BF16)

    hk = hidden(acc[:, 0:hid], acc[:, hid:2 * hid], w1k_ref, pek_ref)
    hv = hidden(acc[:, 2 * hid:3 * hid], acc[:, 3 * hid:4 * hid], w1v_ref, pev_ref)
    out = jnp.dot(hk, w2k_ref[...], preferred_element_type=F32) + jnp.dot(hv, w2v_ref[...], preferred_element_type=F32)
    out_ref[0] = out.astype(BF16)


def _compress_w1cat(w1k, w1v):
    d = HEAD_DIM
    k3 = w1k.reshape(2, CMP_STRIDE, d, CMP_HIDDEN)
    v3 = w1v.reshape(2, CMP_STRIDE, d, CMP_HIDDEN)
    z = jnp.zeros_like(k3[0])
    top = jnp.concatenate([k3[0], k3[1], z, z], axis=2)
    bot = jnp.concatenate([z, z, v3[0], v3[1]], axis=2)
    return jnp.concatenate([top, bot], axis=1)


def _compress(kv32, w1cat, w1k, w1v, w2k, w2v, pek, pev, bsz, seq):
    g_n = NSA_KV_HEADS
    rows = seq // CMP_STRIDE
    full = lambda a: pl.BlockSpec(a.shape, lambda i: (0,) * a.ndim)
    return pl.pallas_call(
        _compress_kernel,
        grid=(bsz * g_n,),
        in_specs=[
            pl.BlockSpec((seq, LANES), lambda i: (i // g_n, i % g_n)),
            full(w1cat), full(w1k), full(w1v), full(w2k), full(w2v), full(pek), full(pev),
        ],
        out_specs=pl.BlockSpec((1, rows, LANES), lambda i: (i, 0, 0)),
        out_shape=jax.ShapeDtypeStruct((bsz * g_n, rows, LANES), BF16),
        compiler_params=_cparams(("parallel",), TILE["compress"]["vmem"]),
        name="compress",
    )(kv32, w1cat, w1k, w1v, w2k, w2v, pek, pev)


def _stack_heads(qpair_refs_or_vals, scale):
    outs = []
    for blk in qpair_refs_or_vals:
        b = blk.astype(F32) * scale
        lane = lax.broadcasted_iota(jnp.int32, b.shape, 1)
        lo = lane < HEAD_DIM
        outs.append(jnp.where(lo, b, 0.0))
        outs.append(jnp.where(lo, pltpu.roll(b, HEAD_DIM, 1), 0.0))
    return jnp.concatenate(outs, axis=0).astype(BF16)


def _qk(q, kv):
    return lax.dot_general(q, kv, (((1,), (1,)), ((), ())), preferred_element_type=F32)


def _lane_pick(x, col):
    lane = lax.broadcasted_iota(jnp.int32, x.shape, 1)
    return jnp.sum(jnp.where(lane == col, x, 0.0), axis=1, keepdims=True)


def _flash_init(n_heads, c):
    return jnp.full((n_heads, c, 1), NEG, F32), jnp.zeros((n_heads, c, LANES), F32)


def _flash_step(s, valid, kv, m, acc):
    n_heads, c, tk = s.shape
    sb = s.astype(BF16) + jnp.where(valid, 0.0, NEG).astype(BF16)
    m_new = jnp.maximum(m, jnp.max(sb, axis=-1, keepdims=True).astype(F32))
    p = jnp.exp(sb - m_new.astype(BF16))
    lane = lax.broadcasted_iota(jnp.int32, kv.shape, 1)
    ones_v = jnp.where(lane < HEAD_DIM, jnp.ones_like(kv), kv)
    pv = jnp.dot(p.reshape(n_heads * c, tk), ones_v, preferred_element_type=F32)
    return m_new, jnp.exp(m - m_new) * acc + pv.reshape(n_heads, c, LANES)


def _normalize(acc):
    den = acc[..., 0:1]
    return acc / jnp.where(den > 0.0, den, 1.0)


def _pack_heads(o, n_heads, c):
    lane = lax.broadcasted_iota(jnp.int32, (c, LANES), 1)
    lo = lane < HEAD_DIM
    blocks = []
    for p in range(n_heads // 2):
        a = o[(2 * p) * c:(2 * p + 1) * c]
        b = o[(2 * p + 1) * c:(2 * p + 2) * c]
        blocks.append(jnp.where(lo, pltpu.roll(a, HEAD_DIM, 1), b))
    return blocks


def _nsa_kernel(q_ref, misc_ref, kc_ref, ks_ref, kw_ref, ovl_ref, exp_ref, o_ref, *, c, tk, seq):
    g = pl.program_id(1)
    ci = pl.program_id(2)
    t0 = ci * c
    hpg = NSA_HEADS // NSA_KV_HEADS
    rows = hpg * c
    scale = HEAD_DIM ** -0.5
    q4 = _stack_heads([q_ref[:, :LANES], q_ref[:, LANES:]], scale)
    trow = t0 + lax.broadcasted_iota(jnp.int32, (c, 1), 0)

    kc = kc_ref[0]
    ncp = kc.shape[0]
    s = _qk(q4, kc).reshape(hpg, c, ncp)
    ncol = lax.broadcasted_iota(jnp.int32, (c, ncp), 1)
    cvalid = (ncol * CMP_STRIDE + (CMP_LEN - 1)) <= trow
    s = jnp.where(cvalid, s, NEG)
    m = jnp.max(s, axis=-1, keepdims=True)
    e = jnp.where(cvalid, jnp.exp(s - m), 0.0)
    den = jnp.sum(e, axis=-1, keepdims=True)
    p = (e / jnp.where(den > 0.0, den, 1.0)).reshape(rows, ncp)
    pb = p.astype(BF16)
    o_cmp = jnp.dot(pb, kc, preferred_element_type=F32)

    imp_t = lax.dot_general(ovl_ref[...], pb, (((1,), (1,)), ((), ())), preferred_element_type=F32)
    nbp = imp_t.shape[0]
    nb = seq // SEL_BLOCK
    imp = imp_t[:nb, 0:c]
    for h in range(1, hpg):
        imp = imp + imp_t[:nb, h * c:(h + 1) * c]
    jrow = lax.broadcasted_iota(jnp.int32, (nb, c), 0)
    tcol = t0 + lax.broadcasted_iota(jnp.int32, (nb, c), 1)
    cur = tcol // SEL_BLOCK
    forced = (jrow == 0) | (jrow == cur) | (jrow == cur - 1)
    visible = (jrow * SEL_BLOCK) <= tcol
    score = jnp.where(visible, imp + jnp.where(forced, FORCE_BONUS, 0.0), -jnp.inf)
    n_pick = min(SEL_COUNT, nb)
    rank, _ = _top_rows(score, n_pick, break_ties=True)
    sel = jnp.where(rank < n_pick, 1.0, 0.0)
    if nbp > nb:
        sel = jnp.concatenate([sel, jnp.zeros((nbp - nb, c), F32)], axis=0)
    sel_b = jnp.transpose(sel).astype(BF16)

    n_tiles = (t0 + c + tk - 1) // tk
    kcol = lax.broadcasted_iota(jnp.int32, (c, tk), 1)

    def sel_body(kt, carry):
        k0 = pl.multiple_of(kt * tk, tk)
        kv = ks_ref[pl.ds(k0, tk), :]
        s_ = _qk(q4, kv).reshape(hpg, c, tk)
        tok = jnp.dot(sel_b, exp_ref[:, pl.ds(k0, tk)], preferred_element_type=F32)
        valid = (tok > 0.5) & ((k0 + kcol) <= trow)
        return _flash_step(s_, valid, kv, *carry)

    init = _flash_init(hpg, c)
    _, acc_s = lax.fori_loop(0, n_tiles, sel_body, init)
    o_sel = _normalize(acc_s).reshape(rows, LANES)

    wk = WINDOW + c
    w0 = pl.multiple_of(jnp.maximum(t0 - WINDOW, 0), c)
    kvw = kw_ref[pl.ds(w0, wk), :]
    s = _qk(q4, kvw).reshape(hpg, c, wk)
    wpos = w0 + lax.broadcasted_iota(jnp.int32, (c, wk), 1)
    wvalid = (wpos <= trow) & (wpos > trow - WINDOW)
    _, acc_w = _flash_step(s, wvalid, kvw, *init)
    o_win = _normalize(acc_w).reshape(rows, LANES)

    gates = jax.nn.sigmoid(misc_ref[...])
    outs = []
    for h in range(hpg):
        col = MISC_G + (g * hpg + h) * 3
        sl = slice(h * c, (h + 1) * c)
        outs.append(_lane_pick(gates, col) * o_cmp[sl] + _lane_pick(gates, col + 1) * o_sel[sl]
                    + _lane_pick(gates, col + 2) * o_win[sl])
    blocks = _pack_heads(jnp.concatenate(outs, axis=0), hpg, c)
    for i, blk in enumerate(blocks):
        o_ref[:, i * LANES:(i + 1) * LANES] = blk.astype(BF16)


def _nsa(hb, misc, kcvc, ovl_t, expand, bsz, seq, c, tk):
    t = hb.shape[0]
    g_n = NSA_KV_HEADS
    nch = seq // c
    ncp = kcvc.shape[1]
    nbp = ovl_t.shape[0]
    kern = functools.partial(_nsa_kernel, c=c, tk=tk, seq=seq)
    return pl.pallas_call(
        kern,
        grid=(bsz, g_n, nch),
        in_specs=[
            pl.BlockSpec((c, 2 * LANES), lambda b, g, i: (b * nch + i, g)),
            pl.BlockSpec((c, LANES), lambda b, g, i: (b * nch + i, 0)),
            pl.BlockSpec((1, ncp, LANES), lambda b, g, i: (b * g_n + g, 0, 0)),
            pl.BlockSpec((seq, LANES), lambda b, g, i: (b, BLK_NKV + 2 + g)),
            pl.BlockSpec((seq, LANES), lambda b, g, i: (b, BLK_NKV + 4 + g)),
            pl.BlockSpec((nbp, ncp), lambda b, g, i: (0, 0)),
            pl.BlockSpec((nbp, seq), lambda b, g, i: (0, 0)),
        ],
        out_specs=pl.BlockSpec((c, 2 * LANES), lambda b, g, i: (b * nch + i, g)),
        out_shape=jax.ShapeDtypeStruct((t, NSA_HEADS * HEAD_DIM), BF16),
        compiler_params=_cparams(("parallel", "parallel", "arbitrary"), TILE["nsa"]["vmem"]),
        name="nsa",
    )(hb, misc, kcvc, hb, hb, ovl_t, expand)


def _nsa_consts(seq):
    nb = seq // SEL_BLOCK
    nbp = max(LANES, nb)
    rows = seq // CMP_STRIDE
    nc = (seq - CMP_LEN) // CMP_STRIDE + 1
    j = np.arange(nbp)[:, None]
    n = np.arange(rows)[None, :]
    ovl = np.clip(np.minimum(n * CMP_STRIDE + CMP_LEN, j * SEL_BLOCK + SEL_BLOCK)
                  - np.maximum(n * CMP_STRIDE, j * SEL_BLOCK), 0, None).astype(np.float32) / CMP_LEN
    ovl = np.where((n < nc) & (j < nb), ovl, 0.0)
    s = np.arange(seq)[None, :]
    expand = (s // SEL_BLOCK == j).astype(np.float32)
    return jnp.asarray(ovl, BF16), jnp.asarray(expand, BF16)


INT_MIN = -2147483648


def _dsa_kernel(q_ref, iq_ref, misc_ref, ik_ref, kv_ref, o_ref, key_scr, hi_scr, lo_scr, *, c, tk, seq, topk):
    ci = pl.program_id(1)
    t0 = ci * c
    n_tiles = (t0 + c + tk - 1) // tk
    tq = t0 + lax.broadcasted_iota(jnp.int32, (1, c), 1)
    krow = lax.broadcasted_iota(jnp.int32, (tk, c), 0)

    lane = lax.broadcasted_iota(jnp.int32, (c, LANES), 1)
    per_blk = LANES // IDX_DIM
    qs = []
    for h in range(IDX_HEADS):
        blk = iq_ref[:, (h // per_blk) * LANES:(h // per_blk + 1) * LANES]
        qs.append(jnp.where(lane // IDX_DIM == h % per_blk, blk, jnp.zeros_like(blk)))
    qst = jnp.concatenate(qs, axis=0)
    w_t = jnp.transpose(misc_ref[...] * (IDX_HEADS ** -0.5 * IDX_DIM ** -0.5))

    def score_body(kt, carry):
        k0 = pl.multiple_of(kt * tk, tk)
        lg = _qk(ik_ref[pl.ds(k0, tk), :], qst)
        sc = jnp.zeros((tk, c), F32)
        for h in range(IDX_HEADS):
            sc = sc + jnp.maximum(lg[:, h * c:(h + 1) * c], 0.0) * w_t[MISC_W + h:MISC_W + h + 1, :]
        sc = sc + 0.0
        sc = jnp.where((k0 + krow) <= tq, sc, -jnp.inf)
        bits = pltpu.bitcast(sc, jnp.int32)
        key = jnp.where(bits < 0, bits ^ jnp.int32(0x7FFFFFFF), bits)
        key_scr[pl.ds(k0, tk), :] = key
        hi_scr[pl.ds(k0, tk), :] = (key >> 16).astype(jnp.int16)
        lo_scr[pl.ds(k0, tk), :] = ((key & jnp.int32(0xFFFF)) - 32768).astype(jnp.int16)
        return carry

    lax.fori_loop(0, n_tiles, score_body, 0)

    n_acc = 8

    def count(pred):
        def body(kt, acc):
            k0 = pl.multiple_of(kt * tk, tk)
            hit = jnp.where(pred(key_scr[pl.ds(k0, tk), :], k0 + krow), 1.0, 0.0)
            return acc + jnp.sum(hit.reshape(tk // (8 * n_acc), n_acc * 8, c), axis=0)
        acc = lax.fori_loop(0, n_tiles, body, jnp.zeros((n_acc * 8, c), F32))
        return jnp.sum(acc, axis=0, keepdims=True)

    n_acc_h = 4
    assert seq // (BF16_ROWS * n_acc_h) < 2 ** 15

    def half_rows(x):
        return jnp.broadcast_to(x, (BF16_ROWS, c)).astype(jnp.int16)[None]

    def tiles16(ref, kt):
        return ref[pl.ds(pl.multiple_of(kt * tk, tk), tk), :].reshape(tk // BF16_ROWS, BF16_ROWS, c)

    def count16(ref, pred):
        def body(kt, acc):
            x = tiles16(ref, kt)
            hit = jnp.where(pred(x), jnp.ones_like(x), jnp.zeros_like(x))
            parts = hit.reshape(tk // (BF16_ROWS * n_acc_h), n_acc_h, BF16_ROWS, c)
            for g in range(parts.shape[0]):
                acc = acc + parts[g]
            return acc
        acc = lax.fori_loop(0, n_tiles, body, jnp.zeros((n_acc_h, BF16_ROWS, c), jnp.int16))
        return jnp.sum(acc.astype(F32).reshape(n_acc_h * BF16_ROWS, c), axis=0, keepdims=True)

    def hi_body(i, ans):
        cand = ans | (jnp.int32(1) << (31 - i))
        cand_b = half_rows((cand ^ jnp.int32(INT_MIN)) >> 16)
        cnt = count16(hi_scr, lambda hi: hi >= cand_b)
        return jnp.where(cnt >= topk, cand, ans)

    thr_hi = lax.fori_loop(0, 16, hi_body, jnp.zeros((1, c), jnp.int32))
    hi_b = half_rows((thr_hi ^ jnp.int32(INT_MIN)) >> 16)
    above = count16(hi_scr, lambda hi: hi > hi_b)

    def keep_equal(kt, carry):
        k0 = pl.multiple_of(kt * tk, tk)
        lo = jnp.where(tiles16(hi_scr, kt) == hi_b, tiles16(lo_scr, kt), jnp.int16(-32768))
        lo_scr[pl.ds(k0, tk), :] = lo.reshape(tk, c)
        return carry

    lax.fori_loop(0, n_tiles, keep_equal, 0)

    def lo_body(i, carry):
        ans, cnt_ans = carry
        cand = ans | (jnp.int32(1) << (15 - i))
        cand_b = half_rows(cand - 32768)
        cnt = above + count16(lo_scr, lambda lo: lo >= cand_b)
        ok = cnt >= topk
        return jnp.where(ok, cand, ans), jnp.where(ok, cnt, cnt_ans)

    thr_lo, cnt_ge = lax.fori_loop(0, 16, lo_body,
                                   (jnp.zeros((1, c), jnp.int32), jnp.full((1, c), float(seq + topk), F32)))
    thr = (thr_hi | thr_lo) ^ jnp.int32(INT_MIN)
    lo_b = half_rows(thr_lo - 32768)
    cnt_gt = above + count16(lo_scr, lambda lo: lo > lo_b)
    need = topk - cnt_gt
    n_ties = cnt_ge - cnt_gt

    n_bits = max(1, int(np.ceil(np.log2(seq))))

    def tie_search():
        def tie_body(i, ans):
            cand = ans | (jnp.int32(1) << (n_bits - 1 - i))
            cnt = count(lambda keys, idx: (keys == thr) & (idx < cand))
            return jnp.where(cnt < need, cand, ans)
        return lax.fori_loop(0, n_bits, tie_body, jnp.zeros((1, c), jnp.int32))

    surplus = jnp.max(n_ties - need)
    last_tie = lax.cond(surplus > 0.0, tie_search, lambda: jnp.full((1, c), seq, jnp.int32))

    q4 = _stack_heads([q_ref[:, :LANES], q_ref[:, LANES:]], HEAD_DIM ** -0.5)
    eye = (lax.broadcasted_iota(jnp.int32, (c, c), 0) == lax.broadcasted_iota(jnp.int32, (c, c), 1)).astype(BF16)

    def att_body(kt, carry):
        k0 = pl.multiple_of(kt * tk, tk)
        kv = kv_ref[pl.ds(k0, tk), :]
        keys = key_scr[pl.ds(k0, tk), :]
        idx = k0 + krow
        picked = (idx <= tq) & ((keys > thr) | ((keys == thr) & (idx <= last_tie)))
        valid = _qk(eye, jnp.where(picked, 1.0, 0.0).astype(BF16)) > 0.5
        s = _qk(q4, kv).reshape(DSA_HEADS, c, tk)
        return _flash_step(s, valid, kv, *carry)

    _, acc = lax.fori_loop(0, n_tiles, att_body, _flash_init(DSA_HEADS, c))
    o = _normalize(acc).reshape(DSA_HEADS * c, LANES)
    for i, blk in enumerate(_pack_heads(o, DSA_HEADS, c)):
        o_ref[:, i * LANES:(i + 1) * LANES] = blk.astype(BF16)


def _dsa(hb, misc, bsz, seq, c, tk):
    t = hb.shape[0]
    nch = seq // c
    topk = min(DSA_TOPK_MAX, seq // 4)
    kern = functools.partial(_dsa_kernel, c=c, tk=tk, seq=seq, topk=topk)
    return pl.pallas_call(
        kern,
        grid=(bsz, nch),
        in_specs=[
            pl.BlockSpec((c, 2 * LANES), lambda b, i: (b * nch + i, BLK_DQ // 2)),
            pl.BlockSpec((c, 2 * LANES), lambda b, i: (b * nch + i, BLK_IQ // 2)),
            pl.BlockSpec((c, LANES), lambda b, i: (b * nch + i, 0)),
            pl.BlockSpec((seq, LANES), lambda b, i: (b, BLK_IK)),
            pl.BlockSpec((seq, LANES), lambda b, i: (b, BLK_DKV)),
        ],
        out_specs=pl.BlockSpec((c, 2 * LANES), lambda b, i: (b * nch + i, 0)),
        out_shape=jax.ShapeDtypeStruct((t, DSA_HEADS * HEAD_DIM), BF16),
        scratch_shapes=[pltpu.VMEM((seq, c), jnp.int32), pltpu.VMEM((seq, c), jnp.int16),
                        pltpu.VMEM((seq, c), jnp.int16)],
        compiler_params=_cparams(("parallel", "arbitrary"), TILE["dsa"]["vmem"]),
        name="dsa",
    )(hb, hb, misc, hb, hb)


def _matmul_kernel(a_ref, w_ref, o_ref):
    o_ref[...] = jnp.dot(a_ref[...].astype(BF16), w_ref[...], preferred_element_type=F32).astype(o_ref.dtype)


def _matmul(a, w, tm, out_dtype):
    m, k = a.shape
    n = w.shape[1]
    return pl.pallas_call(
        _matmul_kernel,
        grid=(m // tm,),
        in_specs=[pl.BlockSpec((tm, k), lambda i: (i, 0)), pl.BlockSpec((k, n), lambda i: (0, 0))],
        out_specs=pl.BlockSpec((tm, n), lambda i: (i, 0)),
        out_shape=jax.ShapeDtypeStruct((m, n), out_dtype),
        compiler_params=_cparams(("parallel",), TILE["mem"]["vmem"]),
        name="matmul",
    )(a, w)


def _mem_attention(q_ref, mkv_ref, c):
    q4 = _stack_heads([q_ref[:, :LANES], q_ref[:, LANES:]], HEAD_DIM ** -0.5)
    outs = []
    for h in range(MEM_HEADS):
        kv = mkv_ref[:, h * LANES:(h + 1) * LANES]
        s = _qk(q4[h * c:(h + 1) * c], kv)
        e = jnp.exp(s - jnp.max(s, axis=-1, keepdims=True))
        p = e / jnp.sum(e, axis=-1, keepdims=True)
        outs.append(jnp.dot(p.astype(BF16), kv, preferred_element_type=F32))
    return _pack_heads(jnp.concatenate(outs, axis=0), MEM_HEADS, c)


def _layer_norm(v, g, b):
    mu = jnp.mean(v, axis=-1, keepdims=True)
    d = v - mu
    var = jnp.mean(d * d, axis=-1, keepdims=True)
    return d * lax.rsqrt(var + LN_EPS) * g + b


def _out_ln_kernel(on_ref, od_ref, mq_ref, mkv_ref, x_ref, w_ref, g_ref, b_ref, o_ref):
    n0 = on_ref.shape[1]
    n1 = n0 + od_ref.shape[1]
    o_mem = jnp.concatenate(_mem_attention(mq_ref, mkv_ref, x_ref.shape[0]), axis=1).astype(BF16)
    mix = jnp.dot(on_ref[...], w_ref[:n0, :], preferred_element_type=F32)
    mix = mix + jnp.dot(od_ref[...], w_ref[n0:n1, :], preferred_element_type=F32)
    mix = mix + jnp.dot(o_mem, w_ref[n1:, :], preferred_element_type=F32)
    o_ref[...] = _layer_norm(DEEPNORM_ALPHA * x_ref[...] + mix, g_ref[...], b_ref[...])


def _out_ln(o_nsa, o_dsa, hb, mkv, x2, w_out, g, b, seq, m_len, tm):
    t, dm = x2.shape
    per_batch = seq // tm
    row = lambda a: pl.BlockSpec((tm, a.shape[1]), lambda i: (i, 0))
    full = lambda a: pl.BlockSpec(a.shape, lambda i: (0, 0))
    return pl.pallas_call(
        _out_ln_kernel,
        grid=(t // tm,),
        in_specs=[
            row(o_nsa), row(o_dsa),
            pl.BlockSpec((tm, 2 * LANES), lambda i: (i, BLK_MQ // 2)),
            pl.BlockSpec((m_len, MEM_HEADS * LANES), lambda i: (i // per_batch, 0)),
            row(x2), full(w_out), full(g), full(b),
        ],
        out_specs=pl.BlockSpec((tm, dm), lambda i: (i, 0)),
        out_shape=jax.ShapeDtypeStruct((t, dm), F32),
        compiler_params=_cparams(("parallel",), TILE["out_ln"]["vmem"]),
        name="out_ln",
    )(o_nsa, o_dsa, hb, mkv, x2, w_out, g, b)


NOT_PICKED = 64.0
PICK_BASE = -(2.0 ** 126)
PICK_STEPS = 32


def _top_rows(s, k, break_ties):
    n_rows = s.shape[0]
    rows = lax.broadcasted_iota(jnp.int32, s.shape, 0)
    assert k <= PICK_STEPS
    vals = []
    for r in range(k):
        best = jnp.max(s, axis=0, keepdims=True)
        hit = s == best
        if break_ties:
            hit = rows == jnp.min(jnp.where(hit, rows, n_rows), axis=0, keepdims=True)
        s = jnp.where(hit, PICK_BASE * (1.0 + r / PICK_STEPS), s)
        vals.append(best)
    rank = jnp.where(s <= PICK_BASE, (s * (1.0 / PICK_BASE) - 1.0) * PICK_STEPS, NOT_PICKED)
    return rank, vals


def _ranked_exactly(rank, k):
    return jnp.sum(jnp.where(rank < k, 1.0, 0.0), axis=0, keepdims=True) == float(k)


def _peer_select(s1, s2, k, break_ties):
    rank1, v1 = _top_rows(s1, k, break_ties)
    rank2, v2 = _top_rows(s2, k, break_ties)
    v2a = jnp.concatenate(v2, axis=0)
    v2lo = v2a[:8]
    r8 = lax.broadcasted_iota(jnp.int32, v2lo.shape, 0)
    pieces = [v1[0] + v2a, v1[1] + v2lo]
    for a in range(2, 8):
        pieces.append(jnp.where(r8 < k // (a + 1), v1[a] + v2lo, -jnp.inf))
    pieces.append(jnp.concatenate(v1[8:], axis=0) + v2[0])
    cand = jnp.concatenate(pieces, axis=0)
    rank_c, top = _top_rows(cand, k, break_ties)
    return rank1, rank2, rank_c, v1[0], v2[0], jnp.concatenate(top, axis=0)


def _peer_kernel(x1_ref, wqt_ref, k1_ref, k2_ref, u_ref, vt_ref, g_ref, b_ref, o_ref,
                 xt_scr, s1_scr, s2_scr, e1_scr, r2_scr, e2_scr, y_scr, *, tm, eb, nsub):
    j = pl.program_id(1)
    nk = PEER_N_KEYS
    half = PEER_KEY_DIM // 2
    n_lt = tm // LANES
    k = PEER_TOPK
    pack = BF16_ROWS

    @pl.when(j == 0)
    def _select():
        xt = jnp.transpose(x1_ref[...]).astype(BF16)
        xt_scr[...] = xt
        for h in range(PEER_HEADS):
            qh = jnp.dot(wqt_ref[h * PEER_KEY_DIM:(h + 1) * PEER_KEY_DIM, :], xt, preferred_element_type=F32).astype(BF16)
            s1_scr[h] = jnp.dot(k1_ref[...], qh[:half], preferred_element_type=F32)
            s2_scr[h] = jnp.dot(k2_ref[...], qh[half:], preferred_element_type=F32)

        n_par = 4
        per_head = n_lt // n_par

        def chunk(i, carry):
            h = i // per_head
            lanes = [pl.multiple_of(((i % per_head) * n_par + p) * LANES, LANES) for p in range(n_par)]
            s1s = [s1_scr[h, :, pl.ds(l0, LANES)] for l0 in lanes]
            s2s = [s2_scr[h, :, pl.ds(l0, LANES)] for l0 in lanes]
            fast = [_peer_select(s1, s2, k, break_ties=False) for s1, s2 in zip(s1s, s2s)]
            n_unclean = 0.0
            for f in fast:
                clean = _ranked_exactly(f[0], k) & _ranked_exactly(f[1], k) & _ranked_exactly(f[2], k)
                n_unclean = n_unclean + jnp.sum(jnp.where(clean, 0.0, 1.0))
            picks = lax.cond(
                n_unclean > 0.0,
                lambda: [_peer_select(s1, s2, k, break_ties=True) for s1, s2 in zip(s1s, s2s)],
                lambda: fast)
            for l0, s1, s2, (rank1, rank2, rank_c, v1_max, v2_max, top) in zip(lanes, s1s, s2s, picks):
                den = jnp.sum(jnp.exp(top - top[0:1]), axis=0, keepdims=True)
                picked = jnp.where(rank_c < k, 1.0, 0.0)
                n_of_rank = [jnp.sum(picked[0:16], axis=0, keepdims=True)]
                n_of_rank += [jnp.sum(picked[8 * a + 8:8 * a + 16], axis=0, keepdims=True) for a in range(1, 8)]
                n_of_rank += [picked[72 + a:73 + a] for a in range(8)]
                n1 = jnp.zeros_like(s1)
                for a in range(k):
                    n1 = jnp.where(rank1 == float(a), n_of_rank[a], n1)
                s1_scr[h, :, pl.ds(l0, LANES)] = n1
                e1_scr[h, :, pl.ds(l0, LANES)] = jnp.where(rank1 < k, 0.5 * jnp.exp(s1 - v1_max), 0.0)
                slab = h * n_lt + l0 // LANES
                r2_scr[slab] = rank2.astype(BF16)
                e2_scr[slab] = (jnp.where(rank2 < k, jnp.exp(s2 - v2_max), 0.0) / den).astype(BF16)
            return carry

        lax.fori_loop(0, PEER_HEADS * per_head, chunk, 0)
        y_scr[...] = jnp.zeros_like(y_scr)

    xt = xt_scr[...]
    y_add = None
    rows_per_step = nsub * eb // nk
    for sub in range(nsub):
        a = jnp.dot(u_ref[sub * eb:(sub + 1) * eb, :], xt, preferred_element_type=F32)
        zs = []
        for ib in range(eb // nk):
            row = sub * (eb // nk) + ib
            grp = pl.multiple_of(j * rows_per_step + (row // SUBLANES) * SUBLANES, SUBLANES)
            r_in = row % SUBLANES
            zrow = []
            for lt in range(n_lt):
                ls = slice(lt * LANES, (lt + 1) * LANES)
                w = jnp.zeros((nk // pack, pack, LANES), BF16)
                for h in range(PEER_HEADS):
                    n1r = s1_scr[h, pl.ds(grp, SUBLANES), ls][r_in:r_in + 1]
                    e1r = e1_scr[h, pl.ds(grp, SUBLANES), ls][r_in:r_in + 1]
                    n1b = jnp.broadcast_to(n1r, (pack, LANES)).astype(BF16)[None]
                    e1b = jnp.broadcast_to(e1r, (pack, LANES)).astype(BF16)[None]
                    r2 = r2_scr[h * n_lt + lt].reshape(nk // pack, pack, LANES)
                    e2 = e2_scr[h * n_lt + lt].reshape(nk // pack, pack, LANES)
                    w = w + jnp.where(r2 < n1b, e2, jnp.zeros_like(e2)) * e1b
                x_blk = a[ib * nk:(ib + 1) * nk, ls]
                act = (x_blk * (1.0 + lax.erf(x_blk * np.float32(1.0 / np.sqrt(2.0))))).astype(BF16)
                zrow.append(w.reshape(nk, LANES) * act)
            zs.append(jnp.concatenate(zrow, axis=1))
        z = jnp.concatenate(zs, axis=0)
        y_sub = jnp.dot(vt_ref[:, sub * eb:(sub + 1) * eb], z, preferred_element_type=F32)
        y_add = y_sub if y_add is None else y_add + y_sub
    y_scr[...] += y_add

    @pl.when(j == pl.num_programs(1) - 1)
    def _finish():
        y = jnp.transpose(y_scr[...])
        o_ref[...] = _layer_norm(DEEPNORM_ALPHA * x1_ref[...] + y, g_ref[...], b_ref[...])


def _peer(x1, wqt, k1, k2, u, vt, g, b, tm, eb, nsub):
    t, dm = x1.shape
    n_e = u.shape[0]
    assert (nsub * eb // PEER_N_KEYS) % SUBLANES == 0, "a grid step covers whole sublane groups of first-key rows"
    assert PEER_TOPK == 16 and PEER_N_KEYS == LANES, "the candidate-pair layout in _peer_select is written for k = 16"
    full = lambda a: pl.BlockSpec(a.shape, lambda i, j: (0, 0))
    tab = pltpu.VMEM((PEER_HEADS, PEER_N_KEYS, tm), F32)
    tab_b = pltpu.VMEM((PEER_HEADS * (tm // LANES), PEER_N_KEYS, LANES), BF16)
    return pl.pallas_call(
        functools.partial(_peer_kernel, tm=tm, eb=eb, nsub=nsub),
        grid=(t // tm, n_e // (nsub * eb)),
        in_specs=[
            pl.BlockSpec((tm, dm), lambda i, j: (i, 0)),
            full(wqt), full(k1), full(k2),
            pl.BlockSpec((nsub * eb, dm), lambda i, j: (j, 0)),
            pl.BlockSpec((dm, nsub * eb), lambda i, j: (0, j)),
            full(g), full(b),
        ],
        out_specs=pl.BlockSpec((tm, dm), lambda i, j: (i, 0)),
        out_shape=jax.ShapeDtypeStruct((t, dm), F32),
        scratch_shapes=[
            pltpu.VMEM((dm, tm), BF16), tab, tab, tab, tab_b, tab_b,
            pltpu.VMEM((dm, tm), F32),
        ],
        compiler_params=_cparams(("parallel", "arbitrary"), TILE["peer"]["vmem"]),
        name="peer",
    )(x1, wqt, k1, k2, u, vt, g, b)


def _layer(x, mem, positions, w_in, pe_k, pe_v, w1k, w2k, w1v, w2v, w_mem_kv, w_out, ln1_g, ln1_b,
           w_query, sk1, sk2, pu, pv, ln2_g, ln2_b):
    bsz, seq, dm = x.shape
    m_len = mem.shape[1]
    t = bsz * seq
    d = HEAD_DIM
    g_n = NSA_KV_HEADS
    nsa_t, dsa_t, peer_t = TILE["nsa"], TILE["dsa"], TILE["peer"]
    assert seq % nsa_t["tk"] == 0 and seq % dsa_t["tk"] == 0 and seq >= WINDOW + nsa_t["c"]
    assert seq % TILE["out_ln"]["tm"] == 0, "an out_ln token tile stays inside one batch row (one memory block)"
    assert dsa_t["tk"] >= min(DSA_TOPK_MAX, seq // 4), "a key tile holds at least top-k candidates"
    assert t % peer_t["tm"] == 0 and t % TILE["proj"]["tm"] == 0 and t % TILE["out_ln"]["tm"] == 0
    x2 = x.reshape(t, dm)

    hb, misc, kv32 = _proj(x2, _regroup_w_in(w_in), _rope_inputs(positions), tm=TILE["proj"]["tm"])

    zpad = jnp.zeros((CMP_HIDDEN, d), BF16)
    pe_rows = lambda pe: jnp.pad(pe.reshape(1, CMP_LEN * d), ((0, 7), (0, 0))).astype(BF16)
    w1k_b, w1v_b = w1k.astype(BF16), w1v.astype(BF16)
    kcvc = _compress(kv32, _compress_w1cat(w1k_b, w1v_b), w1k_b, w1v_b,
                     jnp.concatenate([w2k.astype(BF16), zpad], axis=1), jnp.concatenate([zpad, w2v.astype(BF16)], axis=1),
                     pe_rows(pe_k), pe_rows(pe_v), bsz, seq)

    ovl_t, expand = _nsa_consts(seq)
    o_nsa = _nsa(hb, misc, kcvc, ovl_t, expand, bsz, seq, c=nsa_t["c"], tk=nsa_t["tk"])
    o_dsa = _dsa(hb, misc, bsz, seq, c=dsa_t["c"], tk=dsa_t["tk"])

    wm = w_mem_kv.reshape(dm, 2, MEM_HEADS, d).transpose(0, 2, 1, 3).reshape(dm, MEM_HEADS * 2 * d).astype(BF16)
    mkv = _matmul(mem.reshape(bsz * m_len, dm), wm, tm=m_len, out_dtype=BF16)
    x1 = _out_ln(o_nsa, o_dsa, hb, mkv, x2, w_out.astype(BF16), ln1_g.reshape(1, dm), ln1_b.reshape(1, dm),
                 seq, m_len, tm=TILE["out_ln"]["tm"])

    x2o = _peer(x1, w_query.T.astype(BF16), sk1.astype(BF16), sk2.astype(BF16), pu.astype(BF16), pv.T.astype(BF16),
                ln2_g.reshape(1, dm), ln2_b.reshape(1, dm), tm=peer_t["tm"], eb=peer_t["eb"], nsub=peer_t["nsub"])
    return x2o.reshape(bsz, seq, dm)


def kernel(x, mem, positions, w_in, nsa_pe_k, nsa_pe_v, nsa_cmp_w1_k, nsa_cmp_w2_k, nsa_cmp_w1_v, nsa_cmp_w2_v,
           w_mem_kv, w_out, ln1_g, ln1_b, peer_w_query, peer_sub_keys_1, peer_sub_keys_2, peer_u, peer_v, ln2_g, ln2_b):
    assert w_in.shape[0] == DEPTH
    return _layer(x, mem, positions, w_in[0], nsa_pe_k[0], nsa_pe_v[0], nsa_cmp_w1_k[0], nsa_cmp_w2_k[0],
                  nsa_cmp_w1_v[0], nsa_cmp_w2_v[0], w_mem_kv[0], w_out[0], ln1_g[0], ln1_b[0], peer_w_query[0],
                  peer_sub_keys_1[0], peer_sub_keys_2[0], peer_u[0], peer_v[0], ln2_g[0], ln2_b[0])
```

```python
import functools

import numpy as np
import jax
import jax.numpy as jnp
from jax import lax
from jax.experimental import pallas as pl
from jax.experimental.pallas import tpu as pltpu

F32 = jnp.float32
BF16 = jnp.bfloat16

LANES = 128
SUBLANES = 8
BF16_ROWS = 16
MIB = 1 << 20

TILE = dict(
    proj=dict(tm=512, vmem=40 * MIB),
    compress=dict(vmem=16 * MIB),
    nsa=dict(c=512, tk=1024, vmem=56 * MIB),
    dsa=dict(c=256, tk=512, vmem=32 * MIB),
    mem=dict(vmem=16 * MIB),
    out_ln=dict(tm=512, vmem=24 * MIB),
    peer=dict(tm=512, eb=1024, nsub=2, vmem=56 * MIB),
)

HEAD_DIM = 64
ROPE_THETA = 500000.0
LN_EPS = 1e-5
NSA_HEADS = 8
NSA_KV_HEADS = 2
CMP_LEN = 32
CMP_STRIDE = 16
CMP_HIDDEN = 128
SEL_BLOCK = 64
SEL_COUNT = 16
WINDOW = 512
FORCE_BONUS = 1e4
DSA_HEADS = 4
IDX_HEADS = 8
IDX_DIM = 32
DSA_TOPK_MAX = 256
MEM_HEADS = 4
PEER_HEADS = 8
PEER_N_KEYS = 128
PEER_KEY_DIM = 256
PEER_TOPK = 16
DEPTH = 1
DEEPNORM_ALPHA = (2.0 * DEPTH) ** 0.25

NEG = -1e30

BLK_QN = 0
BLK_NKV = 4
BLK_DQ = 10
BLK_IQ = 12
BLK_MQ = 14
BLK_DKV = 16
BLK_IK = 17
BLK_MISC = 18
N_BLK = 19
MISC_W = 0
MISC_G = IDX_HEADS


def _cparams(sem, vmem):
    return pltpu.CompilerParams(dimension_semantics=sem, vmem_limit_bytes=vmem)


def _regroup_w_in(w_in):
    d = HEAD_DIM
    o_q = 0
    o_kv = o_q + NSA_HEADS * d
    o_g = o_kv + 6 * NSA_KV_HEADS * d
    o_dq = o_g + 3 * NSA_HEADS
    o_dkv = o_dq + DSA_HEADS * d
    o_iq = o_dkv + 2 * d
    o_ik = o_iq + IDX_HEADS * IDX_DIM
    o_iw = o_ik + IDX_DIM
    o_mq = o_iw + IDX_HEADS
    cols = [w_in[:, o_q:o_kv]]
    for br in range(3):
        for g in range(NSA_KV_HEADS):
            k0 = o_kv + ((2 * br) * NSA_KV_HEADS + g) * d
            v0 = o_kv + ((2 * br + 1) * NSA_KV_HEADS + g) * d
            cols += [w_in[:, k0:k0 + d], w_in[:, v0:v0 + d]]
    cols.append(w_in[:, o_dq:o_dkv])
    cols.append(w_in[:, o_iq:o_ik])
    cols.append(w_in[:, o_mq:o_mq + MEM_HEADS * d])
    cols.append(w_in[:, o_dkv:o_iq])
    cols += [w_in[:, o_ik:o_iw]] * (LANES // IDX_DIM)
    cols += [w_in[:, o_iw:o_mq], w_in[:, o_g:o_dq]]
    pad = LANES - IDX_HEADS - 3 * NSA_HEADS
    cols.append(jnp.zeros((w_in.shape[0], pad), w_in.dtype))
    w = jnp.concatenate(cols, axis=1)
    assert w.shape[1] == N_BLK * LANES
    return w.astype(BF16)


def _rot_half(head_dim):
    return head_dim // 8


def _rope_inputs(positions):
    pos = positions.reshape(-1).astype(F32)
    vals, place, ones = [], [], []
    src = 0
    for hd in (HEAD_DIM, IDX_DIM):
        half = _rot_half(hd)
        freqs = jnp.power(ROPE_THETA, -jnp.arange(half, dtype=F32) * 2.0 / (2 * half))
        ang = pos[:, None] * freqs
        vals += [jnp.cos(ang), jnp.sin(ang)]
        d = np.arange(LANES) % hd
        lanes = np.arange(LANES)
        sel = np.zeros((3, LANES, LANES), np.float32)
        rot, first, second = d < 2 * half, d < half, (d >= half) & (d < 2 * half)
        sel[0, src + d[rot] % half, lanes[rot]] = 1.0
        sel[1, src + half + d[first], lanes[first]] = -1.0
        sel[2, src + half + d[second] - half, lanes[second]] = 1.0
        place += [sel[0], sel[1], sel[2]]
        ones += [(~rot).astype(np.float32), np.zeros(LANES, np.float32), np.zeros(LANES, np.float32)]
        src += 2 * half
    cs = jnp.concatenate(vals, axis=1)
    cs = jnp.pad(cs, ((0, 0), (0, LANES - cs.shape[1])))
    return cs, jnp.asarray(np.concatenate(place, axis=1), BF16), jnp.asarray(np.concatenate(ones)[None, :], F32)


def _proj_kernel(x_ref, w_ref, cs_ref, place_ref, ones_ref, hb_ref, misc_ref, cmp_ref):
    xb = x_ref[...].astype(BF16)
    tm = xb.shape[0]
    lane = lax.broadcasted_iota(jnp.int32, (tm, LANES), 1)
    lo = lane < HEAD_DIM
    cs = cs_ref[...]
    hi = cs.astype(BF16)
    rest = cs - hi.astype(F32)
    mid = rest.astype(BF16)
    low = (rest - mid.astype(F32)).astype(BF16)
    tabs = ones_ref[...]
    for piece in (hi, mid, low):
        tabs = tabs + jnp.dot(piece, place_ref[...], preferred_element_type=F32)
    c64, sa64, sb64 = (tabs[:, i * LANES:(i + 1) * LANES] for i in range(3))
    c32, sa32, sb32 = (tabs[:, i * LANES:(i + 1) * LANES] for i in range(3, 6))
    c64h, sa64h, sb64h = jnp.where(lo, c64, 1.0), jnp.where(lo, sa64, 0.0), jnp.where(lo, sb64, 0.0)

    def rope(h, c, sa, sb, half):
        return h * c + pltpu.roll(h, LANES - half, 1) * sa + pltpu.roll(h, half, 1) * sb

    def finish(blk, h):
        if blk < BLK_NKV or BLK_DQ <= blk < BLK_IQ:
            return rope(h, c64, sa64, sb64, _rot_half(HEAD_DIM))
        if BLK_NKV <= blk < BLK_DQ or blk == BLK_DKV:
            return rope(h, c64h, sa64h, sb64h, _rot_half(HEAD_DIM))
        if BLK_IQ <= blk < BLK_MQ or blk == BLK_IK:
            return rope(h, c32, sa32, sb32, _rot_half(IDX_DIM))
        return h

    for j in range(BLK_MISC // 2):
        h = jnp.dot(xb, w_ref[:, j * 2 * LANES:(j + 1) * 2 * LANES], preferred_element_type=F32)
        for s in range(2):
            blk = 2 * j + s
            roped = finish(blk, h[:, s * LANES:(s + 1) * LANES])
            hb_ref[:, blk * LANES:(blk + 1) * LANES] = roped.astype(BF16)
            if BLK_NKV <= blk < BLK_NKV + NSA_KV_HEADS:
                cmp_ref[:, (blk - BLK_NKV) * LANES:(blk - BLK_NKV + 1) * LANES] = roped
    misc_ref[...] = jnp.dot(xb, w_ref[:, BLK_MISC * LANES:], preferred_element_type=F32)


def _proj(x2, w2, rope, tm):
    t, dm = x2.shape
    cs, place, ones = rope
    return pl.pallas_call(
        _proj_kernel,
        grid=(t // tm,),
        in_specs=[
            pl.BlockSpec((tm, dm), lambda i: (i, 0)),
            pl.BlockSpec((dm, N_BLK * LANES), lambda i: (0, 0)),
            pl.BlockSpec((tm, LANES), lambda i: (i, 0)),
            pl.BlockSpec(place.shape, lambda i: (0, 0)),
            pl.BlockSpec(ones.shape, lambda i: (0, 0)),
        ],
        out_specs=[
            pl.BlockSpec((tm, BLK_MISC * LANES), lambda i: (i, 0)),
            pl.BlockSpec((tm, LANES), lambda i: (i, 0)),
            pl.BlockSpec((tm, NSA_KV_HEADS * LANES), lambda i: (i, 0)),
        ],
        out_shape=[
            jax.ShapeDtypeStruct((t, BLK_MISC * LANES), BF16),
            jax.ShapeDtypeStruct((t, LANES), F32),
            jax.ShapeDtypeStruct((t, NSA_KV_HEADS * LANES), F32),
        ],
        compiler_params=_cparams(("parallel",), TILE["proj"]["vmem"]),
        name="proj",
    )(x2, w2, cs, place, ones)


def _gelu(x):
    return 0.5 * x * (1.0 + lax.erf(x * np.float32(1.0 / np.sqrt(2.0))))


def _compress_kernel(kv_ref, w1cat_ref, w1k_ref, w1v_ref, w2k_ref, w2v_ref, pek_ref, pev_ref, out_ref):
    rows = out_ref.shape[1]
    hid = w1k_ref.shape[1]
    acc = jnp.zeros((rows, 4 * hid), F32)
    for l in range(CMP_STRIDE):
        tok = kv_ref[pl.ds(l, rows, stride=CMP_STRIDE), :].astype(BF16)
        acc = acc + jnp.dot(tok, w1cat_ref[l], preferred_element_type=F32)

    def hidden(a, b, w1_ref, pe_ref):
        bias = jnp.dot(pe_ref[...], w1_ref[...], preferred_element_type=F32)[0:1, :]
        return _gelu(a + pltpu.roll(b, rows - 1, 0) + bias).astype(BF16)

    hk = hidden(acc[:, 0:hid], acc[:, hid:2 * hid], w1k_ref, pek_ref)
    hv = hidden(acc[:, 2 * hid:3 * hid], acc[:, 3 * hid:4 * hid], w1v_ref, pev_ref)
    out = jnp.dot(hk, w2k_ref[...], preferred_element_type=F32) + jnp.dot(hv, w2v_ref[...], preferred_element_type=F32)
    out_ref[0] = out.astype(BF16)


def _compress_w1cat(w1k, w1v):
    d = HEAD_DIM
    k3 = w1k.reshape(2, CMP_STRIDE, d, CMP_HIDDEN)
    v3 = w1v.reshape(2, CMP_STRIDE, d, CMP_HIDDEN)
    z = jnp.zeros_like(k3[0])
    top = jnp.concatenate([k3[0], k3[1], z, z], axis=2)
    bot = jnp.concatenate([z, z, v3[0], v3[1]], axis=2)
    return jnp.concatenate([top, bot], axis=1)


def _compress(kv32, w1cat, w1k, w1v, w2k, w2v, pek, pev, bsz, seq):
    g_n = NSA_KV_HEADS
    rows = seq // CMP_STRIDE
    full = lambda a: pl.BlockSpec(a.shape, lambda i: (0,) * a.ndim)
    return pl.pallas_call(
        _compress_kernel,
        grid=(bsz * g_n,),
        in_specs=[
            pl.BlockSpec((seq, LANES), lambda i: (i // g_n, i % g_n)),
            full(w1cat), full(w1k), full(w1v), full(w2k), full(w2v), full(pek), full(pev),
        ],
        out_specs=pl.BlockSpec((1, rows, LANES), lambda i: (i, 0, 0)),
        out_shape=jax.ShapeDtypeStruct((bsz * g_n, rows, LANES), BF16),
        compiler_params=_cparams(("parallel",), TILE["compress"]["vmem"]),
        name="compress",
    )(kv32, w1cat, w1k, w1v, w2k, w2v, pek, pev)


def _stack_heads(qpair_refs_or_vals, scale):
    outs = []
    for blk in qpair_refs_or_vals:
        b = blk.astype(F32) * scale
        lane = lax.broadcasted_iota(jnp.int32, b.shape, 1)
        lo = lane < HEAD_DIM
        outs.append(jnp.where(lo, b, 0.0))
        outs.append(jnp.where(lo, pltpu.roll(b, HEAD_DIM, 1), 0.0))
    return jnp.concatenate(outs, axis=0).astype(BF16)


def _qk(q, kv):
    return lax.dot_general(q, kv, (((1,), (1,)), ((), ())), preferred_element_type=F32)


def _lane_pick(x, col):
    lane = lax.broadcasted_iota(jnp.int32, x.shape, 1)
    return jnp.sum(jnp.where(lane == col, x, 0.0), axis=1, keepdims=True)


def _flash_init(n_heads, c):
    return jnp.full((n_heads, c, 1), NEG, F32), jnp.zeros((n_heads, c, LANES), F32)


def _flash_step(s, valid, kv, m, acc):
    n_heads, c, tk = s.shape
    sb = s.astype(BF16) + jnp.where(valid, 0.0, NEG).astype(BF16)
    m_new = jnp.maximum(m, jnp.max(sb, axis=-1, keepdims=True).astype(F32))
    p = jnp.exp(sb - m_new.astype(BF16))
    lane = lax.broadcasted_iota(jnp.int32, kv.shape, 1)
    ones_v = jnp.where(lane < HEAD_DIM, jnp.ones_like(kv), kv)
    pv = jnp.dot(p.reshape(n_heads * c, tk), ones_v, preferred_element_type=F32)
    return m_new, jnp.exp(m - m_new) * acc + pv.reshape(n_heads, c, LANES)


def _normalize(acc):
    den = acc[..., 0:1]
    return acc / jnp.where(den > 0.0, den, 1.0)


def _pack_heads(o, n_heads, c):
    lane = lax.broadcasted_iota(jnp.int32, (c, LANES), 1)
    lo = lane < HEAD_DIM
    blocks = []
    for p in range(n_heads // 2):
        a = o[(2 * p) * c:(2 * p + 1) * c]
        b = o[(2 * p + 1) * c:(2 * p + 2) * c]
        blocks.append(jnp.where(lo, pltpu.roll(a, HEAD_DIM, 1), b))
    return blocks


def _nsa_kernel(q_ref, misc_ref, kc_ref, ks_ref, kw_ref, ovl_ref, exp_ref, o_ref, *, c, tk, seq):
    g = pl.program_id(1)
    ci = pl.program_id(2)
    t0 = ci * c
    hpg = NSA_HEADS // NSA_KV_HEADS
    rows = hpg * c
    scale = HEAD_DIM ** -0.5
    q4 = _stack_heads([q_ref[:, :LANES], q_ref[:, LANES:]], scale)
    trow = t0 + lax.broadcasted_iota(jnp.int32, (c, 1), 0)

    kc = kc_ref[0]
    ncp = kc.shape[0]
    s = _qk(q4, kc).reshape(hpg, c, ncp)
    ncol = lax.broadcasted_iota(jnp.int32, (c, ncp), 1)
    cvalid = (ncol * CMP_STRIDE + (CMP_LEN - 1)) <= trow
    s = jnp.where(cvalid, s, NEG)
    m = jnp.max(s, axis=-1, keepdims=True)
    e = jnp.where(cvalid, jnp.exp(s - m), 0.0)
    den = jnp.sum(e, axis=-1, keepdims=True)
    p = (e / jnp.where(den > 0.0, den, 1.0)).reshape(rows, ncp)
    pb = p.astype(BF16)
    o_cmp = jnp.dot(pb, kc, preferred_element_type=F32)

    imp_t = lax.dot_general(ovl_ref[...], pb, (((1,), (1,)), ((), ())), preferred_element_type=F32)
    nbp = imp_t.shape[0]
    nb = seq // SEL_BLOCK
    imp = imp_t[:nb, 0:c]
    for h in range(1, hpg):
        imp = imp + imp_t[:nb, h * c:(h + 1) * c]
    jrow = lax.broadcasted_iota(jnp.int32, (nb, c), 0)
    tcol = t0 + lax.broadcasted_iota(jnp.int32, (nb, c), 1)
    cur = tcol // SEL_BLOCK
    forced = (jrow == 0) | (jrow == cur) | (jrow == cur - 1)
    visible = (jrow * SEL_BLOCK) <= tcol
    score = jnp.where(visible, imp + jnp.where(forced, FORCE_BONUS, 0.0), -jnp.inf)
    n_pick = min(SEL_COUNT, nb)
    rank, _ = _top_rows(score, n_pick, break_ties=True)
    sel = jnp.where(rank < n_pick, 1.0, 0.0)
    if nbp > nb:
        sel = jnp.concatenate([sel, jnp.zeros((nbp - nb, c), F32)], axis=0)
    sel_b = jnp.transpose(sel).astype(BF16)

    n_tiles = (t0 + c + tk - 1) // tk
    kcol = lax.broadcasted_iota(jnp.int32, (c, tk), 1)

    def sel_body(kt, carry):
        k0 = pl.multiple_of(kt * tk, tk)
        kv = ks_ref[pl.ds(k0, tk), :]
        s_ = _qk(q4, kv).reshape(hpg, c, tk)
        tok = jnp.dot(sel_b, exp_ref[:, pl.ds(k0, tk)], preferred_element_type=F32)
        valid = (tok > 0.5) & ((k0 + kcol) <= trow)
        return _flash_step(s_, valid, kv, *carry)

    init = _flash_init(hpg, c)
    _, acc_s = lax.fori_loop(0, n_tiles, sel_body, init)
    o_sel = _normalize(acc_s).reshape(rows, LANES)

    wk = WINDOW + c
    w0 = pl.multiple_of(jnp.maximum(t0 - WINDOW, 0), c)
    kvw = kw_ref[pl.ds(w0, wk), :]
    s = _qk(q4, kvw).reshape(hpg, c, wk)
    wpos = w0 + lax.broadcasted_iota(jnp.int32, (c, wk), 1)
    wvalid = (wpos <= trow) & (wpos > trow - WINDOW)
    _, acc_w = _flash_step(s, wvalid, kvw, *init)
    o_win = _normalize(acc_w).reshape(rows, LANES)

    gates = jax.nn.sigmoid(misc_ref[...])
    outs = []
    for h in range(hpg):
        col = MISC_G + (g * hpg + h) * 3
        sl = slice(h * c, (h + 1) * c)
        outs.append(_lane_pick(gates, col) * o_cmp[sl] + _lane_pick(gates, col + 1) * o_sel[sl]
                    + _lane_pick(gates, col + 2) * o_win[sl])
    blocks = _pack_heads(jnp.concatenate(outs, axis=0), hpg, c)
    for i, blk in enumerate(blocks):
        o_ref[:, i * LANES:(i + 1) * LANES] = blk.astype(BF16)


def _nsa(hb, misc, kcvc, ovl_t, expand, bsz, seq, c, tk):
    t = hb.shape[0]
    g_n = NSA_KV_HEADS
    nch = seq // c
    ncp = kcvc.shape[1]
    nbp = ovl_t.shape[0]
    kern = functools.partial(_nsa_kernel, c=c, tk=tk, seq=seq)
    return pl.pallas_call(
        kern,
        grid=(bsz, g_n, nch),
        in_specs=[
            pl.BlockSpec((c, 2 * LANES), lambda b, g, i: (b * nch + i, g)),
            pl.BlockSpec((c, LANES), lambda b, g, i: (b * nch + i, 0)),
            pl.BlockSpec((1, ncp, LANES), lambda b, g, i: (b * g_n + g, 0, 0)),
            pl.BlockSpec((seq, LANES), lambda b, g, i: (b, BLK_NKV + 2 + g)),
            pl.BlockSpec((seq, LANES), lambda b, g, i: (b, BLK_NKV + 4 + g)),
            pl.BlockSpec((nbp, ncp), lambda b, g, i: (0, 0)),
            pl.BlockSpec((nbp, seq), lambda b, g, i: (0, 0)),
        ],
        out_specs=pl.BlockSpec((c, 2 * LANES), lambda b, g, i: (b * nch + i, g)),
        out_shape=jax.ShapeDtypeStruct((t, NSA_HEADS * HEAD_DIM), BF16),
        compiler_params=_cparams(("parallel", "parallel", "arbitrary"), TILE["nsa"]["vmem"]),
        name="nsa",
    )(hb, misc, kcvc, hb, hb, ovl_t, expand)


def _nsa_consts(seq):
    nb = seq // SEL_BLOCK
    nbp = max(LANES, nb)
    rows = seq // CMP_STRIDE
    nc = (seq - CMP_LEN) // CMP_STRIDE + 1
    j = np.arange(nbp)[:, None]
    n = np.arange(rows)[None, :]
    ovl = np.clip(np.minimum(n * CMP_STRIDE + CMP_LEN, j * SEL_BLOCK + SEL_BLOCK)
                  - np.maximum(n * CMP_STRIDE, j * SEL_BLOCK), 0, None).astype(np.float32) / CMP_LEN
    ovl = np.where((n < nc) & (j < nb), ovl, 0.0)
    s = np.arange(seq)[None, :]
    expand = (s // SEL_BLOCK == j).astype(np.float32)
    return jnp.asarray(ovl, BF16), jnp.asarray(expand, BF16)


INT_MIN = -2147483648


def _dsa_kernel(q_ref, iq_ref, misc_ref, ik_ref, kv_ref, o_ref, key_scr, hi_scr, lo_scr, *, c, tk, seq, topk):
    ci = pl.program_id(1)
    t0 = ci * c
    n_tiles = (t0 + c + tk - 1) // tk
    tq = t0 + lax.broadcasted_iota(jnp.int32, (1, c), 1)
    krow = lax.broadcasted_iota(jnp.int32, (tk, c), 0)

    lane = lax.broadcasted_iota(jnp.int32, (c, LANES), 1)
    per_blk = LANES // IDX_DIM
    qs = []
    for h in range(IDX_HEADS):
        blk = iq_ref[:, (h // per_blk) * LANES:(h // per_blk + 1) * LANES]
        qs.append(jnp.where(lane // IDX_DIM == h % per_blk, blk, jnp.zeros_like(blk)))
    qst = jnp.concatenate(qs, axis=0)
    w_t = jnp.transpose(misc_ref[...] * (IDX_HEADS ** -0.5 * IDX_DIM ** -0.5))

    def score_body(kt, carry):
        k0 = pl.multiple_of(kt * tk, tk)
        lg = _qk(ik_ref[pl.ds(k0, tk), :], qst)
        sc = jnp.zeros((tk, c), F32)
        for h in range(IDX_HEADS):
            sc = sc + jnp.maximum(lg[:, h * c:(h + 1) * c], 0.0) * w_t[MISC_W + h:MISC_W + h + 1, :]
        sc = sc + 0.0
        sc = jnp.where((k0 + krow) <= tq, sc, -jnp.inf)
        bits = pltpu.bitcast(sc, jnp.int32)
        key = jnp.where(bits < 0, bits ^ jnp.int32(0x7FFFFFFF), bits)
        key_scr[pl.ds(k0, tk), :] = key
        hi_scr[pl.ds(k0, tk), :] = (key >> 16).astype(jnp.int16)
        lo_scr[pl.ds(k0, tk), :] = ((key & jnp.int32(0xFFFF)) - 32768).astype(jnp.int16)
        return carry

    lax.fori_loop(0, n_tiles, score_body, 0)

    n_acc = 8

    def count(pred):
        def body(kt, acc):
            k0 = pl.multiple_of(kt * tk, tk)
            hit = jnp.where(pred(key_scr[pl.ds(k0, tk), :], k0 + krow), 1.0, 0.0)
            return acc + jnp.sum(hit.reshape(tk // (8 * n_acc), n_acc * 8, c), axis=0)
        acc = lax.fori_loop(0, n_tiles, body, jnp.zeros((n_acc * 8, c), F32))
        return jnp.sum(acc, axis=0, keepdims=True)

    n_acc_h = 4
    assert seq // (BF16_ROWS * n_acc_h) < 2 ** 15

    def half_rows(x):
        return jnp.broadcast_to(x, (BF16_ROWS, c)).astype(jnp.int16)[None]

    def tiles16(ref, kt):
        return ref[pl.ds(pl.multiple_of(kt * tk, tk), tk), :].reshape(tk // BF16_ROWS, BF16_ROWS, c)

    def count16(ref, pred):
        def body(kt, acc):
            x = tiles16(ref, kt)
            hit = jnp.where(pred(x), jnp.ones_like(x), jnp.zeros_like(x))
            parts = hit.reshape(tk // (BF16_ROWS * n_acc_h), n_acc_h, BF16_ROWS, c)
            for g in range(parts.shape[0]):
                acc = acc + parts[g]
            return acc
        acc = lax.fori_loop(0, n_tiles, body, jnp.zeros((n_acc_h, BF16_ROWS, c), jnp.int16))
        return jnp.sum(acc.astype(F32).reshape(n_acc_h * BF16_ROWS, c), axis=0, keepdims=True)

    def hi_body(i, ans):
        cand = ans | (jnp.int32(1) << (31 - i))
        cand_b = half_rows((cand ^ jnp.int32(INT_MIN)) >> 16)
        cnt = count16(hi_scr, lambda hi: hi >= cand_b)
        return jnp.where(cnt >= topk, cand, ans)

    thr_hi = lax.fori_loop(0, 16, hi_body, jnp.zeros((1, c), jnp.int32))
    hi_b = half_rows((thr_hi ^ jnp.int32(INT_MIN)) >> 16)
    above = count16(hi_scr, lambda hi: hi > hi_b)

    def keep_equal(kt, carry):
        k0 = pl.multiple_of(kt * tk, tk)
        lo = jnp.where(tiles16(hi_scr, kt) == hi_b, tiles16(lo_scr, kt), jnp.int16(-32768))
        lo_scr[pl.ds(k0, tk), :] = lo.reshape(tk, c)
        return carry

    lax.fori_loop(0, n_tiles, keep_equal, 0)

    def lo_body(i, carry):
        ans, cnt_ans = carry
        cand = ans | (jnp.int32(1) << (15 - i))
        cand_b = half_rows(cand - 32768)
        cnt = above + count16(lo_scr, lambda lo: lo >= cand_b)
        ok = cnt >= topk
        return jnp.where(ok, cand, ans), jnp.where(ok, cnt, cnt_ans)

    thr_lo, cnt_ge = lax.fori_loop(0, 16, lo_body,
                                   (jnp.zeros((1, c), jnp.int32), jnp.full((1, c), float(seq + topk), F32)))
    thr = (thr_hi | thr_lo) ^ jnp.int32(INT_MIN)
    lo_b = half_rows(thr_lo - 32768)
    cnt_gt = above + count16(lo_scr, lambda lo: lo > lo_b)
    need = topk - cnt_gt
    n_ties = cnt_ge - cnt_gt

    n_bits = max(1, int(np.ceil(np.log2(seq))))

    def tie_search():
        def tie_body(i, ans):
            cand = ans | (jnp.int32(1) << (n_bits - 1 - i))
            cnt = count(lambda keys, idx: (keys == thr) & (idx < cand))
            return jnp.where(cnt < need, cand, ans)
        return lax.fori_loop(0, n_bits, tie_body, jnp.zeros((1, c), jnp.int32))

    surplus = jnp.max(n_ties - need)
    last_tie = lax.cond(surplus > 0.0, tie_search, lambda: jnp.full((1, c), seq, jnp.int32))

    q4 = _stack_heads([q_ref[:, :LANES], q_ref[:, LANES:]], HEAD_DIM ** -0.5)
    eye = (lax.broadcasted_iota(jnp.int32, (c, c), 0) == lax.broadcasted_iota(jnp.int32, (c, c), 1)).astype(BF16)

    def att_body(kt, carry):
        k0 = pl.multiple_of(kt * tk, tk)
        kv = kv_ref[pl.ds(k0, tk), :]
        keys = key_scr[pl.ds(k0, tk), :]
        idx = k0 + krow
        picked = (idx <= tq) & ((keys > thr) | ((keys == thr) & (idx <= last_tie)))
        valid = _qk(eye, jnp.where(picked, 1.0, 0.0).astype(BF16)) > 0.5
        s = _qk(q4, kv).reshape(DSA_HEADS, c, tk)
        return _flash_step(s, valid, kv, *carry)

    _, acc = lax.fori_loop(0, n_tiles, att_body, _flash_init(DSA_HEADS, c))
    o = _normalize(acc).reshape(DSA_HEADS * c, LANES)
    for i, blk in enumerate(_pack_heads(o, DSA_HEADS, c)):
        o_ref[:, i * LANES:(i + 1) * LANES] = blk.astype(BF16)


def _dsa(hb, misc, bsz, seq, c, tk):
    t = hb.shape[0]
    nch = seq // c
    topk = min(DSA_TOPK_MAX, seq // 4)
    kern = functools.partial(_dsa_kernel, c=c, tk=tk, seq=seq, topk=topk)
    return pl.pallas_call(
        kern,
        grid=(bsz, nch),
        in_specs=[
            pl.BlockSpec((c, 2 * LANES), lambda b, i: (b * nch + i, BLK_DQ // 2)),
            pl.BlockSpec((c, 2 * LANES), lambda b, i: (b * nch + i, BLK_IQ // 2)),
            pl.BlockSpec((c, LANES), lambda b, i: (b * nch + i, 0)),
            pl.BlockSpec((seq, LANES), lambda b, i: (b, BLK_IK)),
            pl.BlockSpec((seq, LANES), lambda b, i: (b, BLK_DKV)),
        ],
        out_specs=pl.BlockSpec((c, 2 * LANES), lambda b, i: (b * nch + i, 0)),
        out_shape=jax.ShapeDtypeStruct((t, DSA_HEADS * HEAD_DIM), BF16),
        scratch_shapes=[pltpu.VMEM((seq, c), jnp.int32), pltpu.VMEM((seq, c), jnp.int16),
                        pltpu.VMEM((seq, c), jnp.int16)],
        compiler_params=_cparams(("parallel", "arbitrary"), TILE["dsa"]["vmem"]),
        name="dsa",
    )(hb, hb, misc, hb, hb)


def _matmul_kernel(a_ref, w_ref, o_ref):
    o_ref[...] = jnp.dot(a_ref[...].astype(BF16), w_ref[...], preferred_element_type=F32).astype(o_ref.dtype)


def _matmul(a, w, tm, out_dtype):
    m, k = a.shape
    n = w.shape[1]
    return pl.pallas_call(
        _matmul_kernel,
        grid=(m // tm,),
        in_specs=[pl.BlockSpec((tm, k), lambda i: (i, 0)), pl.BlockSpec((k, n), lambda i: (0, 0))],
        out_specs=pl.BlockSpec((tm, n), lambda i: (i, 0)),
        out_shape=jax.ShapeDtypeStruct((m, n), out_dtype),
        compiler_params=_cparams(("parallel",), TILE["mem"]["vmem"]),
        name="matmul",
    )(a, w)


def _mem_attention(q_ref, mkv_ref, c):
    q4 = _stack_heads([q_ref[:, :LANES], q_ref[:, LANES:]], HEAD_DIM ** -0.5)
    outs = []
    for h in range(MEM_HEADS):
        kv = mkv_ref[:, h * LANES:(h + 1) * LANES]
        s = _qk(q4[h * c:(h + 1) * c], kv)
        e = jnp.exp(s - jnp.max(s, axis=-1, keepdims=True))
        p = e / jnp.sum(e, axis=-1, keepdims=True)
        outs.append(jnp.dot(p.astype(BF16), kv, preferred_element_type=F32))
    return _pack_heads(jnp.concatenate(outs, axis=0), MEM_HEADS, c)


def _layer_norm(v, g, b):
    mu = jnp.mean(v, axis=-1, keepdims=True)
    d = v - mu
    var = jnp.mean(d * d, axis=-1, keepdims=True)
    return d * lax.rsqrt(var + LN_EPS) * g + b


def _out_ln_kernel(on_ref, od_ref, mq_ref, mkv_ref, x_ref, w_ref, g_ref, b_ref, o_ref):
    n0 = on_ref.shape[1]
    n1 = n0 + od_ref.shape[1]
    o_mem = jnp.concatenate(_mem_attention(mq_ref, mkv_ref, x_ref.shape[0]), axis=1).astype(BF16)
    mix = jnp.dot(on_ref[...], w_ref[:n0, :], preferred_element_type=F32)
    mix = mix + jnp.dot(od_ref[...], w_ref[n0:n1, :], preferred_element_type=F32)
    mix = mix + jnp.dot(o_mem, w_ref[n1:, :], preferred_element_type=F32)
    o_ref[...] = _layer_norm(DEEPNORM_ALPHA * x_ref[...] + mix, g_ref[...], b_ref[...])


def _out_ln(o_nsa, o_dsa, hb, mkv, x2, w_out, g, b, seq, m_len, tm):
    t, dm = x2.shape
    per_batch = seq // tm
    row = lambda a: pl.BlockSpec((tm, a.shape[1]), lambda i: (i, 0))
    full = lambda a: pl.BlockSpec(a.shape, lambda i: (0, 0))
    return pl.pallas_call(
        _out_ln_kernel,
        grid=(t // tm,),
        in_specs=[
            row(o_nsa), row(o_dsa),
            pl.BlockSpec((tm, 2 * LANES), lambda i: (i, BLK_MQ // 2)),
            pl.BlockSpec((m_len, MEM_HEADS * LANES), lambda i: (i // per_batch, 0)),
            row(x2), full(w_out), full(g), full(b),
        ],
        out_specs=pl.BlockSpec((tm, dm), lambda i: (i, 0)),
        out_shape=jax.ShapeDtypeStruct((t, dm), F32),
        compiler_params=_cparams(("parallel",), TILE["out_ln"]["vmem"]),
        name="out_ln",
    )(o_nsa, o_dsa, hb, mkv, x2, w_out, g, b)


NOT_PICKED = 64.0
PICK_BASE = -(2.0 ** 126)
PICK_STEPS = 32


def _top_rows(s, k, break_ties):
    n_rows = s.shape[0]
    rows = lax.broadcasted_iota(jnp.int32, s.shape, 0)
    assert k <= PICK_STEPS
    vals = []
    for r in range(k):
        best = jnp.max(s, axis=0, keepdims=True)
        hit = s == best
        if break_ties:
            hit = rows == jnp.min(jnp.where(hit, rows, n_rows), axis=0, keepdims=True)
        s = jnp.where(hit, PICK_BASE * (1.0 + r / PICK_STEPS), s)
        vals.append(best)
    rank = jnp.where(s <= PICK_BASE, (s * (1.0 / PICK_BASE) - 1.0) * PICK_STEPS, NOT_PICKED)
    return rank, vals


def _ranked_exactly(rank, k):
    return jnp.sum(jnp.where(rank < k, 1.0, 0.0), axis=0, keepdims=True) == float(k)


def _peer_select(s1, s2, k, break_ties):
    rank1, v1 = _top_rows(s1, k, break_ties)
    rank2, v2 = _top_rows(s2, k, break_ties)
    v2a = jnp.concatenate(v2, axis=0)
    v2lo = v2a[:8]
    r8 = lax.broadcasted_iota(jnp.int32, v2lo.shape, 0)
    pieces = [v1[0] + v2a, v1[1] + v2lo]
    for a in range(2, 8):
        pieces.append(jnp.where(r8 < k // (a + 1), v1[a] + v2lo, -jnp.inf))
    pieces.append(jnp.concatenate(v1[8:], axis=0) + v2[0])
    cand = jnp.concatenate(pieces, axis=0)
    rank_c, top = _top_rows(cand, k, break_ties)
    return rank1, rank2, rank_c, v1[0], v2[0], jnp.concatenate(top, axis=0)


def _peer_kernel(x1_ref, wqt_ref, k1_ref, k2_ref, u_ref, vt_ref, g_ref, b_ref, o_ref,
                 xt_scr, s1_scr, s2_scr, e1_scr, r2_scr, e2_scr, y_scr, *, tm, eb, nsub):
    j = pl.program_id(1)
    nk = PEER_N_KEYS
    half = PEER_KEY_DIM // 2
    n_lt = tm // LANES
    k = PEER_TOPK
    pack = BF16_ROWS

    @pl.when(j == 0)
    def _select():
        xt = jnp.transpose(x1_ref[...]).astype(BF16)
        xt_scr[...] = xt
        for h in range(PEER_HEADS):
            qh = jnp.dot(wqt_ref[h * PEER_KEY_DIM:(h + 1) * PEER_KEY_DIM, :], xt, preferred_element_type=F32).astype(BF16)
            s1_scr[h] = jnp.dot(k1_ref[...], qh[:half], preferred_element_type=F32)
            s2_scr[h] = jnp.dot(k2_ref[...], qh[half:], preferred_element_type=F32)

        n_par = 4
        per_head = n_lt // n_par

        def chunk(i, carry):
            h = i // per_head
            lanes = [pl.multiple_of(((i % per_head) * n_par + p) * LANES, LANES) for p in range(n_par)]
            s1s = [s1_scr[h, :, pl.ds(l0, LANES)] for l0 in lanes]
            s2s = [s2_scr[h, :, pl.ds(l0, LANES)] for l0 in lanes]
            fast = [_peer_select(s1, s2, k, break_ties=False) for s1, s2 in zip(s1s, s2s)]
            n_unclean = 0.0
            for f in fast:
                clean = _ranked_exactly(f[0], k) & _ranked_exactly(f[1], k) & _ranked_exactly(f[2], k)
                n_unclean = n_unclean + jnp.sum(jnp.where(clean, 0.0, 1.0))
            picks = lax.cond(
                n_unclean > 0.0,
                lambda: [_peer_select(s1, s2, k, break_ties=True) for s1, s2 in zip(s1s, s2s)],
                lambda: fast)
            for l0, s1, s2, (rank1, rank2, rank_c, v1_max, v2_max, top) in zip(lanes, s1s, s2s, picks):
                den = jnp.sum(jnp.exp(top - top[0:1]), axis=0, keepdims=True)
                picked = jnp.where(rank_c < k, 1.0, 0.0)
                n_of_rank = [jnp.sum(picked[0:16], axis=0, keepdims=True)]
                n_of_rank += [jnp.sum(picked[8 * a + 8:8 * a + 16], axis=0, keepdims=True) for a in range(1, 8)]
                n_of_rank += [picked[72 + a:73 + a] for a in range(8)]
                n1 = jnp.zeros_like(s1)
                for a in range(k):
                    n1 = jnp.where(rank1 == float(a), n_of_rank[a], n1)
                s1_scr[h, :, pl.ds(l0, LANES)] = n1
                e1_scr[h, :, pl.ds(l0, LANES)] = jnp.where(rank1 < k, 0.5 * jnp.exp(s1 - v1_max), 0.0)
                slab = h * n_lt + l0 // LANES
                r2_scr[slab] = rank2.astype(BF16)
                e2_scr[slab] = (jnp.where(rank2 < k, jnp.exp(s2 - v2_max), 0.0) / den).astype(BF16)
            return carry

        lax.fori_loop(0, PEER_HEADS * per_head, chunk, 0)
        y_scr[...] = jnp.zeros_like(y_scr)

    xt = xt_scr[...]
    y_add = None
    rows_per_step = nsub * eb // nk
    for sub in range(nsub):
        a = jnp.dot(u_ref[sub * eb:(sub + 1) * eb, :], xt, preferred_element_type=F32)
        zs = []
        for ib in range(eb // nk):
            row = sub * (eb // nk) + ib
            grp = pl.multiple_of(j * rows_per_step + (row // SUBLANES) * SUBLANES, SUBLANES)
            r_in = row % SUBLANES
            zrow = []
            for lt in range(n_lt):
                ls = slice(lt * LANES, (lt + 1) * LANES)
                w = None
                for h in range(PEER_HEADS):
                    n1r = s1_scr[h, pl.ds(grp, SUBLANES), ls][r_in:r_in + 1]
                    e1r = e1_scr[h, pl.ds(grp, SUBLANES), ls][r_in:r_in + 1]
                    n1b = jnp.broadcast_to(n1r, (pack, LANES)).astype(BF16)[None]
                    e1b = jnp.broadcast_to(e1r, (pack, LANES)).astype(BF16)[None]
                    r2 = r2_scr[h * n_lt + lt].reshape(nk // pack, pack, LANES)
                    e2 = e2_scr[h * n_lt + lt].reshape(nk // pack, pack, LANES)
                    term = jnp.where(r2 < n1b, e2, jnp.zeros_like(e2)) * e1b
                    w = term if w is None else w + term
                x_blk = a[ib * nk:(ib + 1) * nk, ls]
                act = (x_blk * (1.0 + lax.erf(x_blk * np.float32(1.0 / np.sqrt(2.0))))).astype(BF16)
                zrow.append(w.reshape(nk, LANES) * act)
            zs.append(jnp.concatenate(zrow, axis=1))
        z = jnp.concatenate(zs, axis=0)
        y_sub = jnp.dot(vt_ref[:, sub * eb:(sub + 1) * eb], z, preferred_element_type=F32)
        y_add = y_sub if y_add is None else y_add + y_sub
    y_scr[...] += y_add

    @pl.when(j == pl.num_programs(1) - 1)
    def _finish():
        y = jnp.transpose(y_scr[...])
        o_ref[...] = _layer_norm(DEEPNORM_ALPHA * x1_ref[...] + y, g_ref[...], b_ref[...])


def _peer(x1, wqt, k1, k2, u, vt, g, b, tm, eb, nsub):
    t, dm = x1.shape
    n_e = u.shape[0]
    assert (nsub * eb // PEER_N_KEYS) % SUBLANES == 0, "a grid step covers whole sublane groups of first-key rows"
    assert PEER_TOPK == 16 and PEER_N_KEYS == LANES, "the candidate-pair layout in _peer_select is written for k = 16"
    full = lambda a: pl.BlockSpec(a.shape, lambda i, j: (0, 0))
    tab = pltpu.VMEM((PEER_HEADS, PEER_N_KEYS, tm), F32)
    tab_b = pltpu.VMEM((PEER_HEADS * (tm // LANES), PEER_N_KEYS, LANES), BF16)
    return pl.pallas_call(
        functools.partial(_peer_kernel, tm=tm, eb=eb, nsub=nsub),
        grid=(t // tm, n_e // (nsub * eb)),
        in_specs=[
            pl.BlockSpec((tm, dm), lambda i, j: (i, 0)),
            full(wqt), full(k1), full(k2),
            pl.BlockSpec((nsub * eb, dm), lambda i, j: (j, 0)),
            pl.BlockSpec((dm, nsub * eb), lambda i, j: (0, j)),
            full(g), full(b),
        ],
        out_specs=pl.BlockSpec((tm, dm), lambda i, j: (i, 0)),
        out_shape=jax.ShapeDtypeStruct((t, dm), F32),
        scratch_shapes=[
            pltpu.VMEM((dm, tm), BF16), tab, tab, tab, tab_b, tab_b,
            pltpu.VMEM((dm, tm), F32),
        ],
        compiler_params=_cparams(("parallel", "arbitrary"), TILE["peer"]["vmem"]),
        name="peer",
    )(x1, wqt, k1, k2, u, vt, g, b)


def _layer(x, mem, positions, w_in, pe_k, pe_v, w1k, w2k, w1v, w2v, w_mem_kv, w_out, ln1_g, ln1_b,
           w_query, sk1, sk2, pu, pv, ln2_g, ln2_b):
    bsz, seq, dm = x.shape
    m_len = mem.shape[1]
    t = bsz * seq
    d = HEAD_DIM
    g_n = NSA_KV_HEADS
    nsa_t, dsa_t, peer_t = TILE["nsa"], TILE["dsa"], TILE["peer"]
    assert seq % nsa_t["tk"] == 0 and seq % dsa_t["tk"] == 0 and seq >= WINDOW + nsa_t["c"]
    assert seq % TILE["out_ln"]["tm"] == 0, "an out_ln token tile stays inside one batch row (one memory block)"
    assert dsa_t["tk"] >= min(DSA_TOPK_MAX, seq // 4), "a key tile holds at least top-k candidates"
    assert t % peer_t["tm"] == 0 and t % TILE["proj"]["tm"] == 0 and t % TILE["out_ln"]["tm"] == 0
    x2 = x.reshape(t, dm)

    hb, misc, kv32 = _proj(x2, _regroup_w_in(w_in), _rope_inputs(positions), tm=TILE["proj"]["tm"])

    zpad = jnp.zeros((CMP_HIDDEN, d), BF16)
    pe_rows = lambda pe: jnp.pad(pe.reshape(1, CMP_LEN * d), ((0, 7), (0, 0))).astype(BF16)
    w1k_b, w1v_b = w1k.astype(BF16), w1v.astype(BF16)
    kcvc = _compress(kv32, _compress_w1cat(w1k_b, w1v_b), w1k_b, w1v_b,
                     jnp.concatenate([w2k.astype(BF16), zpad], axis=1), jnp.concatenate([zpad, w2v.astype(BF16)], axis=1),
                     pe_rows(pe_k), pe_rows(pe_v), bsz, seq)

    ovl_t, expand = _nsa_consts(seq)
    o_nsa = _nsa(hb, misc, kcvc, ovl_t, expand, bsz, seq, c=nsa_t["c"], tk=nsa_t["tk"])
    o_dsa = _dsa(hb, misc, bsz, seq, c=dsa_t["c"], tk=dsa_t["tk"])

    wm = w_mem_kv.reshape(dm, 2, MEM_HEADS, d).transpose(0, 2, 1, 3).reshape(dm, MEM_HEADS * 2 * d).astype(BF16)
    mkv = _matmul(mem.reshape(bsz * m_len, dm), wm, tm=m_len, out_dtype=BF16)
    x1 = _out_ln(o_nsa, o_dsa, hb, mkv, x2, w_out.astype(BF16), ln1_g.reshape(1, dm), ln1_b.reshape(1, dm),
                 seq, m_len, tm=TILE["out_ln"]["tm"])

    x2o = _peer(x1, w_query.T.astype(BF16), sk1.astype(BF16), sk2.astype(BF16), pu.astype(BF16), pv.T.astype(BF16),
                ln2_g.reshape(1, dm), ln2_b.reshape(1, dm), tm=peer_t["tm"], eb=peer_t["eb"], nsub=peer_t["nsub"])
    return x2o.reshape(bsz, seq, dm)


def kernel(x, mem, positions, w_in, nsa_pe_k, nsa_pe_v, nsa_cmp_w1_k, nsa_cmp_w2_k, nsa_cmp_w1_v, nsa_cmp_w2_v,
           w_mem_kv, w_out, ln1_g, ln1_b, peer_w_query, peer_sub_keys_1, peer_sub_keys_2, peer_u, peer_v, ln2_g, ln2_b):
    assert w_in.shape[0] == DEPTH
    return _layer(x, mem, positions, w_in[0], nsa_pe_k[0], nsa_pe_v[0], nsa_cmp_w1_k[0], nsa_cmp_w2_k[0],
                  nsa_cmp_w1_v[0], nsa_cmp_w2_v[0], w_mem_kv[0], w_out[0], ln1_g[0], ln1_b[0], peer_w_query[0],
                  peer_sub_keys_1[0], peer_sub_keys_2[0], peer_u[0], peer_v[0], ln2_g[0], ln2_b[0])
```

```python
import functools

import numpy as np
import jax
import jax.numpy as jnp
from jax import lax
from jax.experimental import pallas as pl
from jax.experimental.pallas import tpu as pltpu

F32 = jnp.float32
BF16 = jnp.bfloat16

LANES = 128
SUBLANES = 8
BF16_ROWS = 16
MIB = 1 << 20

TILE = dict(
    proj=dict(tm=512, vmem=40 * MIB),
    compress=dict(vmem=16 * MIB),
    nsa=dict(c=512, tk=1024, vmem=56 * MIB),
    dsa=dict(c=256, tk=512, vmem=32 * MIB),
    mem=dict(vmem=16 * MIB),
    out_ln=dict(tm=512, vmem=24 * MIB),
    peer=dict(tm=512, eb=1024, nsub=2, vmem=56 * MIB),
)

HEAD_DIM = 64
ROPE_THETA = 500000.0
LN_EPS = 1e-5
NSA_HEADS = 8
NSA_KV_HEADS = 2
CMP_LEN = 32
CMP_STRIDE = 16
CMP_HIDDEN = 128
SEL_BLOCK = 64
SEL_COUNT = 16
WINDOW = 512
FORCE_BONUS = 1e4
DSA_HEADS = 4
IDX_HEADS = 8
IDX_DIM = 32
DSA_TOPK_MAX = 256
MEM_HEADS = 4
PEER_HEADS = 8
PEER_N_KEYS = 128
PEER_KEY_DIM = 256
PEER_TOPK = 16
DEPTH = 1
DEEPNORM_ALPHA = (2.0 * DEPTH) ** 0.25

NEG = -1e30

BLK_QN = 0
BLK_NKV = 4
BLK_DQ = 10
BLK_IQ = 12
BLK_MQ = 14
BLK_DKV = 16
BLK_IK = 17
BLK_MISC = 18
N_BLK = 19
MISC_W = 0
MISC_G = IDX_HEADS


def _cparams(sem, vmem):
    return pltpu.CompilerParams(dimension_semantics=sem, vmem_limit_bytes=vmem)


def _regroup_w_in(w_in):
    d = HEAD_DIM
    o_q = 0
    o_kv = o_q + NSA_HEADS * d
    o_g = o_kv + 6 * NSA_KV_HEADS * d
    o_dq = o_g + 3 * NSA_HEADS
    o_dkv = o_dq + DSA_HEADS * d
    o_iq = o_dkv + 2 * d
    o_ik = o_iq + IDX_HEADS * IDX_DIM
    o_iw = o_ik + IDX_DIM
    o_mq = o_iw + IDX_HEADS
    cols = [w_in[:, o_q:o_kv]]
    for br in range(3):
        for g in range(NSA_KV_HEADS):
            k0 = o_kv + ((2 * br) * NSA_KV_HEADS + g) * d
            v0 = o_kv + ((2 * br + 1) * NSA_KV_HEADS + g) * d
            cols += [w_in[:, k0:k0 + d], w_in[:, v0:v0 + d]]
    cols.append(w_in[:, o_dq:o_dkv])
    cols.append(w_in[:, o_iq:o_ik])
    cols.append(w_in[:, o_mq:o_mq + MEM_HEADS * d])
    cols.append(w_in[:, o_dkv:o_iq])
    cols += [w_in[:, o_ik:o_iw]] * (LANES // IDX_DIM)
    cols += [w_in[:, o_iw:o_mq], w_in[:, o_g:o_dq]]
    pad = LANES - IDX_HEADS - 3 * NSA_HEADS
    cols.append(jnp.zeros((w_in.shape[0], pad), w_in.dtype))
    w = jnp.concatenate(cols, axis=1)
    assert w.shape[1] == N_BLK * LANES
    return w.astype(BF16)


def _rot_half(head_dim):
    return head_dim // 8


def _rope_inputs(positions):
    pos = positions.reshape(-1).astype(F32)
    vals, place, ones = [], [], []
    src = 0
    for hd in (HEAD_DIM, IDX_DIM):
        half = _rot_half(hd)
        freqs = jnp.power(ROPE_THETA, -jnp.arange(half, dtype=F32) * 2.0 / (2 * half))
        ang = pos[:, None] * freqs
        vals += [jnp.cos(ang), jnp.sin(ang)]
        d = np.arange(LANES) % hd
        lanes = np.arange(LANES)
        sel = np.zeros((3, LANES, LANES), np.float32)
        rot, first, second = d < 2 * half, d < half, (d >= half) & (d < 2 * half)
        sel[0, src + d[rot] % half, lanes[rot]] = 1.0
        sel[1, src + half + d[first], lanes[first]] = -1.0
        sel[2, src + half + d[second] - half, lanes[second]] = 1.0
        place += [sel[0], sel[1], sel[2]]
        ones += [(~rot).astype(np.float32), np.zeros(LANES, np.float32), np.zeros(LANES, np.float32)]
        src += 2 * half
    cs = jnp.concatenate(vals, axis=1)
    cs = jnp.pad(cs, ((0, 0), (0, LANES - cs.shape[1])))
    return cs, jnp.asarray(np.concatenate(place, axis=1), BF16), jnp.asarray(np.concatenate(ones)[None, :], F32)


def _proj_kernel(x_ref, w_ref, cs_ref, place_ref, ones_ref, hb_ref, misc_ref, cmp_ref):
    xb = x_ref[...].astype(BF16)
    tm = xb.shape[0]
    lane = lax.broadcasted_iota(jnp.int32, (tm, LANES), 1)
    lo = lane < HEAD_DIM
    cs = cs_ref[...]
    hi = cs.astype(BF16)
    rest = cs - hi.astype(F32)
    mid = rest.astype(BF16)
    low = (rest - mid.astype(F32)).astype(BF16)
    tabs = ones_ref[...]
    for piece in (hi, mid, low):
        tabs = tabs + jnp.dot(piece, place_ref[...], preferred_element_type=F32)
    c64, sa64, sb64 = (tabs[:, i * LANES:(i + 1) * LANES] for i in range(3))
    c32, sa32, sb32 = (tabs[:, i * LANES:(i + 1) * LANES] for i in range(3, 6))
    c64h, sa64h, sb64h = jnp.where(lo, c64, 1.0), jnp.where(lo, sa64, 0.0), jnp.where(lo, sb64, 0.0)

    def rope(h, c, sa, sb, half):
        return h * c + pltpu.roll(h, LANES - half, 1) * sa + pltpu.roll(h, half, 1) * sb

    def finish(blk, h):
        if blk < BLK_NKV or BLK_DQ <= blk < BLK_IQ:
            return rope(h, c64, sa64, sb64, _rot_half(HEAD_DIM))
        if BLK_NKV <= blk < BLK_DQ or blk == BLK_DKV:
            return rope(h, c64h, sa64h, sb64h, _rot_half(HEAD_DIM))
        if BLK_IQ <= blk < BLK_MQ or blk == BLK_IK:
            return rope(h, c32, sa32, sb32, _rot_half(IDX_DIM))
        return h

    for j in range(BLK_MISC // 2):
        h = jnp.dot(xb, w_ref[:, j * 2 * LANES:(j + 1) * 2 * LANES], preferred_element_type=F32)
        for s in range(2):
            blk = 2 * j + s
            roped = finish(blk, h[:, s * LANES:(s + 1) * LANES])
            hb_ref[:, blk * LANES:(blk + 1) * LANES] = roped.astype(BF16)
            if BLK_NKV <= blk < BLK_NKV + NSA_KV_HEADS:
                cmp_ref[:, (blk - BLK_NKV) * LANES:(blk - BLK_NKV + 1) * LANES] = roped
    misc_ref[...] = jnp.dot(xb, w_ref[:, BLK_MISC * LANES:], preferred_element_type=F32)


def _proj(x2, w2, rope, tm):
    t, dm = x2.shape
    cs, place, ones = rope
    return pl.pallas_call(
        _proj_kernel,
        grid=(t // tm,),
        in_specs=[
            pl.BlockSpec((tm, dm), lambda i: (i, 0)),
            pl.BlockSpec((dm, N_BLK * LANES), lambda i: (0, 0)),
            pl.BlockSpec((tm, LANES), lambda i: (i, 0)),
            pl.BlockSpec(place.shape, lambda i: (0, 0)),
            pl.BlockSpec(ones.shape, lambda i: (0, 0)),
        ],
        out_specs=[
            pl.BlockSpec((tm, BLK_MISC * LANES), lambda i: (i, 0)),
            pl.BlockSpec((tm, LANES), lambda i: (i, 0)),
            pl.BlockSpec((tm, NSA_KV_HEADS * LANES), lambda i: (i, 0)),
        ],
        out_shape=[
            jax.ShapeDtypeStruct((t, BLK_MISC * LANES), BF16),
            jax.ShapeDtypeStruct((t, LANES), F32),
            jax.ShapeDtypeStruct((t, NSA_KV_HEADS * LANES), F32),
        ],
        compiler_params=_cparams(("parallel",), TILE["proj"]["vmem"]),
        name="proj",
    )(x2, w2, cs, place, ones)


def _gelu(x):
    return 0.5 * x * (1.0 + lax.erf(x * np.float32(1.0 / np.sqrt(2.0))))


def _compress_kernel(kv_ref, w1cat_ref, w1k_ref, w1v_ref, w2k_ref, w2v_ref, pek_ref, pev_ref, out_ref):
    rows = out_ref.shape[1]
    hid = w1k_ref.shape[1]
    acc = jnp.zeros((rows, 4 * hid), F32)
    for l in range(CMP_STRIDE):
        tok = kv_ref[pl.ds(l, rows, stride=CMP_STRIDE), :].astype(BF16)
        acc = acc + jnp.dot(tok, w1cat_ref[l], preferred_element_type=F32)

    def hidden(a, b, w1_ref, pe_ref):
        bias = jnp.dot(pe_ref[...], w1_ref[...], preferred_element_type=F32)[0:1, :]
        return _gelu(a + pltpu.roll(b, rows - 1, 0) + bias).astype(BF16)

    hk = hidden(acc[:, 0:hid], acc[:, hid:2 * hid], w1k_ref, pek_ref)
    hv = hidden(acc[:, 2 * hid:3 * hid], acc[:, 3 * hid:4 * hid], w1v_ref, pev_ref)
    out = jnp.dot(hk, w2k_ref[...], preferred_element_type=F32) + jnp.dot(hv, w2v_ref[...], preferred_element_type=F32)
    out_ref[0] = out.astype(BF16)


def _compress_w1cat(w1k, w1v):
    d = HEAD_DIM
    k3 = w1k.reshape(2, CMP_STRIDE, d, CMP_HIDDEN)
    v3 = w1v.reshape(2, CMP_STRIDE, d, CMP_HIDDEN)
    z = jnp.zeros_like(k3[0])
    top = jnp.concatenate([k3[0], k3[1], z, z], axis=2)
    bot = jnp.concatenate([z, z, v3[0], v3[1]], axis=2)
    return jnp.concatenate([top, bot], axis=1)


def _compress(kv32, w1cat, w1k, w1v, w2k, w2v, pek, pev, bsz, seq):
    g_n = NSA_KV_HEADS
    rows = seq // CMP_STRIDE
    full = lambda a: pl.BlockSpec(a.shape, lambda i: (0,) * a.ndim)
    return pl.pallas_call(
        _compress_kernel,
        grid=(bsz * g_n,),
        in_specs=[
            pl.BlockSpec((seq, LANES), lambda i: (i // g_n, i % g_n)),
            full(w1cat), full(w1k), full(w1v), full(w2k), full(w2v), full(pek), full(pev),
        ],
        out_specs=pl.BlockSpec((1, rows, LANES), lambda i: (i, 0, 0)),
        out_shape=jax.ShapeDtypeStruct((bsz * g_n, rows, LANES), BF16),
        compiler_params=_cparams(("parallel",), TILE["compress"]["vmem"]),
        name="compress",
    )(kv32, w1cat, w1k, w1v, w2k, w2v, pek, pev)


def _stack_heads(qpair_refs_or_vals, scale):
    outs = []
    for blk in qpair_refs_or_vals:
        b = blk.astype(F32) * scale
        lane = lax.broadcasted_iota(jnp.int32, b.shape, 1)
        lo = lane < HEAD_DIM
        outs.append(jnp.where(lo, b, 0.0))
        outs.append(jnp.where(lo, pltpu.roll(b, HEAD_DIM, 1), 0.0))
    return jnp.concatenate(outs, axis=0).astype(BF16)


def _qk(q, kv):
    return lax.dot_general(q, kv, (((1,), (1,)), ((), ())), preferred_element_type=F32)


def _lane_pick(x, col):
    lane = lax.broadcasted_iota(jnp.int32, x.shape, 1)
    return jnp.sum(jnp.where(lane == col, x, 0.0), axis=1, keepdims=True)


def _flash_init(n_heads, c):
    return jnp.full((n_heads, c, 1), NEG, F32), jnp.zeros((n_heads, c, LANES), F32)


def _flash_step(s, valid, kv, m, acc):
    n_heads, c, tk = s.shape
    sb = s.astype(BF16) + jnp.where(valid, 0.0, NEG).astype(BF16)
    m_new = jnp.maximum(m, jnp.max(sb, axis=-1, keepdims=True).astype(F32))
    p = jnp.exp(sb - m_new.astype(BF16))
    lane = lax.broadcasted_iota(jnp.int32, kv.shape, 1)
    ones_v = jnp.where(lane < HEAD_DIM, jnp.ones_like(kv), kv)
    pv = jnp.dot(p.reshape(n_heads * c, tk), ones_v, preferred_element_type=F32)
    return m_new, jnp.exp(m - m_new) * acc + pv.reshape(n_heads, c, LANES)


def _normalize(acc):
    den = acc[..., 0:1]
    return acc / jnp.where(den > 0.0, den, 1.0)


def _pack_heads(o, n_heads, c):
    lane = lax.broadcasted_iota(jnp.int32, (c, LANES), 1)
    lo = lane < HEAD_DIM
    blocks = []
    for p in range(n_heads // 2):
        a = o[(2 * p) * c:(2 * p + 1) * c]
        b = o[(2 * p + 1) * c:(2 * p + 2) * c]
        blocks.append(jnp.where(lo, pltpu.roll(a, HEAD_DIM, 1), b))
    return blocks


def _nsa_kernel(q_ref, misc_ref, kc_ref, ks_ref, kw_ref, ovl_ref, exp_ref, o_ref, *, c, tk, seq):
    g = pl.program_id(1)
    ci = pl.program_id(2)
    t0 = ci * c
    hpg = NSA_HEADS // NSA_KV_HEADS
    rows = hpg * c
    scale = HEAD_DIM ** -0.5
    q4 = _stack_heads([q_ref[:, :LANES], q_ref[:, LANES:]], scale)
    trow = t0 + lax.broadcasted_iota(jnp.int32, (c, 1), 0)

    kc = kc_ref[0]
    ncp = kc.shape[0]
    s = _qk(q4, kc).reshape(hpg, c, ncp)
    ncol = lax.broadcasted_iota(jnp.int32, (c, ncp), 1)
    cvalid = (ncol * CMP_STRIDE + (CMP_LEN - 1)) <= trow
    s = jnp.where(cvalid, s, NEG)
    m = jnp.max(s, axis=-1, keepdims=True)
    e = jnp.where(cvalid, jnp.exp(s - m), 0.0)
    den = jnp.sum(e, axis=-1, keepdims=True)
    p = (e / jnp.where(den > 0.0, den, 1.0)).reshape(rows, ncp)
    pb = p.astype(BF16)
    o_cmp = jnp.dot(pb, kc, preferred_element_type=F32)

    imp_t = lax.dot_general(ovl_ref[...], pb, (((1,), (1,)), ((), ())), preferred_element_type=F32)
    nbp = imp_t.shape[0]
    nb = seq // SEL_BLOCK
    imp = imp_t[:nb, 0:c]
    for h in range(1, hpg):
        imp = imp + imp_t[:nb, h * c:(h + 1) * c]
    jrow = lax.broadcasted_iota(jnp.int32, (nb, c), 0)
    tcol = t0 + lax.broadcasted_iota(jnp.int32, (nb, c), 1)
    cur = tcol // SEL_BLOCK
    forced = (jrow == 0) | (jrow == cur) | (jrow == cur - 1)
    visible = (jrow * SEL_BLOCK) <= tcol
    score = jnp.where(visible, imp + jnp.where(forced, FORCE_BONUS, 0.0), -jnp.inf)
    n_pick = min(SEL_COUNT, nb)
    rank, _ = _top_rows(score, n_pick, break_ties=True)
    sel = jnp.where(rank < n_pick, 1.0, 0.0)
    if nbp > nb:
        sel = jnp.concatenate([sel, jnp.zeros((nbp - nb, c), F32)], axis=0)
    sel_b = jnp.transpose(sel).astype(BF16)

    n_tiles = (t0 + c + tk - 1) // tk
    kcol = lax.broadcasted_iota(jnp.int32, (c, tk), 1)

    def sel_body(kt, carry):
        k0 = pl.multiple_of(kt * tk, tk)
        kv = ks_ref[pl.ds(k0, tk), :]
        s_ = _qk(q4, kv).reshape(hpg, c, tk)
        tok = jnp.dot(sel_b, exp_ref[:, pl.ds(k0, tk)], preferred_element_type=F32)
        valid = (tok > 0.5) & ((k0 + kcol) <= trow)
        return _flash_step(s_, valid, kv, *carry)

    init = _flash_init(hpg, c)
    _, acc_s = lax.fori_loop(0, n_tiles, sel_body, init)
    o_sel = _normalize(acc_s).reshape(rows, LANES)

    wk = WINDOW + c
    w0 = pl.multiple_of(jnp.maximum(t0 - WINDOW, 0), c)
    kvw = kw_ref[pl.ds(w0, wk), :]
    s = _qk(q4, kvw).reshape(hpg, c, wk)
    wpos = w0 + lax.broadcasted_iota(jnp.int32, (c, wk), 1)
    wvalid = (wpos <= trow) & (wpos > trow - WINDOW)
    _, acc_w = _flash_step(s, wvalid, kvw, *init)
    o_win = _normalize(acc_w).reshape(rows, LANES)

    gates = jax.nn.sigmoid(misc_ref[...])
    outs = []
    for h in range(hpg):
        col = MISC_G + (g * hpg + h) * 3
        sl = slice(h * c, (h + 1) * c)
        outs.append(_lane_pick(gates, col) * o_cmp[sl] + _lane_pick(gates, col + 1) * o_sel[sl]
                    + _lane_pick(gates, col + 2) * o_win[sl])
    blocks = _pack_heads(jnp.concatenate(outs, axis=0), hpg, c)
    for i, blk in enumerate(blocks):
        o_ref[:, i * LANES:(i + 1) * LANES] = blk.astype(BF16)


def _nsa(hb, misc, kcvc, ovl_t, expand, bsz, seq, c, tk):
    t = hb.shape[0]
    g_n = NSA_KV_HEADS
    nch = seq // c
    ncp = kcvc.shape[1]
    nbp = ovl_t.shape[0]
    kern = functools.partial(_nsa_kernel, c=c, tk=tk, seq=seq)
    return pl.pallas_call(
        kern,
        grid=(bsz, g_n, nch),
        in_specs=[
            pl.BlockSpec((c, 2 * LANES), lambda b, g, i: (b * nch + i, g)),
            pl.BlockSpec((c, LANES), lambda b, g, i: (b * nch + i, 0)),
            pl.BlockSpec((1, ncp, LANES), lambda b, g, i: (b * g_n + g, 0, 0)),
            pl.BlockSpec((seq, LANES), lambda b, g, i: (b, BLK_NKV + 2 + g)),
            pl.BlockSpec((seq, LANES), lambda b, g, i: (b, BLK_NKV + 4 + g)),
            pl.BlockSpec((nbp, ncp), lambda b, g, i: (0, 0)),
            pl.BlockSpec((nbp, seq), lambda b, g, i: (0, 0)),
        ],
        out_specs=pl.BlockSpec((c, 2 * LANES), lambda b, g, i: (b * nch + i, g)),
        out_shape=jax.ShapeDtypeStruct((t, NSA_HEADS * HEAD_DIM), BF16),
        compiler_params=_cparams(("parallel", "parallel", "arbitrary"), TILE["nsa"]["vmem"]),
        name="nsa",
    )(hb, misc, kcvc, hb, hb, ovl_t, expand)


def _nsa_consts(seq):
    nb = seq // SEL_BLOCK
    nbp = max(LANES, nb)
    rows = seq // CMP_STRIDE
    nc = (seq - CMP_LEN) // CMP_STRIDE + 1
    j = np.arange(nbp)[:, None]
    n = np.arange(rows)[None, :]
    ovl = np.clip(np.minimum(n * CMP_STRIDE + CMP_LEN, j * SEL_BLOCK + SEL_BLOCK)
                  - np.maximum(n * CMP_STRIDE, j * SEL_BLOCK), 0, None).astype(np.float32) / CMP_LEN
    ovl = np.where((n < nc) & (j < nb), ovl, 0.0)
    s = np.arange(seq)[None, :]
    expand = (s // SEL_BLOCK == j).astype(np.float32)
    return jnp.asarray(ovl, BF16), jnp.asarray(expand, BF16)


INT_MIN = -2147483648


def _dsa_kernel(q_ref, iq_ref, misc_ref, ik_ref, kv_ref, o_ref, key_scr, hi_scr, lo_scr, *, c, tk, seq, topk):
    ci = pl.program_id(1)
    t0 = ci * c
    n_tiles = (t0 + c + tk - 1) // tk
    tq = t0 + lax.broadcasted_iota(jnp.int32, (1, c), 1)
    krow = lax.broadcasted_iota(jnp.int32, (tk, c), 0)

    lane = lax.broadcasted_iota(jnp.int32, (c, LANES), 1)
    per_blk = LANES // IDX_DIM
    qs = []
    for h in range(IDX_HEADS):
        blk = iq_ref[:, (h // per_blk) * LANES:(h // per_blk + 1) * LANES]
        qs.append(jnp.where(lane // IDX_DIM == h % per_blk, blk, jnp.zeros_like(blk)))
    qst = jnp.concatenate(qs, axis=0)
    w_t = jnp.transpose(misc_ref[...] * (IDX_HEADS ** -0.5 * IDX_DIM ** -0.5))

    def score_body(kt, carry):
        k0 = pl.multiple_of(kt * tk, tk)
        lg = _qk(ik_ref[pl.ds(k0, tk), :], qst)
        sc = jnp.zeros((tk, c), F32)
        for h in range(IDX_HEADS):
            sc = sc + jnp.maximum(lg[:, h * c:(h + 1) * c], 0.0) * w_t[MISC_W + h:MISC_W + h + 1, :]
        sc = sc + 0.0
        sc = jnp.where((k0 + krow) <= tq, sc, -jnp.inf)
        bits = pltpu.bitcast(sc, jnp.int32)
        key = jnp.where(bits < 0, bits ^ jnp.int32(0x7FFFFFFF), bits)
        key_scr[pl.ds(k0, tk), :] = key
        hi_scr[pl.ds(k0, tk), :] = (key >> 16).astype(jnp.int16)
        lo_scr[pl.ds(k0, tk), :] = ((key & jnp.int32(0xFFFF)) - 32768).astype(jnp.int16)
        return carry

    lax.fori_loop(0, n_tiles, score_body, 0)

    n_acc = 8

    def count(pred):
        def body(kt, acc):
            k0 = pl.multiple_of(kt * tk, tk)
            hit = jnp.where(pred(key_scr[pl.ds(k0, tk), :], k0 + krow), 1.0, 0.0)
            return acc + jnp.sum(hit.reshape(tk // (8 * n_acc), n_acc * 8, c), axis=0)
        acc = lax.fori_loop(0, n_tiles, body, jnp.zeros((n_acc * 8, c), F32))
        return jnp.sum(acc, axis=0, keepdims=True)

    n_acc_h = 4
    assert seq // (BF16_ROWS * n_acc_h) < 2 ** 15

    def half_rows(x):
        return jnp.broadcast_to(x, (BF16_ROWS, c)).astype(jnp.int16)[None]

    def tiles16(ref, kt):
        return ref[pl.ds(pl.multiple_of(kt * tk, tk), tk), :].reshape(tk // BF16_ROWS, BF16_ROWS, c)

    def count16(ref, pred):
        def body(kt, acc):
            x = tiles16(ref, kt)
            hit = jnp.where(pred(x), jnp.ones_like(x), jnp.zeros_like(x))
            parts = hit.reshape(tk // (BF16_ROWS * n_acc_h), n_acc_h, BF16_ROWS, c)
            for g in range(parts.shape[0]):
                acc = acc + parts[g]
            return acc
        acc = lax.fori_loop(0, n_tiles, body, jnp.zeros((n_acc_h, BF16_ROWS, c), jnp.int16))
        return jnp.sum(acc.astype(F32).reshape(n_acc_h * BF16_ROWS, c), axis=0, keepdims=True)

    def hi_body(i, carry):
        ans, cnt_rej = carry
        cand = ans | (jnp.int32(1) << (31 - i))
        cand_b = half_rows((cand ^ jnp.int32(INT_MIN)) >> 16)
        cnt = count16(hi_scr, lambda hi: hi >= cand_b)
        ok = cnt >= topk
        return jnp.where(ok, cand, ans), jnp.where(ok, cnt_rej, cnt)

    thr_hi, above = lax.fori_loop(0, 16, hi_body, (jnp.zeros((1, c), jnp.int32), jnp.zeros((1, c), F32)))
    hi_b = half_rows((thr_hi ^ jnp.int32(INT_MIN)) >> 16)

    def keep_equal(kt, carry):
        k0 = pl.multiple_of(kt * tk, tk)
        lo = jnp.where(tiles16(hi_scr, kt) == hi_b, tiles16(lo_scr, kt), jnp.int16(-32768))
        lo_scr[pl.ds(k0, tk), :] = lo.reshape(tk, c)
        return carry

    lax.fori_loop(0, n_tiles, keep_equal, 0)

    def lo_body(i, carry):
        ans, cnt_ans, cnt_rej = carry
        cand = ans | (jnp.int32(1) << (15 - i))
        cand_b = half_rows(cand - 32768)
        cnt = above + count16(lo_scr, lambda lo: lo >= cand_b)
        ok = cnt >= topk
        return jnp.where(ok, cand, ans), jnp.where(ok, cnt, cnt_ans), jnp.where(ok, cnt_rej, cnt)

    thr_lo, cnt_ge, cnt_gt = lax.fori_loop(
        0, 16, lo_body, (jnp.zeros((1, c), jnp.int32), jnp.full((1, c), float(seq + topk), F32), above))
    thr = (thr_hi | thr_lo) ^ jnp.int32(INT_MIN)
    need = topk - cnt_gt
    n_ties = cnt_ge - cnt_gt

    n_bits = max(1, int(np.ceil(np.log2(seq))))

    def tie_search():
        def tie_body(i, ans):
            cand = ans | (jnp.int32(1) << (n_bits - 1 - i))
            cnt = count(lambda keys, idx: (keys == thr) & (idx < cand))
            return jnp.where(cnt < need, cand, ans)
        return lax.fori_loop(0, n_bits, tie_body, jnp.zeros((1, c), jnp.int32))

    surplus = jnp.max(n_ties - need)
    last_tie = lax.cond(surplus > 0.0, tie_search, lambda: jnp.full((1, c), seq, jnp.int32))

    q4 = _stack_heads([q_ref[:, :LANES], q_ref[:, LANES:]], HEAD_DIM ** -0.5)
    eye = (lax.broadcasted_iota(jnp.int32, (c, c), 0) == lax.broadcasted_iota(jnp.int32, (c, c), 1)).astype(BF16)

    def att_body(kt, carry):
        k0 = pl.multiple_of(kt * tk, tk)
        kv = kv_ref[pl.ds(k0, tk), :]
        keys = key_scr[pl.ds(k0, tk), :]
        idx = k0 + krow
        picked = (idx <= tq) & ((keys > thr) | ((keys == thr) & (idx <= last_tie)))
        valid = _qk(eye, jnp.where(picked, 1.0, 0.0).astype(BF16)) > 0.5
        s = _qk(q4, kv).reshape(DSA_HEADS, c, tk)
        return _flash_step(s, valid, kv, *carry)

    _, acc = lax.fori_loop(0, n_tiles, att_body, _flash_init(DSA_HEADS, c))
    o = _normalize(acc).reshape(DSA_HEADS * c, LANES)
    for i, blk in enumerate(_pack_heads(o, DSA_HEADS, c)):
        o_ref[:, i * LANES:(i + 1) * LANES] = blk.astype(BF16)


def _dsa(hb, misc, bsz, seq, c, tk):
    t = hb.shape[0]
    nch = seq // c
    topk = min(DSA_TOPK_MAX, seq // 4)
    kern = functools.partial(_dsa_kernel, c=c, tk=tk, seq=seq, topk=topk)
    return pl.pallas_call(
        kern,
        grid=(bsz, nch),
        in_specs=[
            pl.BlockSpec((c, 2 * LANES), lambda b, i: (b * nch + i, BLK_DQ // 2)),
            pl.BlockSpec((c, 2 * LANES), lambda b, i: (b * nch + i, BLK_IQ // 2)),
            pl.BlockSpec((c, LANES), lambda b, i: (b * nch + i, 0)),
            pl.BlockSpec((seq, LANES), lambda b, i: (b, BLK_IK)),
            pl.BlockSpec((seq, LANES), lambda b, i: (b, BLK_DKV)),
        ],
        out_specs=pl.BlockSpec((c, 2 * LANES), lambda b, i: (b * nch + i, 0)),
        out_shape=jax.ShapeDtypeStruct((t, DSA_HEADS * HEAD_DIM), BF16),
        scratch_shapes=[pltpu.VMEM((seq, c), jnp.int32), pltpu.VMEM((seq, c), jnp.int16),
                        pltpu.VMEM((seq, c), jnp.int16)],
        compiler_params=_cparams(("parallel", "arbitrary"), TILE["dsa"]["vmem"]),
        name="dsa",
    )(hb, hb, misc, hb, hb)


def _matmul_kernel(a_ref, w_ref, o_ref):
    o_ref[...] = jnp.dot(a_ref[...].astype(BF16), w_ref[...], preferred_element_type=F32).astype(o_ref.dtype)


def _matmul(a, w, tm, out_dtype):
    m, k = a.shape
    n = w.shape[1]
    return pl.pallas_call(
        _matmul_kernel,
        grid=(m // tm,),
        in_specs=[pl.BlockSpec((tm, k), lambda i: (i, 0)), pl.BlockSpec((k, n), lambda i: (0, 0))],
        out_specs=pl.BlockSpec((tm, n), lambda i: (i, 0)),
        out_shape=jax.ShapeDtypeStruct((m, n), out_dtype),
        compiler_params=_cparams(("parallel",), TILE["mem"]["vmem"]),
        name="matmul",
    )(a, w)


def _mem_attention(q_ref, mkv_ref, c):
    q4 = _stack_heads([q_ref[:, :LANES], q_ref[:, LANES:]], HEAD_DIM ** -0.5)
    outs = []
    for h in range(MEM_HEADS):
        kv = mkv_ref[:, h * LANES:(h + 1) * LANES]
        s = _qk(q4[h * c:(h + 1) * c], kv)
        e = jnp.exp(s - jnp.max(s, axis=-1, keepdims=True))
        p = e / jnp.sum(e, axis=-1, keepdims=True)
        outs.append(jnp.dot(p.astype(BF16), kv, preferred_element_type=F32))
    return _pack_heads(jnp.concatenate(outs, axis=0), MEM_HEADS, c)


def _layer_norm(v, g, b):
    mu = jnp.mean(v, axis=-1, keepdims=True)
    d = v - mu
    var = jnp.mean(d * d, axis=-1, keepdims=True)
    return d * lax.rsqrt(var + LN_EPS) * g + b


def _out_ln_kernel(on_ref, od_ref, mq_ref, mkv_ref, x_ref, w_ref, g_ref, b_ref, o_ref):
    n0 = on_ref.shape[1]
    n1 = n0 + od_ref.shape[1]
    o_mem = jnp.concatenate(_mem_attention(mq_ref, mkv_ref, x_ref.shape[0]), axis=1).astype(BF16)
    mix = jnp.dot(on_ref[...], w_ref[:n0, :], preferred_element_type=F32)
    mix = mix + jnp.dot(od_ref[...], w_ref[n0:n1, :], preferred_element_type=F32)
    mix = mix + jnp.dot(o_mem, w_ref[n1:, :], preferred_element_type=F32)
    o_ref[...] = _layer_norm(DEEPNORM_ALPHA * x_ref[...] + mix, g_ref[...], b_ref[...])


def _out_ln(o_nsa, o_dsa, hb, mkv, x2, w_out, g, b, seq, m_len, tm):
    t, dm = x2.shape
    per_batch = seq // tm
    row = lambda a: pl.BlockSpec((tm, a.shape[1]), lambda i: (i, 0))
    full = lambda a: pl.BlockSpec(a.shape, lambda i: (0, 0))
    return pl.pallas_call(
        _out_ln_kernel,
        grid=(t // tm,),
        in_specs=[
            row(o_nsa), row(o_dsa),
            pl.BlockSpec((tm, 2 * LANES), lambda i: (i, BLK_MQ // 2)),
            pl.BlockSpec((m_len, MEM_HEADS * LANES), lambda i: (i // per_batch, 0)),
            row(x2), full(w_out), full(g), full(b),
        ],
        out_specs=pl.BlockSpec((tm, dm), lambda i: (i, 0)),
        out_shape=jax.ShapeDtypeStruct((t, dm), F32),
        compiler_params=_cparams(("parallel",), TILE["out_ln"]["vmem"]),
        name="out_ln",
    )(o_nsa, o_dsa, hb, mkv, x2, w_out, g, b)


NOT_PICKED = 64.0
PICK_BASE = -(2.0 ** 126)
PICK_STEPS = 32


def _top_rows(s, k, break_ties):
    n_rows = s.shape[0]
    rows = lax.broadcasted_iota(jnp.int32, s.shape, 0)
    assert k <= PICK_STEPS
    vals = []
    for r in range(k):
        best = jnp.max(s, axis=0, keepdims=True)
        hit = s == best
        if break_ties:
            hit = rows == jnp.min(jnp.where(hit, rows, n_rows), axis=0, keepdims=True)
        s = jnp.where(hit, PICK_BASE * (1.0 + r / PICK_STEPS), s)
        vals.append(best)
    rank = jnp.where(s <= PICK_BASE, (s * (1.0 / PICK_BASE) - 1.0) * PICK_STEPS, NOT_PICKED)
    return rank, vals


def _ranked_exactly(rank, k):
    return jnp.sum(jnp.where(rank < k, 1.0, 0.0), axis=0, keepdims=True) == float(k)


def _peer_select(s1, s2, k, break_ties):
    rank1, v1 = _top_rows(s1, k, break_ties)
    rank2, v2 = _top_rows(s2, k, break_ties)
    v2a = jnp.concatenate(v2, axis=0)
    v2lo = v2a[:8]
    r8 = lax.broadcasted_iota(jnp.int32, v2lo.shape, 0)
    pieces = [v1[0] + v2a, v1[1] + v2lo]
    for a in range(2, 8):
        pieces.append(jnp.where(r8 < k // (a + 1), v1[a] + v2lo, -jnp.inf))
    pieces.append(jnp.concatenate(v1[8:], axis=0) + v2[0])
    cand = jnp.concatenate(pieces, axis=0)
    rank_c, top = _top_rows(cand, k, break_ties)
    return rank1, rank2, rank_c, v1[0], v2[0], jnp.concatenate(top, axis=0)


def _peer_kernel(x1_ref, wqt_ref, k1_ref, k2_ref, u_ref, vt_ref, g_ref, b_ref, o_ref,
                 xt_scr, s1_scr, s2_scr, e1_scr, r2_scr, e2_scr, y_scr, *, tm, eb, nsub):
    j = pl.program_id(1)
    nk = PEER_N_KEYS
    half = PEER_KEY_DIM // 2
    n_lt = tm // LANES
    k = PEER_TOPK
    pack = BF16_ROWS

    @pl.when(j == 0)
    def _select():
        xt = jnp.transpose(x1_ref[...]).astype(BF16)
        xt_scr[...] = xt
        for h in range(PEER_HEADS):
            qh = jnp.dot(wqt_ref[h * PEER_KEY_DIM:(h + 1) * PEER_KEY_DIM, :], xt, preferred_element_type=F32).astype(BF16)
            s1_scr[h] = jnp.dot(k1_ref[...], qh[:half], preferred_element_type=F32)
            s2_scr[h] = jnp.dot(k2_ref[...], qh[half:], preferred_element_type=F32)

        n_par = 4
        per_head = n_lt // n_par

        def chunk(i, carry):
            h = i // per_head
            lanes = [pl.multiple_of(((i % per_head) * n_par + p) * LANES, LANES) for p in range(n_par)]
            s1s = [s1_scr[h, :, pl.ds(l0, LANES)] for l0 in lanes]
            s2s = [s2_scr[h, :, pl.ds(l0, LANES)] for l0 in lanes]
            fast = [_peer_select(s1, s2, k, break_ties=False) for s1, s2 in zip(s1s, s2s)]
            n_unclean = 0.0
            for f in fast:
                clean = _ranked_exactly(f[0], k) & _ranked_exactly(f[1], k) & _ranked_exactly(f[2], k)
                n_unclean = n_unclean + jnp.sum(jnp.where(clean, 0.0, 1.0))
            picks = lax.cond(
                n_unclean > 0.0,
                lambda: [_peer_select(s1, s2, k, break_ties=True) for s1, s2 in zip(s1s, s2s)],
                lambda: fast)
            for l0, s1, s2, (rank1, rank2, rank_c, v1_max, v2_max, top) in zip(lanes, s1s, s2s, picks):
                den = jnp.sum(jnp.exp(top - top[0:1]), axis=0, keepdims=True)
                picked = jnp.where(rank_c < k, 1.0, 0.0)
                n_of_rank = [jnp.sum(picked[0:16], axis=0, keepdims=True)]
                n_of_rank += [jnp.sum(picked[8 * a + 8:8 * a + 16], axis=0, keepdims=True) for a in range(1, 8)]
                n_of_rank += [picked[72 + a:73 + a] for a in range(8)]
                n1 = jnp.zeros_like(s1)
                for a in range(k):
                    n1 = jnp.where(rank1 == float(a), n_of_rank[a], n1)
                s1_scr[h, :, pl.ds(l0, LANES)] = n1
                e1_scr[h, :, pl.ds(l0, LANES)] = jnp.where(rank1 < k, 0.5 * jnp.exp(s1 - v1_max), 0.0)
                slab = h * n_lt + l0 // LANES
                r2_scr[slab] = rank2.astype(BF16)
                e2_scr[slab] = (jnp.where(rank2 < k, jnp.exp(s2 - v2_max), 0.0) / den).astype(BF16)
            return carry

        lax.fori_loop(0, PEER_HEADS * per_head, chunk, 0)
        y_scr[...] = jnp.zeros_like(y_scr)

    xt = xt_scr[...]
    y_add = None
    rows_per_step = nsub * eb // nk
    for sub in range(nsub):
        a = jnp.dot(u_ref[sub * eb:(sub + 1) * eb, :], xt, preferred_element_type=F32)
        zs = []
        for ib in range(eb // nk):
            row = sub * (eb // nk) + ib
            grp = pl.multiple_of(j * rows_per_step + (row // SUBLANES) * SUBLANES, SUBLANES)
            r_in = row % SUBLANES
            zrow = []
            for lt in range(n_lt):
                ls = slice(lt * LANES, (lt + 1) * LANES)
                w = None
                for h in range(PEER_HEADS):
                    n1r = s1_scr[h, pl.ds(grp, SUBLANES), ls][r_in:r_in + 1]
                    e1r = e1_scr[h, pl.ds(grp, SUBLANES), ls][r_in:r_in + 1]
                    n1b = jnp.broadcast_to(n1r, (pack, LANES)).astype(BF16)[None]
                    e1b = jnp.broadcast_to(e1r, (pack, LANES)).astype(BF16)[None]
                    r2 = r2_scr[h * n_lt + lt].reshape(nk // pack, pack, LANES)
                    e2 = e2_scr[h * n_lt + lt].reshape(nk // pack, pack, LANES)
                    term = jnp.where(r2 < n1b, e2, jnp.zeros_like(e2)) * e1b
                    w = term if w is None else w + term
                x_blk = a[ib * nk:(ib + 1) * nk, ls]
                act = (x_blk * (1.0 + lax.erf(x_blk * np.float32(1.0 / np.sqrt(2.0))))).astype(BF16)
                zrow.append(w.reshape(nk, LANES) * act)
            zs.append(jnp.concatenate(zrow, axis=1))
        z = jnp.concatenate(zs, axis=0)
        y_sub = jnp.dot(vt_ref[:, sub * eb:(sub + 1) * eb], z, preferred_element_type=F32)
        y_add = y_sub if y_add is None else y_add + y_sub
    y_scr[...] += y_add

    @pl.when(j == pl.num_programs(1) - 1)
    def _finish():
        y = jnp.transpose(y_scr[...])
        o_ref[...] = _layer_norm(DEEPNORM_ALPHA * x1_ref[...] + y, g_ref[...], b_ref[...])


def _peer(x1, wqt, k1, k2, u, vt, g, b, tm, eb, nsub):
    t, dm = x1.shape
    n_e = u.shape[0]
    assert (nsub * eb // PEER_N_KEYS) % SUBLANES == 0, "a grid step covers whole sublane groups of first-key rows"
    assert PEER_TOPK == 16 and PEER_N_KEYS == LANES, "the candidate-pair layout in _peer_select is written for k = 16"
    full = lambda a: pl.BlockSpec(a.shape, lambda i, j: (0, 0))
    tab = pltpu.VMEM((PEER_HEADS, PEER_N_KEYS, tm), F32)
    tab_b = pltpu.VMEM((PEER_HEADS * (tm // LANES), PEER_N_KEYS, LANES), BF16)
    return pl.pallas_call(
        functools.partial(_peer_kernel, tm=tm, eb=eb, nsub=nsub),
        grid=(t // tm, n_e // (nsub * eb)),
        in_specs=[
            pl.BlockSpec((tm, dm), lambda i, j: (i, 0)),
            full(wqt), full(k1), full(k2),
            pl.BlockSpec((nsub * eb, dm), lambda i, j: (j, 0)),
            pl.BlockSpec((dm, nsub * eb), lambda i, j: (0, j)),
            full(g), full(b),
        ],
        out_specs=pl.BlockSpec((tm, dm), lambda i, j: (i, 0)),
        out_shape=jax.ShapeDtypeStruct((t, dm), F32),
        scratch_shapes=[
            pltpu.VMEM((dm, tm), BF16), tab, tab, tab, tab_b, tab_b,
            pltpu.VMEM((dm, tm), F32),
        ],
        compiler_params=_cparams(("parallel", "arbitrary"), TILE["peer"]["vmem"]),
        name="peer",
    )(x1, wqt, k1, k2, u, vt, g, b)


def _layer(x, mem, positions, w_in, pe_k, pe_v, w1k, w2k, w1v, w2v, w_mem_kv, w_out, ln1_g, ln1_b,
           w_query, sk1, sk2, pu, pv, ln2_g, ln2_b):
    bsz, seq, dm = x.shape
    m_len = mem.shape[1]
    t = bsz * seq
    d = HEAD_DIM
    g_n = NSA_KV_HEADS
    nsa_t, dsa_t, peer_t = TILE["nsa"], TILE["dsa"], TILE["peer"]
    assert seq % nsa_t["tk"] == 0 and seq % dsa_t["tk"] == 0 and seq >= WINDOW + nsa_t["c"]
    assert seq % TILE["out_ln"]["tm"] == 0, "an out_ln token tile stays inside one batch row (one memory block)"
    assert dsa_t["tk"] >= min(DSA_TOPK_MAX, seq // 4), "a key tile holds at least top-k candidates"
    assert t % peer_t["tm"] == 0 and t % TILE["proj"]["tm"] == 0 and t % TILE["out_ln"]["tm"] == 0
    x2 = x.reshape(t, dm)

    hb, misc, kv32 = _proj(x2, _regroup_w_in(w_in), _rope_inputs(positions), tm=TILE["proj"]["tm"])

    zpad = jnp.zeros((CMP_HIDDEN, d), BF16)
    pe_rows = lambda pe: jnp.pad(pe.reshape(1, CMP_LEN * d), ((0, 7), (0, 0))).astype(BF16)
    w1k_b, w1v_b = w1k.astype(BF16), w1v.astype(BF16)
    kcvc = _compress(kv32, _compress_w1cat(w1k_b, w1v_b), w1k_b, w1v_b,
                     jnp.concatenate([w2k.astype(BF16), zpad], axis=1), jnp.concatenate([zpad, w2v.astype(BF16)], axis=1),
                     pe_rows(pe_k), pe_rows(pe_v), bsz, seq)

    ovl_t, expand = _nsa_consts(seq)
    o_nsa = _nsa(hb, misc, kcvc, ovl_t, expand, bsz, seq, c=nsa_t["c"], tk=nsa_t["tk"])
    o_dsa = _dsa(hb, misc, bsz, seq, c=dsa_t["c"], tk=dsa_t["tk"])

    wm = w_mem_kv.reshape(dm, 2, MEM_HEADS, d).transpose(0, 2, 1, 3).reshape(dm, MEM_HEADS * 2 * d).astype(BF16)
    mkv = _matmul(mem.reshape(bsz * m_len, dm), wm, tm=m_len, out_dtype=BF16)
    x1 = _out_ln(o_nsa, o_dsa, hb, mkv, x2, w_out.astype(BF16), ln1_g.reshape(1, dm), ln1_b.reshape(1, dm),
                 seq, m_len, tm=TILE["out_ln"]["tm"])

    x2o = _peer(x1, w_query.T.astype(BF16), sk1.astype(BF16), sk2.astype(BF16), pu.astype(BF16), pv.T.astype(BF16),
                ln2_g.reshape(1, dm), ln2_b.reshape(1, dm), tm=peer_t["tm"], eb=peer_t["eb"], nsub=peer_t["nsub"])
    return x2o.reshape(bsz, seq, dm)


def kernel(x, mem, positions, w_in, nsa_pe_k, nsa_pe_v, nsa_cmp_w1_k, nsa_cmp_w2_k, nsa_cmp_w1_v, nsa_cmp_w2_v,
           w_mem_kv, w_out, ln1_g, ln1_b, peer_w_query, peer_sub_keys_1, peer_sub_keys_2, peer_u, peer_v, ln2_g, ln2_b):
    assert w_in.shape[0] == DEPTH
    return _layer(x, mem, positions, w_in[0], nsa_pe_k[0], nsa_pe_v[0], nsa_cmp_w1_k[0], nsa_cmp_w2_k[0],
                  nsa_cmp_w1_v[0], nsa_cmp_w2_v[0], w_mem_kv[0], w_out[0], ln1_g[0], ln1_b[0], peer_w_query[0],
                  peer_sub_keys_1[0], peer_sub_keys_2[0], peer_u[0], peer_v[0], ln2_g[0], ln2_b[0])
```
